```python
import math
import jax, jax.numpy as jnp
from jax import lax
import numpy as np


D_MODEL = 2048
BATCH = 2
SEQ = 4096
DEPTH = 2
DEC_BATCH = 8
DEC_SEQ = 1
PAST_LEN = 16384
PAGE_SIZE = 128

N_EVEN = (DEPTH + 1) // 2
N_ODD = DEPTH // 2
EPS = 1e-5

A_WIDTH = D_MODEL
A_HEAD_DIM = 64
A_HEADS = A_WIDTH // A_HEAD_DIM
A_GROUPS = 4
A_STATE = 128
A_CONV_W = 4
A_CONV_DIM = A_WIDTH + 2 * A_GROUPS * A_STATE
SSD_CHUNK = 128

B_WIDTH = D_MODEL
B_HEAD_DIM = 128
B_HEADS = B_WIDTH // B_HEAD_DIM
B_PATTERNS = ((128, 1), (512, 4), (2048, 16))
B_MAX_WINDOW = 2048
B_Q_BLOCK = 128

E0 = A_WIDTH + B_WIDTH
IN0 = A_WIDTH + A_CONV_DIM + A_HEADS + 4 * B_WIDTH

E1 = 2 * D_MODEL
C_GROUP = 16
C_GROUPS = E1 // C_GROUP
C_STATE = 64
C_CHUNK = 128
IN1 = 2 * E1

kernel_name = "hybrid_ssd_dilated_s5_step"


def _rmsnorm(x, w):
    xf = x.astype(jnp.float32)
    y = xf * lax.rsqrt(jnp.mean(xf * xf, axis=-1, keepdims=True) + EPS)
    return (y * w.astype(jnp.float32)).astype(x.dtype)


def _chunk_len(L, chunk):
    return chunk if L % chunk == 0 else L


def _causal_conv(xbc, conv_state, w, b):
    L = xbc.shape[1]
    full = jnp.concatenate([conv_state.astype(xbc.dtype), xbc], axis=1)
    out = b + full[:, 0:L] * w[0]
    for k in range(1, A_CONV_W):
        out = out + full[:, k:k + L] * w[k]
    return jax.nn.silu(out), full[:, -(A_CONV_W - 1):]


def _ssd(x, dt, A, Bm, Cm, h0):
    b, L, H, P = x.shape
    G, N = Bm.shape[2], Bm.shape[3]
    R = H // G
    Q = _chunk_len(L, SSD_CHUNK)
    nc = L // Q
    xdt = (x * dt[..., None]).reshape(b, nc, Q, G, R, P)
    a_cum = jnp.cumsum((dt * A).reshape(b, nc, Q, G, R), axis=2)
    Bc = Bm.reshape(b, nc, Q, G, N)
    Cc = Cm.reshape(b, nc, Q, G, N)
    seg = a_cum[:, :, :, None] - a_cum[:, :, None]
    causal = jnp.tril(jnp.ones((Q, Q), bool))[None, None, :, :, None, None]
    decay = jnp.exp(jnp.where(causal, seg, -jnp.inf))
    cb = jnp.einsum("bcqgn,bcsgn->bcqsg", Cc, Bc)
    y_diag = jnp.einsum("bcqsgr,bcsgrp->bcqgrp", cb[..., None] * decay, xdt)
    decay_end = jnp.exp(a_cum[:, :, -1:] - a_cum)
    chunk_states = jnp.einsum("bcqgn,bcqgrp->bcgrpn", Bc, xdt * decay_end[..., None])
    chunk_decay = jnp.exp(a_cum[:, :, -1])

    def step(h, inp):
        dec, st = inp
        return dec[..., None, None] * h + st, h

    h_final, h_prev = lax.scan(step, h0.reshape(b, G, R, P, N),
                               (jnp.moveaxis(chunk_decay, 1, 0), jnp.moveaxis(chunk_states, 1, 0)))
    h_prev = jnp.moveaxis(h_prev, 0, 1)
    y_off = jnp.einsum("bcqgn,bcgrpn->bcqgrp", Cc, h_prev) * jnp.exp(a_cum)[..., None]
    y = (y_diag + y_off).reshape(b, L, H, P)
    return y, h_final.reshape(b, H, P, N)


def _mixer_a(z, xbc, dt_raw, conv_state, h0, conv_w, conv_b, dt_bias, A_log, Dskip, gn_w):
    b, L, _ = xbc.shape
    f32 = jnp.float32
    xc, new_conv = _causal_conv(xbc, conv_state, conv_w, conv_b)
    xs, Bm, Cm = jnp.split(xc, [A_WIDTH, A_WIDTH + A_GROUPS * A_STATE], axis=-1)
    xs = xs.reshape(b, L, A_HEADS, A_HEAD_DIM).astype(f32)
    Bm = Bm.reshape(b, L, A_GROUPS, A_STATE).astype(f32)
    Cm = Cm.reshape(b, L, A_GROUPS, A_STATE).astype(f32)
    dt = jax.nn.softplus(dt_raw.astype(f32) + dt_bias.astype(f32))
    A = -jnp.exp(A_log.astype(f32))
    y, h_new = _ssd(xs, dt, A, Bm, Cm, h0.astype(f32))
    y = (y + Dskip.astype(f32)[:, None] * xs).reshape(b, L, A_WIDTH)
    y = _rmsnorm(y * jax.nn.silu(z.astype(f32)), gn_w)
    return y.astype(z.dtype), new_conv, h_new


def _dilated_block(q, k_slab, v_slab, q_offset, valid_start):
    f32 = jnp.float32
    Q = q.shape[1]
    S = k_slab.shape[1]
    slopes = jnp.asarray(2.0 ** (-8.0 * np.arange(1, B_HEADS + 1) / B_HEADS), dtype=f32)
    scale = B_HEAD_DIM ** -0.5
    qi = jnp.arange(Q)
    outs, lses = [], []
    for window, dil in B_PATTERNS:
        n_keys = window // dil + 1
        dist = dil * jnp.arange(n_keys)
        idx = q_offset + qi[:, None] - dist[None, :]
        valid = idx >= valid_start
        idx_c = jnp.clip(idx, 0, S - 1)
        kg = jnp.take(k_slab, idx_c, axis=1)
        vg = jnp.take(v_slab, idx_c, axis=1)
        s = jnp.einsum("bqhd,bqkhd->bhqk", q, kg).astype(f32) * scale
        s = s - (slopes[:, None] * dist.astype(f32)[None, :])[:, None, :]
        s = jnp.where(valid[None, None], s, -jnp.inf)
        lse = jax.nn.logsumexp(s, axis=-1, keepdims=True)
        p = jnp.exp(s - lse)
        outs.append(jnp.einsum("bhqk,bqkhd->bqhd", p.astype(vg.dtype), vg).astype(f32))
        lses.append(lse)
    wts = jax.nn.softmax(jnp.stack(lses), axis=0)
    wts = jnp.transpose(wts[..., 0], (0, 1, 3, 2))[..., None]
    return jnp.sum(wts * jnp.stack(outs), axis=0).astype(q.dtype)


def _dilated_prompt(q, k, v):
    b, L, H, d = q.shape
    Qb = _chunk_len(L, B_Q_BLOCK)
    nb = L // Qb
    W = B_MAX_WINDOW
    pad = jnp.zeros((b, W, H, d), k.dtype)
    k_pad = jnp.concatenate([pad, k], axis=1)
    v_pad = jnp.concatenate([pad, v], axis=1)
    qb = jnp.moveaxis(q.reshape(b, nb, Qb, H, d), 1, 0)

    def blk(args):
        i, q_blk = args
        start = i * Qb
        k_slab = lax.dynamic_slice_in_dim(k_pad, start, W + Qb, axis=1)
        v_slab = lax.dynamic_slice_in_dim(v_pad, start, W + Qb, axis=1)
        return _dilated_block(q_blk, k_slab, v_slab, W, W - start)

    o = lax.map(blk, (jnp.arange(nb), qb))
    return jnp.moveaxis(o, 0, 1).reshape(b, L, H, d)


def _even_mixer(h, conv_state, ssm_state, kv_buf, w_in, conv_w, conv_b, dt_bias, A_log, Dskip, gn_w, w_out):
    b, L, _ = h.shape
    s0 = A_WIDTH
    s1 = s0 + A_CONV_DIM
    s2 = s1 + A_HEADS
    s3 = s2 + B_WIDTH
    s4 = s3 + B_WIDTH
    s5 = s4 + B_WIDTH
    z, xbc, dt_raw, q, k, v, gate_b = jnp.split(h @ w_in, [s0, s1, s2, s3, s4, s5], axis=-1)
    y_a, new_conv, new_ssm = _mixer_a(z, xbc, dt_raw, conv_state, ssm_state,
                                      conv_w, conv_b, dt_bias, A_log, Dskip, gn_w)
    q = q.reshape(b, L, B_HEADS, B_HEAD_DIM)
    k = k.reshape(b, L, B_HEADS, B_HEAD_DIM)
    v = v.reshape(b, L, B_HEADS, B_HEAD_DIM)
    if kv_buf is None:
        o = _dilated_prompt(q, k, v)
        n_keep = min(B_MAX_WINDOW, L)
        new_buf = jnp.stack([k, v], axis=2)[:, L - n_keep:]
    else:
        L_B = kv_buf.shape[1]
        k_all = jnp.concatenate([kv_buf[:, :, 0].astype(k.dtype), k], axis=1)
        v_all = jnp.concatenate([kv_buf[:, :, 1].astype(v.dtype), v], axis=1)
        o = _dilated_block(q, k_all, v_all, L_B, 0)
        new_buf = jnp.stack([k_all, v_all], axis=2)[:, -L_B:]
    y_b = o.reshape(b, L, B_WIDTH) * jax.nn.silu(gate_b)
    out = jnp.concatenate([y_a, y_b], axis=-1) @ w_out
    return out, new_conv, new_ssm, new_buf


def _s5(u, h0, A_re, A_im, B_re, B_im, C_re, C_im, Dc, log_dt):
    f32 = jnp.float32
    b, L, _ = u.shape
    uf = u.astype(f32).reshape(b, L, C_GROUPS, C_GROUP)
    A = lax.complex(A_re.astype(f32), A_im.astype(f32))
    dt = jnp.exp(log_dt.astype(f32))[:, None]
    A_bar = jnp.exp(dt * A)
    Bmat = lax.complex(B_re.astype(f32), B_im.astype(f32))
    B_bar = ((A_bar - 1.0) / A)[..., None] * Bmat
    Cmat = lax.complex(C_re.astype(f32), C_im.astype(f32))
    Q = _chunk_len(L, C_CHUNK)
    nc = L // Q
    u_c = jnp.moveaxis(uf.reshape(b, nc, Q, C_GROUPS, C_GROUP), 1, 0)
    h_init = lax.complex(h0[..., 0].astype(f32), h0[..., 1].astype(f32))

    def combine(e1, e2):
        a1, b1 = e1
        a2, b2 = e2
        return a1 * a2, a2 * b1 + b2

    def chunk(h, uc):
        bu = jnp.einsum("gnc,bqgc->bqgn", B_bar, uc.astype(jnp.complex64))
        bu = bu.at[:, 0].add(A_bar * h)
        a = jnp.broadcast_to(A_bar, bu.shape)
        _, hs = lax.associative_scan(combine, (a, bu), axis=1)
        y = jnp.real(jnp.einsum("gcn,bqgn->bqgc", Cmat, hs))
        return hs[:, -1], y

    h_last, y = lax.scan(chunk, h_init, u_c)
    y = jnp.moveaxis(y, 0, 1).reshape(b, L, E1) + Dc.astype(f32) * u.astype(f32)
    new_state = jnp.stack([jnp.real(h_last), jnp.imag(h_last)], axis=-1)
    return y, new_state


def _odd_mixer(h, c_state, w_in, A_re, A_im, B_re, B_im, C_re, C_im, Dc, log_dt, glu_w, glu_b, w_out):
    u, gate = jnp.split(h @ w_in, 2, axis=-1)
    y, new_state = _s5(u, c_state, A_re, A_im, B_re, B_im, C_re, C_im, Dc, log_dt)
    g = jax.nn.gelu(y).astype(h.dtype)
    ga, gb = jnp.split(g @ glu_w + glu_b, 2, axis=-1)
    y = ga * jax.nn.sigmoid(gb) * jax.nn.silu(gate)
    return y @ w_out, new_state


def setup_inputs(seed: int = 0) -> dict:
    key = jax.random.key(seed)
    ks = jax.random.split(key, 32)
    f32 = jnp.float32

    def nrm(k, shape, scale):
        return jax.random.normal(k, shape, f32) * scale

    L_B = min(B_MAX_WINDOW, PAST_LEN)
    dt_a = jnp.exp(jax.random.uniform(ks[10], (N_EVEN, A_HEADS), f32, math.log(1e-3), math.log(1e-1)))
    n_idx = jnp.arange(C_STATE, dtype=f32)
    return {
        "x_prompt": nrm(ks[0], (BATCH, SEQ, D_MODEL), 1.0),
        "x_sample": nrm(ks[1], (DEC_BATCH, DEC_SEQ, D_MODEL), 1.0),
        "state_a_ssm": nrm(ks[2], (N_EVEN, DEC_BATCH, A_HEADS, A_HEAD_DIM, A_STATE), 0.1),
        "state_a_conv": nrm(ks[3], (N_EVEN, DEC_BATCH, A_CONV_W - 1, A_CONV_DIM), 1.0),
        "cache_b_kv": nrm(ks[4], (N_EVEN, DEC_BATCH, L_B, 2, B_HEADS, B_HEAD_DIM), 1.0),
        "state_c": nrm(ks[5], (N_ODD, DEC_BATCH, C_GROUPS, C_STATE, 2), 0.1),
        "norm_w": 1.0 + nrm(ks[6], (DEPTH, D_MODEL), 0.02),
        "final_norm_w": 1.0 + nrm(ks[7], (D_MODEL,), 0.02),
        "even_w_in": nrm(ks[8], (N_EVEN, D_MODEL, IN0), D_MODEL ** -0.5),
        "a_conv_w": nrm(ks[9], (N_EVEN, A_CONV_W, A_CONV_DIM), A_CONV_W ** -0.5),
        "a_conv_b": nrm(ks[11], (N_EVEN, A_CONV_DIM), 0.02),
        "a_dt_bias": dt_a + jnp.log(-jnp.expm1(-dt_a)),
        "a_A_log": jnp.log(jax.random.uniform(ks[12], (N_EVEN, A_HEADS), f32, 1.0, 16.0)),
        "a_D": 1.0 + nrm(ks[13], (N_EVEN, A_HEADS), 0.1),
        "a_norm_w": 1.0 + nrm(ks[14], (N_EVEN, A_WIDTH), 0.02),
        "even_w_out": nrm(ks[15], (N_EVEN, E0, D_MODEL), E0 ** -0.5),
        "odd_w_in": nrm(ks[16], (N_ODD, D_MODEL, IN1), D_MODEL ** -0.5),
        "c_A_re": -0.5 + nrm(ks[17], (N_ODD, C_GROUPS, C_STATE), 0.01),
        "c_A_im": math.pi * n_idx + nrm(ks[18], (N_ODD, C_GROUPS, C_STATE), 0.01),
        "c_B_re": nrm(ks[19], (N_ODD, C_GROUPS, C_STATE, C_GROUP), (2 * C_GROUP) ** -0.5),
        "c_B_im": nrm(ks[20], (N_ODD, C_GROUPS, C_STATE, C_GROUP), (2 * C_GROUP) ** -0.5),
        "c_C_re": nrm(ks[21], (N_ODD, C_GROUPS, C_GROUP, C_STATE), C_STATE ** -0.5),
        "c_C_im": nrm(ks[22], (N_ODD, C_GROUPS, C_GROUP, C_STATE), C_STATE ** -0.5),
        "c_D": 1.0 + nrm(ks[23], (N_ODD, E1), 0.1),
        "c_log_dt": jax.random.uniform(ks[24], (N_ODD, C_GROUPS), f32, math.log(1e-3), math.log(1e-1)),
        "c_glu_w": nrm(ks[25], (N_ODD, E1, 2 * E1), E1 ** -0.5),
        "c_glu_b": nrm(ks[26], (N_ODD, 2 * E1), 0.02),
        "odd_w_out": nrm(ks[27], (N_ODD, E1, D_MODEL), E1 ** -0.5),
    }


def reference(x_prompt, x_sample, state_a_ssm, state_a_conv, cache_b_kv, state_c,
              norm_w, final_norm_w, even_w_in, a_conv_w, a_conv_b, a_dt_bias, a_A_log, a_D,
              a_norm_w, even_w_out, odd_w_in, c_A_re, c_A_im, c_B_re, c_B_im, c_C_re, c_C_im,
              c_D, c_log_dt, c_glu_w, c_glu_b, odd_w_out):
    hp, hs = x_prompt, x_sample
    bp = x_prompt.shape[0]
    ssm_p, conv_p, kv_p, c_p = [], [], [], []
    ssm_s, conv_s, kv_s, c_s = [], [], [], []
    for i in range(DEPTH):
        j = i // 2
        if i % 2 == 0:
            wts = (even_w_in[j], a_conv_w[j], a_conv_b[j], a_dt_bias[j], a_A_log[j], a_D[j],
                   a_norm_w[j], even_w_out[j])
            zc = jnp.zeros((bp, A_CONV_W - 1, A_CONV_DIM), hp.dtype)
            zs = jnp.zeros((bp, A_HEADS, A_HEAD_DIM, A_STATE), jnp.float32)
            dp, c1, s1, k1 = _even_mixer(_rmsnorm(hp, norm_w[i]), zc, zs, None, *wts)
            ds, c2, s2, k2 = _even_mixer(_rmsnorm(hs, norm_w[i]), state_a_conv[j], state_a_ssm[j],
                                         cache_b_kv[j], *wts)
            conv_p.append(c1); ssm_p.append(s1); kv_p.append(k1)
            conv_s.append(c2); ssm_s.append(s2); kv_s.append(k2)
        else:
            wts = (odd_w_in[j], c_A_re[j], c_A_im[j], c_B_re[j], c_B_im[j], c_C_re[j], c_C_im[j],
                   c_D[j], c_log_dt[j], c_glu_w[j], c_glu_b[j], odd_w_out[j])
            zc = jnp.zeros((bp, C_GROUPS, C_STATE, 2), jnp.float32)
            dp, st1 = _odd_mixer(_rmsnorm(hp, norm_w[i]), zc, *wts)
            ds, st2 = _odd_mixer(_rmsnorm(hs, norm_w[i]), state_c[j], *wts)
            c_p.append(st1); c_s.append(st2)
        hp = hp + dp.astype(hp.dtype)
        hs = hs + ds.astype(hs.dtype)
    y_prompt = _rmsnorm(hp, final_norm_w)
    y_sample = _rmsnorm(hs, final_norm_w)
    return (y_prompt, y_sample, jnp.stack(ssm_p), jnp.stack(conv_p), jnp.stack(kv_p), jnp.stack(c_p),
            jnp.stack(ssm_s), jnp.stack(conv_s), jnp.stack(kv_s), jnp.stack(c_s))
```

```python
import functools

import numpy as np
import jax
import jax.numpy as jnp
from jax import lax
from jax.experimental import pallas as pl
from jax.experimental.pallas import tpu as pltpu

F32 = jnp.float32
BF16 = jnp.bfloat16

D_MODEL = 2048
EPS = 1e-5
A_WIDTH = 2048
A_HEAD_DIM = 64
A_HEADS = 32
A_GROUPS = 4
A_STATE = 128
A_CONV_W = 4
A_CONV_DIM = 3072
SSD_CHUNK = 128
B_HEADS = 16
B_HEAD_DIM = 128
B_PATTERNS = ((128, 1), (512, 4), (2048, 16))
B_MAX_WINDOW = 2048
C_GROUP = 16
C_GROUPS = 256
C_STATE = 64
E1 = 4096

LANE = 128
S5_Q = 8
S5_BLK = 8
S5_NBLK = C_GROUPS // S5_BLK
S5_NS = S5_BLK * C_STATE
TN = 512
NEG = -1e30
ATT_SCALE = B_HEAD_DIM ** -0.5
VMEM_LIMIT = 56 * 1024 * 1024

NT_DIMS = (((1,), (1,)), ((), ()))


def _cp(*sem):
    return pltpu.CompilerParams(dimension_semantics=sem, vmem_limit_bytes=VMEM_LIMIT)


def _silu(x):
    return x * jax.nn.sigmoid(x)


def _softplus(x):
    return jnp.maximum(x, 0.0) + jnp.log1p(jnp.exp(-jnp.abs(x)))


def _split3(a):
    hi = a.astype(BF16)
    r = a - hi.astype(F32)
    mid = r.astype(BF16)
    lo = (r - mid.astype(F32)).astype(BF16)
    return hi, mid, lo


def _dot3_left(m, a):
    hi, mid, lo = _split3(a)
    d = lambda p: jnp.dot(m, p, preferred_element_type=F32)
    return (d(lo) + d(mid)) + d(hi)


def _dot3_right(a, m):
    hi, mid, lo = _split3(a)
    d = lambda p: jnp.dot(p, m, preferred_element_type=F32)
    return (d(lo) + d(mid)) + d(hi)


def _rms(x, w):
    ms = jnp.mean(x * x, axis=-1, keepdims=True)
    return (x * lax.rsqrt(ms + EPS)) * w


def _inproj0_kernel(x_ref, nw_ref, w_ref, a_ref, q_ref, k_ref, v_ref, kv_ref, xn_ref):
    j = pl.program_id(1)

    @pl.when(j == 0)
    def _():
        xn_ref[...] = _rms(x_ref[...], nw_ref[...]).astype(BF16)

    acc = jnp.dot(xn_ref[...], w_ref[...], preferred_element_type=F32)

    def heads(ref):
        for hh in range(TN // LANE):
            ref[hh] = acc[:, hh * LANE:(hh + 1) * LANE].astype(BF16)

    @pl.when((j < 7) | (j >= 19))
    def _():
        a_ref[...] = acc

    @pl.when((j >= 7) & (j < 11))
    def _():
        heads(q_ref)

    @pl.when((j >= 11) & (j < 15))
    def _():
        heads(k_ref)
        kv_ref[...] = acc

    @pl.when((j >= 15) & (j < 19))
    def _():
        heads(v_ref)
        kv_ref[...] = acc


def _inproj0(x2d, nw, wcat, B, L):
    M = B * L
    tm = 1024
    rpb = L // tm
    keep = min(B_MAX_WINDOW, L)
    nskip = (L - keep) // tm
    nj = wcat.shape[1] // TN

    def a_map(i, j):
        return (i, jnp.where(j < 7, j, jnp.where(j < 19, 6, j - 11)))

    def head_map(j0):
        return lambda i, j: (i // rpb, jnp.clip(j - j0, 0, 3), i % rpb, 0)

    def kv_map(i, j):
        ri = i % rpb
        return (i // rpb, jnp.maximum(ri - nskip, 0), jnp.where(ri < nskip, 0, jnp.clip(j - 11, 0, 7)))

    hd = jax.ShapeDtypeStruct((B, B_HEADS, L, B_HEAD_DIM), BF16)
    return pl.pallas_call(
        _inproj0_kernel,
        grid=(M // tm, nj),
        in_specs=[pl.BlockSpec((tm, D_MODEL), lambda i, j: (i, 0)),
                  pl.BlockSpec((1, D_MODEL), lambda i, j: (0, 0)),
                  pl.BlockSpec((D_MODEL, TN), lambda i, j: (0, j))],
        out_specs=[pl.BlockSpec((tm, TN), a_map),
                   pl.BlockSpec((None, TN // LANE, tm, LANE), head_map(7)),
                   pl.BlockSpec((None, TN // LANE, tm, LANE), head_map(11)),
                   pl.BlockSpec((None, TN // LANE, tm, LANE), head_map(15)),
                   pl.BlockSpec((None, tm, TN), kv_map)],
        out_shape=[jax.ShapeDtypeStruct((M, 16 * TN), F32), hd, hd, hd,
                   jax.ShapeDtypeStruct((B, keep, 8 * TN), F32)],
        scratch_shapes=[pltpu.VMEM((tm, D_MODEL), BF16)],
        compiler_params=_cp("arbitrary", "arbitrary"),
    )(x2d, nw, wcat)


def _mm_norm_kernel(x_ref, nw_ref, w_ref, o_ref, xn_ref):
    @pl.when(pl.program_id(1) == 0)
    def _():
        xn_ref[...] = _rms(x_ref[...], nw_ref[...]).astype(BF16)

    o_ref[...] = jnp.dot(xn_ref[...], w_ref[...], preferred_element_type=F32)


def _mm_norm(x, nw, w, tm, n_row_tiles, x_map):
    K, N = w.shape
    return pl.pallas_call(
        _mm_norm_kernel,
        grid=(n_row_tiles, N // TN),
        in_specs=[pl.BlockSpec((tm, K), lambda i, j: x_map(i)),
                  pl.BlockSpec((1, K), lambda i, j: (0, 0)),
                  pl.BlockSpec((K, TN), lambda i, j: (0, j))],
        out_specs=pl.BlockSpec((tm, TN), lambda i, j: (i, j)),
        out_shape=jax.ShapeDtypeStruct((n_row_tiles * tm, N), F32),
        scratch_shapes=[pltpu.VMEM((tm, K), BF16)],
        compiler_params=_cp("arbitrary", "arbitrary"),
    )(x, nw, w)


def _outproj0_kernel(ya_ref, yb_ref, wa_ref, wb_ref, res_ref, o_ref):
    o_ref[...] = (res_ref[...]
                  + jnp.dot(ya_ref[...].astype(BF16), wa_ref[...], preferred_element_type=F32)
                  + jnp.dot(yb_ref[...].astype(BF16), wb_ref[...], preferred_element_type=F32))


def _outproj0(ya, yb, w, res, tm):
    M = ya.shape[0]
    K = ya.shape[1]
    return pl.pallas_call(
        _outproj0_kernel,
        grid=(M // tm, D_MODEL // TN),
        in_specs=[pl.BlockSpec((tm, K), lambda i, j: (i, 0)),
                  pl.BlockSpec((tm, K), lambda i, j: (i, 0)),
                  pl.BlockSpec((K, TN), lambda i, j: (0, j)),
                  pl.BlockSpec((K, TN), lambda i, j: (1, j)),
                  pl.BlockSpec((tm, TN), lambda i, j: (i, j))],
        out_specs=pl.BlockSpec((tm, TN), lambda i, j: (i, j)),
        out_shape=jax.ShapeDtypeStruct((M, D_MODEL), F32),
        compiler_params=_cp("arbitrary", "arbitrary"),
    )(ya, yb, w, w, res)


def _glu_kernel(g_ref, wa_ref, wb_ref, ba_ref, bb_ref, gate_ref, o_ref):
    g = g_ref[...].astype(BF16)
    ga = jnp.dot(g, wa_ref[...], preferred_element_type=F32) + ba_ref[...]
    gb = jnp.dot(g, wb_ref[...], preferred_element_type=F32) + bb_ref[...]
    o_ref[...] = (ga * jax.nn.sigmoid(gb) * _silu(gate_ref[...])).astype(o_ref.dtype)


def _glu(g, w, bias, u_gate, tm, out_dtype):
    M = g.shape[0]
    nb = E1 // TN
    return pl.pallas_call(
        _glu_kernel,
        grid=(M // tm, nb),
        in_specs=[pl.BlockSpec((tm, E1), lambda i, j: (i, 0)),
                  pl.BlockSpec((E1, TN), lambda i, j: (0, j)),
                  pl.BlockSpec((E1, TN), lambda i, j: (0, j + nb)),
                  pl.BlockSpec((1, TN), lambda i, j: (0, j)),
                  pl.BlockSpec((1, TN), lambda i, j: (0, j + nb)),
                  pl.BlockSpec((tm, TN), lambda i, j: (i, j + nb))],
        out_specs=pl.BlockSpec((tm, TN), lambda i, j: (i, j)),
        out_shape=jax.ShapeDtypeStruct((M, E1), out_dtype),
        compiler_params=_cp("arbitrary", "arbitrary"),
    )(g, w, w, bias, bias, u_gate)


def _outproj1_kernel(y_ref, w_ref, res_ref, fnw_ref, o_ref, acc_ref):
    k = pl.program_id(1)

    @pl.when(k == 0)
    def _():
        acc_ref[...] = jnp.zeros_like(acc_ref)

    acc_ref[...] += jnp.dot(y_ref[...].astype(BF16), w_ref[...], preferred_element_type=F32)

    @pl.when(k == pl.num_programs(1) - 1)
    def _():
        o_ref[...] = _rms(res_ref[...] + acc_ref[...], fnw_ref[...])


def _outproj1(y, w, res_view, fnw, tm, n_row_tiles, view_map, out_rows):
    K = y.shape[1]
    tk = 512
    return pl.pallas_call(
        _outproj1_kernel,
        grid=(n_row_tiles, K // tk),
        in_specs=[pl.BlockSpec((tm, tk), lambda i, k: (i, k)),
                  pl.BlockSpec((tk, D_MODEL), lambda i, k: (k, 0)),
                  pl.BlockSpec((tm, D_MODEL), lambda i, k: view_map(i)),
                  pl.BlockSpec((1, D_MODEL), lambda i, k: (0, 0))],
        out_specs=pl.BlockSpec((tm, D_MODEL), lambda i, k: view_map(i)),
        out_shape=jax.ShapeDtypeStruct(res_view.shape, F32),
        scratch_shapes=[pltpu.VMEM((tm, D_MODEL), F32)],
        compiler_params=_cp("arbitrary", "arbitrary"),
    )(y, w, res_view, fnw)


def _ssd_kernel(xbc_ref, dt_ref, z_ref, cw_ref, cb_ref, dtb_ref, alog_ref, dsk_ref, gnw_ref,
                y_ref, st_ref, tail_ref, h_scr, tail_scr, y_scr):
    c = pl.program_id(1)
    Q = SSD_CHUNK

    @pl.when(c == 0)
    def _():
        h_scr[...] = jnp.zeros_like(h_scr)
        tail_scr[...] = jnp.zeros_like(tail_scr)

    x_raw = xbc_ref[...]
    tail8 = tail_scr[...]
    row8 = lax.broadcasted_iota(jnp.int32, (8, 1), 0)
    conv = cb_ref[...] + cw_ref[3:4, :] * x_raw
    for k in (1, 2, 3):
        xr = pltpu.roll(x_raw, k, 0)
        first = jnp.where(row8 < k, pltpu.roll(tail8, k, 0), xr[0:8, :])
        conv = conv + cw_ref[3 - k:4 - k, :] * jnp.concatenate([first, xr[8:, :]], axis=0)
    tail_scr[...] = x_raw[Q - 8:Q, :]
    xc = _silu(conv)

    dt = _softplus(dt_ref[...] + dtb_ref[...])
    a = dt * (-jnp.exp(alog_ref[...]))
    qi = lax.broadcasted_iota(jnp.int32, (Q, Q), 0)
    si = lax.broadcasted_iota(jnp.int32, (Q, Q), 1)
    tri = qi >= si
    a_cum = _dot3_left(tri.astype(BF16), a)
    a_cum_t = a_cum.T
    lo_half = si < A_HEAD_DIM

    for g in range(A_GROUPS):
        bg = xc[:, A_WIDTH + g * A_STATE:A_WIDTH + (g + 1) * A_STATE].astype(BF16)
        cg = xc[:, A_WIDTH + (A_GROUPS + g) * A_STATE:A_WIDTH + (A_GROUPS + g + 1) * A_STATE].astype(BF16)
        cbm = lax.dot_general(cg, bg, NT_DIMS, preferred_element_type=F32)
        hprev = h_scr[g * 512:(g + 1) * 512, :]
        yoff = lax.dot_general(cg, hprev.astype(BF16), NT_DIMS, preferred_element_type=F32)
        for jj in range(4):
            p = g * 4 + jj
            h0, h1 = 2 * p, 2 * p + 1
            dtp = jnp.where(lo_half, dt[:, h0:h0 + 1], dt[:, h1:h1 + 1])
            ap = jnp.where(lo_half, a_cum[:, h0:h0 + 1], a_cum[:, h1:h1 + 1])
            x_p = xc[:, p * LANE:(p + 1) * LANE]
            xdt = x_p * dtp
            g0 = cbm * jnp.exp(jnp.where(tri, a_cum[:, h0:h0 + 1] - a_cum_t[h0:h0 + 1, :], NEG))
            g1 = cbm * jnp.exp(jnp.where(tri, a_cum[:, h1:h1 + 1] - a_cum_t[h1:h1 + 1, :], NEG))
            lhs = jnp.concatenate([g0, g1], axis=1).astype(BF16)
            rhs = jnp.concatenate([jnp.where(lo_half, xdt, 0.0), jnp.where(lo_half, 0.0, xdt)],
                                  axis=0).astype(BF16)
            ydiag = jnp.dot(lhs, rhs, preferred_element_type=F32)
            y_scr[:, p * LANE:(p + 1) * LANE] = (
                ydiag + yoff[:, jj * LANE:(jj + 1) * LANE] * jnp.exp(ap)
                + dsk_ref[:, p * LANE:(p + 1) * LANE] * x_p)
            xd = xdt * jnp.exp(ap[Q - 1:Q, :] - ap)
            st = jnp.dot(xd.T.astype(BF16), bg, preferred_element_type=F32)
            cd = jnp.exp(jnp.where(qi < A_HEAD_DIM, a_cum_t[h0:h0 + 1, Q - 1:Q],
                                   a_cum_t[h1:h1 + 1, Q - 1:Q]))
            h_scr[p * LANE:(p + 1) * LANE, :] = cd * hprev[jj * LANE:(jj + 1) * LANE, :] + st

    y = y_scr[...] * _silu(z_ref[...])
    y_ref[...] = _rms(y, gnw_ref[...]).astype(y_ref.dtype)

    @pl.when(c == pl.num_programs(1) - 1)
    def _():
        st_ref[...] = h_scr[...]
        tail_ref[...] = x_raw[Q - 8:Q, :]


def _ssd(a_arr, B, L, cw, cb, dtb, alog, dsk, gnw):
    nc = L // SSD_CHUNK
    Q = SSD_CHUNK
    row = lambda b, c: b * nc + c
    full = lambda shp: pl.BlockSpec(shp, lambda b, c: (0, 0))
    return pl.pallas_call(
        _ssd_kernel,
        grid=(B, nc),
        in_specs=[pl.BlockSpec((Q, A_CONV_DIM), lambda b, c: (row(b, c), 0)),
                  pl.BlockSpec((Q, LANE), lambda b, c: (row(b, c), A_CONV_DIM // LANE)),
                  pl.BlockSpec((Q, A_WIDTH), lambda b, c: (row(b, c), 2)),
                  full((A_CONV_W, A_CONV_DIM)), full((1, A_CONV_DIM)), full((1, LANE)),
                  full((1, LANE)), full((1, A_WIDTH)), full((1, A_WIDTH))],
        out_specs=[pl.BlockSpec((Q, A_WIDTH), lambda b, c: (row(b, c), 0)),
                   pl.BlockSpec((None, A_WIDTH, A_STATE), lambda b, c: (b, 0, 0)),
                   pl.BlockSpec((None, 8, A_CONV_DIM), lambda b, c: (b, 0, 0))],
        out_shape=[jax.ShapeDtypeStruct((B * L, A_WIDTH), BF16),
                   jax.ShapeDtypeStruct((B, A_WIDTH, A_STATE), F32),
                   jax.ShapeDtypeStruct((B, 8, A_CONV_DIM), F32)],
        scratch_shapes=[pltpu.VMEM((A_WIDTH, A_STATE), F32), pltpu.VMEM((8, A_CONV_DIM), F32),
                        pltpu.VMEM((Q, A_WIDTH), F32)],
        compiler_params=_cp("arbitrary", "arbitrary"),
    )(a_arr, a_arr, a_arr, cw, cb, dtb, alog, dsk, gnw)


def _ssd_step_kernel(xbc_ref, cs_ref, dt_ref, z_ref, h0_ref, cw_ref, cb_ref, dtb_ref, alog_ref,
                     e_ref, dsk_ref, gnw_ref, y_ref, hn_ref, nc_ref):
    x_raw = xbc_ref[...]
    cs = cs_ref[...]
    conv = (cb_ref[...] + cw_ref[0:1, :] * cs[0:1, :] + cw_ref[1:2, :] * cs[1:2, :]
            + cw_ref[2:3, :] * cs[2:3, :] + cw_ref[3:4, :] * x_raw)
    xc = _silu(conv)
    nc_ref[0:1, :] = cs[1:2, :]
    nc_ref[1:2, :] = cs[2:3, :]
    nc_ref[2:3, :] = x_raw

    dt = _softplus(dt_ref[...] + dtb_ref[...])
    da = dt * (-jnp.exp(alog_ref[...]))
    e = e_ref[...]
    dt_e = _dot3_right(jnp.broadcast_to(dt, (8, LANE)), e)[0:1, :]
    da_e = _dot3_right(jnp.broadcast_to(da, (8, LANE)), e)[0:1, :]
    xs = xc[:, :A_WIDTH]
    sub = lax.broadcasted_iota(jnp.int32, (8, 1), 0)
    rows = jnp.where(sub == 0, xs * dt_e, jnp.where(sub == 1, jnp.exp(da_e), 0.0))
    cols = jnp.concatenate([rows, jnp.zeros((LANE - 8, A_WIDTH), F32)], axis=0).T
    xdt_col = cols[:, 0:1]
    dec_col = cols[:, 1:2]
    ys = []
    for g in range(A_GROUPS):
        bg = xc[:, A_WIDTH + g * A_STATE:A_WIDTH + (g + 1) * A_STATE]
        cg = xc[:, A_WIDTH + (A_GROUPS + g) * A_STATE:A_WIDTH + (A_GROUPS + g + 1) * A_STATE]
        sl = slice(g * 512, (g + 1) * 512)
        hn = dec_col[sl, :] * h0_ref[sl, :] + xdt_col[sl, :] * bg
        hn_ref[sl, :] = hn
        c8 = jnp.broadcast_to(cg, (8, A_STATE)).astype(BF16)
        ys.append(lax.dot_general(c8, hn.astype(BF16), NT_DIMS, preferred_element_type=F32)[0:1, :])
    y = jnp.concatenate(ys, axis=1) + dsk_ref[...] * xs
    y_ref[...] = _rms(y * _silu(z_ref[...]), gnw_ref[...])


def _ssd_step(xbc, cs, dt, z, h0, cw, cb, dtb, alog, e, dsk, gnw):
    nb = xbc.shape[0]
    per_b = lambda n: pl.BlockSpec((None, 1, n), lambda b: (b, 0, 0))
    full = lambda shp: pl.BlockSpec(shp, lambda b: (0, 0))
    return pl.pallas_call(
        _ssd_step_kernel,
        grid=(nb,),
        in_specs=[per_b(A_CONV_DIM), pl.BlockSpec((None, 3, A_CONV_DIM), lambda b: (b, 0, 0)),
                  per_b(LANE), per_b(A_WIDTH),
                  pl.BlockSpec((None, A_WIDTH, A_STATE), lambda b: (b, 0, 0)),
                  full((A_CONV_W, A_CONV_DIM)), full((1, A_CONV_DIM)), full((1, LANE)),
                  full((1, LANE)), full((LANE, A_WIDTH)), full((1, A_WIDTH)), full((1, A_WIDTH))],
        out_specs=[per_b(A_WIDTH),
                   pl.BlockSpec((None, A_WIDTH, A_STATE), lambda b: (b, 0, 0)),
                   pl.BlockSpec((None, 3, A_CONV_DIM), lambda b: (b, 0, 0))],
        out_shape=[jax.ShapeDtypeStruct((nb, 1, A_WIDTH), F32),
                   jax.ShapeDtypeStruct((nb, A_WIDTH, A_STATE), F32),
                   jax.ShapeDtypeStruct((nb, 3, A_CONV_DIM), F32)],
        compiler_params=_cp("arbitrary"),
    )(xbc, cs, dt, z, h0, cw, cb, dtb, alog, e, dsk, gnw)


def _bias_tile(slope, dil):
    i = lax.broadcasted_iota(jnp.int32, (LANE, 2 * LANE), 0)
    c = lax.broadcasted_iota(jnp.int32, (LANE, 2 * LANE), 1)
    j = LANE + i - c
    valid = (j >= 0) & (j <= LANE)
    return jnp.where(valid, -(slope * (dil * j).astype(F32)), NEG), c


def _attn_tile(q, kcat, vcat, bias):
    s = lax.dot_general(q, kcat, NT_DIMS, preferred_element_type=F32) * ATT_SCALE + bias
    m = jnp.max(s, axis=1, keepdims=True)
    p = jnp.exp(s - m)
    l = jnp.sum(p, axis=1, keepdims=True)
    acc = jnp.dot(p.astype(BF16), vcat, preferred_element_type=F32)
    return m, l, acc


def _attn_tiles(dil, nsub, first_block, slope_ref, q_ref, k_ref, v_ref, kp_ref, vp_ref):
    bias_full, c = _bias_tile(slope_ref[0:1, 0:1], dil)
    bias0 = jnp.where((c >= LANE) | jnp.logical_not(first_block), bias_full, NEG)
    for r in range(dil):
        cs = slice(r * LANE, (r + 1) * LANE)
        for sub in range(nsub):
            rs = slice(sub * LANE, (sub + 1) * LANE)
            if sub == 0:
                kprev, vprev, bias = kp_ref[:, cs], vp_ref[:, cs], bias0
            else:
                ps = slice((sub - 1) * LANE, sub * LANE)
                kprev, vprev, bias = k_ref[ps, cs], v_ref[ps, cs], bias_full
            kcat = jnp.concatenate([kprev, k_ref[rs, cs]], axis=0)
            vcat = jnp.concatenate([vprev, v_ref[rs, cs]], axis=0)
            yield (r, sub, rs, cs) + _attn_tile(q_ref[rs, cs], kcat, vcat, bias)


def _attn_pattern_kernel(dil, nsub, slope_ref, q_ref, k_ref, v_ref, kp_ref, vp_ref, o_ref, lse_ref):
    first_block = pl.program_id(2) == 0
    lane = lax.broadcasted_iota(jnp.int32, (LANE, LANE), 1)
    lse_mat = jnp.zeros((LANE, LANE), F32)
    for r, sub, rs, cs, m, l, acc in _attn_tiles(dil, nsub, first_block, slope_ref, q_ref, k_ref,
                                                 v_ref, kp_ref, vp_ref):
        o_ref[rs, cs] = acc / l
        lse_mat = jnp.where(lane == r * nsub + sub, m + jnp.log(l), lse_mat)
    lse_ref[...] = lse_mat


def _attn_specs(dil, R, B, L):
    nsub = R // LANE
    cur = pl.BlockSpec((None, None, R, dil * LANE), lambda b, h, i: (b, h, i, 0))
    prev = pl.BlockSpec((None, None, LANE, dil * LANE),
                        lambda b, h, i: (b, h, jnp.maximum(i * nsub - 1, 0), 0))
    slope = pl.BlockSpec((None, 8, LANE), lambda b, h, i: (h, 0, 0))
    return slope, cur, prev


def _attn_pattern(dil, q, k, v, slopes):
    B, H, L, dh = q.shape
    R = 2048 // dil
    nsub = R // LANE
    nblk = L // 2048
    view = lambda x: x.reshape(B, H, L // dil, dil * dh)
    slope, cur, prev = _attn_specs(dil, R, B, L)
    o, lse = pl.pallas_call(
        functools.partial(_attn_pattern_kernel, dil, nsub),
        grid=(B, H, nblk),
        in_specs=[slope, cur, cur, cur, prev, prev],
        out_specs=[cur, pl.BlockSpec((None, None, None, LANE, LANE), lambda b, h, i: (b, h, i, 0, 0))],
        out_shape=[jax.ShapeDtypeStruct((B, H, L // dil, dil * dh), F32),
                   jax.ShapeDtypeStruct((B, H, nblk, LANE, LANE), F32)],
        compiler_params=_cp("arbitrary", "arbitrary", "arbitrary"),
    )(slopes, view(q), view(k), view(v), view(k), view(v))
    lse = lse[..., :dil * nsub].reshape(B, H, nblk, LANE, dil, nsub)
    lse = jnp.transpose(lse, (0, 1, 2, 5, 3, 4)).reshape(B, H, L // LANE, LANE)
    return o.reshape(B, H, L, dh), lse


def _attn_final_kernel(slope_ref, q_ref, k_ref, v_ref, kp_ref, vp_ref, o16_ref, o4_ref,
                       l16_ref, l4_ref, gate_ref, y_ref):
    first_block = pl.program_id(2) == 0
    for r, sub, rs, cs, m, l, acc in _attn_tiles(1, 16, first_block, slope_ref, q_ref, k_ref,
                                                 v_ref, kp_ref, vp_ref):
        lse1 = m + jnp.log(l)
        c16 = jnp.broadcast_to(l16_ref[sub:sub + 1, :], (LANE, LANE)).T
        c4 = jnp.broadcast_to(l4_ref[sub:sub + 1, :], (LANE, LANE)).T
        mm = jnp.maximum(jnp.maximum(c16, c4), lse1)
        w1 = jnp.exp(lse1 - mm)
        w16 = jnp.exp(c16 - mm)
        w4 = jnp.exp(c4 - mm)
        o = (w1 * (acc / l) + w16 * o16_ref[rs, :] + w4 * o4_ref[rs, :]) / (w1 + w16 + w4)
        y_ref[rs, :] = (o * _silu(gate_ref[rs, :])).astype(y_ref.dtype)


def _attn_final(q, k, v, slopes, o16, l16, o4, l4, a_arr):
    B, H, L, dh = q.shape
    R = 2048
    nblk = L // R
    slope, cur, prev = _attn_specs(1, R, B, L)
    lspec = pl.BlockSpec((None, None, R // LANE, LANE), lambda b, h, i: (b, h, i, 0))
    gate = pl.BlockSpec((R, LANE), lambda b, h, i: (b * nblk + i, 48 + h))
    return pl.pallas_call(
        _attn_final_kernel,
        grid=(B, H, nblk),
        in_specs=[slope, cur, cur, cur, prev, prev, cur, cur, lspec, lspec, gate],
        out_specs=pl.BlockSpec((R, LANE), lambda b, h, i: (b * nblk + i, h)),
        out_shape=jax.ShapeDtypeStruct((B * L, H * dh), BF16),
        compiler_params=_cp("arbitrary", "arbitrary", "arbitrary"),
    )(slopes, q, k, v, k, v, o16, o4, l16, l4, a_arr)


def _attn_step_kernel(slope_ref, q_ref, kn_ref, vn_ref, gate_ref, c_ref, oc_ref, y_ref,
                      p_scr, st_scr):
    cb = pl.program_id(1)
    is_k = cb < B_HEADS
    h = cb % B_HEADS
    W = B_MAX_WINDOW
    x = c_ref[...]
    new_row = jnp.where(is_k, kn_ref[...], vn_ref[...])
    row = lax.broadcasted_iota(jnp.int32, (W, 1), 0)
    oc_ref[...] = jnp.where(row == W - 1, new_row, pltpu.roll(x, W - 1, 0))
    lane = lax.broadcasted_iota(jnp.int32, (1, LANE), 1)

    @pl.when(is_k)
    def _():
        qb = q_ref[...].astype(BF16)
        q8 = jnp.broadcast_to(qb, (8, B_HEAD_DIM))
        s = lax.dot_general(q8, x.astype(BF16), NT_DIMS, preferred_element_type=F32)[0:1, :]
        dist = W - lax.broadcasted_iota(jnp.int32, (1, W), 1)
        mult = jnp.zeros((1, W), F32)
        for window, dil in B_PATTERNS:
            mult = mult + jnp.where(((dist & (dil - 1)) == 0) & (dist <= window), 1.0, 0.0)
        slope = slope_ref[0:1, 0:1]
        s = jnp.where(mult > 0.0, s * ATT_SCALE - slope * dist.astype(F32), NEG)
        s_self = jnp.sum(qb.astype(F32) * kn_ref[...].astype(BF16).astype(F32), axis=1,
                         keepdims=True) * ATT_SCALE
        m = jnp.maximum(jnp.max(s, axis=1, keepdims=True), s_self)
        p = mult * jnp.exp(s - m)
        p_self = float(len(B_PATTERNS)) * jnp.exp(s_self - m)
        l = jnp.sum(p, axis=1, keepdims=True) + p_self
        p_scr[pl.ds(h, 1), :] = p
        st_scr[pl.ds(h, 1), :] = jnp.where(lane == 0, p_self, l)

    @pl.when(jnp.logical_not(is_k))
    def _():
        p8 = jnp.broadcast_to(p_scr[pl.ds(h, 1), :], (8, W)).astype(BF16)
        st = st_scr[pl.ds(h, 1), :]
        o = jnp.dot(p8, x.astype(BF16), preferred_element_type=F32)[0:1, :]
        o = (o + st[:, 0:1] * vn_ref[...].astype(BF16).astype(F32)) / st[:, 1:2]
        y_ref[...] = o * _silu(gate_ref[...])


def _attn_step(q, kn, vn, gate, cache, slopes):
    nb = q.shape[0]
    W = B_MAX_WINDOW
    hspec = pl.BlockSpec((None, None, 1, B_HEAD_DIM), lambda b, c: (b, c % B_HEADS, 0, 0))
    cspec = pl.BlockSpec((None, W, LANE), lambda b, c: (b, 0, c))
    return pl.pallas_call(
        _attn_step_kernel,
        grid=(nb, 2 * B_HEADS),
        in_specs=[pl.BlockSpec((None, 8, LANE), lambda b, c: (c % B_HEADS, 0, 0)),
                  hspec, hspec, hspec, hspec, cspec],
        out_specs=[cspec,
                   pl.BlockSpec((None, None, 1, B_HEAD_DIM),
                                lambda b, c: (b, jnp.maximum(c - B_HEADS, 0), 0, 0))],
        out_shape=[jax.ShapeDtypeStruct(cache.shape, F32),
                   jax.ShapeDtypeStruct((nb, B_HEADS, 1, B_HEAD_DIM), F32)],
        scratch_shapes=[pltpu.VMEM((B_HEADS, W), F32), pltpu.VMEM((B_HEADS, LANE), F32)],
        compiler_params=_cp("arbitrary", "arbitrary"),
    )(slopes, q, kn, vn, gate, cache)


def _cmul(ar, ai, br, bi):
    return ar * br - ai * bi, ar * bi + ai * br


def _s5_prep_kernel(are_ref, aim_ref, ldt_ref, bre_ref, bim_ref, ctre_ref, ctim_ref,
                    t_ref, wb_ref, cb_ref, lam_ref, lamq_ref, cpl_ref):
    Q = S5_Q
    are, aim = are_ref[...], aim_ref[...]
    dt = jnp.exp(ldt_ref[...])
    mag = jnp.exp(dt * are)
    l_re = mag * jnp.cos(dt * aim)
    l_im = mag * jnp.sin(dt * aim)
    den = are * are + aim * aim
    f_re = ((l_re - 1.0) * are + l_im * aim) / den
    f_im = (l_im * are - (l_re - 1.0) * aim) / den
    lr, li = l_re[0:1, :], l_im[0:1, :]
    bb_re, bb_im = _cmul(f_re[0:1, :], f_im[0:1, :], bre_ref[...], bim_ref[...])
    ctre, ctim = ctre_ref[...], ctim_ref[...]
    cplain = jnp.concatenate([ctre, -ctim], axis=0).astype(BF16)
    cpl_ref[...] = cplain
    stk = jnp.concatenate([l_re, l_im, jnp.zeros((LANE - 16, S5_NS), F32)], axis=0).T
    lc_re, lc_im = stk[:, 0:1], stk[:, 8:9]
    p_re, p_im = jnp.ones((1, S5_NS), F32), jnp.zeros((1, S5_NS), F32)
    pc_re, pc_im = lc_re, lc_im
    dts = []
    for tau in range(Q):
        e_re, e_im = _cmul(bb_re, bb_im, p_re, p_im)
        wp = jnp.concatenate([e_re, e_im], axis=1).astype(BF16)
        s = Q - 1 - tau
        wb_ref[s * LANE:(s + 1) * LANE, :] = wp
        dts.append(jnp.dot(wp, cplain, preferred_element_type=F32).astype(BF16))
        c_re, c_im = _cmul(ctre, ctim, pc_re, pc_im)
        cb_ref[:, tau * LANE:(tau + 1) * LANE] = jnp.concatenate([c_re, -c_im], axis=0).astype(BF16)
        p_re, p_im = _cmul(p_re, p_im, lr, li)
        pc_re, pc_im = _cmul(pc_re, pc_im, lc_re, lc_im)
    zero = jnp.zeros((LANE, LANE), BF16)
    for s in range(Q):
        for q in range(Q):
            t_ref[s * LANE:(s + 1) * LANE, q * LANE:(q + 1) * LANE] = dts[q - s] if q >= s else zero
    lam_ref[...] = jnp.concatenate([l_re, l_im], axis=1)
    lamq_ref[...] = jnp.broadcast_to(jnp.concatenate([p_re, p_im], axis=1), (8, 2 * S5_NS))


def _s5_prep(are, aim, ldt, bre, bim, ctre, ctim):
    Q = S5_Q
    row = pl.BlockSpec((None, 8, S5_NS), lambda i: (i, 0, 0))
    blk = lambda a, b: pl.BlockSpec((None, a, b), lambda i: (i, 0, 0))
    return pl.pallas_call(
        _s5_prep_kernel,
        grid=(S5_NBLK,),
        in_specs=[row, row, row, blk(LANE, S5_NS), blk(LANE, S5_NS), blk(S5_NS, LANE), blk(S5_NS, LANE)],
        out_specs=[blk(Q * LANE, Q * LANE), blk(Q * LANE, 2 * S5_NS), blk(2 * S5_NS, Q * LANE),
                   blk(8, 2 * S5_NS), blk(8, 2 * S5_NS), blk(2 * S5_NS, LANE)],
        out_shape=[jax.ShapeDtypeStruct((S5_NBLK, Q * LANE, Q * LANE), BF16),
                   jax.ShapeDtypeStruct((S5_NBLK, Q * LANE, 2 * S5_NS), BF16),
                   jax.ShapeDtypeStruct((S5_NBLK, 2 * S5_NS, Q * LANE), BF16),
                   jax.ShapeDtypeStruct((S5_NBLK, 8, 2 * S5_NS), F32),
                   jax.ShapeDtypeStruct((S5_NBLK, 8, 2 * S5_NS), F32),
                   jax.ShapeDtypeStruct((S5_NBLK, 2 * S5_NS, LANE), BF16)],
        compiler_params=_cp("arbitrary"),
    )(are, aim, ldt, bre, bim, ctre, ctim)


def _s5_main_kernel(nb, u_ref, t_ref, wb_ref, cb_ref, lamq_ref, dsk_ref, g_ref, hfin_ref,
                    lhs_scr, s_scr, hp_scr):
    Q = S5_Q
    NS = S5_NS
    nrow = u_ref.shape[1]
    nch = nrow // nb
    for s in range(Q):
        lhs_scr[:, s * LANE:(s + 1) * LANE] = u_ref[s].astype(BF16)
    lhs = lhs_scr[...]
    s_scr[...] = jnp.dot(lhs, wb_ref[...], preferred_element_type=F32)
    lq = lamq_ref[0:1, :]
    lq_re, lq_im = lq[:, :NS], lq[:, NS:]
    sub = lax.broadcasted_iota(jnp.int32, (8, 1), 0)

    def tile_body(t, carry):
        base = pl.multiple_of(t * 8, 8)
        new = []
        for b in range(nb):
            h_re, h_im = carry[2 * b], carry[2 * b + 1]
            st = s_scr[pl.ds(b * nch + base, 8), :]
            hp = jnp.zeros((8, 2 * NS), F32)
            for r in range(8):
                hp = jnp.where(sub == r, jnp.concatenate([h_re, h_im], axis=1), hp)
                n_re, n_im = _cmul(lq_re, lq_im, h_re, h_im)
                h_re, h_im = n_re + st[r:r + 1, :NS], n_im + st[r:r + 1, NS:]
            hp_scr[pl.ds(b * nch + base, 8), :] = hp
            new += [h_re, h_im]
        return tuple(new)

    zero = jnp.zeros((1, NS), F32)
    fin = lax.fori_loop(0, nch // 8, tile_body, (zero,) * (2 * nb))
    for b in range(nb):
        hfin_ref[b:b + 1, :] = jnp.concatenate([fin[2 * b], fin[2 * b + 1]], axis=1)

    y = (jnp.dot(lhs, t_ref[...], preferred_element_type=F32)
         + jnp.dot(hp_scr[...].astype(BF16), cb_ref[...], preferred_element_type=F32))
    for q in range(Q):
        yq = y[:, q * LANE:(q + 1) * LANE] + dsk_ref[...] * u_ref[q]
        g_ref[q] = jax.nn.gelu(yq).astype(g_ref.dtype)


def _s5_main(u_gate, nb, t, wb, cb, lamq, dsk):
    Q = S5_Q
    nrow = u_gate.shape[0] // Q
    uv = u_gate.reshape(Q, nrow, 2 * E1)
    blk = lambda a, b: pl.BlockSpec((None, a, b), lambda i: (i, 0, 0))
    g, hfin = pl.pallas_call(
        functools.partial(_s5_main_kernel, nb),
        grid=(S5_NBLK,),
        in_specs=[pl.BlockSpec((Q, nrow, LANE), lambda i: (0, 0, i)),
                  blk(Q * LANE, Q * LANE), blk(Q * LANE, 2 * S5_NS), blk(2 * S5_NS, Q * LANE),
                  blk(8, 2 * S5_NS), pl.BlockSpec((1, LANE), lambda i: (0, i))],
        out_specs=[pl.BlockSpec((Q, nrow, LANE), lambda i: (0, 0, i)), blk(nb, 2 * S5_NS)],
        out_shape=[jax.ShapeDtypeStruct((Q, nrow, E1), BF16),
                   jax.ShapeDtypeStruct((S5_NBLK, nb, 2 * S5_NS), F32)],
        scratch_shapes=[pltpu.VMEM((nrow, Q * LANE), BF16), pltpu.VMEM((nrow, 2 * S5_NS), F32),
                        pltpu.VMEM((nrow, 2 * S5_NS), F32)],
        compiler_params=_cp("arbitrary"),
    )(uv, t, wb, cb, lamq, dsk)
    return g.reshape(Q * nrow, E1), hfin


def _s5_step_kernel(u_ref, h0_ref, bb_ref, cpl_ref, lam_ref, dsk_ref, g_ref, hn_ref):
    NS = S5_NS
    u = u_ref[...]
    s = jnp.dot(u.astype(BF16), bb_ref[...], preferred_element_type=F32)
    lam = lam_ref[0:1, :]
    h0 = h0_ref[...]
    n_re, n_im = _cmul(lam[:, :NS], lam[:, NS:], h0[:, :NS], h0[:, NS:])
    hn = jnp.concatenate([n_re + s[:, :NS], n_im + s[:, NS:]], axis=1)
    hn_ref[...] = hn
    y = jnp.dot(hn.astype(BF16), cpl_ref[...], preferred_element_type=F32) + dsk_ref[...] * u
    g_ref[...] = jax.nn.gelu(y)


def _s5_step(u_gate, h0, wb, cpl, lam, dsk):
    nb = u_gate.shape[0]
    blk = lambda a, b: pl.BlockSpec((None, a, b), lambda i: (i, 0, 0))
    return pl.pallas_call(
        _s5_step_kernel,
        grid=(S5_NBLK,),
        in_specs=[pl.BlockSpec((nb, LANE), lambda i: (0, i)), blk(nb, 2 * S5_NS),
                  pl.BlockSpec((None, LANE, 2 * S5_NS), lambda i: (i, S5_Q - 1, 0)),
                  blk(2 * S5_NS, LANE), blk(8, 2 * S5_NS), pl.BlockSpec((1, LANE), lambda i: (0, i))],
        out_specs=[pl.BlockSpec((nb, LANE), lambda i: (0, i)), blk(nb, 2 * S5_NS)],
        out_shape=[jax.ShapeDtypeStruct((nb, E1), F32),
                   jax.ShapeDtypeStruct((S5_NBLK, nb, 2 * S5_NS), F32)],
        compiler_params=_cp("arbitrary"),
    )(u_gate, h0, wb, cpl, lam, dsk)


def _state_to_blocks(st):
    nb = st.shape[0]
    x = st.reshape(nb, S5_NBLK, S5_NS, 2)
    return jnp.transpose(x, (1, 0, 3, 2)).reshape(S5_NBLK, nb, 2 * S5_NS)


def _blocks_to_state(h):
    nb = h.shape[1]
    x = h.reshape(S5_NBLK, nb, 2, S5_NS)
    return jnp.transpose(x, (1, 0, 3, 2)).reshape(nb, C_GROUPS, C_STATE, 2)


def kernel(x_prompt, x_sample, state_a_ssm, state_a_conv, cache_b_kv, state_c, norm_w, final_norm_w, even_w_in, a_conv_w, a_conv_b, a_dt_bias, a_A_log, a_D, a_norm_w, even_w_out, odd_w_in, c_A_re, c_A_im, c_B_re, c_B_im, c_C_re, c_C_im, c_D, c_log_dt, c_glu_w, c_glu_b, odd_w_out):
    B, L, D = x_prompt.shape
    nS = x_sample.shape[0]
    M = B * L
    Q = S5_Q

    w0 = even_w_in[0]
    s1 = A_WIDTH + A_CONV_DIM
    s2 = s1 + A_HEADS
    seg = lambda a, b: w0[:, a:b].astype(BF16)
    w_dt = jnp.pad(seg(s1, s2), ((0, 0), (0, TN - A_HEADS)))
    wcat = jnp.concatenate([seg(A_WIDTH, s1), w_dt, seg(s2, s2 + 2048), seg(s2 + 2048, s2 + 4096),
                            seg(s2 + 4096, s2 + 6144), seg(0, A_WIDTH), seg(s2 + 6144, s2 + 8192)], axis=1)
    w_out0 = even_w_out[0].astype(BF16)
    w_in1 = odd_w_in[0].astype(BF16)
    w_glu = c_glu_w[0].astype(BF16)
    w_out1 = odd_w_out[0].astype(BF16)
    nw0 = norm_w[0][None, :]
    nw1 = norm_w[1][None, :]
    fnw = final_norm_w[None, :]
    cw, cb = a_conv_w[0], a_conv_b[0][None, :]
    pad_h = lambda v: jnp.pad(v, (0, LANE - A_HEADS))[None, :]
    dtb, alog = pad_h(a_dt_bias[0]), pad_h(a_A_log[0])
    dsk_a = jnp.repeat(a_D[0], A_HEAD_DIM)[None, :]
    gnw = a_norm_w[0][None, :]
    head_expand = np.kron(np.eye(A_HEADS, dtype=np.float32), np.ones((1, A_HEAD_DIM), np.float32))
    head_expand = jnp.asarray(np.pad(head_expand, ((0, LANE - A_HEADS), (0, 0)))).astype(BF16)
    slopes = np.asarray(2.0 ** (-8.0 * np.arange(1, B_HEADS + 1) / B_HEADS), np.float32)
    slopes = jnp.asarray(np.broadcast_to(slopes[:, None, None], (B_HEADS, 8, LANE)))

    eye = jnp.eye(S5_BLK, dtype=F32)
    rows8 = lambda p: jnp.broadcast_to(p.reshape(S5_NBLK, 1, S5_NS), (S5_NBLK, 8, S5_NS))
    bexp = lambda p: jnp.einsum("kgnc,gh->kgchn", p.reshape(S5_NBLK, S5_BLK, C_STATE, C_GROUP),
                                eye).reshape(S5_NBLK, LANE, S5_NS)
    cexp = lambda p: jnp.einsum("kgcn,gh->kgnhc", p.reshape(S5_NBLK, S5_BLK, C_GROUP, C_STATE),
                                eye).reshape(S5_NBLK, S5_NS, LANE)
    ldt = jnp.repeat(c_log_dt[0], C_STATE).reshape(S5_NBLK, S5_NS)
    t_m, wb_m, cb_m, lam, lamq, cpl = _s5_prep(rows8(c_A_re[0]), rows8(c_A_im[0]), rows8(ldt),
                                                bexp(c_B_re[0]), bexp(c_B_im[0]),
                                                cexp(c_C_re[0]), cexp(c_C_im[0]))
    dsk_c = c_D[0][None, :]
    glu_b = c_glu_b[0][None, :]

    xp = x_prompt.reshape(M, D)
    a_arr, q, k, v, kv_keep = _inproj0(xp, nw0, wcat, B, L)
    y_a, ssm_p, tail_p = _ssd(a_arr, B, L, cw, cb, dtb, alog, dsk_a, gnw)
    o16, l16 = _attn_pattern(16, q, k, v, slopes)
    o4, l4 = _attn_pattern(4, q, k, v, slopes)
    y_b = _attn_final(q, k, v, slopes, o16, l16, o4, l4, a_arr)
    hp1 = _outproj0(y_a, y_b, w_out0, xp, 1024)

    nrow = M // Q
    hp1_view = hp1.reshape(nrow, Q * D)
    u_gate = _mm_norm(hp1_view, nw1, w_in1, nrow, Q, lambda i: (0, i))
    g, hfin_p = _s5_main(u_gate, B, t_m, wb_m, cb_m, lamq, dsk_c)
    y1 = _glu(g, w_glu, glu_b, u_gate, 1024, BF16)
    tm1 = 512
    per = nrow // tm1
    y_prompt = _outproj1(y1, w_out1, hp1_view, fnw, tm1, M // tm1,
                         lambda i: (i % per, i // per), M).reshape(B, L, D)

    xs = x_sample.reshape(nS, D)
    proj = _mm_norm(xs, nw0, wcat, nS, 1, lambda i: (0, 0))
    col = lambda a, b: proj[:, a:b]
    hv = lambda a: col(a, a + 2048).reshape(nS, B_HEADS, 1, B_HEAD_DIM)
    y_as, ssm_s, conv_s = _ssd_step(col(0, 3072)[:, None, :], state_a_conv[0],
                                    col(3072, 3200)[:, None, :], col(9728, 11776)[:, None, :],
                                    state_a_ssm[0].reshape(nS, A_WIDTH, A_STATE),
                                    cw, cb, dtb, alog, head_expand, dsk_a, gnw)
    cache = cache_b_kv[0].reshape(nS, B_MAX_WINDOW, 2 * B_HEADS * B_HEAD_DIM)
    new_cache, y_bs = _attn_step(hv(3584), hv(5632), hv(7680), hv(11776), cache, slopes)
    hs1 = _outproj0(y_as.reshape(nS, A_WIDTH), y_bs.reshape(nS, A_WIDTH), w_out0, xs, nS)
    u_gate_s = _mm_norm(hs1, nw1, w_in1, nS, 1, lambda i: (0, 0))
    g_s, hnew_s = _s5_step(u_gate_s, _state_to_blocks(state_c[0]), wb_m, cpl, lam, dsk_c)
    y1_s = _glu(g_s, w_glu, glu_b, u_gate_s, nS, F32)
    y_sample = _outproj1(y1_s, w_out1, hs1, fnw, nS, 1, lambda i: (0, 0), nS).reshape(nS, 1, D)

    keep = kv_keep.shape[1]
    return (y_prompt, y_sample,
            ssm_p.reshape(1, B, A_HEADS, A_HEAD_DIM, A_STATE),
            tail_p[:, 5:8, :][None],
            kv_keep.reshape(1, B, keep, 2, B_HEADS, B_HEAD_DIM),
            _blocks_to_state(hfin_p)[None],
            ssm_s.reshape(1, nS, A_HEADS, A_HEAD_DIM, A_STATE),
            conv_s[None],
            new_cache.reshape(1, nS, B_MAX_WINDOW, 2, B_HEADS, B_HEAD_DIM),
            _blocks_to_state(hnew_s)[None])
```

```python
import functools

import numpy as np
import jax
import jax.numpy as jnp
from jax import lax
from jax.experimental import pallas as pl
from jax.experimental.pallas import tpu as pltpu

F32 = jnp.float32
BF16 = jnp.bfloat16

D_MODEL = 2048
EPS = 1e-5
A_WIDTH = 2048
A_HEAD_DIM = 64
A_HEADS = 32
A_GROUPS = 4
A_STATE = 128
A_CONV_W = 4
A_CONV_DIM = 3072
SSD_CHUNK = 128
B_HEADS = 16
B_HEAD_DIM = 128
B_PATTERNS = ((128, 1), (512, 4), (2048, 16))
B_MAX_WINDOW = 2048
C_GROUP = 16
C_GROUPS = 256
C_STATE = 64
E1 = 4096

LANE = 128
S5_Q = 8
S5_BLK = 8
S5_NBLK = C_GROUPS // S5_BLK
S5_NS = S5_BLK * C_STATE
TN = 512
ATT_BLOCK = 2048
CACHE_ROWS = 128
NEG = -1e30
ATT_SCALE = B_HEAD_DIM ** -0.5
VMEM_LIMIT = 56 * 1024 * 1024

NT_DIMS = (((1,), (1,)), ((), ()))


def _cp(*sem):
    return pltpu.CompilerParams(dimension_semantics=sem, vmem_limit_bytes=VMEM_LIMIT)


def _silu(x):
    return x * jax.nn.sigmoid(x)


def _softplus(x):
    return jnp.maximum(x, 0.0) + jnp.log1p(jnp.exp(-jnp.abs(x)))


def _split3(a):
    hi = a.astype(BF16)
    r = a - hi.astype(F32)
    mid = r.astype(BF16)
    lo = (r - mid.astype(F32)).astype(BF16)
    return hi, mid, lo


def _dot3_left(m, a):
    hi, mid, lo = _split3(a)
    d = lambda p: jnp.dot(m, p, preferred_element_type=F32)
    return (d(lo) + d(mid)) + d(hi)


def _dot3_right(a, m):
    hi, mid, lo = _split3(a)
    d = lambda p: jnp.dot(p, m, preferred_element_type=F32)
    return (d(lo) + d(mid)) + d(hi)


def _rms(x, w):
    ms = jnp.mean(x * x, axis=-1, keepdims=True)
    return (x * lax.rsqrt(ms + EPS)) * w


def _inproj0_kernel(x_ref, nw_ref, w_ref, a_ref, q_ref, k_ref, v_ref, kv_ref, xn_ref):
    j = pl.program_id(1)

    @pl.when(j == 0)
    def _():
        xn_ref[...] = _rms(x_ref[...], nw_ref[...]).astype(BF16)

    acc = jnp.dot(xn_ref[...], w_ref[...], preferred_element_type=F32)

    def heads(ref):
        for hh in range(TN // LANE):
            ref[hh] = acc[:, hh * LANE:(hh + 1) * LANE]

    @pl.when((j < 7) | (j >= 19))
    def _():
        a_ref[...] = acc

    @pl.when((j >= 7) & (j < 11))
    def _():
        heads(q_ref)

    @pl.when((j >= 11) & (j < 15))
    def _():
        heads(k_ref)
        kv_ref[...] = acc

    @pl.when((j >= 15) & (j < 19))
    def _():
        heads(v_ref)
        kv_ref[...] = acc


def _inproj0(x2d, nw, wcat, B, L):
    M = B * L
    tm = 1024
    rpb = L // tm
    keep = min(B_MAX_WINDOW, L)
    nskip = (L - keep) // tm
    nj = wcat.shape[1] // TN

    def a_map(i, j):
        return (i, jnp.where(j < 6, j, jnp.where(j < 19, 14, j - 13)))

    def head_map(j0):
        return lambda i, j: (i // rpb, jnp.clip(j - j0, 0, 3), i % rpb, 0)

    def kv_map(i, j):
        ri = i % rpb
        return (i // rpb, jnp.maximum(ri - nskip, 0), jnp.where(ri < nskip, 0, jnp.clip(j - 11, 0, 7)))

    hd = jax.ShapeDtypeStruct((B, B_HEADS, L, B_HEAD_DIM), F32)
    return pl.pallas_call(
        _inproj0_kernel,
        grid=(M // tm, nj),
        in_specs=[pl.BlockSpec((tm, D_MODEL), lambda i, j: (i, 0)),
                  pl.BlockSpec((1, D_MODEL), lambda i, j: (0, 0)),
                  pl.BlockSpec((D_MODEL, TN), lambda i, j: (0, j))],
        out_specs=[pl.BlockSpec((tm, TN), a_map),
                   pl.BlockSpec((None, TN // LANE, tm, LANE), head_map(7)),
                   pl.BlockSpec((None, TN // LANE, tm, LANE), head_map(11)),
                   pl.BlockSpec((None, TN // LANE, tm, LANE), head_map(15)),
                   pl.BlockSpec((None, tm, TN), kv_map)],
        out_shape=[jax.ShapeDtypeStruct((M, 15 * TN), F32), hd, hd, hd,
                   jax.ShapeDtypeStruct((B, keep, 8 * TN), F32)],
        scratch_shapes=[pltpu.VMEM((tm, D_MODEL), BF16)],
        compiler_params=_cp("arbitrary", "arbitrary"),
    )(x2d, nw, wcat)


def _mm_norm_kernel(x_ref, nw_ref, w_ref, o_ref, xn_ref):
    @pl.when(pl.program_id(1) == 0)
    def _():
        xn_ref[...] = _rms(x_ref[...], nw_ref[...]).astype(BF16)

    o_ref[...] = jnp.dot(xn_ref[...], w_ref[...], preferred_element_type=F32)


def _mm_norm(x, nw, w, tm, tn=TN):
    M = x.shape[0]
    K, N = w.shape
    return pl.pallas_call(
        _mm_norm_kernel,
        grid=(M // tm, N // tn),
        in_specs=[pl.BlockSpec((tm, K), lambda i, j: (i, 0)),
                  pl.BlockSpec((1, K), lambda i, j: (0, 0)),
                  pl.BlockSpec((K, tn), lambda i, j: (0, j))],
        out_specs=pl.BlockSpec((tm, tn), lambda i, j: (i, j)),
        out_shape=jax.ShapeDtypeStruct((M, N), F32),
        scratch_shapes=[pltpu.VMEM((tm, K), BF16)],
        compiler_params=_cp("arbitrary", "arbitrary"),
    )(x, nw, w)


def _outproj0_kernel(ya_ref, yb_ref, wa_ref, wb_ref, res_ref, o_ref):
    o_ref[...] = (res_ref[...]
                  + jnp.dot(ya_ref[...].astype(BF16), wa_ref[...], preferred_element_type=F32)
                  + jnp.dot(yb_ref[...].astype(BF16), wb_ref[...], preferred_element_type=F32))


def _outproj0(ya, yb, w, res, tm):
    M = ya.shape[0]
    K = ya.shape[1]
    return pl.pallas_call(
        _outproj0_kernel,
        grid=(M // tm, D_MODEL // TN),
        in_specs=[pl.BlockSpec((tm, K), lambda i, j: (i, 0)),
                  pl.BlockSpec((tm, K), lambda i, j: (i, 0)),
                  pl.BlockSpec((K, TN), lambda i, j: (0, j)),
                  pl.BlockSpec((K, TN), lambda i, j: (1, j)),
                  pl.BlockSpec((tm, TN), lambda i, j: (i, j))],
        out_specs=pl.BlockSpec((tm, TN), lambda i, j: (i, j)),
        out_shape=jax.ShapeDtypeStruct((M, D_MODEL), F32),
        compiler_params=_cp("arbitrary", "arbitrary"),
    )(ya, yb, w, w, res)


def _glu_kernel(g_ref, wa_ref, wb_ref, ba_ref, bb_ref, gate_ref, o_ref):
    g = g_ref[...].astype(BF16)
    ga = jnp.dot(g, wa_ref[...], preferred_element_type=F32) + ba_ref[...]
    gb = jnp.dot(g, wb_ref[...], preferred_element_type=F32) + bb_ref[...]
    o_ref[...] = (ga * jax.nn.sigmoid(gb) * _silu(gate_ref[...])).astype(o_ref.dtype)


def _glu(g, w, bias, u_gate, tm, out_dtype):
    M = g.shape[0]
    nb = E1 // TN
    return pl.pallas_call(
        _glu_kernel,
        grid=(M // tm, nb),
        in_specs=[pl.BlockSpec((tm, E1), lambda i, j: (i, 0)),
                  pl.BlockSpec((E1, TN), lambda i, j: (0, j)),
                  pl.BlockSpec((E1, TN), lambda i, j: (0, j + nb)),
                  pl.BlockSpec((1, TN), lambda i, j: (0, j)),
                  pl.BlockSpec((1, TN), lambda i, j: (0, j + nb)),
                  pl.BlockSpec((tm, TN), lambda i, j: (i, j + nb))],
        out_specs=pl.BlockSpec((tm, TN), lambda i, j: (i, j)),
        out_shape=jax.ShapeDtypeStruct((M, E1), out_dtype),
        compiler_params=_cp("arbitrary", "arbitrary"),
    )(g, w, w, bias, bias, u_gate)


def _outproj1_kernel(y_ref, w_ref, res_ref, fnw_ref, o_ref):
    j = pl.program_id(1)
    nj = D_MODEL // TN
    h = res_ref[...] + jnp.dot(y_ref[...].astype(BF16), w_ref[...], preferred_element_type=F32)
    for jj in range(nj):
        @pl.when(j == jj)
        def _(jj=jj):
            o_ref[:, jj * TN:(jj + 1) * TN] = h

    @pl.when(j == nj - 1)
    def _():
        o_ref[...] = _rms(o_ref[...], fnw_ref[...])


def _outproj1(y, w, res, fnw, tm):
    M, K = y.shape
    return pl.pallas_call(
        _outproj1_kernel,
        grid=(M // tm, D_MODEL // TN),
        in_specs=[pl.BlockSpec((tm, K), lambda i, j: (i, 0)),
                  pl.BlockSpec((K, TN), lambda i, j: (0, j)),
                  pl.BlockSpec((tm, TN), lambda i, j: (i, j)),
                  pl.BlockSpec((1, D_MODEL), lambda i, j: (0, 0))],
        out_specs=pl.BlockSpec((tm, D_MODEL), lambda i, j: (i, 0)),
        out_shape=jax.ShapeDtypeStruct((M, D_MODEL), F32),
        compiler_params=_cp("arbitrary", "arbitrary"),
    )(y, w, res, fnw)


def _ssd_kernel(xbc_ref, dt_ref, z0_ref, z1_ref, cw_ref, cb_ref, dtb_ref, alog_ref, dsk_ref, gnw_ref,
                y_ref, st_ref, tail_ref, h_scr, tail_scr, y_scr):
    c = pl.program_id(1)
    Q = SSD_CHUNK

    @pl.when(c == 0)
    def _():
        h_scr[...] = jnp.zeros_like(h_scr)
        tail_scr[...] = jnp.zeros_like(tail_scr)

    x_raw = xbc_ref[...]
    tail8 = tail_scr[...]
    row8 = lax.broadcasted_iota(jnp.int32, (8, 1), 0)
    conv = cb_ref[...] + cw_ref[3:4, :] * x_raw
    for k in (1, 2, 3):
        xr = pltpu.roll(x_raw, k, 0)
        first = jnp.where(row8 < k, pltpu.roll(tail8, k, 0), xr[0:8, :])
        conv = conv + cw_ref[3 - k:4 - k, :] * jnp.concatenate([first, xr[8:, :]], axis=0)
    tail_scr[...] = x_raw[Q - 8:Q, :]
    xc = _silu(conv)

    dt = _softplus(dt_ref[...] + dtb_ref[...])
    a = dt * (-jnp.exp(alog_ref[...]))
    qi = lax.broadcasted_iota(jnp.int32, (Q, Q), 0)
    si = lax.broadcasted_iota(jnp.int32, (Q, Q), 1)
    tri = qi >= si
    a_cum = _dot3_left(tri.astype(BF16), a)
    a_cum_t = a_cum.T
    lo_half = si < A_HEAD_DIM

    for g in range(A_GROUPS):
        bg = xc[:, A_WIDTH + g * A_STATE:A_WIDTH + (g + 1) * A_STATE].astype(BF16)
        cg = xc[:, A_WIDTH + (A_GROUPS + g) * A_STATE:A_WIDTH + (A_GROUPS + g + 1) * A_STATE].astype(BF16)
        cbm = lax.dot_general(cg, bg, NT_DIMS, preferred_element_type=F32)
        hprev = h_scr[g * 512:(g + 1) * 512, :]
        yoff = lax.dot_general(cg, hprev.astype(BF16), NT_DIMS, preferred_element_type=F32)
        for jj in range(4):
            p = g * 4 + jj
            h0, h1 = 2 * p, 2 * p + 1
            dtp = jnp.where(lo_half, dt[:, h0:h0 + 1], dt[:, h1:h1 + 1])
            ap = jnp.where(lo_half, a_cum[:, h0:h0 + 1], a_cum[:, h1:h1 + 1])
            x_p = xc[:, p * LANE:(p + 1) * LANE]
            xdt = x_p * dtp
            g0 = cbm * jnp.exp(jnp.where(tri, a_cum[:, h0:h0 + 1] - a_cum_t[h0:h0 + 1, :], NEG))
            g1 = cbm * jnp.exp(jnp.where(tri, a_cum[:, h1:h1 + 1] - a_cum_t[h1:h1 + 1, :], NEG))
            lhs = jnp.concatenate([g0, g1], axis=1).astype(BF16)
            rhs = jnp.concatenate([jnp.where(lo_half, xdt, 0.0), jnp.where(lo_half, 0.0, xdt)],
                                  axis=0).astype(BF16)
            ydiag = jnp.dot(lhs, rhs, preferred_element_type=F32)
            y_scr[:, p * LANE:(p + 1) * LANE] = (
                ydiag + yoff[:, jj * LANE:(jj + 1) * LANE] * jnp.exp(ap)
                + dsk_ref[:, p * LANE:(p + 1) * LANE] * x_p)
            xd = xdt * jnp.exp(ap[Q - 1:Q, :] - ap)
            st = jnp.dot(xd.T.astype(BF16), bg, preferred_element_type=F32)
            cd = jnp.exp(jnp.where(qi < A_HEAD_DIM, a_cum_t[h0:h0 + 1, Q - 1:Q],
                                   a_cum_t[h1:h1 + 1, Q - 1:Q]))
            h_scr[p * LANE:(p + 1) * LANE, :] = cd * hprev[jj * LANE:(jj + 1) * LANE, :] + st

    z = jnp.concatenate([z0_ref[...], z1_ref[...]], axis=1)
    y_ref[...] = _rms(y_scr[...] * _silu(z), gnw_ref[...]).astype(y_ref.dtype)

    @pl.when(c == pl.num_programs(1) - 1)
    def _():
        st_ref[...] = h_scr[...]
        tail_ref[...] = x_raw[Q - 8:Q, :]


def _ssd(a_arr, B, L, cw, cb, dtb, alog, dsk, gnw):
    nc = L // SSD_CHUNK
    Q = SSD_CHUNK
    row = lambda b, c: b * nc + c
    full = lambda shp: pl.BlockSpec(shp, lambda b, c: (0, 0))
    return pl.pallas_call(
        _ssd_kernel,
        grid=(B, nc),
        in_specs=[pl.BlockSpec((Q, A_CONV_DIM), lambda b, c: (row(b, c), 0)),
                  pl.BlockSpec((Q, LANE), lambda b, c: (row(b, c), 14 * TN // LANE)),
                  pl.BlockSpec((Q, A_WIDTH // 2), lambda b, c: (row(b, c), 3)),
                  pl.BlockSpec((Q, A_WIDTH // 2), lambda b, c: (row(b, c), 4)),
                  full((A_CONV_W, A_CONV_DIM)), full((1, A_CONV_DIM)), full((1, LANE)),
                  full((1, LANE)), full((1, A_WIDTH)), full((1, A_WIDTH))],
        out_specs=[pl.BlockSpec((Q, A_WIDTH), lambda b, c: (row(b, c), 0)),
                   pl.BlockSpec((None, A_WIDTH, A_STATE), lambda b, c: (b, 0, 0)),
                   pl.BlockSpec((None, 8, A_CONV_DIM), lambda b, c: (b, 0, 0))],
        out_shape=[jax.ShapeDtypeStruct((B * L, A_WIDTH), BF16),
                   jax.ShapeDtypeStruct((B, A_WIDTH, A_STATE), F32),
                   jax.ShapeDtypeStruct((B, 8, A_CONV_DIM), F32)],
        scratch_shapes=[pltpu.VMEM((A_WIDTH, A_STATE), F32), pltpu.VMEM((8, A_CONV_DIM), F32),
                        pltpu.VMEM((Q, A_WIDTH), F32)],
        compiler_params=_cp("arbitrary", "arbitrary"),
    )(a_arr, a_arr, a_arr, a_arr, cw, cb, dtb, alog, dsk, gnw)


def _ssd_step_kernel(xbc_ref, cs_ref, dt_ref, z_ref, h0_ref, cw_ref, cb_ref, dtb_ref, alog_ref,
                     e_ref, dsk_ref, gnw_ref, y_ref, hn_ref, nc_ref):
    x_raw = xbc_ref[...]
    cs = cs_ref[...]
    conv = (cb_ref[...] + cw_ref[0:1, :] * cs[0:1, :] + cw_ref[1:2, :] * cs[1:2, :]
            + cw_ref[2:3, :] * cs[2:3, :] + cw_ref[3:4, :] * x_raw)
    xc = _silu(conv)
    nc_ref[0:1, :] = cs[1:2, :]
    nc_ref[1:2, :] = cs[2:3, :]
    nc_ref[2:3, :] = x_raw

    dt = _softplus(dt_ref[...] + dtb_ref[...])
    da = dt * (-jnp.exp(alog_ref[...]))
    e = e_ref[...]
    dt_e = _dot3_right(jnp.broadcast_to(dt, (8, LANE)), e)[0:1, :]
    da_e = _dot3_right(jnp.broadcast_to(da, (8, LANE)), e)[0:1, :]
    xs = xc[:, :A_WIDTH]
    sub = lax.broadcasted_iota(jnp.int32, (8, 1), 0)
    rows = jnp.where(sub == 0, xs * dt_e, jnp.where(sub == 1, jnp.exp(da_e), 0.0))
    cols = jnp.concatenate([rows, jnp.zeros((LANE - 8, A_WIDTH), F32)], axis=0).T
    xdt_col = cols[:, 0:1]
    dec_col = cols[:, 1:2]
    ys = []
    for g in range(A_GROUPS):
        bg = xc[:, A_WIDTH + g * A_STATE:A_WIDTH + (g + 1) * A_STATE]
        cg = xc[:, A_WIDTH + (A_GROUPS + g) * A_STATE:A_WIDTH + (A_GROUPS + g + 1) * A_STATE]
        sl = slice(g * 512, (g + 1) * 512)
        hn = dec_col[sl, :] * h0_ref[sl, :] + xdt_col[sl, :] * bg
        hn_ref[sl, :] = hn
        c8 = jnp.broadcast_to(cg, (8, A_STATE)).astype(BF16)
        ys.append(lax.dot_general(c8, hn.astype(BF16), NT_DIMS, preferred_element_type=F32)[0:1, :])
    y = jnp.concatenate(ys, axis=1) + dsk_ref[...] * xs
    y_ref[...] = _rms(y * _silu(z_ref[...]), gnw_ref[...])


def _ssd_step(xbc, cs, dt, z, h0, cw, cb, dtb, alog, e, dsk, gnw):
    nb = xbc.shape[0]
    per_b = lambda n: pl.BlockSpec((None, 1, n), lambda b: (b, 0, 0))
    full = lambda shp: pl.BlockSpec(shp, lambda b: (0, 0))
    return pl.pallas_call(
        _ssd_step_kernel,
        grid=(nb,),
        in_specs=[per_b(A_CONV_DIM), pl.BlockSpec((None, 3, A_CONV_DIM), lambda b: (b, 0, 0)),
                  per_b(LANE), per_b(A_WIDTH),
                  pl.BlockSpec((None, A_WIDTH, A_STATE), lambda b: (b, 0, 0)),
                  full((A_CONV_W, A_CONV_DIM)), full((1, A_CONV_DIM)), full((1, LANE)),
                  full((1, LANE)), full((LANE, A_WIDTH)), full((1, A_WIDTH)), full((1, A_WIDTH))],
        out_specs=[per_b(A_WIDTH),
                   pl.BlockSpec((None, A_WIDTH, A_STATE), lambda b: (b, 0, 0)),
                   pl.BlockSpec((None, 3, A_CONV_DIM), lambda b: (b, 0, 0))],
        out_shape=[jax.ShapeDtypeStruct((nb, 1, A_WIDTH), F32),
                   jax.ShapeDtypeStruct((nb, A_WIDTH, A_STATE), F32),
                   jax.ShapeDtypeStruct((nb, 3, A_CONV_DIM), F32)],
        compiler_params=_cp("arbitrary"),
    )(xbc, cs, dt, z, h0, cw, cb, dtb, alog, e, dsk, gnw)


def _bias_tile(slope, dil):
    i = lax.broadcasted_iota(jnp.int32, (LANE, 2 * LANE), 0)
    c = lax.broadcasted_iota(jnp.int32, (LANE, 2 * LANE), 1)
    j = LANE + i - c
    valid = (j >= 0) & (j <= LANE)
    return jnp.where(valid, -(slope * (dil * j).astype(F32)), NEG), c


def _attn_tile(q, kcat, vcat, bias):
    s = lax.dot_general(q, kcat, NT_DIMS, preferred_element_type=F32) * ATT_SCALE + bias
    m = jnp.max(s, axis=1, keepdims=True)
    p = jnp.exp(s - m)
    l = jnp.sum(p, axis=1, keepdims=True)
    acc = jnp.dot(p.astype(BF16), vcat, preferred_element_type=F32)
    return m, l, acc


def _rows(start, stride):
    return pl.ds(start, LANE, stride=stride) if stride > 1 else pl.ds(start, LANE)


def _attn_kernel(slope_ref, q_ref, k_ref, v_ref, kp_ref, vp_ref, gate_ref, y_ref,
                 m_scr, l_scr, acc_scr):
    first_block = pl.program_id(2) == 0
    slope = slope_ref[0:1, 0:1]
    for _, dil in B_PATTERNS:
        bias_full, c = _bias_tile(slope, dil)
        bias0 = jnp.where((c >= LANE) | jnp.logical_not(first_block), bias_full, NEG)
        nsub = ATT_BLOCK // dil // LANE
        for r in range(dil):
            kc = vc = None
            for sub in range(nsub):
                rows = _rows(sub * LANE * dil + r, dil)
                if sub == 0:
                    prow = _rows((nsub - 1) * LANE * dil + r, dil)
                    kp, vp, bias = kp_ref[prow, :], vp_ref[prow, :], bias0
                else:
                    kp, vp, bias = kc, vc, bias_full
                kc, vc = k_ref[rows, :], v_ref[rows, :]
                kcat = jnp.concatenate([kp, kc], axis=0).astype(BF16)
                vcat = jnp.concatenate([vp, vc], axis=0).astype(BF16)
                m, l, acc = _attn_tile(q_ref[rows, :].astype(BF16), kcat, vcat, bias)
                if dil == 1:
                    m_scr[rows, :] = jnp.broadcast_to(m, (LANE, LANE))
                    l_scr[rows, :] = jnp.broadcast_to(l, (LANE, LANE))
                    acc_scr[rows, :] = acc
                else:
                    m_old = m_scr[rows, :]
                    m_new = jnp.maximum(m_old, m)
                    a_old = jnp.exp(m_old - m_new)
                    a_new = jnp.exp(m - m_new)
                    m_scr[rows, :] = m_new
                    l_scr[rows, :] = a_old * l_scr[rows, :] + a_new * l
                    acc_scr[rows, :] = a_old * acc_scr[rows, :] + a_new * acc
    for sub in range(ATT_BLOCK // LANE):
        rs = pl.ds(sub * LANE, LANE)
        o = acc_scr[rs, :] / l_scr[rs, :]
        y_ref[rs, :] = (o * _silu(gate_ref[rs, :])).astype(y_ref.dtype)


def _attn(q, k, v, slopes, a_arr):
    B, H, L, dh = q.shape
    R = ATT_BLOCK
    nblk = L // R
    cur = pl.BlockSpec((None, None, R, dh), lambda b, h, i: (b, h, i, 0))
    prev = pl.BlockSpec((None, None, R, dh), lambda b, h, i: (b, h, jnp.maximum(i - 1, 0), 0))
    slope = pl.BlockSpec((None, 8, LANE), lambda b, h, i: (h, 0, 0))
    gate = pl.BlockSpec((R, LANE), lambda b, h, i: (b * nblk + i, 10 * TN // LANE + h))
    return pl.pallas_call(
        _attn_kernel,
        grid=(B, H, nblk),
        in_specs=[slope, cur, cur, cur, prev, prev, gate],
        out_specs=pl.BlockSpec((R, LANE), lambda b, h, i: (b * nblk + i, h)),
        out_shape=jax.ShapeDtypeStruct((B * L, H * dh), BF16),
        scratch_shapes=[pltpu.VMEM((R, LANE), F32)] * 3,
        compiler_params=_cp("arbitrary", "arbitrary", "arbitrary"),
    )(slopes, q, k, v, k, v, a_arr)


def _cache_shift_kernel(x_ref, nxt_ref, kvn_ref, o_ref):
    R = x_ref.shape[0]
    o_ref[pl.ds(0, R - 1)] = x_ref[pl.ds(1, R - 1)]
    last = pl.program_id(1) == pl.num_programs(1) - 1
    o_ref[R - 1] = jnp.where(last, kvn_ref[...], nxt_ref[0])


def _cache_shift(cache6, kvn):
    nb, W = cache6.shape[1], cache6.shape[2]
    R = CACHE_ROWS
    tail = (2, B_HEADS, B_HEAD_DIM)
    return pl.pallas_call(
        _cache_shift_kernel,
        grid=(nb, W // R),
        in_specs=[pl.BlockSpec((None, None, R) + tail, lambda b, i: (0, b, i, 0, 0, 0)),
                  pl.BlockSpec((None, None, 1) + tail,
                               lambda b, i: (0, b, jnp.minimum((i + 1) * R, W - 1), 0, 0, 0)),
                  pl.BlockSpec((None,) + tail, lambda b, i: (b, 0, 0, 0))],
        out_specs=pl.BlockSpec((None, None, R) + tail, lambda b, i: (0, b, i, 0, 0, 0)),
        out_shape=jax.ShapeDtypeStruct(cache6.shape, F32),
        compiler_params=_cp("arbitrary", "arbitrary"),
    )(cache6, cache6, kvn)


def _attn_step_kernel(bias_ref, q_ref, kn_ref, vn_ref, gate_ref, c1_ref, c4_ref, c16_ref, y_ref):
    q, kn, vn = q_ref[...], kn_ref[...], vn_ref[...]
    s_self = jnp.sum(q * kn, axis=-1, keepdims=True) * ATT_SCALE
    blocks = (c1_ref, c4_ref, c16_ref)
    scores = []
    for p, c_ref in enumerate(blocks):
        kblk = c_ref[:, 0]
        scores.append(jnp.sum(kblk * q[None], axis=-1, keepdims=True) * ATT_SCALE - bias_ref[p])
    m = s_self
    for s in scores:
        m = jnp.maximum(m, jnp.max(s, axis=0))
    p_self = float(len(B_PATTERNS)) * jnp.exp(s_self - m)
    l = p_self
    o = p_self * vn
    for s, c_ref in zip(scores, blocks):
        p = jnp.exp(s - m[None])
        l = l + jnp.sum(p, axis=0)
        o = o + jnp.sum(p * c_ref[:, 1], axis=0)
    y_ref[...] = (o / l) * _silu(gate_ref[...])


def _attn_step(q, kn, vn, gate, cache6, bias):
    nb = q.shape[0]
    W = B_MAX_WINDOW
    hspec = pl.BlockSpec((None, B_HEADS, B_HEAD_DIM), lambda b: (b, 0, 0))
    tail = (2, B_HEADS, B_HEAD_DIM)
    views, specs = [], []
    for _, dil in B_PATTERNS:
        if dil == 1:
            views.append(cache6)
            specs.append(pl.BlockSpec((None, None, LANE) + tail,
                                      lambda b: (0, b, W // LANE - 1, 0, 0, 0)))
        else:
            views.append(cache6.reshape((1, nb, W // dil, dil) + tail))
            specs.append(pl.BlockSpec((None, None, LANE, None) + tail,
                                      lambda b, dil=dil: (0, b, W // dil // LANE - 1, 0, 0, 0, 0)))
    return pl.pallas_call(
        _attn_step_kernel,
        grid=(nb,),
        in_specs=[pl.BlockSpec(bias.shape, lambda b: (0, 0, 0, 0)), hspec, hspec, hspec, hspec]
                 + specs,
        out_specs=hspec,
        out_shape=jax.ShapeDtypeStruct((nb, B_HEADS, B_HEAD_DIM), F32),
        compiler_params=_cp("arbitrary"),
    )(bias, q, kn, vn, gate, *views)


def _cmul(ar, ai, br, bi):
    return ar * br - ai * bi, ar * bi + ai * br


def _s5_prep_kernel(are_ref, aim_ref, ldt_ref, bre_ref, bim_ref, ctre_ref, ctim_ref,
                    t_ref, wb_ref, cb_ref, lam_ref, lamq_ref, cpl_ref):
    Q = S5_Q
    are, aim = are_ref[...], aim_ref[...]
    dt = jnp.exp(ldt_ref[...])
    mag = jnp.exp(dt * are)
    l_re = mag * jnp.cos(dt * aim)
    l_im = mag * jnp.sin(dt * aim)
    den = are * are + aim * aim
    f_re = ((l_re - 1.0) * are + l_im * aim) / den
    f_im = (l_im * are - (l_re - 1.0) * aim) / den
    lr, li = l_re[0:1, :], l_im[0:1, :]
    bb_re, bb_im = _cmul(f_re[0:1, :], f_im[0:1, :], bre_ref[...], bim_ref[...])
    ctre, ctim = ctre_ref[...], ctim_ref[...]
    cplain = jnp.concatenate([ctre, -ctim], axis=0).astype(BF16)
    cpl_ref[...] = cplain
    stk = jnp.concatenate([l_re, l_im, jnp.zeros((LANE - 16, S5_NS), F32)], axis=0).T
    lc_re, lc_im = stk[:, 0:1], stk[:, 8:9]
    p_re, p_im = jnp.ones((1, S5_NS), F32), jnp.zeros((1, S5_NS), F32)
    pc_re, pc_im = lc_re, lc_im
    dts = []
    for tau in range(Q):
        e_re, e_im = _cmul(bb_re, bb_im, p_re, p_im)
        wp = jnp.concatenate([e_re, e_im], axis=1).astype(BF16)
        s = Q - 1 - tau
        wb_ref[s * LANE:(s + 1) * LANE, :] = wp
        dts.append(jnp.dot(wp, cplain, preferred_element_type=F32).astype(BF16))
        c_re, c_im = _cmul(ctre, ctim, pc_re, pc_im)
        cb_ref[:, tau * LANE:(tau + 1) * LANE] = jnp.concatenate([c_re, -c_im], axis=0).astype(BF16)
        p_re, p_im = _cmul(p_re, p_im, lr, li)
        pc_re, pc_im = _cmul(pc_re, pc_im, lc_re, lc_im)
    zero = jnp.zeros((LANE, LANE), BF16)
    for s in range(Q):
        for q in range(Q):
            t_ref[s * LANE:(s + 1) * LANE, q * LANE:(q + 1) * LANE] = dts[q - s] if q >= s else zero
    lam_ref[...] = jnp.concatenate([l_re, l_im], axis=1)
    lamq_ref[...] = jnp.broadcast_to(jnp.concatenate([p_re, p_im], axis=1), (8, 2 * S5_NS))


def _s5_prep(are, aim, ldt, bre, bim, ctre, ctim):
    Q = S5_Q
    row = pl.BlockSpec((None, 8, S5_NS), lambda i: (i, 0, 0))
    blk = lambda a, b: pl.BlockSpec((None, a, b), lambda i: (i, 0, 0))
    return pl.pallas_call(
        _s5_prep_kernel,
        grid=(S5_NBLK,),
        in_specs=[row, row, row, blk(LANE, S5_NS), blk(LANE, S5_NS), blk(S5_NS, LANE), blk(S5_NS, LANE)],
        out_specs=[blk(Q * LANE, Q * LANE), blk(Q * LANE, 2 * S5_NS), blk(2 * S5_NS, Q * LANE),
                   blk(8, 2 * S5_NS), blk(8, 2 * S5_NS), blk(2 * S5_NS, LANE)],
        out_shape=[jax.ShapeDtypeStruct((S5_NBLK, Q * LANE, Q * LANE), BF16),
                   jax.ShapeDtypeStruct((S5_NBLK, Q * LANE, 2 * S5_NS), BF16),
                   jax.ShapeDtypeStruct((S5_NBLK, 2 * S5_NS, Q * LANE), BF16),
                   jax.ShapeDtypeStruct((S5_NBLK, 8, 2 * S5_NS), F32),
                   jax.ShapeDtypeStruct((S5_NBLK, 8, 2 * S5_NS), F32),
                   jax.ShapeDtypeStruct((S5_NBLK, 2 * S5_NS, LANE), BF16)],
        compiler_params=_cp("arbitrary"),
    )(are, aim, ldt, bre, bim, ctre, ctim)


def _s5_main_kernel(nb, u_ref, t_ref, wb_ref, cb_ref, lamq_ref, dsk_ref, g_ref, hfin_ref,
                    lhs_scr, s_scr, hp_scr):
    Q = S5_Q
    NS = S5_NS
    nrow = u_ref.shape[0] // Q
    nch = nrow // nb
    for s in range(Q):
        lhs_scr[:, s * LANE:(s + 1) * LANE] = u_ref[pl.ds(s, nrow, stride=Q), :].astype(BF16)
    lhs = lhs_scr[...]
    s_scr[...] = jnp.dot(lhs, wb_ref[...], preferred_element_type=F32)
    lq = lamq_ref[0:1, :]
    lq_re, lq_im = lq[:, :NS], lq[:, NS:]
    sub = lax.broadcasted_iota(jnp.int32, (8, 1), 0)

    def tile_body(t, carry):
        base = pl.multiple_of(t * 8, 8)
        new = []
        for b in range(nb):
            h_re, h_im = carry[2 * b], carry[2 * b + 1]
            st = s_scr[pl.ds(b * nch + base, 8), :]
            hp = jnp.zeros((8, 2 * NS), F32)
            for r in range(8):
                hp = jnp.where(sub == r, jnp.concatenate([h_re, h_im], axis=1), hp)
                n_re, n_im = _cmul(lq_re, lq_im, h_re, h_im)
                h_re, h_im = n_re + st[r:r + 1, :NS], n_im + st[r:r + 1, NS:]
            hp_scr[pl.ds(b * nch + base, 8), :] = hp
            new += [h_re, h_im]
        return tuple(new)

    zero = jnp.zeros((1, NS), F32)
    fin = lax.fori_loop(0, nch // 8, tile_body, (zero,) * (2 * nb))
    for b in range(nb):
        hfin_ref[b:b + 1, :] = jnp.concatenate([fin[2 * b], fin[2 * b + 1]], axis=1)

    y = (jnp.dot(lhs, t_ref[...], preferred_element_type=F32)
         + jnp.dot(hp_scr[...].astype(BF16), cb_ref[...], preferred_element_type=F32))
    for q in range(Q):
        rows = pl.ds(q, nrow, stride=Q)
        yq = y[:, q * LANE:(q + 1) * LANE] + dsk_ref[...] * u_ref[rows, :]
        g_ref[rows, :] = jax.nn.gelu(yq)


def _s5_main(u_gate, nb, t, wb, cb, lamq, dsk):
    Q = S5_Q
    M = u_gate.shape[0]
    nrow = M // Q
    blk = lambda a, b: pl.BlockSpec((None, a, b), lambda i: (i, 0, 0))
    return pl.pallas_call(
        functools.partial(_s5_main_kernel, nb),
        grid=(S5_NBLK,),
        in_specs=[pl.BlockSpec((M, LANE), lambda i: (0, i)),
                  blk(Q * LANE, Q * LANE), blk(Q * LANE, 2 * S5_NS), blk(2 * S5_NS, Q * LANE),
                  blk(8, 2 * S5_NS), pl.BlockSpec((1, LANE), lambda i: (0, i))],
        out_specs=[pl.BlockSpec((M, LANE), lambda i: (0, i)), blk(nb, 2 * S5_NS)],
        out_shape=[jax.ShapeDtypeStruct((M, E1), F32),
                   jax.ShapeDtypeStruct((S5_NBLK, nb, 2 * S5_NS), F32)],
        scratch_shapes=[pltpu.VMEM((nrow, Q * LANE), BF16), pltpu.VMEM((nrow, 2 * S5_NS), F32),
                        pltpu.VMEM((nrow, 2 * S5_NS), F32)],
        compiler_params=_cp("arbitrary"),
    )(u_gate, t, wb, cb, lamq, dsk)


def _s5_step_kernel(u_ref, h0_ref, bb_ref, cpl_ref, lam_ref, dsk_ref, g_ref, hn_ref):
    NS = S5_NS
    u = u_ref[...]
    s = jnp.dot(u.astype(BF16), bb_ref[...], preferred_element_type=F32)
    lam = lam_ref[0:1, :]
    h0 = h0_ref[...]
    n_re, n_im = _cmul(lam[:, :NS], lam[:, NS:], h0[:, :NS], h0[:, NS:])
    hn = jnp.concatenate([n_re + s[:, :NS], n_im + s[:, NS:]], axis=1)
    hn_ref[...] = hn
    y = jnp.dot(hn.astype(BF16), cpl_ref[...], preferred_element_type=F32) + dsk_ref[...] * u
    g_ref[...] = jax.nn.gelu(y)


def _s5_step(u_gate, h0, wb, cpl, lam, dsk):
    nb = u_gate.shape[0]
    blk = lambda a, b: pl.BlockSpec((None, a, b), lambda i: (i, 0, 0))
    return pl.pallas_call(
        _s5_step_kernel,
        grid=(S5_NBLK,),
        in_specs=[pl.BlockSpec((nb, LANE), lambda i: (0, i)), blk(nb, 2 * S5_NS),
                  pl.BlockSpec((None, LANE, 2 * S5_NS), lambda i: (i, S5_Q - 1, 0)),
                  blk(2 * S5_NS, LANE), blk(8, 2 * S5_NS), pl.BlockSpec((1, LANE), lambda i: (0, i))],
        out_specs=[pl.BlockSpec((nb, LANE), lambda i: (0, i)), blk(nb, 2 * S5_NS)],
        out_shape=[jax.ShapeDtypeStruct((nb, E1), F32),
                   jax.ShapeDtypeStruct((S5_NBLK, nb, 2 * S5_NS), F32)],
        compiler_params=_cp("arbitrary"),
    )(u_gate, h0, wb, cpl, lam, dsk)


def _state_to_blocks(st):
    nb = st.shape[0]
    x = st.reshape(nb, S5_NBLK, S5_NS, 2)
    return jnp.transpose(x, (1, 0, 3, 2)).reshape(S5_NBLK, nb, 2 * S5_NS)


def _blocks_to_state(h):
    nb = h.shape[1]
    x = h.reshape(S5_NBLK, nb, 2, S5_NS)
    return jnp.transpose(x, (1, 0, 3, 2)).reshape(nb, C_GROUPS, C_STATE, 2)


def kernel(x_prompt, x_sample, state_a_ssm, state_a_conv, cache_b_kv, state_c, norm_w, final_norm_w, even_w_in, a_conv_w, a_conv_b, a_dt_bias, a_A_log, a_D, a_norm_w, even_w_out, odd_w_in, c_A_re, c_A_im, c_B_re, c_B_im, c_C_re, c_C_im, c_D, c_log_dt, c_glu_w, c_glu_b, odd_w_out):
    B, L, D = x_prompt.shape
    nS = x_sample.shape[0]
    M = B * L
    Q = S5_Q

    w0 = even_w_in[0]
    s1 = A_WIDTH + A_CONV_DIM
    s2 = s1 + A_HEADS
    seg = lambda a, b: w0[:, a:b].astype(BF16)
    w_dt = jnp.pad(seg(s1, s2), ((0, 0), (0, TN - A_HEADS)))
    wcat = jnp.concatenate([seg(A_WIDTH, s1), w_dt, seg(s2, s2 + 2048), seg(s2 + 2048, s2 + 4096),
                            seg(s2 + 4096, s2 + 6144), seg(0, A_WIDTH), seg(s2 + 6144, s2 + 8192)], axis=1)
    w_out0 = even_w_out[0].astype(BF16)
    w_in1 = odd_w_in[0].astype(BF16)
    w_glu = c_glu_w[0].astype(BF16)
    w_out1 = odd_w_out[0].astype(BF16)
    nw0 = norm_w[0][None, :]
    nw1 = norm_w[1][None, :]
    fnw = final_norm_w[None, :]
    cw, cb = a_conv_w[0], a_conv_b[0][None, :]
    pad_h = lambda v: jnp.pad(v, (0, LANE - A_HEADS))[None, :]
    dtb, alog = pad_h(a_dt_bias[0]), pad_h(a_A_log[0])
    dsk_a = jnp.repeat(a_D[0], A_HEAD_DIM)[None, :]
    gnw = a_norm_w[0][None, :]
    head_expand = np.kron(np.eye(A_HEADS, dtype=np.float32), np.ones((1, A_HEAD_DIM), np.float32))
    head_expand = jnp.asarray(np.pad(head_expand, ((0, LANE - A_HEADS), (0, 0)))).astype(BF16)
    slopes_np = np.asarray(2.0 ** (-8.0 * np.arange(1, B_HEADS + 1) / B_HEADS), np.float32)
    slopes = jnp.asarray(np.broadcast_to(slopes_np[:, None, None], (B_HEADS, 8, LANE)))
    dist = np.stack([dil * (LANE - np.arange(LANE)) for _, dil in B_PATTERNS]).astype(np.float32)
    step_bias = jnp.asarray((dist[:, :, None] * slopes_np[None, None, :])[..., None])

    eye = jnp.eye(S5_BLK, dtype=F32)
    rows8 = lambda p: jnp.broadcast_to(p.reshape(S5_NBLK, 1, S5_NS), (S5_NBLK, 8, S5_NS))
    bexp = lambda p: jnp.einsum("kgnc,gh->kgchn", p.reshape(S5_NBLK, S5_BLK, C_STATE, C_GROUP),
                                eye).reshape(S5_NBLK, LANE, S5_NS)
    cexp = lambda p: jnp.einsum("kgcn,gh->kgnhc", p.reshape(S5_NBLK, S5_BLK, C_GROUP, C_STATE),
                                eye).reshape(S5_NBLK, S5_NS, LANE)
    ldt = jnp.repeat(c_log_dt[0], C_STATE).reshape(S5_NBLK, S5_NS)
    t_m, wb_m, cb_m, lam, lamq, cpl = _s5_prep(rows8(c_A_re[0]), rows8(c_A_im[0]), rows8(ldt),
                                                bexp(c_B_re[0]), bexp(c_B_im[0]),
                                                cexp(c_C_re[0]), cexp(c_C_im[0]))
    dsk_c = c_D[0][None, :]
    glu_b = c_glu_b[0][None, :]

    xp = x_prompt.reshape(M, D)
    a_arr, q, k, v, kv_keep = _inproj0(xp, nw0, wcat, B, L)
    y_a, ssm_p, tail_p = _ssd(a_arr, B, L, cw, cb, dtb, alog, dsk_a, gnw)
    y_b = _attn(q, k, v, slopes, a_arr)
    hp1 = _outproj0(y_a, y_b, w_out0, xp, 1024)
    u_gate = _mm_norm(hp1, nw1, w_in1, 1024, 1024)
    g, hfin_p = _s5_main(u_gate, B, t_m, wb_m, cb_m, lamq, dsk_c)
    y1 = _glu(g, w_glu, glu_b, u_gate, 512, BF16)
    y_prompt = _outproj1(y1, w_out1, hp1, fnw, 1024).reshape(B, L, D)

    xs = x_sample.reshape(nS, D)
    proj = _mm_norm(xs, nw0, wcat, nS)
    col = lambda a, b: proj[:, a:b]
    hv = lambda a: col(a, a + 2048).reshape(nS, B_HEADS, B_HEAD_DIM)
    y_as, ssm_s, conv_s = _ssd_step(col(0, 3072)[:, None, :], state_a_conv[0],
                                    col(3072, 3200)[:, None, :], col(9728, 11776)[:, None, :],
                                    state_a_ssm[0].reshape(nS, A_WIDTH, A_STATE),
                                    cw, cb, dtb, alog, head_expand, dsk_a, gnw)
    y_bs = _attn_step(hv(3584), hv(5632), hv(7680), hv(11776), cache_b_kv, step_bias)
    new_cache = _cache_shift(cache_b_kv, jnp.stack([hv(5632), hv(7680)], axis=1))
    hs1 = _outproj0(y_as.reshape(nS, A_WIDTH), y_bs.reshape(nS, A_WIDTH), w_out0, xs, nS)
    u_gate_s = _mm_norm(hs1, nw1, w_in1, nS)
    g_s, hnew_s = _s5_step(u_gate_s, _state_to_blocks(state_c[0]), wb_m, cpl, lam, dsk_c)
    y1_s = _glu(g_s, w_glu, glu_b, u_gate_s, nS, F32)
    y_sample = _outproj1(y1_s, w_out1, hs1, fnw, nS).reshape(nS, 1, D)

    keep = kv_keep.shape[1]
    return (y_prompt, y_sample,
            ssm_p.reshape(1, B, A_HEADS, A_HEAD_DIM, A_STATE),
            tail_p[:, 5:8, :][None],
            kv_keep.reshape(1, B, keep, 2, B_HEADS, B_HEAD_DIM),
            _blocks_to_state(hfin_p)[None],
            ssm_s.reshape(1, nS, A_HEADS, A_HEAD_DIM, A_STATE),
            conv_s[None],
            new_cache,
            _blocks_to_state(hnew_s)[None])
```

```python
import functools

import numpy as np
import jax
import jax.numpy as jnp
from jax import lax
from jax.experimental import pallas as pl
from jax.experimental.pallas import tpu as pltpu

F32 = jnp.float32
BF16 = jnp.bfloat16

D_MODEL = 2048
EPS = 1e-5
A_WIDTH = 2048
A_HEAD_DIM = 64
A_HEADS = 32
A_GROUPS = 4
A_STATE = 128
A_CONV_W = 4
A_CONV_DIM = 3072
SSD_CHUNK = 128
B_HEADS = 16
B_HEAD_DIM = 128
B_PATTERNS = ((128, 1), (512, 4), (2048, 16))
B_MAX_WINDOW = 2048
C_GROUP = 16
C_GROUPS = 256
C_STATE = 64
E1 = 4096

LANE = 128
S5_Q = 8
S5_BLK = 8
S5_NBLK = C_GROUPS // S5_BLK
S5_NS = S5_BLK * C_STATE
TN = 512
ATT_BLOCK = 2048
NEG = -1e30
ATT_SCALE = B_HEAD_DIM ** -0.5
VMEM_LIMIT = 56 * 1024 * 1024

NT_DIMS = (((1,), (1,)), ((), ()))


def _cp(*sem):
    return pltpu.CompilerParams(dimension_semantics=sem, vmem_limit_bytes=VMEM_LIMIT)


def _silu(x):
    return x * jax.nn.sigmoid(x)


def _softplus(x):
    return jnp.maximum(x, 0.0) + jnp.log1p(jnp.exp(-jnp.abs(x)))


def _split3(a):
    hi = a.astype(BF16)
    r = a - hi.astype(F32)
    mid = r.astype(BF16)
    lo = (r - mid.astype(F32)).astype(BF16)
    return hi, mid, lo


def _dot3_left(m, a):
    hi, mid, lo = _split3(a)
    d = lambda p: jnp.dot(m, p, preferred_element_type=F32)
    return (d(lo) + d(mid)) + d(hi)


def _dot3_right(a, m):
    hi, mid, lo = _split3(a)
    d = lambda p: jnp.dot(p, m, preferred_element_type=F32)
    return (d(lo) + d(mid)) + d(hi)


def _rms(x, w):
    ms = jnp.mean(x * x, axis=-1, keepdims=True)
    return (x * lax.rsqrt(ms + EPS)) * w


def _inproj0_kernel(x_ref, nw_ref, w_ref, a_ref, q_ref, k_ref, v_ref, kv_ref, xn_ref):
    j = pl.program_id(1)

    @pl.when(j == 0)
    def _():
        xn_ref[...] = _rms(x_ref[...], nw_ref[...]).astype(BF16)

    acc = jnp.dot(xn_ref[...], w_ref[...], preferred_element_type=F32)

    def heads(ref):
        for hh in range(TN // LANE):
            ref[hh] = acc[:, hh * LANE:(hh + 1) * LANE]

    @pl.when((j < 7) | (j >= 19))
    def _():
        a_ref[...] = acc

    @pl.when((j >= 7) & (j < 11))
    def _():
        heads(q_ref)

    @pl.when((j >= 11) & (j < 15))
    def _():
        heads(k_ref)
        kv_ref[...] = acc

    @pl.when((j >= 15) & (j < 19))
    def _():
        heads(v_ref)
        kv_ref[...] = acc


def _inproj0(x2d, nw, wcat, B, L):
    M = B * L
    tm = 1024
    rpb = L // tm
    keep = min(B_MAX_WINDOW, L)
    nskip = (L - keep) // tm
    nj = wcat.shape[1] // TN

    def a_map(i, j):
        return (i, jnp.where(j < 6, j, jnp.where(j < 19, 14, j - 13)))

    def head_map(j0):
        return lambda i, j: (i // rpb, jnp.clip(j - j0, 0, 3), i % rpb, 0)

    def kv_map(i, j):
        ri = i % rpb
        return (i // rpb, jnp.maximum(ri - nskip, 0), jnp.where(ri < nskip, 0, jnp.clip(j - 11, 0, 7)))

    hd = jax.ShapeDtypeStruct((B, B_HEADS, L, B_HEAD_DIM), F32)
    return pl.pallas_call(
        _inproj0_kernel,
        grid=(M // tm, nj),
        in_specs=[pl.BlockSpec((tm, D_MODEL), lambda i, j: (i, 0)),
                  pl.BlockSpec((1, D_MODEL), lambda i, j: (0, 0)),
                  pl.BlockSpec((D_MODEL, TN), lambda i, j: (0, j))],
        out_specs=[pl.BlockSpec((tm, TN), a_map),
                   pl.BlockSpec((None, TN // LANE, tm, LANE), head_map(7)),
                   pl.BlockSpec((None, TN // LANE, tm, LANE), head_map(11)),
                   pl.BlockSpec((None, TN // LANE, tm, LANE), head_map(15)),
                   pl.BlockSpec((None, tm, TN), kv_map)],
        out_shape=[jax.ShapeDtypeStruct((M, 15 * TN), F32), hd, hd, hd,
                   jax.ShapeDtypeStruct((B, keep, 8 * TN), F32)],
        scratch_shapes=[pltpu.VMEM((tm, D_MODEL), BF16)],
        compiler_params=_cp("arbitrary", "arbitrary"),
    )(x2d, nw, wcat)


def _mm_norm_kernel(x_ref, nw_ref, w_ref, o_ref, xn_ref):
    @pl.when(pl.program_id(1) == 0)
    def _():
        xn_ref[...] = _rms(x_ref[...], nw_ref[...]).astype(BF16)

    o_ref[...] = jnp.dot(xn_ref[...], w_ref[...], preferred_element_type=F32)


def _mm_norm(x, nw, w, tm, tn=TN):
    M = x.shape[0]
    K, N = w.shape
    return pl.pallas_call(
        _mm_norm_kernel,
        grid=(M // tm, N // tn),
        in_specs=[pl.BlockSpec((tm, K), lambda i, j: (i, 0)),
                  pl.BlockSpec((1, K), lambda i, j: (0, 0)),
                  pl.BlockSpec((K, tn), lambda i, j: (0, j))],
        out_specs=pl.BlockSpec((tm, tn), lambda i, j: (i, j)),
        out_shape=jax.ShapeDtypeStruct((M, N), F32),
        scratch_shapes=[pltpu.VMEM((tm, K), BF16)],
        compiler_params=_cp("arbitrary", "arbitrary"),
    )(x, nw, w)


def _outproj0_kernel(ya_ref, yb_ref, wa_ref, wb_ref, res_ref, o_ref):
    o_ref[...] = (res_ref[...]
                  + jnp.dot(ya_ref[...].astype(BF16), wa_ref[...], preferred_element_type=F32)
                  + jnp.dot(yb_ref[...].astype(BF16), wb_ref[...], preferred_element_type=F32))


def _outproj0(ya, yb, w, res, tm, tn=TN):
    M = ya.shape[0]
    K = ya.shape[1]
    return pl.pallas_call(
        _outproj0_kernel,
        grid=(M // tm, D_MODEL // tn),
        in_specs=[pl.BlockSpec((tm, K), lambda i, j: (i, 0)),
                  pl.BlockSpec((tm, K), lambda i, j: (i, 0)),
                  pl.BlockSpec((K, tn), lambda i, j: (0, j)),
                  pl.BlockSpec((K, tn), lambda i, j: (1, j)),
                  pl.BlockSpec((tm, tn), lambda i, j: (i, j))],
        out_specs=pl.BlockSpec((tm, tn), lambda i, j: (i, j)),
        out_shape=jax.ShapeDtypeStruct((M, D_MODEL), F32),
        compiler_params=_cp("arbitrary", "arbitrary"),
    )(ya, yb, w, w, res)


def _glu_kernel(g_ref, wa_ref, wb_ref, ba_ref, bb_ref, gate_ref, o_ref):
    g = g_ref[...].astype(BF16)
    ga = jnp.dot(g, wa_ref[...], preferred_element_type=F32) + ba_ref[...]
    gb = jnp.dot(g, wb_ref[...], preferred_element_type=F32) + bb_ref[...]
    o_ref[...] = (ga * jax.nn.sigmoid(gb) * _silu(gate_ref[...])).astype(o_ref.dtype)


def _glu(g, w, bias, u_gate, tm, out_dtype, tn=TN):
    M = g.shape[0]
    nb = E1 // tn
    return pl.pallas_call(
        _glu_kernel,
        grid=(M // tm, nb),
        in_specs=[pl.BlockSpec((tm, E1), lambda i, j: (i, 0)),
                  pl.BlockSpec((E1, tn), lambda i, j: (0, j)),
                  pl.BlockSpec((E1, tn), lambda i, j: (0, j + nb)),
                  pl.BlockSpec((1, tn), lambda i, j: (0, j)),
                  pl.BlockSpec((1, tn), lambda i, j: (0, j + nb)),
                  pl.BlockSpec((tm, tn), lambda i, j: (i, j + nb))],
        out_specs=pl.BlockSpec((tm, tn), lambda i, j: (i, j)),
        out_shape=jax.ShapeDtypeStruct((M, E1), out_dtype),
        compiler_params=_cp("arbitrary", "arbitrary"),
    )(g, w, w, bias, bias, u_gate)


def _outproj1_kernel(y_ref, w_ref, res_ref, fnw_ref, o_ref):
    j = pl.program_id(1)
    nj = D_MODEL // TN
    h = res_ref[...] + jnp.dot(y_ref[...].astype(BF16), w_ref[...], preferred_element_type=F32)
    for jj in range(nj):
        @pl.when(j == jj)
        def _(jj=jj):
            o_ref[:, jj * TN:(jj + 1) * TN] = h

    @pl.when(j == nj - 1)
    def _():
        o_ref[...] = _rms(o_ref[...], fnw_ref[...])


def _outproj1(y, w, res, fnw, tm):
    M, K = y.shape
    return pl.pallas_call(
        _outproj1_kernel,
        grid=(M // tm, D_MODEL // TN),
        in_specs=[pl.BlockSpec((tm, K), lambda i, j: (i, 0)),
                  pl.BlockSpec((K, TN), lambda i, j: (0, j)),
                  pl.BlockSpec((tm, TN), lambda i, j: (i, j)),
                  pl.BlockSpec((1, D_MODEL), lambda i, j: (0, 0))],
        out_specs=pl.BlockSpec((tm, D_MODEL), lambda i, j: (i, 0)),
        out_shape=jax.ShapeDtypeStruct((M, D_MODEL), F32),
        compiler_params=_cp("arbitrary", "arbitrary"),
    )(y, w, res, fnw)


def _ssd_kernel(xbc_ref, dt_ref, z0_ref, z1_ref, cw_ref, cb_ref, dtb_ref, alog_ref, dsk_ref, gnw_ref,
                y_ref, st_ref, tail_ref, h_scr, tail_scr, y_scr):
    c = pl.program_id(1)
    Q = SSD_CHUNK

    @pl.when(c == 0)
    def _():
        h_scr[...] = jnp.zeros_like(h_scr)
        tail_scr[...] = jnp.zeros_like(tail_scr)

    x_raw = xbc_ref[...]
    tail8 = tail_scr[...]
    row8 = lax.broadcasted_iota(jnp.int32, (8, 1), 0)
    conv = cb_ref[...] + cw_ref[3:4, :] * x_raw
    for k in (1, 2, 3):
        xr = pltpu.roll(x_raw, k, 0)
        first = jnp.where(row8 < k, pltpu.roll(tail8, k, 0), xr[0:8, :])
        conv = conv + cw_ref[3 - k:4 - k, :] * jnp.concatenate([first, xr[8:, :]], axis=0)
    tail_scr[...] = x_raw[Q - 8:Q, :]
    xc = _silu(conv)

    dt = _softplus(dt_ref[...] + dtb_ref[...])
    a = dt * (-jnp.exp(alog_ref[...]))
    qi = lax.broadcasted_iota(jnp.int32, (Q, Q), 0)
    si = lax.broadcasted_iota(jnp.int32, (Q, Q), 1)
    tri = qi >= si
    a_cum = _dot3_left(tri.astype(BF16), a)
    a_cum_t = a_cum.T
    lo_half = si < A_HEAD_DIM

    for g in range(A_GROUPS):
        bg = xc[:, A_WIDTH + g * A_STATE:A_WIDTH + (g + 1) * A_STATE].astype(BF16)
        cg = xc[:, A_WIDTH + (A_GROUPS + g) * A_STATE:A_WIDTH + (A_GROUPS + g + 1) * A_STATE].astype(BF16)
        cbm = lax.dot_general(cg, bg, NT_DIMS, preferred_element_type=F32)
        hprev = h_scr[g * 512:(g + 1) * 512, :]
        yoff = lax.dot_general(cg, hprev.astype(BF16), NT_DIMS, preferred_element_type=F32)
        for jj in range(4):
            p = g * 4 + jj
            h0, h1 = 2 * p, 2 * p + 1
            dtp = jnp.where(lo_half, dt[:, h0:h0 + 1], dt[:, h1:h1 + 1])
            ap = jnp.where(lo_half, a_cum[:, h0:h0 + 1], a_cum[:, h1:h1 + 1])
            x_p = xc[:, p * LANE:(p + 1) * LANE]
            xdt = x_p * dtp
            g0 = cbm * jnp.exp(jnp.where(tri, a_cum[:, h0:h0 + 1] - a_cum_t[h0:h0 + 1, :], NEG))
            g1 = cbm * jnp.exp(jnp.where(tri, a_cum[:, h1:h1 + 1] - a_cum_t[h1:h1 + 1, :], NEG))
            lhs = jnp.concatenate([g0, g1], axis=1).astype(BF16)
            rhs = jnp.concatenate([jnp.where(lo_half, xdt, 0.0), jnp.where(lo_half, 0.0, xdt)],
                                  axis=0).astype(BF16)
            ydiag = jnp.dot(lhs, rhs, preferred_element_type=F32)
            y_scr[:, p * LANE:(p + 1) * LANE] = (
                ydiag + yoff[:, jj * LANE:(jj + 1) * LANE] * jnp.exp(ap)
                + dsk_ref[:, p * LANE:(p + 1) * LANE] * x_p)
            xd = xdt * jnp.exp(ap[Q - 1:Q, :] - ap)
            st = jnp.dot(xd.T.astype(BF16), bg, preferred_element_type=F32)
            cd = jnp.exp(jnp.where(qi < A_HEAD_DIM, a_cum_t[h0:h0 + 1, Q - 1:Q],
                                   a_cum_t[h1:h1 + 1, Q - 1:Q]))
            h_scr[p * LANE:(p + 1) * LANE, :] = cd * hprev[jj * LANE:(jj + 1) * LANE, :] + st

    z = jnp.concatenate([z0_ref[...], z1_ref[...]], axis=1)
    y_ref[...] = _rms(y_scr[...] * _silu(z), gnw_ref[...]).astype(y_ref.dtype)

    @pl.when(c == pl.num_programs(1) - 1)
    def _():
        st_ref[...] = h_scr[...]
        tail_ref[...] = x_raw[Q - 8:Q, :]


def _ssd(a_arr, B, L, cw, cb, dtb, alog, dsk, gnw):
    nc = L // SSD_CHUNK
    Q = SSD_CHUNK
    row = lambda b, c: b * nc + c
    full = lambda shp: pl.BlockSpec(shp, lambda b, c: (0, 0))
    return pl.pallas_call(
        _ssd_kernel,
        grid=(B, nc),
        in_specs=[pl.BlockSpec((Q, A_CONV_DIM), lambda b, c: (row(b, c), 0)),
                  pl.BlockSpec((Q, LANE), lambda b, c: (row(b, c), 14 * TN // LANE)),
                  pl.BlockSpec((Q, A_WIDTH // 2), lambda b, c: (row(b, c), 3)),
                  pl.BlockSpec((Q, A_WIDTH // 2), lambda b, c: (row(b, c), 4)),
                  full((A_CONV_W, A_CONV_DIM)), full((1, A_CONV_DIM)), full((1, LANE)),
                  full((1, LANE)), full((1, A_WIDTH)), full((1, A_WIDTH))],
        out_specs=[pl.BlockSpec((Q, A_WIDTH), lambda b, c: (row(b, c), 0)),
                   pl.BlockSpec((None, A_WIDTH, A_STATE), lambda b, c: (b, 0, 0)),
                   pl.BlockSpec((None, 8, A_CONV_DIM), lambda b, c: (b, 0, 0))],
        out_shape=[jax.ShapeDtypeStruct((B * L, A_WIDTH), BF16),
                   jax.ShapeDtypeStruct((B, A_WIDTH, A_STATE), F32),
                   jax.ShapeDtypeStruct((B, 8, A_CONV_DIM), F32)],
        scratch_shapes=[pltpu.VMEM((A_WIDTH, A_STATE), F32), pltpu.VMEM((8, A_CONV_DIM), F32),
                        pltpu.VMEM((Q, A_WIDTH), F32)],
        compiler_params=_cp("arbitrary", "arbitrary"),
    )(a_arr, a_arr, a_arr, a_arr, cw, cb, dtb, alog, dsk, gnw)


def _ssd_step_kernel(xbc_ref, cs_ref, dt_ref, z_ref, h0_ref, cw_ref, cb_ref, dtb_ref, alog_ref,
                     e_ref, dsk_ref, gnw_ref, y_ref, hn_ref, nc_ref):
    x_raw = xbc_ref[...]
    cs = cs_ref[...]
    conv = (cb_ref[...] + cw_ref[0:1, :] * cs[0:1, :] + cw_ref[1:2, :] * cs[1:2, :]
            + cw_ref[2:3, :] * cs[2:3, :] + cw_ref[3:4, :] * x_raw)
    xc = _silu(conv)
    nc_ref[0:1, :] = cs[1:2, :]
    nc_ref[1:2, :] = cs[2:3, :]
    nc_ref[2:3, :] = x_raw

    dt = _softplus(dt_ref[...] + dtb_ref[...])
    da = dt * (-jnp.exp(alog_ref[...]))
    e = e_ref[...]
    dt_e = _dot3_right(jnp.broadcast_to(dt, (8, LANE)), e)[0:1, :]
    da_e = _dot3_right(jnp.broadcast_to(da, (8, LANE)), e)[0:1, :]
    xs = xc[:, :A_WIDTH]
    sub = lax.broadcasted_iota(jnp.int32, (8, 1), 0)
    rows = jnp.where(sub == 0, xs * dt_e, jnp.where(sub == 1, jnp.exp(da_e), 0.0))
    cols = jnp.concatenate([rows, jnp.zeros((LANE - 8, A_WIDTH), F32)], axis=0).T
    xdt_col = cols[:, 0:1]
    dec_col = cols[:, 1:2]
    ys = []
    for g in range(A_GROUPS):
        bg = xc[:, A_WIDTH + g * A_STATE:A_WIDTH + (g + 1) * A_STATE]
        cg = xc[:, A_WIDTH + (A_GROUPS + g) * A_STATE:A_WIDTH + (A_GROUPS + g + 1) * A_STATE]
        sl = slice(g * 512, (g + 1) * 512)
        hn = dec_col[sl, :] * h0_ref[sl, :] + xdt_col[sl, :] * bg
        hn_ref[sl, :] = hn
        c8 = jnp.broadcast_to(cg, (8, A_STATE)).astype(BF16)
        ys.append(lax.dot_general(c8, hn.astype(BF16), NT_DIMS, preferred_element_type=F32)[0:1, :])
    y = jnp.concatenate(ys, axis=1) + dsk_ref[...] * xs
    y_ref[...] = _rms(y * _silu(z_ref[...]), gnw_ref[...])


def _ssd_step(xbc, cs, dt, z, h0, cw, cb, dtb, alog, e, dsk, gnw):
    nb = xbc.shape[0]
    per_b = lambda n: pl.BlockSpec((None, 1, n), lambda b: (b, 0, 0))
    full = lambda shp: pl.BlockSpec(shp, lambda b: (0, 0))
    return pl.pallas_call(
        _ssd_step_kernel,
        grid=(nb,),
        in_specs=[per_b(A_CONV_DIM), pl.BlockSpec((None, 3, A_CONV_DIM), lambda b: (b, 0, 0)),
                  per_b(LANE), per_b(A_WIDTH),
                  pl.BlockSpec((None, A_WIDTH, A_STATE), lambda b: (b, 0, 0)),
                  full((A_CONV_W, A_CONV_DIM)), full((1, A_CONV_DIM)), full((1, LANE)),
                  full((1, LANE)), full((LANE, A_WIDTH)), full((1, A_WIDTH)), full((1, A_WIDTH))],
        out_specs=[per_b(A_WIDTH),
                   pl.BlockSpec((None, A_WIDTH, A_STATE), lambda b: (b, 0, 0)),
                   pl.BlockSpec((None, 3, A_CONV_DIM), lambda b: (b, 0, 0))],
        out_shape=[jax.ShapeDtypeStruct((nb, 1, A_WIDTH), F32),
                   jax.ShapeDtypeStruct((nb, A_WIDTH, A_STATE), F32),
                   jax.ShapeDtypeStruct((nb, 3, A_CONV_DIM), F32)],
        compiler_params=_cp("arbitrary"),
    )(xbc, cs, dt, z, h0, cw, cb, dtb, alog, e, dsk, gnw)


def _bias_tile(slope, dil):
    i = lax.broadcasted_iota(jnp.int32, (LANE, 2 * LANE), 0)
    c = lax.broadcasted_iota(jnp.int32, (LANE, 2 * LANE), 1)
    j = LANE + i - c
    valid = (j >= 0) & (j <= LANE)
    return jnp.where(valid, -(slope * (dil * j).astype(F32)), NEG), c


def _attn_tile(q, kcat, vcat, bias):
    s = lax.dot_general(q, kcat, NT_DIMS, preferred_element_type=F32) * ATT_SCALE + bias
    m = jnp.max(s, axis=1, keepdims=True)
    p = jnp.exp(s - m)
    l = jnp.sum(p, axis=1, keepdims=True)
    acc = jnp.dot(p.astype(BF16), vcat, preferred_element_type=F32)
    return m, l, acc


def _rows(start, stride):
    return pl.ds(start, LANE, stride=stride) if stride > 1 else pl.ds(start, LANE)


def _shift_cache_block(step, steps_per_seq, x_ref, nxt_ref, kvn_ref, o_ref):
    R = x_ref.shape[0]
    o_ref[pl.ds(0, R - 1)] = x_ref[pl.ds(1, R - 1)]
    last = step % steps_per_seq == steps_per_seq - 1
    o_ref[R - 1] = jnp.where(last, kvn_ref[...], nxt_ref[0])


def _attn_kernel(steps_per_seq, slope_ref, q_ref, k_ref, v_ref, kp_ref, vp_ref, gate_ref,
                 x_ref, nxt_ref, kvn_ref, y_ref, oc_ref, m_scr, l_scr, acc_scr):
    step = ((pl.program_id(0) * pl.num_programs(1) + pl.program_id(1)) * pl.num_programs(2)
            + pl.program_id(2))
    _shift_cache_block(step, steps_per_seq, x_ref, nxt_ref, kvn_ref, oc_ref)
    first_block = pl.program_id(2) == 0
    slope = slope_ref[0:1, 0:1]
    for _, dil in B_PATTERNS:
        bias_full, c = _bias_tile(slope, dil)
        bias0 = jnp.where((c >= LANE) | jnp.logical_not(first_block), bias_full, NEG)
        nsub = ATT_BLOCK // dil // LANE
        for r in range(dil):
            kc = vc = None
            for sub in range(nsub):
                rows = _rows(sub * LANE * dil + r, dil)
                if sub == 0:
                    prow = _rows((nsub - 1) * LANE * dil + r, dil)
                    kp, vp, bias = kp_ref[prow, :], vp_ref[prow, :], bias0
                else:
                    kp, vp, bias = kc, vc, bias_full
                kc, vc = k_ref[rows, :], v_ref[rows, :]
                kcat = jnp.concatenate([kp, kc], axis=0).astype(BF16)
                vcat = jnp.concatenate([vp, vc], axis=0).astype(BF16)
                m, l, acc = _attn_tile(q_ref[rows, :].astype(BF16), kcat, vcat, bias)
                if dil == 1:
                    m_scr[rows, :] = jnp.broadcast_to(m, (LANE, LANE))
                    l_scr[rows, :] = jnp.broadcast_to(l, (LANE, LANE))
                    acc_scr[rows, :] = acc
                else:
                    m_old = m_scr[rows, :]
                    m_new = jnp.maximum(m_old, m)
                    a_old = jnp.exp(m_old - m_new)
                    a_new = jnp.exp(m - m_new)
                    m_scr[rows, :] = m_new
                    l_scr[rows, :] = a_old * l_scr[rows, :] + a_new * l
                    acc_scr[rows, :] = a_old * acc_scr[rows, :] + a_new * acc
    for sub in range(ATT_BLOCK // LANE):
        rs = pl.ds(sub * LANE, LANE)
        o = acc_scr[rs, :] / l_scr[rs, :]
        y_ref[rs, :] = (o * _silu(gate_ref[rs, :])).astype(y_ref.dtype)


def _attn(q, k, v, slopes, a_arr, cache6, kvn):
    B, H, L, dh = q.shape
    R = ATT_BLOCK
    nblk = L // R
    cur = pl.BlockSpec((None, None, R, dh), lambda b, h, i: (b, h, i, 0))
    prev = pl.BlockSpec((None, None, R, dh), lambda b, h, i: (b, h, jnp.maximum(i - 1, 0), 0))
    slope = pl.BlockSpec((None, 8, LANE), lambda b, h, i: (h, 0, 0))
    gate = pl.BlockSpec((R, LANE), lambda b, h, i: (b * nblk + i, 10 * TN // LANE + h))

    nb, W = cache6.shape[1], cache6.shape[2]
    steps = B * H * nblk
    per = steps // nb
    rows = W // per
    assert per * nb == steps and rows * per == W, (steps, nb, W)
    tail = (2, B_HEADS, B_HEAD_DIM)
    step = lambda b, h, i: (b * H + h) * nblk + i
    cblk = pl.BlockSpec((None, None, rows) + tail,
                        lambda b, h, i: (0, step(b, h, i) // per, step(b, h, i) % per, 0, 0, 0))
    cnext = pl.BlockSpec((None, None, 1) + tail,
                         lambda b, h, i: (0, step(b, h, i) // per,
                                          jnp.minimum((step(b, h, i) % per + 1) * rows, W - 1), 0, 0, 0))
    cnew = pl.BlockSpec((None,) + tail, lambda b, h, i: (step(b, h, i) // per, 0, 0, 0))
    return pl.pallas_call(
        functools.partial(_attn_kernel, per),
        grid=(B, H, nblk),
        in_specs=[slope, cur, cur, cur, prev, prev, gate, cblk, cnext, cnew],
        out_specs=[pl.BlockSpec((R, LANE), lambda b, h, i: (b * nblk + i, h)), cblk],
        out_shape=[jax.ShapeDtypeStruct((B * L, H * dh), BF16),
                   jax.ShapeDtypeStruct(cache6.shape, F32)],
        scratch_shapes=[pltpu.VMEM((R, LANE), F32)] * 3,
        compiler_params=_cp("arbitrary", "arbitrary", "arbitrary"),
    )(slopes, q, k, v, k, v, a_arr, cache6, cache6, kvn)


def _attn_step_kernel(bias_ref, q_ref, kn_ref, vn_ref, gate_ref, c1_ref, c4_ref, c16_ref, y_ref):
    q, kn, vn = q_ref[...], kn_ref[...], vn_ref[...]
    s_self = jnp.sum(q * kn, axis=-1, keepdims=True) * ATT_SCALE
    blocks = (c1_ref, c4_ref, c16_ref)
    scores = []
    for p, c_ref in enumerate(blocks):
        kblk = c_ref[:, 0]
        scores.append(jnp.sum(kblk * q[None], axis=-1, keepdims=True) * ATT_SCALE - bias_ref[p])
    m = s_self
    for s in scores:
        m = jnp.maximum(m, jnp.max(s, axis=0))
    p_self = float(len(B_PATTERNS)) * jnp.exp(s_self - m)
    l = p_self
    o = p_self * vn
    for s, c_ref in zip(scores, blocks):
        p = jnp.exp(s - m[None])
        l = l + jnp.sum(p, axis=0)
        o = o + jnp.sum(p * c_ref[:, 1], axis=0)
    y_ref[...] = (o / l) * _silu(gate_ref[...])


def _attn_step(q, kn, vn, gate, cache6, bias):
    nb = q.shape[0]
    W = B_MAX_WINDOW
    hspec = pl.BlockSpec((None, B_HEADS, B_HEAD_DIM), lambda b: (b, 0, 0))
    tail = (2, B_HEADS, B_HEAD_DIM)
    views, specs = [], []
    for _, dil in B_PATTERNS:
        if dil == 1:
            views.append(cache6)
            specs.append(pl.BlockSpec((None, None, LANE) + tail,
                                      lambda b: (0, b, W // LANE - 1, 0, 0, 0)))
        else:
            views.append(cache6.reshape((1, nb, W // dil, dil) + tail))
            specs.append(pl.BlockSpec((None, None, LANE, None) + tail,
                                      lambda b, dil=dil: (0, b, W // dil // LANE - 1, 0, 0, 0, 0)))
    return pl.pallas_call(
        _attn_step_kernel,
        grid=(nb,),
        in_specs=[pl.BlockSpec(bias.shape, lambda b: (0, 0, 0, 0)), hspec, hspec, hspec, hspec]
                 + specs,
        out_specs=hspec,
        out_shape=jax.ShapeDtypeStruct((nb, B_HEADS, B_HEAD_DIM), F32),
        compiler_params=_cp("arbitrary"),
    )(bias, q, kn, vn, gate, *views)


def _cmul(ar, ai, br, bi):
    return ar * br - ai * bi, ar * bi + ai * br


def _s5_prep_kernel(are_ref, aim_ref, ldt_ref, bre_ref, bim_ref, ctre_ref, ctim_ref,
                    t_ref, wb_ref, cb_ref, lam_ref, lamq_ref, cpl_ref):
    Q = S5_Q
    are, aim = are_ref[...], aim_ref[...]
    dt = jnp.exp(ldt_ref[...])
    mag = jnp.exp(dt * are)
    l_re = mag * jnp.cos(dt * aim)
    l_im = mag * jnp.sin(dt * aim)
    den = are * are + aim * aim
    f_re = ((l_re - 1.0) * are + l_im * aim) / den
    f_im = (l_im * are - (l_re - 1.0) * aim) / den
    lr, li = l_re[0:1, :], l_im[0:1, :]
    bb_re, bb_im = _cmul(f_re[0:1, :], f_im[0:1, :], bre_ref[...], bim_ref[...])
    ctre, ctim = ctre_ref[...], ctim_ref[...]
    cplain = jnp.concatenate([ctre, -ctim], axis=0).astype(BF16)
    cpl_ref[...] = cplain
    stk = jnp.concatenate([l_re, l_im, jnp.zeros((LANE - 16, S5_NS), F32)], axis=0).T
    lc_re, lc_im = stk[:, 0:1], stk[:, 8:9]
    p_re, p_im = jnp.ones((1, S5_NS), F32), jnp.zeros((1, S5_NS), F32)
    pc_re, pc_im = lc_re, lc_im
    dts = []
    for tau in range(Q):
        e_re, e_im = _cmul(bb_re, bb_im, p_re, p_im)
        wp = jnp.concatenate([e_re, e_im], axis=1).astype(BF16)
        s = Q - 1 - tau
        wb_ref[s * LANE:(s + 1) * LANE, :] = wp
        dts.append(jnp.dot(wp, cplain, preferred_element_type=F32).astype(BF16))
        c_re, c_im = _cmul(ctre, ctim, pc_re, pc_im)
        cb_ref[:, tau * LANE:(tau + 1) * LANE] = jnp.concatenate([c_re, -c_im], axis=0).astype(BF16)
        p_re, p_im = _cmul(p_re, p_im, lr, li)
        pc_re, pc_im = _cmul(pc_re, pc_im, lc_re, lc_im)
    zero = jnp.zeros((LANE, LANE), BF16)
    for s in range(Q):
        for q in range(Q):
            t_ref[s * LANE:(s + 1) * LANE, q * LANE:(q + 1) * LANE] = dts[q - s] if q >= s else zero
    lam_ref[...] = jnp.concatenate([l_re, l_im], axis=1)
    sub = lax.broadcasted_iota(jnp.int32, (8, 1), 0)
    q_re, q_im = p_re, p_im
    tab_re, tab_im = jnp.zeros((8, S5_NS), F32), jnp.zeros((8, S5_NS), F32)
    for r in range(8):
        tab_re = jnp.where(sub == r, q_re, tab_re)
        tab_im = jnp.where(sub == r, q_im, tab_im)
        q_re, q_im = _cmul(q_re, q_im, p_re, p_im)
    lamq_ref[...] = jnp.concatenate([tab_re, tab_im], axis=1)


def _s5_prep(are, aim, ldt, bre, bim, ctre, ctim):
    Q = S5_Q
    row = pl.BlockSpec((None, 8, S5_NS), lambda i: (i, 0, 0))
    blk = lambda a, b: pl.BlockSpec((None, a, b), lambda i: (i, 0, 0))
    return pl.pallas_call(
        _s5_prep_kernel,
        grid=(S5_NBLK,),
        in_specs=[row, row, row, blk(LANE, S5_NS), blk(LANE, S5_NS), blk(S5_NS, LANE), blk(S5_NS, LANE)],
        out_specs=[blk(Q * LANE, Q * LANE), blk(Q * LANE, 2 * S5_NS), blk(2 * S5_NS, Q * LANE),
                   blk(8, 2 * S5_NS), blk(8, 2 * S5_NS), blk(2 * S5_NS, LANE)],
        out_shape=[jax.ShapeDtypeStruct((S5_NBLK, Q * LANE, Q * LANE), BF16),
                   jax.ShapeDtypeStruct((S5_NBLK, Q * LANE, 2 * S5_NS), BF16),
                   jax.ShapeDtypeStruct((S5_NBLK, 2 * S5_NS, Q * LANE), BF16),
                   jax.ShapeDtypeStruct((S5_NBLK, 8, 2 * S5_NS), F32),
                   jax.ShapeDtypeStruct((S5_NBLK, 8, 2 * S5_NS), F32),
                   jax.ShapeDtypeStruct((S5_NBLK, 2 * S5_NS, LANE), BF16)],
        compiler_params=_cp("arbitrary"),
    )(are, aim, ldt, bre, bim, ctre, ctim)


def _s5_main_kernel(nb, u_ref, t_ref, wb_ref, cb_ref, lamq_ref, dsk_ref, g_ref, hfin_ref,
                    lhs_scr, s_scr, hp_scr):
    Q = S5_Q
    NS = S5_NS
    nrow = u_ref.shape[0] // Q
    nch = nrow // nb
    for s in range(Q):
        lhs_scr[:, s * LANE:(s + 1) * LANE] = u_ref[pl.ds(s, nrow, stride=Q), :].astype(BF16)
    lhs = lhs_scr[...]
    s_scr[...] = jnp.dot(lhs, wb_ref[...], preferred_element_type=F32)
    tab = lamq_ref[...]
    tab_re, tab_im = tab[:, :NS], tab[:, NS:]
    sub = lax.broadcasted_iota(jnp.int32, (8, 1), 0)

    def tile_body(t, carry):
        base = pl.multiple_of(t * 8, 8)
        new = []
        for b in range(nb):
            h_re, h_im = carry[2 * b], carry[2 * b + 1]
            st = s_scr[pl.ds(b * nch + base, 8), :]
            x_re, x_im = st[:, :NS], st[:, NS:]
            for sh in (1, 2, 4):
                y_re = jnp.where(sub >= sh, pltpu.roll(x_re, sh, 0), 0.0)
                y_im = jnp.where(sub >= sh, pltpu.roll(x_im, sh, 0), 0.0)
                d_re, d_im = _cmul(tab_re[sh - 1:sh, :], tab_im[sh - 1:sh, :], y_re, y_im)
                x_re, x_im = x_re + d_re, x_im + d_im
            c_re, c_im = _cmul(tab_re, tab_im, h_re, h_im)
            x_re, x_im = x_re + c_re, x_im + c_im
            p_re = jnp.where(sub == 0, h_re, pltpu.roll(x_re, 1, 0))
            p_im = jnp.where(sub == 0, h_im, pltpu.roll(x_im, 1, 0))
            hp_scr[pl.ds(b * nch + base, 8), :] = jnp.concatenate([p_re, p_im], axis=1)
            new += [x_re[7:8, :], x_im[7:8, :]]
        return tuple(new)

    zero = jnp.zeros((1, NS), F32)
    fin = lax.fori_loop(0, nch // 8, tile_body, (zero,) * (2 * nb))
    for b in range(nb):
        hfin_ref[b:b + 1, :] = jnp.concatenate([fin[2 * b], fin[2 * b + 1]], axis=1)

    y = (jnp.dot(lhs, t_ref[...], preferred_element_type=F32)
         + jnp.dot(hp_scr[...].astype(BF16), cb_ref[...], preferred_element_type=F32))
    for q in range(Q):
        rows = pl.ds(q, nrow, stride=Q)
        yq = y[:, q * LANE:(q + 1) * LANE] + dsk_ref[...] * u_ref[rows, :]
        g_ref[rows, :] = jax.nn.gelu(yq)


def _s5_main(u_gate, nb, t, wb, cb, lamq, dsk):
    Q = S5_Q
    M = u_gate.shape[0]
    nrow = M // Q
    blk = lambda a, b: pl.BlockSpec((None, a, b), lambda i: (i, 0, 0))
    return pl.pallas_call(
        functools.partial(_s5_main_kernel, nb),
        grid=(S5_NBLK,),
        in_specs=[pl.BlockSpec((M, LANE), lambda i: (0, i)),
                  blk(Q * LANE, Q * LANE), blk(Q * LANE, 2 * S5_NS), blk(2 * S5_NS, Q * LANE),
                  blk(8, 2 * S5_NS), pl.BlockSpec((1, LANE), lambda i: (0, i))],
        out_specs=[pl.BlockSpec((M, LANE), lambda i: (0, i)), blk(nb, 2 * S5_NS)],
        out_shape=[jax.ShapeDtypeStruct((M, E1), F32),
                   jax.ShapeDtypeStruct((S5_NBLK, nb, 2 * S5_NS), F32)],
        scratch_shapes=[pltpu.VMEM((nrow, Q * LANE), BF16), pltpu.VMEM((nrow, 2 * S5_NS), F32),
                        pltpu.VMEM((nrow, 2 * S5_NS), F32)],
        compiler_params=_cp("arbitrary"),
    )(u_gate, t, wb, cb, lamq, dsk)


def _s5_step_kernel(u_ref, h0_ref, bb_ref, cpl_ref, lam_ref, dsk_ref, g_ref, hn_ref):
    NS = S5_NS
    u = u_ref[...]
    s = jnp.dot(u.astype(BF16), bb_ref[...], preferred_element_type=F32)
    lam = lam_ref[0:1, :]
    h0 = h0_ref[...]
    n_re, n_im = _cmul(lam[:, :NS], lam[:, NS:], h0[:, :NS], h0[:, NS:])
    hn = jnp.concatenate([n_re + s[:, :NS], n_im + s[:, NS:]], axis=1)
    hn_ref[...] = hn
    y = jnp.dot(hn.astype(BF16), cpl_ref[...], preferred_element_type=F32) + dsk_ref[...] * u
    g_ref[...] = jax.nn.gelu(y)


def _s5_step(u_gate, h0, wb, cpl, lam, dsk):
    nb = u_gate.shape[0]
    blk = lambda a, b: pl.BlockSpec((None, a, b), lambda i: (i, 0, 0))
    return pl.pallas_call(
        _s5_step_kernel,
        grid=(S5_NBLK,),
        in_specs=[pl.BlockSpec((nb, LANE), lambda i: (0, i)), blk(nb, 2 * S5_NS),
                  pl.BlockSpec((None, LANE, 2 * S5_NS), lambda i: (i, S5_Q - 1, 0)),
                  blk(2 * S5_NS, LANE), blk(8, 2 * S5_NS), pl.BlockSpec((1, LANE), lambda i: (0, i))],
        out_specs=[pl.BlockSpec((nb, LANE), lambda i: (0, i)), blk(nb, 2 * S5_NS)],
        out_shape=[jax.ShapeDtypeStruct((nb, E1), F32),
                   jax.ShapeDtypeStruct((S5_NBLK, nb, 2 * S5_NS), F32)],
        compiler_params=_cp("arbitrary"),
    )(u_gate, h0, wb, cpl, lam, dsk)


def _state_to_blocks(st):
    nb = st.shape[0]
    x = st.reshape(nb, S5_NBLK, S5_NS, 2)
    return jnp.transpose(x, (1, 0, 3, 2)).reshape(S5_NBLK, nb, 2 * S5_NS)


def _blocks_to_state(h):
    nb = h.shape[1]
    x = h.reshape(S5_NBLK, nb, 2, S5_NS)
    return jnp.transpose(x, (1, 0, 3, 2)).reshape(nb, C_GROUPS, C_STATE, 2)


def kernel(x_prompt, x_sample, state_a_ssm, state_a_conv, cache_b_kv, state_c, norm_w, final_norm_w, even_w_in, a_conv_w, a_conv_b, a_dt_bias, a_A_log, a_D, a_norm_w, even_w_out, odd_w_in, c_A_re, c_A_im, c_B_re, c_B_im, c_C_re, c_C_im, c_D, c_log_dt, c_glu_w, c_glu_b, odd_w_out):
    B, L, D = x_prompt.shape
    nS = x_sample.shape[0]
    M = B * L
    Q = S5_Q

    w0 = even_w_in[0]
    s1 = A_WIDTH + A_CONV_DIM
    s2 = s1 + A_HEADS
    seg = lambda a, b: w0[:, a:b].astype(BF16)
    w_dt = jnp.pad(seg(s1, s2), ((0, 0), (0, TN - A_HEADS)))
    wcat = jnp.concatenate([seg(A_WIDTH, s1), w_dt, seg(s2, s2 + 2048), seg(s2 + 2048, s2 + 4096),
                            seg(s2 + 4096, s2 + 6144), seg(0, A_WIDTH), seg(s2 + 6144, s2 + 8192)], axis=1)
    w_out0 = even_w_out[0].astype(BF16)
    w_in1 = odd_w_in[0].astype(BF16)
    w_glu = c_glu_w[0].astype(BF16)
    w_out1 = odd_w_out[0].astype(BF16)
    nw0 = norm_w[0][None, :]
    nw1 = norm_w[1][None, :]
    fnw = final_norm_w[None, :]
    cw, cb = a_conv_w[0], a_conv_b[0][None, :]
    pad_h = lambda v: jnp.pad(v, (0, LANE - A_HEADS))[None, :]
    dtb, alog = pad_h(a_dt_bias[0]), pad_h(a_A_log[0])
    dsk_a = jnp.repeat(a_D[0], A_HEAD_DIM)[None, :]
    gnw = a_norm_w[0][None, :]
    head_expand = np.kron(np.eye(A_HEADS, dtype=np.float32), np.ones((1, A_HEAD_DIM), np.float32))
    head_expand = jnp.asarray(np.pad(head_expand, ((0, LANE - A_HEADS), (0, 0)))).astype(BF16)
    slopes_np = np.asarray(2.0 ** (-8.0 * np.arange(1, B_HEADS + 1) / B_HEADS), np.float32)
    slopes = jnp.asarray(np.broadcast_to(slopes_np[:, None, None], (B_HEADS, 8, LANE)))
    dist = np.stack([dil * (LANE - np.arange(LANE)) for _, dil in B_PATTERNS]).astype(np.float32)
    step_bias = jnp.asarray((dist[:, :, None] * slopes_np[None, None, :])[..., None])

    eye = jnp.eye(S5_BLK, dtype=F32)
    rows8 = lambda p: jnp.broadcast_to(p.reshape(S5_NBLK, 1, S5_NS), (S5_NBLK, 8, S5_NS))
    bexp = lambda p: jnp.einsum("kgnc,gh->kgchn", p.reshape(S5_NBLK, S5_BLK, C_STATE, C_GROUP),
                                eye).reshape(S5_NBLK, LANE, S5_NS)
    cexp = lambda p: jnp.einsum("kgcn,gh->kgnhc", p.reshape(S5_NBLK, S5_BLK, C_GROUP, C_STATE),
                                eye).reshape(S5_NBLK, S5_NS, LANE)
    ldt = jnp.repeat(c_log_dt[0], C_STATE).reshape(S5_NBLK, S5_NS)
    t_m, wb_m, cb_m, lam, lamq, cpl = _s5_prep(rows8(c_A_re[0]), rows8(c_A_im[0]), rows8(ldt),
                                                bexp(c_B_re[0]), bexp(c_B_im[0]),
                                                cexp(c_C_re[0]), cexp(c_C_im[0]))
    dsk_c = c_D[0][None, :]
    glu_b = c_glu_b[0][None, :]

    xs = x_sample.reshape(nS, D)
    proj = _mm_norm(xs, nw0, wcat, nS, 1536)
    col = lambda a, b: proj[:, a:b]
    hv = lambda a: col(a, a + 2048).reshape(nS, B_HEADS, B_HEAD_DIM)
    kvn = jnp.stack([hv(5632), hv(7680)], axis=1)

    xp = x_prompt.reshape(M, D)
    a_arr, q, k, v, kv_keep = _inproj0(xp, nw0, wcat, B, L)
    y_a, ssm_p, tail_p = _ssd(a_arr, B, L, cw, cb, dtb, alog, dsk_a, gnw)
    y_b, new_cache = _attn(q, k, v, slopes, a_arr, cache_b_kv, kvn)
    hp1 = _outproj0(y_a, y_b, w_out0, xp, 1024)
    u_gate = _mm_norm(hp1, nw1, w_in1, 1024, 1024)
    g, hfin_p = _s5_main(u_gate, B, t_m, wb_m, cb_m, lamq, dsk_c)
    y1 = _glu(g, w_glu, glu_b, u_gate, 512, BF16)
    y_prompt = _outproj1(y1, w_out1, hp1, fnw, 1024).reshape(B, L, D)

    y_as, ssm_s, conv_s = _ssd_step(col(0, 3072)[:, None, :], state_a_conv[0],
                                    col(3072, 3200)[:, None, :], col(9728, 11776)[:, None, :],
                                    state_a_ssm[0].reshape(nS, A_WIDTH, A_STATE),
                                    cw, cb, dtb, alog, head_expand, dsk_a, gnw)
    y_bs = _attn_step(hv(3584), hv(5632), hv(7680), hv(11776), cache_b_kv, step_bias)
    hs1 = _outproj0(y_as.reshape(nS, A_WIDTH), y_bs.reshape(nS, A_WIDTH), w_out0, xs, nS, 1024)
    u_gate_s = _mm_norm(hs1, nw1, w_in1, nS, 2048)
    g_s, hnew_s = _s5_step(u_gate_s, _state_to_blocks(state_c[0]), wb_m, cpl, lam, dsk_c)
    y1_s = _glu(g_s, w_glu, glu_b, u_gate_s, nS, F32, 1024)
    y_sample = _outproj1(y1_s, w_out1, hs1, fnw, nS).reshape(nS, 1, D)

    keep = kv_keep.shape[1]
    return (y_prompt, y_sample,
            ssm_p.reshape(1, B, A_HEADS, A_HEAD_DIM, A_STATE),
            tail_p[:, 5:8, :][None],
            kv_keep.reshape(1, B, keep, 2, B_HEADS, B_HEAD_DIM),
            _blocks_to_state(hfin_p)[None],
            ssm_s.reshape(1, nS, A_HEADS, A_HEAD_DIM, A_STATE),
            conv_s[None],
            new_cache,
            _blocks_to_state(hnew_s)[None])
```

```python
import functools

import numpy as np
import jax
import jax.numpy as jnp
from jax import lax
from jax.experimental import pallas as pl
from jax.experimental.pallas import tpu as pltpu

F32 = jnp.float32
BF16 = jnp.bfloat16

D_MODEL = 2048
EPS = 1e-5
A_WIDTH = 2048
A_HEAD_DIM = 64
A_HEADS = 32
A_GROUPS = 4
A_STATE = 128
A_CONV_W = 4
A_CONV_DIM = 3072
SSD_CHUNK = 128
B_HEADS = 16
B_HEAD_DIM = 128
B_PATTERNS = ((128, 1), (512, 4), (2048, 16))
B_MAX_WINDOW = 2048
C_GROUP = 16
C_GROUPS = 256
C_STATE = 64
E1 = 4096

LANE = 128
S5_Q = 8
S5_BLK = 8
S5_NBLK = C_GROUPS // S5_BLK
S5_NS = S5_BLK * C_STATE
TN = 512
ATT_BLOCK = 2048
NEG = -1e30
ATT_SCALE = B_HEAD_DIM ** -0.5
VMEM_LIMIT = 56 * 1024 * 1024

NT_DIMS = (((1,), (1,)), ((), ()))


def _cp(*sem):
    return pltpu.CompilerParams(dimension_semantics=sem, vmem_limit_bytes=VMEM_LIMIT)


def _silu(x):
    return x * jax.nn.sigmoid(x)


def _softplus(x):
    return jnp.maximum(x, 0.0) + jnp.log1p(jnp.exp(-jnp.abs(x)))


def _split3(a):
    hi = a.astype(BF16)
    r = a - hi.astype(F32)
    mid = r.astype(BF16)
    lo = (r - mid.astype(F32)).astype(BF16)
    return hi, mid, lo


def _dot3_left(m, a):
    hi, mid, lo = _split3(a)
    d = lambda p: jnp.dot(m, p, preferred_element_type=F32)
    return (d(lo) + d(mid)) + d(hi)


def _dot3_right(a, m):
    hi, mid, lo = _split3(a)
    d = lambda p: jnp.dot(p, m, preferred_element_type=F32)
    return (d(lo) + d(mid)) + d(hi)


def _rms(x, w):
    ms = jnp.mean(x * x, axis=-1, keepdims=True)
    return (x * lax.rsqrt(ms + EPS)) * w


def _inproj0_kernel(x_ref, nw_ref, w_ref, a_ref, q_ref, k_ref, v_ref, kv_ref, xn_ref):
    j = pl.program_id(1)

    @pl.when(j == 0)
    def _():
        xn_ref[...] = _rms(x_ref[...], nw_ref[...]).astype(BF16)

    def tile():
        return jnp.dot(xn_ref[...], w_ref[...], preferred_element_type=F32)

    def heads(ref, acc):
        for hh in range(TN // LANE):
            ref[hh] = acc[:, hh * LANE:(hh + 1) * LANE]

    @pl.when((j < 7) | (j >= 19))
    def _():
        a_ref[...] = tile()

    @pl.when((j >= 7) & (j < 11))
    def _():
        heads(q_ref, tile())

    @pl.when((j >= 11) & (j < 15))
    def _():
        acc = tile()
        heads(k_ref, acc)
        kv_ref[...] = acc

    @pl.when((j >= 15) & (j < 19))
    def _():
        acc = tile()
        heads(v_ref, acc)
        kv_ref[...] = acc


def _inproj0(x2d, nw, wcat, B, L):
    M = B * L
    tm = 1024
    rpb = L // tm
    keep = min(B_MAX_WINDOW, L)
    nskip = (L - keep) // tm
    nj = wcat.shape[1] // TN

    def a_map(i, j):
        return (i, jnp.where(j < 6, j, jnp.where(j < 19, 14, j - 13)))

    def head_map(j0):
        return lambda i, j: (i // rpb, jnp.clip(j - j0, 0, 3), i % rpb, 0)

    def kv_map(i, j):
        ri = i % rpb
        return (i // rpb, jnp.maximum(ri - nskip, 0), jnp.where(ri < nskip, 0, jnp.clip(j - 11, 0, 7)))

    hd = jax.ShapeDtypeStruct((B, B_HEADS, L, B_HEAD_DIM), F32)
    return pl.pallas_call(
        _inproj0_kernel,
        grid=(M // tm, nj),
        in_specs=[pl.BlockSpec((tm, D_MODEL), lambda i, j: (i, 0)),
                  pl.BlockSpec((1, D_MODEL), lambda i, j: (0, 0)),
                  pl.BlockSpec((D_MODEL, TN), lambda i, j: (0, j))],
        out_specs=[pl.BlockSpec((tm, TN), a_map),
                   pl.BlockSpec((None, TN // LANE, tm, LANE), head_map(7)),
                   pl.BlockSpec((None, TN // LANE, tm, LANE), head_map(11)),
                   pl.BlockSpec((None, TN // LANE, tm, LANE), head_map(15)),
                   pl.BlockSpec((None, tm, TN), kv_map)],
        out_shape=[jax.ShapeDtypeStruct((M, 15 * TN), F32), hd, hd, hd,
                   jax.ShapeDtypeStruct((B, keep, 8 * TN), F32)],
        scratch_shapes=[pltpu.VMEM((tm, D_MODEL), BF16)],
        compiler_params=_cp("arbitrary", "arbitrary"),
    )(x2d, nw, wcat)


def _mm_norm_kernel(x_ref, nw_ref, w_ref, o_ref, xn_ref):
    @pl.when(pl.program_id(1) == 0)
    def _():
        xn_ref[...] = _rms(x_ref[...], nw_ref[...]).astype(BF16)

    o_ref[...] = jnp.dot(xn_ref[...], w_ref[...], preferred_element_type=F32)


def _mm_norm(x, nw, w, tm, tn=TN):
    M = x.shape[0]
    K, N = w.shape
    return pl.pallas_call(
        _mm_norm_kernel,
        grid=(M // tm, N // tn),
        in_specs=[pl.BlockSpec((tm, K), lambda i, j: (i, 0)),
                  pl.BlockSpec((1, K), lambda i, j: (0, 0)),
                  pl.BlockSpec((K, tn), lambda i, j: (0, j))],
        out_specs=pl.BlockSpec((tm, tn), lambda i, j: (i, j)),
        out_shape=jax.ShapeDtypeStruct((M, N), F32),
        scratch_shapes=[pltpu.VMEM((tm, K), BF16)],
        compiler_params=_cp("arbitrary", "arbitrary"),
    )(x, nw, w)


def _outproj0_kernel(ya_ref, yb_ref, wa_ref, wb_ref, res_ref, o_ref):
    o_ref[...] = (res_ref[...]
                  + jnp.dot(ya_ref[...].astype(BF16), wa_ref[...], preferred_element_type=F32)
                  + jnp.dot(yb_ref[...].astype(BF16), wb_ref[...], preferred_element_type=F32))


def _outproj0(ya, yb, w, res, tm, tn=TN):
    M = ya.shape[0]
    K = ya.shape[1]
    return pl.pallas_call(
        _outproj0_kernel,
        grid=(M // tm, D_MODEL // tn),
        in_specs=[pl.BlockSpec((tm, K), lambda i, j: (i, 0)),
                  pl.BlockSpec((tm, K), lambda i, j: (i, 0)),
                  pl.BlockSpec((K, tn), lambda i, j: (0, j)),
                  pl.BlockSpec((K, tn), lambda i, j: (1, j)),
                  pl.BlockSpec((tm, tn), lambda i, j: (i, j))],
        out_specs=pl.BlockSpec((tm, tn), lambda i, j: (i, j)),
        out_shape=jax.ShapeDtypeStruct((M, D_MODEL), F32),
        compiler_params=_cp("arbitrary", "arbitrary"),
    )(ya, yb, w, w, res)


def _glu_kernel(g_ref, wa_ref, wb_ref, ba_ref, bb_ref, gate_ref, o_ref):
    tm = g_ref.shape[0]
    hr = min(tm, 512)
    for r in range(tm // hr):
        rs = pl.ds(r * hr, hr)
        g = g_ref[rs, :].astype(BF16)
        ga = jnp.dot(g, wa_ref[...], preferred_element_type=F32) + ba_ref[...]
        gb = jnp.dot(g, wb_ref[...], preferred_element_type=F32) + bb_ref[...]
        o_ref[rs, :] = (ga * jax.nn.sigmoid(gb) * _silu(gate_ref[rs, :])).astype(o_ref.dtype)


def _glu(g, w, bias, u_gate, tm, out_dtype, tn=TN):
    M = g.shape[0]
    nb = E1 // tn
    return pl.pallas_call(
        _glu_kernel,
        grid=(M // tm, nb),
        in_specs=[pl.BlockSpec((tm, E1), lambda i, j: (i, 0)),
                  pl.BlockSpec((E1, tn), lambda i, j: (0, j)),
                  pl.BlockSpec((E1, tn), lambda i, j: (0, j + nb)),
                  pl.BlockSpec((1, tn), lambda i, j: (0, j)),
                  pl.BlockSpec((1, tn), lambda i, j: (0, j + nb)),
                  pl.BlockSpec((tm, tn), lambda i, j: (i, j + nb))],
        out_specs=pl.BlockSpec((tm, tn), lambda i, j: (i, j)),
        out_shape=jax.ShapeDtypeStruct((M, E1), out_dtype),
        compiler_params=_cp("arbitrary", "arbitrary"),
    )(g, w, w, bias, bias, u_gate)


def _outproj1_kernel(y_ref, w_ref, res_ref, fnw_ref, o_ref):
    j = pl.program_id(1)
    nj = D_MODEL // TN
    h = res_ref[...] + jnp.dot(y_ref[...].astype(BF16), w_ref[...], preferred_element_type=F32)
    for jj in range(nj):
        @pl.when(j == jj)
        def _(jj=jj):
            o_ref[:, jj * TN:(jj + 1) * TN] = h

    @pl.when(j == nj - 1)
    def _():
        o_ref[...] = _rms(o_ref[...], fnw_ref[...])


def _outproj1(y, w, res, fnw, tm):
    M, K = y.shape
    return pl.pallas_call(
        _outproj1_kernel,
        grid=(M // tm, D_MODEL // TN),
        in_specs=[pl.BlockSpec((tm, K), lambda i, j: (i, 0)),
                  pl.BlockSpec((K, TN), lambda i, j: (0, j)),
                  pl.BlockSpec((tm, TN), lambda i, j: (i, j)),
                  pl.BlockSpec((1, D_MODEL), lambda i, j: (0, 0))],
        out_specs=pl.BlockSpec((tm, D_MODEL), lambda i, j: (i, 0)),
        out_shape=jax.ShapeDtypeStruct((M, D_MODEL), F32),
        compiler_params=_cp("arbitrary", "arbitrary"),
    )(y, w, res, fnw)


def _ssd_kernel(xbc_ref, dt_ref, z0_ref, z1_ref, cw_ref, cb_ref, dtb_ref, alog_ref, dsk_ref, gnw_ref,
                y_ref, st_ref, tail_ref, h_scr, tail_scr, y_scr):
    c = pl.program_id(1)
    Q = SSD_CHUNK

    @pl.when(c == 0)
    def _():
        h_scr[...] = jnp.zeros_like(h_scr)
        tail_scr[...] = jnp.zeros_like(tail_scr)

    x_raw = xbc_ref[...]
    tail8 = tail_scr[...]
    row8 = lax.broadcasted_iota(jnp.int32, (8, 1), 0)
    conv = cb_ref[...] + cw_ref[3:4, :] * x_raw
    for k in (1, 2, 3):
        xr = pltpu.roll(x_raw, k, 0)
        first = jnp.where(row8 < k, pltpu.roll(tail8, k, 0), xr[0:8, :])
        conv = conv + cw_ref[3 - k:4 - k, :] * jnp.concatenate([first, xr[8:, :]], axis=0)
    tail_scr[...] = x_raw[Q - 8:Q, :]
    xc = _silu(conv)

    dt = _softplus(dt_ref[...] + dtb_ref[...])
    a = dt * (-jnp.exp(alog_ref[...]))
    qi = lax.broadcasted_iota(jnp.int32, (Q, Q), 0)
    si = lax.broadcasted_iota(jnp.int32, (Q, Q), 1)
    tri = qi >= si
    a_cum = _dot3_left(tri.astype(BF16), a)
    a_cum_t = a_cum.T
    lo_half = si < A_HEAD_DIM

    for g in range(A_GROUPS):
        bg = xc[:, A_WIDTH + g * A_STATE:A_WIDTH + (g + 1) * A_STATE].astype(BF16)
        cg = xc[:, A_WIDTH + (A_GROUPS + g) * A_STATE:A_WIDTH + (A_GROUPS + g + 1) * A_STATE].astype(BF16)
        cbm = lax.dot_general(cg, bg, NT_DIMS, preferred_element_type=F32)
        hprev = h_scr[g * 512:(g + 1) * 512, :]
        yoff = lax.dot_general(cg, hprev.astype(BF16), NT_DIMS, preferred_element_type=F32)
        for jj in range(4):
            p = g * 4 + jj
            h0, h1 = 2 * p, 2 * p + 1
            dtp = jnp.where(lo_half, dt[:, h0:h0 + 1], dt[:, h1:h1 + 1])
            ap = jnp.where(lo_half, a_cum[:, h0:h0 + 1], a_cum[:, h1:h1 + 1])
            x_p = xc[:, p * LANE:(p + 1) * LANE]
            xdt = x_p * dtp
            g0 = cbm * jnp.exp(jnp.where(tri, a_cum[:, h0:h0 + 1] - a_cum_t[h0:h0 + 1, :], NEG))
            g1 = cbm * jnp.exp(jnp.where(tri, a_cum[:, h1:h1 + 1] - a_cum_t[h1:h1 + 1, :], NEG))
            lhs = jnp.concatenate([g0, g1], axis=1).astype(BF16)
            rhs = jnp.concatenate([jnp.where(lo_half, xdt, 0.0), jnp.where(lo_half, 0.0, xdt)],
                                  axis=0).astype(BF16)
            ydiag = jnp.dot(lhs, rhs, preferred_element_type=F32)
            y_scr[:, p * LANE:(p + 1) * LANE] = (
                ydiag + yoff[:, jj * LANE:(jj + 1) * LANE] * jnp.exp(ap)
                + dsk_ref[:, p * LANE:(p + 1) * LANE] * x_p)
            xd = xdt * jnp.exp(ap[Q - 1:Q, :] - ap)
            st = jnp.dot(xd.T.astype(BF16), bg, preferred_element_type=F32)
            cd = jnp.exp(jnp.where(qi < A_HEAD_DIM, a_cum_t[h0:h0 + 1, Q - 1:Q],
                                   a_cum_t[h1:h1 + 1, Q - 1:Q]))
            h_scr[p * LANE:(p + 1) * LANE, :] = cd * hprev[jj * LANE:(jj + 1) * LANE, :] + st

    z = jnp.concatenate([z0_ref[...], z1_ref[...]], axis=1)
    y_ref[...] = _rms(y_scr[...] * _silu(z), gnw_ref[...]).astype(y_ref.dtype)

    @pl.when(c == pl.num_programs(1) - 1)
    def _():
        st_ref[...] = h_scr[...]
        tail_ref[...] = x_raw[Q - 8:Q, :]


def _ssd(a_arr, B, L, cw, cb, dtb, alog, dsk, gnw):
    nc = L // SSD_CHUNK
    Q = SSD_CHUNK
    row = lambda b, c: b * nc + c
    full = lambda shp: pl.BlockSpec(shp, lambda b, c: (0, 0))
    return pl.pallas_call(
        _ssd_kernel,
        grid=(B, nc),
        in_specs=[pl.BlockSpec((Q, A_CONV_DIM), lambda b, c: (row(b, c), 0)),
                  pl.BlockSpec((Q, LANE), lambda b, c: (row(b, c), 14 * TN // LANE)),
                  pl.BlockSpec((Q, A_WIDTH // 2), lambda b, c: (row(b, c), 3)),
                  pl.BlockSpec((Q, A_WIDTH // 2), lambda b, c: (row(b, c), 4)),
                  full((A_CONV_W, A_CONV_DIM)), full((1, A_CONV_DIM)), full((1, LANE)),
                  full((1, LANE)), full((1, A_WIDTH)), full((1, A_WIDTH))],
        out_specs=[pl.BlockSpec((Q, A_WIDTH), lambda b, c: (row(b, c), 0)),
                   pl.BlockSpec((None, A_WIDTH, A_STATE), lambda b, c: (b, 0, 0)),
                   pl.BlockSpec((None, 8, A_CONV_DIM), lambda b, c: (b, 0, 0))],
        out_shape=[jax.ShapeDtypeStruct((B * L, A_WIDTH), BF16),
                   jax.ShapeDtypeStruct((B, A_WIDTH, A_STATE), F32),
                   jax.ShapeDtypeStruct((B, 8, A_CONV_DIM), F32)],
        scratch_shapes=[pltpu.VMEM((A_WIDTH, A_STATE), F32), pltpu.VMEM((8, A_CONV_DIM), F32),
                        pltpu.VMEM((Q, A_WIDTH), F32)],
        compiler_params=_cp("arbitrary", "arbitrary"),
    )(a_arr, a_arr, a_arr, a_arr, cw, cb, dtb, alog, dsk, gnw)


def _ssd_step_kernel(xbc_ref, cs_ref, dt_ref, z_ref, h0_ref, cw_ref, cb_ref, dtb_ref, alog_ref,
                     e_ref, dsk_ref, gnw_ref, y_ref, hn_ref, nc_ref):
    x_raw = xbc_ref[...]
    cs = cs_ref[...]
    conv = (cb_ref[...] + cw_ref[0:1, :] * cs[0:1, :] + cw_ref[1:2, :] * cs[1:2, :]
            + cw_ref[2:3, :] * cs[2:3, :] + cw_ref[3:4, :] * x_raw)
    xc = _silu(conv)
    nc_ref[0:1, :] = cs[1:2, :]
    nc_ref[1:2, :] = cs[2:3, :]
    nc_ref[2:3, :] = x_raw

    dt = _softplus(dt_ref[...] + dtb_ref[...])
    da = dt * (-jnp.exp(alog_ref[...]))
    e = e_ref[...]
    dt_e = _dot3_right(jnp.broadcast_to(dt, (8, LANE)), e)[0:1, :]
    da_e = _dot3_right(jnp.broadcast_to(da, (8, LANE)), e)[0:1, :]
    xs = xc[:, :A_WIDTH]
    sub = lax.broadcasted_iota(jnp.int32, (8, 1), 0)
    rows = jnp.where(sub == 0, xs * dt_e, jnp.where(sub == 1, jnp.exp(da_e), 0.0))
    cols = jnp.concatenate([rows, jnp.zeros((LANE - 8, A_WIDTH), F32)], axis=0).T
    xdt_col = cols[:, 0:1]
    dec_col = cols[:, 1:2]
    ys = []
    for g in range(A_GROUPS):
        bg = xc[:, A_WIDTH + g * A_STATE:A_WIDTH + (g + 1) * A_STATE]
        cg = xc[:, A_WIDTH + (A_GROUPS + g) * A_STATE:A_WIDTH + (A_GROUPS + g + 1) * A_STATE]
        sl = slice(g * 512, (g + 1) * 512)
        hn = dec_col[sl, :] * h0_ref[sl, :] + xdt_col[sl, :] * bg
        hn_ref[sl, :] = hn
        c8 = jnp.broadcast_to(cg, (8, A_STATE)).astype(BF16)
        ys.append(lax.dot_general(c8, hn.astype(BF16), NT_DIMS, preferred_element_type=F32)[0:1, :])
    y = jnp.concatenate(ys, axis=1) + dsk_ref[...] * xs
    y_ref[...] = _rms(y * _silu(z_ref[...]), gnw_ref[...])


def _ssd_step(xbc, cs, dt, z, h0, cw, cb, dtb, alog, e, dsk, gnw):
    nb = xbc.shape[0]
    per_b = lambda n: pl.BlockSpec((None, 1, n), lambda b: (b, 0, 0))
    full = lambda shp: pl.BlockSpec(shp, lambda b: (0, 0))
    return pl.pallas_call(
        _ssd_step_kernel,
        grid=(nb,),
        in_specs=[per_b(A_CONV_DIM), pl.BlockSpec((None, 3, A_CONV_DIM), lambda b: (b, 0, 0)),
                  per_b(LANE), per_b(A_WIDTH),
                  pl.BlockSpec((None, A_WIDTH, A_STATE), lambda b: (b, 0, 0)),
                  full((A_CONV_W, A_CONV_DIM)), full((1, A_CONV_DIM)), full((1, LANE)),
                  full((1, LANE)), full((LANE, A_WIDTH)), full((1, A_WIDTH)), full((1, A_WIDTH))],
        out_specs=[per_b(A_WIDTH),
                   pl.BlockSpec((None, A_WIDTH, A_STATE), lambda b: (b, 0, 0)),
                   pl.BlockSpec((None, 3, A_CONV_DIM), lambda b: (b, 0, 0))],
        out_shape=[jax.ShapeDtypeStruct((nb, 1, A_WIDTH), F32),
                   jax.ShapeDtypeStruct((nb, A_WIDTH, A_STATE), F32),
                   jax.ShapeDtypeStruct((nb, 3, A_CONV_DIM), F32)],
        compiler_params=_cp("arbitrary"),
    )(xbc, cs, dt, z, h0, cw, cb, dtb, alog, e, dsk, gnw)


def _bias_tile(slope, dil):
    i = lax.broadcasted_iota(jnp.int32, (LANE, 2 * LANE), 0)
    c = lax.broadcasted_iota(jnp.int32, (LANE, 2 * LANE), 1)
    j = LANE + i - c
    valid = (j >= 0) & (j <= LANE)
    return jnp.where(valid, -(slope * (dil * j).astype(F32)), NEG), c


def _attn_tile(q, kcat, vcat, bias):
    s = lax.dot_general(q, kcat, NT_DIMS, preferred_element_type=F32) * ATT_SCALE + bias
    m = jnp.max(s, axis=1, keepdims=True)
    p = jnp.exp(s - m)
    l = jnp.sum(p, axis=1, keepdims=True)
    acc = jnp.dot(p.astype(BF16), vcat, preferred_element_type=F32)
    return m, l, acc


def _rows(start, stride):
    return pl.ds(start, LANE, stride=stride) if stride > 1 else pl.ds(start, LANE)


def _shift_cache_block(step, steps_per_seq, x_ref, nxt_ref, kvn_ref, o_ref):
    R = x_ref.shape[0]
    o_ref[pl.ds(0, R - 1)] = x_ref[pl.ds(1, R - 1)]
    last = step % steps_per_seq == steps_per_seq - 1
    o_ref[R - 1] = jnp.where(last, kvn_ref[...], nxt_ref[0])


def _attn_kernel(steps_per_seq, slope_ref, q_ref, k_ref, v_ref, kp_ref, vp_ref, gate_ref,
                 x_ref, nxt_ref, kvn_ref, y_ref, oc_ref, m_scr, l_scr, acc_scr):
    step = ((pl.program_id(0) * pl.num_programs(1) + pl.program_id(1)) * pl.num_programs(2)
            + pl.program_id(2))
    _shift_cache_block(step, steps_per_seq, x_ref, nxt_ref, kvn_ref, oc_ref)
    first_block = pl.program_id(2) == 0
    slope = slope_ref[0:1, 0:1]
    for _, dil in B_PATTERNS:
        bias_full, c = _bias_tile(slope, dil)
        bias0 = jnp.where((c >= LANE) | jnp.logical_not(first_block), bias_full, NEG)
        nsub = ATT_BLOCK // dil // LANE
        for r in range(dil):
            kc = vc = None
            for sub in range(nsub):
                rows = _rows(sub * LANE * dil + r, dil)
                if sub == 0:
                    prow = _rows((nsub - 1) * LANE * dil + r, dil)
                    kp, vp, bias = kp_ref[prow, :], vp_ref[prow, :], bias0
                else:
                    kp, vp, bias = kc, vc, bias_full
                kc, vc = k_ref[rows, :], v_ref[rows, :]
                kcat = jnp.concatenate([kp, kc], axis=0).astype(BF16)
                vcat = jnp.concatenate([vp, vc], axis=0).astype(BF16)
                m, l, acc = _attn_tile(q_ref[rows, :].astype(BF16), kcat, vcat, bias)
                if dil == 1:
                    m_scr[rows, :] = jnp.broadcast_to(m, (LANE, LANE))
                    l_scr[rows, :] = jnp.broadcast_to(l, (LANE, LANE))
                    acc_scr[rows, :] = acc
                else:
                    m_old = m_scr[rows, :]
                    m_new = jnp.maximum(m_old, m)
                    a_old = jnp.exp(m_old - m_new)
                    a_new = jnp.exp(m - m_new)
                    m_scr[rows, :] = m_new
                    l_scr[rows, :] = a_old * l_scr[rows, :] + a_new * l
                    acc_scr[rows, :] = a_old * acc_scr[rows, :] + a_new * acc
    for sub in range(ATT_BLOCK // LANE):
        rs = pl.ds(sub * LANE, LANE)
        o = acc_scr[rs, :] / l_scr[rs, :]
        y_ref[rs, :] = (o * _silu(gate_ref[rs, :])).astype(y_ref.dtype)


def _attn(q, k, v, slopes, a_arr, cache6, kvn):
    B, H, L, dh = q.shape
    R = ATT_BLOCK
    nblk = L // R
    cur = pl.BlockSpec((None, None, R, dh), lambda b, h, i: (b, h, i, 0))
    prev = pl.BlockSpec((None, None, R, dh), lambda b, h, i: (b, h, jnp.maximum(i - 1, 0), 0))
    slope = pl.BlockSpec((None, 8, LANE), lambda b, h, i: (h, 0, 0))
    gate = pl.BlockSpec((R, LANE), lambda b, h, i: (b * nblk + i, 10 * TN // LANE + h))

    nb, W = cache6.shape[1], cache6.shape[2]
    steps = B * H * nblk
    per = steps // nb
    rows = W // per
    assert per * nb == steps and rows * per == W, (steps, nb, W)
    tail = (2, B_HEADS, B_HEAD_DIM)
    step = lambda b, h, i: (b * H + h) * nblk + i
    cblk = pl.BlockSpec((None, None, rows) + tail,
                        lambda b, h, i: (0, step(b, h, i) // per, step(b, h, i) % per, 0, 0, 0))
    cnext = pl.BlockSpec((None, None, 1) + tail,
                         lambda b, h, i: (0, step(b, h, i) // per,
                                          jnp.minimum((step(b, h, i) % per + 1) * rows, W - 1), 0, 0, 0))
    cnew = pl.BlockSpec((None,) + tail, lambda b, h, i: (step(b, h, i) // per, 0, 0, 0))
    return pl.pallas_call(
        functools.partial(_attn_kernel, per),
        grid=(B, H, nblk),
        in_specs=[slope, cur, cur, cur, prev, prev, gate, cblk, cnext, cnew],
        out_specs=[pl.BlockSpec((R, LANE), lambda b, h, i: (b * nblk + i, h)), cblk],
        out_shape=[jax.ShapeDtypeStruct((B * L, H * dh), BF16),
                   jax.ShapeDtypeStruct(cache6.shape, F32)],
        scratch_shapes=[pltpu.VMEM((R, LANE), F32)] * 3,
        compiler_params=_cp("arbitrary", "arbitrary", "arbitrary"),
    )(slopes, q, k, v, k, v, a_arr, cache6, cache6, kvn)


def _attn_step_kernel(bias_ref, q_ref, kn_ref, vn_ref, gate_ref, c1_ref, c4_ref, c16_ref, y_ref):
    q, kn, vn = q_ref[...], kn_ref[...], vn_ref[...]
    s_self = jnp.sum(q * kn, axis=-1, keepdims=True) * ATT_SCALE
    blocks = (c1_ref, c4_ref, c16_ref)
    scores = []
    for p, c_ref in enumerate(blocks):
        kblk = c_ref[:, 0]
        scores.append(jnp.sum(kblk * q[None], axis=-1, keepdims=True) * ATT_SCALE - bias_ref[p])
    m = s_self
    for s in scores:
        m = jnp.maximum(m, jnp.max(s, axis=0))
    p_self = float(len(B_PATTERNS)) * jnp.exp(s_self - m)
    l = p_self
    o = p_self * vn
    for s, c_ref in zip(scores, blocks):
        p = jnp.exp(s - m[None])
        l = l + jnp.sum(p, axis=0)
        o = o + jnp.sum(p * c_ref[:, 1], axis=0)
    y_ref[...] = (o / l) * _silu(gate_ref[...])


def _attn_step(q, kn, vn, gate, cache6, bias):
    nb = q.shape[0]
    W = B_MAX_WINDOW
    hspec = pl.BlockSpec((None, B_HEADS, B_HEAD_DIM), lambda b: (b, 0, 0))
    tail = (2, B_HEADS, B_HEAD_DIM)
    views, specs = [], []
    for _, dil in B_PATTERNS:
        if dil == 1:
            views.append(cache6)
            specs.append(pl.BlockSpec((None, None, LANE) + tail,
                                      lambda b: (0, b, W // LANE - 1, 0, 0, 0)))
        else:
            views.append(cache6.reshape((1, nb, W // dil, dil) + tail))
            specs.append(pl.BlockSpec((None, None, LANE, None) + tail,
                                      lambda b, dil=dil: (0, b, W // dil // LANE - 1, 0, 0, 0, 0)))
    return pl.pallas_call(
        _attn_step_kernel,
        grid=(nb,),
        in_specs=[pl.BlockSpec(bias.shape, lambda b: (0, 0, 0, 0)), hspec, hspec, hspec, hspec]
                 + specs,
        out_specs=hspec,
        out_shape=jax.ShapeDtypeStruct((nb, B_HEADS, B_HEAD_DIM), F32),
        compiler_params=_cp("arbitrary"),
    )(bias, q, kn, vn, gate, *views)


def _cmul(ar, ai, br, bi):
    return ar * br - ai * bi, ar * bi + ai * br


def _s5_prep_kernel(are_ref, aim_ref, ldt_ref, bre_ref, bim_ref, ctre_ref, ctim_ref,
                    t_ref, wb_ref, cb_ref, lam_ref, lamq_ref, cpl_ref):
    Q = S5_Q
    are, aim = are_ref[...], aim_ref[...]
    dt = jnp.exp(ldt_ref[...])
    mag = jnp.exp(dt * are)
    l_re = mag * jnp.cos(dt * aim)
    l_im = mag * jnp.sin(dt * aim)
    den = are * are + aim * aim
    f_re = ((l_re - 1.0) * are + l_im * aim) / den
    f_im = (l_im * are - (l_re - 1.0) * aim) / den
    lr, li = l_re[0:1, :], l_im[0:1, :]
    bb_re, bb_im = _cmul(f_re[0:1, :], f_im[0:1, :], bre_ref[...], bim_ref[...])
    ctre, ctim = ctre_ref[...], ctim_ref[...]
    cplain = jnp.concatenate([ctre, -ctim], axis=0).astype(BF16)
    cpl_ref[...] = cplain
    stk = jnp.concatenate([l_re, l_im, jnp.zeros((LANE - 16, S5_NS), F32)], axis=0).T
    lc_re, lc_im = stk[:, 0:1], stk[:, 8:9]
    p_re, p_im = jnp.ones((1, S5_NS), F32), jnp.zeros((1, S5_NS), F32)
    pc_re, pc_im = lc_re, lc_im
    dts = []
    for tau in range(Q):
        e_re, e_im = _cmul(bb_re, bb_im, p_re, p_im)
        wp = jnp.concatenate([e_re, e_im], axis=1).astype(BF16)
        s = Q - 1 - tau
        wb_ref[s * LANE:(s + 1) * LANE, :] = wp
        dts.append(jnp.dot(wp, cplain, preferred_element_type=F32).astype(BF16))
        c_re, c_im = _cmul(ctre, ctim, pc_re, pc_im)
        cb_ref[:, tau * LANE:(tau + 1) * LANE] = jnp.concatenate([c_re, -c_im], axis=0).astype(BF16)
        p_re, p_im = _cmul(p_re, p_im, lr, li)
        pc_re, pc_im = _cmul(pc_re, pc_im, lc_re, lc_im)
    zero = jnp.zeros((LANE, LANE), BF16)
    for s in range(Q):
        for q in range(Q):
            t_ref[s * LANE:(s + 1) * LANE, q * LANE:(q + 1) * LANE] = dts[q - s] if q >= s else zero
    lam_ref[...] = jnp.concatenate([l_re, l_im], axis=1)
    sub = lax.broadcasted_iota(jnp.int32, (8, 1), 0)
    q_re, q_im = p_re, p_im
    tab_re, tab_im = jnp.zeros((8, S5_NS), F32), jnp.zeros((8, S5_NS), F32)
    for r in range(8):
        tab_re = jnp.where(sub == r, q_re, tab_re)
        tab_im = jnp.where(sub == r, q_im, tab_im)
        q_re, q_im = _cmul(q_re, q_im, p_re, p_im)
    lamq_ref[...] = jnp.concatenate([tab_re, tab_im], axis=1)


def _s5_prep(are, aim, ldt, bre, bim, ctre, ctim):
    Q = S5_Q
    row = pl.BlockSpec((None, 8, S5_NS), lambda i: (i, 0, 0))
    blk = lambda a, b: pl.BlockSpec((None, a, b), lambda i: (i, 0, 0))
    return pl.pallas_call(
        _s5_prep_kernel,
        grid=(S5_NBLK,),
        in_specs=[row, row, row, blk(LANE, S5_NS), blk(LANE, S5_NS), blk(S5_NS, LANE), blk(S5_NS, LANE)],
        out_specs=[blk(Q * LANE, Q * LANE), blk(Q * LANE, 2 * S5_NS), blk(2 * S5_NS, Q * LANE),
                   blk(8, 2 * S5_NS), blk(8, 2 * S5_NS), blk(2 * S5_NS, LANE)],
        out_shape=[jax.ShapeDtypeStruct((S5_NBLK, Q * LANE, Q * LANE), BF16),
                   jax.ShapeDtypeStruct((S5_NBLK, Q * LANE, 2 * S5_NS), BF16),
                   jax.ShapeDtypeStruct((S5_NBLK, 2 * S5_NS, Q * LANE), BF16),
                   jax.ShapeDtypeStruct((S5_NBLK, 8, 2 * S5_NS), F32),
                   jax.ShapeDtypeStruct((S5_NBLK, 8, 2 * S5_NS), F32),
                   jax.ShapeDtypeStruct((S5_NBLK, 2 * S5_NS, LANE), BF16)],
        compiler_params=_cp("arbitrary"),
    )(are, aim, ldt, bre, bim, ctre, ctim)


def _s5_main_kernel(nb, u_ref, t_ref, wb_ref, cb_ref, lamq_ref, dsk_ref, g_ref, hfin_ref,
                    lhs_scr, s_scr, hp_scr, gs_scr):
    Q = S5_Q
    NS = S5_NS
    nrow = u_ref.shape[0] // Q
    nch = nrow // nb
    for s in range(Q):
        lhs_scr[:, s * LANE:(s + 1) * LANE] = u_ref[pl.ds(s, nrow, stride=Q), :].astype(BF16)
    lhs = lhs_scr[...]
    s_scr[...] = jnp.dot(lhs, wb_ref[...], preferred_element_type=F32)
    tab = lamq_ref[...]
    tab_re, tab_im = tab[:, :NS], tab[:, NS:]
    sub = lax.broadcasted_iota(jnp.int32, (8, 1), 0)

    def tile_body(t, carry):
        base = t * 8
        new = []
        for b in range(nb):
            h_re, h_im = carry[2 * b], carry[2 * b + 1]
            st = s_scr[pl.ds(b * nch + base, 8), :]
            x_re, x_im = st[:, :NS], st[:, NS:]
            for sh in (1, 2, 4):
                y_re = jnp.where(sub >= sh, pltpu.roll(x_re, sh, 0), 0.0)
                y_im = jnp.where(sub >= sh, pltpu.roll(x_im, sh, 0), 0.0)
                d_re, d_im = _cmul(tab_re[sh - 1:sh, :], tab_im[sh - 1:sh, :], y_re, y_im)
                x_re, x_im = x_re + d_re, x_im + d_im
            c_re, c_im = _cmul(tab_re, tab_im, h_re, h_im)
            x_re, x_im = x_re + c_re, x_im + c_im
            p_re = jnp.where(sub == 0, h_re, pltpu.roll(x_re, 1, 0))
            p_im = jnp.where(sub == 0, h_im, pltpu.roll(x_im, 1, 0))
            hp_scr[pl.ds(b * nch + base, 8), :] = jnp.concatenate([p_re, p_im], axis=1)
            new += [x_re[7:8, :], x_im[7:8, :]]
        return tuple(new)

    fin = (jnp.zeros((1, NS), F32),) * (2 * nb)
    for t in range(nch // 8):
        fin = tile_body(t, fin)
    for b in range(nb):
        hfin_ref[b:b + 1, :] = jnp.concatenate([fin[2 * b], fin[2 * b + 1]], axis=1)

    y = (jnp.dot(lhs, t_ref[...], preferred_element_type=F32)
         + jnp.dot(hp_scr[...].astype(BF16), cb_ref[...], preferred_element_type=F32))
    for q in range(Q):
        rows = pl.ds(q, nrow, stride=Q)
        yq = y[:, q * LANE:(q + 1) * LANE] + dsk_ref[...] * u_ref[rows, :]
        gs_scr[rows, :] = jax.nn.gelu(yq)
    g_ref[...] = gs_scr[...].astype(g_ref.dtype)


def _s5_main(u_gate, nb, t, wb, cb, lamq, dsk):
    Q = S5_Q
    M = u_gate.shape[0]
    nrow = M // Q
    blk = lambda a, b: pl.BlockSpec((None, a, b), lambda i: (i, 0, 0))
    return pl.pallas_call(
        functools.partial(_s5_main_kernel, nb),
        grid=(S5_NBLK,),
        in_specs=[pl.BlockSpec((M, LANE), lambda i: (0, i)),
                  blk(Q * LANE, Q * LANE), blk(Q * LANE, 2 * S5_NS), blk(2 * S5_NS, Q * LANE),
                  blk(8, 2 * S5_NS), pl.BlockSpec((1, LANE), lambda i: (0, i))],
        out_specs=[pl.BlockSpec((M, LANE), lambda i: (0, i)), blk(nb, 2 * S5_NS)],
        out_shape=[jax.ShapeDtypeStruct((M, E1), BF16),
                   jax.ShapeDtypeStruct((S5_NBLK, nb, 2 * S5_NS), F32)],
        scratch_shapes=[pltpu.VMEM((nrow, Q * LANE), BF16), pltpu.VMEM((nrow, 2 * S5_NS), F32),
                        pltpu.VMEM((nrow, 2 * S5_NS), F32), pltpu.VMEM((M, LANE), F32)],
        compiler_params=_cp("arbitrary"),
    )(u_gate, t, wb, cb, lamq, dsk)


def _s5_step_kernel(u_ref, h0_ref, bb_ref, cpl_ref, lam_ref, dsk_ref, g_ref, hn_ref):
    NS = S5_NS
    u = u_ref[...]
    s = jnp.dot(u.astype(BF16), bb_ref[...], preferred_element_type=F32)
    lam = lam_ref[0:1, :]
    h0 = h0_ref[...]
    n_re, n_im = _cmul(lam[:, :NS], lam[:, NS:], h0[:, :NS], h0[:, NS:])
    hn = jnp.concatenate([n_re + s[:, :NS], n_im + s[:, NS:]], axis=1)
    hn_ref[...] = hn
    y = jnp.dot(hn.astype(BF16), cpl_ref[...], preferred_element_type=F32) + dsk_ref[...] * u
    g_ref[...] = jax.nn.gelu(y)


def _s5_step(u_gate, h0, wb, cpl, lam, dsk):
    nb = u_gate.shape[0]
    blk = lambda a, b: pl.BlockSpec((None, a, b), lambda i: (i, 0, 0))
    return pl.pallas_call(
        _s5_step_kernel,
        grid=(S5_NBLK,),
        in_specs=[pl.BlockSpec((nb, LANE), lambda i: (0, i)), blk(nb, 2 * S5_NS),
                  pl.BlockSpec((None, LANE, 2 * S5_NS), lambda i: (i, S5_Q - 1, 0)),
                  blk(2 * S5_NS, LANE), blk(8, 2 * S5_NS), pl.BlockSpec((1, LANE), lambda i: (0, i))],
        out_specs=[pl.BlockSpec((nb, LANE), lambda i: (0, i)), blk(nb, 2 * S5_NS)],
        out_shape=[jax.ShapeDtypeStruct((nb, E1), F32),
                   jax.ShapeDtypeStruct((S5_NBLK, nb, 2 * S5_NS), F32)],
        compiler_params=_cp("arbitrary"),
    )(u_gate, h0, wb, cpl, lam, dsk)


def _state_to_blocks(st):
    nb = st.shape[0]
    x = st.reshape(nb, S5_NBLK, S5_NS, 2)
    return jnp.transpose(x, (1, 0, 3, 2)).reshape(S5_NBLK, nb, 2 * S5_NS)


def _blocks_to_state(h):
    nb = h.shape[1]
    x = h.reshape(S5_NBLK, nb, 2, S5_NS)
    return jnp.transpose(x, (1, 0, 3, 2)).reshape(nb, C_GROUPS, C_STATE, 2)


def kernel(x_prompt, x_sample, state_a_ssm, state_a_conv, cache_b_kv, state_c, norm_w, final_norm_w, even_w_in, a_conv_w, a_conv_b, a_dt_bias, a_A_log, a_D, a_norm_w, even_w_out, odd_w_in, c_A_re, c_A_im, c_B_re, c_B_im, c_C_re, c_C_im, c_D, c_log_dt, c_glu_w, c_glu_b, odd_w_out):
    B, L, D = x_prompt.shape
    nS = x_sample.shape[0]
    M = B * L
    Q = S5_Q

    w0 = even_w_in[0]
    s1 = A_WIDTH + A_CONV_DIM
    s2 = s1 + A_HEADS
    seg = lambda a, b: w0[:, a:b].astype(BF16)
    w_dt = jnp.pad(seg(s1, s2), ((0, 0), (0, TN - A_HEADS)))
    wcat = jnp.concatenate([seg(A_WIDTH, s1), w_dt, seg(s2, s2 + 2048), seg(s2 + 2048, s2 + 4096),
                            seg(s2 + 4096, s2 + 6144), seg(0, A_WIDTH), seg(s2 + 6144, s2 + 8192)], axis=1)
    w_out0 = even_w_out[0].astype(BF16)
    w_in1 = odd_w_in[0].astype(BF16)
    w_glu = c_glu_w[0].astype(BF16)
    w_out1 = odd_w_out[0].astype(BF16)
    nw0 = norm_w[0][None, :]
    nw1 = norm_w[1][None, :]
    fnw = final_norm_w[None, :]
    cw, cb = a_conv_w[0], a_conv_b[0][None, :]
    pad_h = lambda v: jnp.pad(v, (0, LANE - A_HEADS))[None, :]
    dtb, alog = pad_h(a_dt_bias[0]), pad_h(a_A_log[0])
    dsk_a = jnp.repeat(a_D[0], A_HEAD_DIM)[None, :]
    gnw = a_norm_w[0][None, :]
    head_expand = np.kron(np.eye(A_HEADS, dtype=np.float32), np.ones((1, A_HEAD_DIM), np.float32))
    head_expand = jnp.asarray(np.pad(head_expand, ((0, LANE - A_HEADS), (0, 0)))).astype(BF16)
    slopes_np = np.asarray(2.0 ** (-8.0 * np.arange(1, B_HEADS + 1) / B_HEADS), np.float32)
    slopes = jnp.asarray(np.broadcast_to(slopes_np[:, None, None], (B_HEADS, 8, LANE)))
    dist = np.stack([dil * (LANE - np.arange(LANE)) for _, dil in B_PATTERNS]).astype(np.float32)
    step_bias = jnp.asarray((dist[:, :, None] * slopes_np[None, None, :])[..., None])

    eye = jnp.eye(S5_BLK, dtype=F32)
    rows8 = lambda p: jnp.broadcast_to(p.reshape(S5_NBLK, 1, S5_NS), (S5_NBLK, 8, S5_NS))
    bexp = lambda p: jnp.einsum("kgnc,gh->kgchn", p.reshape(S5_NBLK, S5_BLK, C_STATE, C_GROUP),
                                eye).reshape(S5_NBLK, LANE, S5_NS)
    cexp = lambda p: jnp.einsum("kgcn,gh->kgnhc", p.reshape(S5_NBLK, S5_BLK, C_GROUP, C_STATE),
                                eye).reshape(S5_NBLK, S5_NS, LANE)
    ldt = jnp.repeat(c_log_dt[0], C_STATE).reshape(S5_NBLK, S5_NS)
    t_m, wb_m, cb_m, lam, lamq, cpl = _s5_prep(rows8(c_A_re[0]), rows8(c_A_im[0]), rows8(ldt),
                                                bexp(c_B_re[0]), bexp(c_B_im[0]),
                                                cexp(c_C_re[0]), cexp(c_C_im[0]))
    dsk_c = c_D[0][None, :]
    glu_b = c_glu_b[0][None, :]

    xs = x_sample.reshape(nS, D)
    proj = _mm_norm(xs, nw0, wcat, nS, 1536)
    col = lambda a, b: proj[:, a:b]
    hv = lambda a: col(a, a + 2048).reshape(nS, B_HEADS, B_HEAD_DIM)
    kvn = jnp.stack([hv(5632), hv(7680)], axis=1)

    xp = x_prompt.reshape(M, D)
    a_arr, q, k, v, kv_keep = _inproj0(xp, nw0, wcat, B, L)
    y_a, ssm_p, tail_p = _ssd(a_arr, B, L, cw, cb, dtb, alog, dsk_a, gnw)
    y_b, new_cache = _attn(q, k, v, slopes, a_arr, cache_b_kv, kvn)
    hp1 = _outproj0(y_a, y_b, w_out0, xp, 1024)
    u_gate = _mm_norm(hp1, nw1, w_in1, 1024, 1024)
    g, hfin_p = _s5_main(u_gate, B, t_m, wb_m, cb_m, lamq, dsk_c)
    y1 = _glu(g, w_glu, glu_b, u_gate, 1024, BF16)
    y_prompt = _outproj1(y1, w_out1, hp1, fnw, 1024).reshape(B, L, D)

    y_as, ssm_s, conv_s = _ssd_step(col(0, 3072)[:, None, :], state_a_conv[0],
                                    col(3072, 3200)[:, None, :], col(9728, 11776)[:, None, :],
                                    state_a_ssm[0].reshape(nS, A_WIDTH, A_STATE),
                                    cw, cb, dtb, alog, head_expand, dsk_a, gnw)
    y_bs = _attn_step(hv(3584), hv(5632), hv(7680), hv(11776), cache_b_kv, step_bias)
    hs1 = _outproj0(y_as.reshape(nS, A_WIDTH), y_bs.reshape(nS, A_WIDTH), w_out0, xs, nS, 1024)
    u_gate_s = _mm_norm(hs1, nw1, w_in1, nS, 2048)
    g_s, hnew_s = _s5_step(u_gate_s, _state_to_blocks(state_c[0]), wb_m, cpl, lam, dsk_c)
    y1_s = _glu(g_s, w_glu, glu_b, u_gate_s, nS, F32, 1024)
    y_sample = _outproj1(y1_s, w_out1, hs1, fnw, nS).reshape(nS, 1, D)

    keep = kv_keep.shape[1]
    return (y_prompt, y_sample,
            ssm_p.reshape(1, B, A_HEADS, A_HEAD_DIM, A_STATE),
            tail_p[:, 5:8, :][None],
            kv_keep.reshape(1, B, keep, 2, B_HEADS, B_HEAD_DIM),
            _blocks_to_state(hfin_p)[None],
            ssm_s.reshape(1, nS, A_HEADS, A_HEAD_DIM, A_STATE),
            conv_s[None],
            new_cache,
            _blocks_to_state(hnew_s)[None])
```

```python
import functools

import numpy as np
import jax
import jax.numpy as jnp
from jax import lax
from jax.experimental import pallas as pl
from jax.experimental.pallas import tpu as pltpu

F32 = jnp.float32
BF16 = jnp.bfloat16

D_MODEL = 2048
EPS = 1e-5
A_WIDTH = 2048
A_HEAD_DIM = 64
A_HEADS = 32
A_GROUPS = 4
A_STATE = 128
A_CONV_W = 4
A_CONV_DIM = 3072
SSD_CHUNK = 128
B_HEADS = 16
B_HEAD_DIM = 128
B_PATTERNS = ((128, 1), (512, 4), (2048, 16))
B_MAX_WINDOW = 2048
C_GROUP = 16
C_GROUPS = 256
C_STATE = 64
E1 = 4096

LANE = 128
S5_Q = 8
S5_BLK = 8
S5_NBLK = C_GROUPS // S5_BLK
S5_NS = S5_BLK * C_STATE
TN = 512
A_TILE = 1024
ATT_BLOCK = 2048
NEG = -1e30
ATT_SCALE = B_HEAD_DIM ** -0.5
VMEM_LIMIT = 56 * 1024 * 1024

NT_DIMS = (((1,), (1,)), ((), ()))


def _cp(*sem):
    return pltpu.CompilerParams(dimension_semantics=sem, vmem_limit_bytes=VMEM_LIMIT)


def _silu(x):
    return x * jax.nn.sigmoid(x)


def _softplus(x):
    return jnp.maximum(x, 0.0) + jnp.log1p(jnp.exp(-jnp.abs(x)))


def _split3(a):
    hi = a.astype(BF16)
    r = a - hi.astype(F32)
    mid = r.astype(BF16)
    lo = (r - mid.astype(F32)).astype(BF16)
    return hi, mid, lo


def _dot3_left(m, a):
    hi, mid, lo = _split3(a)
    d = lambda p: jnp.dot(m, p, preferred_element_type=F32)
    return (d(lo) + d(mid)) + d(hi)


def _dot3_right(a, m):
    hi, mid, lo = _split3(a)
    d = lambda p: jnp.dot(p, m, preferred_element_type=F32)
    return (d(lo) + d(mid)) + d(hi)


def _rms(x, w):
    ms = jnp.mean(x * x, axis=-1, keepdims=True)
    return (x * lax.rsqrt(ms + EPS)) * w


QKV_HEADS = 8


def _proj_qkv_kernel(rpb, nskip, x_ref, w_ref, qkv_ref, *kv_ref):
    j = pl.program_id(1)
    acc = jnp.dot(x_ref[...].astype(BF16), w_ref[...], preferred_element_type=F32)
    for hh in range(QKV_HEADS):
        qkv_ref[hh] = acc[:, hh * LANE:(hh + 1) * LANE]
    if kv_ref:
        tm = x_ref.shape[0]
        tiles_per = B_HEADS // QKV_HEADS
        kept = pl.program_id(0) % rpb >= nskip
        for part in range(tiles_per):
            @pl.when(kept & (j >= tiles_per) & (j % tiles_per == part))
            def _(part=part):
                for hh in range(QKV_HEADS):
                    kv_ref[0][:, part * QKV_HEADS + hh, :] = acc[:, hh * LANE:(hh + 1) * LANE]


def _proj_qkv(xn, w, B, L, tm, window):
    M = B * L
    tn = QKV_HEADS * LANE
    rpb = L // tm
    keep = min(B_MAX_WINDOW, L)
    nskip = (L - keep) // tm
    tiles_per = B_HEADS // QKV_HEADS

    def kv_map(i, j):
        ri = i % rpb
        return (i // rpb, jnp.maximum(ri - nskip, 0),
                jnp.where((ri < nskip) | (j < 2 * tiles_per), 0, 1), 0, 0)

    out_specs = [pl.BlockSpec((None, QKV_HEADS, tm, LANE), lambda i, j: (i // rpb, j, i % rpb, 0))]
    out_shape = [jax.ShapeDtypeStruct((B, 3 * B_HEADS, L, B_HEAD_DIM), F32)]
    if window:
        out_specs.append(pl.BlockSpec((None, tm, None, B_HEADS, B_HEAD_DIM), kv_map))
        out_shape.append(jax.ShapeDtypeStruct((B, keep, 2, B_HEADS, B_HEAD_DIM), F32))
    return pl.pallas_call(
        functools.partial(_proj_qkv_kernel, rpb, nskip),
        grid=(M // tm, w.shape[1] // tn),
        in_specs=[pl.BlockSpec((tm, D_MODEL), lambda i, j: (i, 0)),
                  pl.BlockSpec((D_MODEL, tn), lambda i, j: (0, j))],
        out_specs=out_specs,
        out_shape=out_shape,
        compiler_params=_cp("arbitrary", "arbitrary"),
    )(xn, w)


def _mm_norm_kernel(x_ref, nw_ref, w_ref, o_ref, xn_ref):
    @pl.when(pl.program_id(1) == 0)
    def _():
        xn_ref[...] = _rms(x_ref[...], nw_ref[...]).astype(xn_ref.dtype)

    o_ref[...] = jnp.dot(xn_ref[...].astype(BF16), w_ref[...], preferred_element_type=F32)


def _mm_norm(x, nw, w, tm, tn=TN, keep_xn=False):
    M = x.shape[0]
    K, N = w.shape
    xn_spec = pl.BlockSpec((tm, K), lambda i, j: (i, 0))
    out_specs = [pl.BlockSpec((tm, tn), lambda i, j: (i, j))]
    out_shape = [jax.ShapeDtypeStruct((M, N), F32)]
    if keep_xn:
        out_specs.append(xn_spec)
        out_shape.append(jax.ShapeDtypeStruct((M, K), BF16 if tm % 16 == 0 else F32))
    out = pl.pallas_call(
        _mm_norm_kernel,
        grid=(M // tm, N // tn),
        in_specs=[pl.BlockSpec((tm, K), lambda i, j: (i, 0)),
                  pl.BlockSpec((1, K), lambda i, j: (0, 0)),
                  pl.BlockSpec((K, tn), lambda i, j: (0, j))],
        out_specs=out_specs,
        out_shape=out_shape,
        scratch_shapes=[] if keep_xn else [pltpu.VMEM((tm, K), BF16)],
        compiler_params=_cp("arbitrary", "arbitrary"),
    )(x, nw, w)
    return out if keep_xn else out[0]


def _outproj0_kernel(ya_ref, yb_ref, wa_ref, wb_ref, res_ref, o_ref):
    o_ref[...] = (res_ref[...]
                  + jnp.dot(ya_ref[...].astype(BF16), wa_ref[...], preferred_element_type=F32)
                  + jnp.dot(yb_ref[...].astype(BF16), wb_ref[...], preferred_element_type=F32))


def _outproj0(ya, yb, w, res, tm, tn=TN):
    M = ya.shape[0]
    K = ya.shape[1]
    return pl.pallas_call(
        _outproj0_kernel,
        grid=(M // tm, D_MODEL // tn),
        in_specs=[pl.BlockSpec((tm, K), lambda i, j: (i, 0)),
                  pl.BlockSpec((tm, K), lambda i, j: (i, 0)),
                  pl.BlockSpec((K, tn), lambda i, j: (0, j)),
                  pl.BlockSpec((K, tn), lambda i, j: (1, j)),
                  pl.BlockSpec((tm, tn), lambda i, j: (i, j))],
        out_specs=pl.BlockSpec((tm, tn), lambda i, j: (i, j)),
        out_shape=jax.ShapeDtypeStruct((M, D_MODEL), F32),
        compiler_params=_cp("arbitrary", "arbitrary"),
    )(ya, yb, w, w, res)


def _glu_kernel(g_ref, wa_ref, wb_ref, ba_ref, bb_ref, gate_ref, o_ref):
    tm = g_ref.shape[0]
    hr = min(tm, 512)
    for r in range(tm // hr):
        rs = pl.ds(r * hr, hr)
        g = g_ref[rs, :].astype(BF16)
        ga = jnp.dot(g, wa_ref[...], preferred_element_type=F32) + ba_ref[...]
        gb = jnp.dot(g, wb_ref[...], preferred_element_type=F32) + bb_ref[...]
        o_ref[rs, :] = (ga * jax.nn.sigmoid(gb) * _silu(gate_ref[rs, :])).astype(o_ref.dtype)


def _glu(g, w, bias, u_gate, tm, out_dtype, tn=TN):
    M = g.shape[0]
    nb = E1 // tn
    return pl.pallas_call(
        _glu_kernel,
        grid=(M // tm, nb),
        in_specs=[pl.BlockSpec((tm, E1), lambda i, j: (i, 0)),
                  pl.BlockSpec((E1, tn), lambda i, j: (0, j)),
                  pl.BlockSpec((E1, tn), lambda i, j: (0, j + nb)),
                  pl.BlockSpec((1, tn), lambda i, j: (0, j)),
                  pl.BlockSpec((1, tn), lambda i, j: (0, j + nb)),
                  pl.BlockSpec((tm, tn), lambda i, j: (i, j + nb))],
        out_specs=pl.BlockSpec((tm, tn), lambda i, j: (i, j)),
        out_shape=jax.ShapeDtypeStruct((M, E1), out_dtype),
        compiler_params=_cp("arbitrary", "arbitrary"),
    )(g, w, w, bias, bias, u_gate)


def _outproj1_kernel(y_ref, w_ref, res_ref, fnw_ref, o_ref):
    j = pl.program_id(1)
    nj = D_MODEL // TN
    h = res_ref[...] + jnp.dot(y_ref[...].astype(BF16), w_ref[...], preferred_element_type=F32)
    for jj in range(nj):
        @pl.when(j == jj)
        def _(jj=jj):
            o_ref[:, jj * TN:(jj + 1) * TN] = h

    @pl.when(j == nj - 1)
    def _():
        o_ref[...] = _rms(o_ref[...], fnw_ref[...])


def _outproj1(y, w, res, fnw, tm):
    M, K = y.shape
    return pl.pallas_call(
        _outproj1_kernel,
        grid=(M // tm, D_MODEL // TN),
        in_specs=[pl.BlockSpec((tm, K), lambda i, j: (i, 0)),
                  pl.BlockSpec((K, TN), lambda i, j: (0, j)),
                  pl.BlockSpec((tm, TN), lambda i, j: (i, j)),
                  pl.BlockSpec((1, D_MODEL), lambda i, j: (0, 0))],
        out_specs=pl.BlockSpec((tm, D_MODEL), lambda i, j: (i, 0)),
        out_shape=jax.ShapeDtypeStruct((M, D_MODEL), F32),
        compiler_params=_cp("arbitrary", "arbitrary"),
    )(y, w, res, fnw)


def _ssd_kernel(xbc_ref, dt_ref, z0_ref, z1_ref, cw_ref, cb_ref, dtb_ref, alog_ref, dsk_ref, gnw_ref,
                y_ref, st_ref, tail_ref, h_scr, tail_scr, y_scr):
    c = pl.program_id(1)
    Q = SSD_CHUNK

    @pl.when(c == 0)
    def _():
        h_scr[...] = jnp.zeros_like(h_scr)
        tail_scr[...] = jnp.zeros_like(tail_scr)

    x_raw = xbc_ref[...]
    tail8 = tail_scr[...]
    row8 = lax.broadcasted_iota(jnp.int32, (8, 1), 0)
    conv = cb_ref[...] + cw_ref[3:4, :] * x_raw
    for k in (1, 2, 3):
        xr = pltpu.roll(x_raw, k, 0)
        first = jnp.where(row8 < k, pltpu.roll(tail8, k, 0), xr[0:8, :])
        conv = conv + cw_ref[3 - k:4 - k, :] * jnp.concatenate([first, xr[8:, :]], axis=0)
    tail_scr[...] = x_raw[Q - 8:Q, :]
    xc = _silu(conv)

    dt = _softplus(dt_ref[...] + dtb_ref[...])
    a = dt * (-jnp.exp(alog_ref[...]))
    qi = lax.broadcasted_iota(jnp.int32, (Q, Q), 0)
    si = lax.broadcasted_iota(jnp.int32, (Q, Q), 1)
    tri = qi >= si
    a_cum = _dot3_left(tri.astype(BF16), a)
    a_cum_t = a_cum.T
    lo_half = si < A_HEAD_DIM

    for g in range(A_GROUPS):
        bg = xc[:, A_WIDTH + g * A_STATE:A_WIDTH + (g + 1) * A_STATE].astype(BF16)
        cg = xc[:, A_WIDTH + (A_GROUPS + g) * A_STATE:A_WIDTH + (A_GROUPS + g + 1) * A_STATE].astype(BF16)
        cbm = lax.dot_general(cg, bg, NT_DIMS, preferred_element_type=F32)
        hprev = h_scr[g * 512:(g + 1) * 512, :]
        yoff = lax.dot_general(cg, hprev.astype(BF16), NT_DIMS, preferred_element_type=F32)
        for jj in range(4):
            p = g * 4 + jj
            h0, h1 = 2 * p, 2 * p + 1
            dtp = jnp.where(lo_half, dt[:, h0:h0 + 1], dt[:, h1:h1 + 1])
            ap = jnp.where(lo_half, a_cum[:, h0:h0 + 1], a_cum[:, h1:h1 + 1])
            x_p = xc[:, p * LANE:(p + 1) * LANE]
            xdt = x_p * dtp
            g0 = cbm * jnp.exp(jnp.where(tri, a_cum[:, h0:h0 + 1] - a_cum_t[h0:h0 + 1, :], NEG))
            g1 = cbm * jnp.exp(jnp.where(tri, a_cum[:, h1:h1 + 1] - a_cum_t[h1:h1 + 1, :], NEG))
            lhs = jnp.concatenate([g0, g1], axis=1).astype(BF16)
            rhs = jnp.concatenate([jnp.where(lo_half, xdt, 0.0), jnp.where(lo_half, 0.0, xdt)],
                                  axis=0).astype(BF16)
            ydiag = jnp.dot(lhs, rhs, preferred_element_type=F32)
            y_scr[:, p * LANE:(p + 1) * LANE] = (
                ydiag + yoff[:, jj * LANE:(jj + 1) * LANE] * jnp.exp(ap)
                + dsk_ref[:, p * LANE:(p + 1) * LANE] * x_p)
            xd = xdt * jnp.exp(ap[Q - 1:Q, :] - ap)
            st = jnp.dot(xd.T.astype(BF16), bg, preferred_element_type=F32)
            cd = jnp.exp(jnp.where(qi < A_HEAD_DIM, a_cum_t[h0:h0 + 1, Q - 1:Q],
                                   a_cum_t[h1:h1 + 1, Q - 1:Q]))
            h_scr[p * LANE:(p + 1) * LANE, :] = cd * hprev[jj * LANE:(jj + 1) * LANE, :] + st

    z = jnp.concatenate([z0_ref[...], z1_ref[...]], axis=1)
    y_ref[...] = _rms(y_scr[...] * _silu(z), gnw_ref[...]).astype(y_ref.dtype)

    @pl.when(c == pl.num_programs(1) - 1)
    def _():
        st_ref[...] = h_scr[...]
        tail_ref[...] = x_raw[Q - 8:Q, :]


def _ssd(a_arr, B, L, cw, cb, dtb, alog, dsk, gnw):
    nc = L // SSD_CHUNK
    Q = SSD_CHUNK
    row = lambda b, c: b * nc + c
    full = lambda shp: pl.BlockSpec(shp, lambda b, c: (0, 0))
    return pl.pallas_call(
        _ssd_kernel,
        grid=(B, nc),
        in_specs=[pl.BlockSpec((Q, A_CONV_DIM), lambda b, c: (row(b, c), 0)),
                  pl.BlockSpec((Q, LANE), lambda b, c: (row(b, c), 14 * TN // LANE)),
                  pl.BlockSpec((Q, A_WIDTH // 2), lambda b, c: (row(b, c), 3)),
                  pl.BlockSpec((Q, A_WIDTH // 2), lambda b, c: (row(b, c), 4)),
                  full((A_CONV_W, A_CONV_DIM)), full((1, A_CONV_DIM)), full((1, LANE)),
                  full((1, LANE)), full((1, A_WIDTH)), full((1, A_WIDTH))],
        out_specs=[pl.BlockSpec((Q, A_WIDTH), lambda b, c: (row(b, c), 0)),
                   pl.BlockSpec((None, A_WIDTH, A_STATE), lambda b, c: (b, 0, 0)),
                   pl.BlockSpec((None, 8, A_CONV_DIM), lambda b, c: (b, 0, 0))],
        out_shape=[jax.ShapeDtypeStruct((B * L, A_WIDTH), BF16),
                   jax.ShapeDtypeStruct((B, A_WIDTH, A_STATE), F32),
                   jax.ShapeDtypeStruct((B, 8, A_CONV_DIM), F32)],
        scratch_shapes=[pltpu.VMEM((A_WIDTH, A_STATE), F32), pltpu.VMEM((8, A_CONV_DIM), F32),
                        pltpu.VMEM((Q, A_WIDTH), F32)],
        compiler_params=_cp("arbitrary", "arbitrary"),
    )(a_arr, a_arr, a_arr, a_arr, cw, cb, dtb, alog, dsk, gnw)


def _ssd_step_kernel(xbc_ref, cs_ref, dt_ref, z_ref, h0_ref, cw_ref, cb_ref, dtb_ref, alog_ref,
                     e_ref, dsk_ref, gnw_ref, y_ref, hn_ref, nc_ref):
    x_raw = xbc_ref[...]
    cs = cs_ref[...]
    conv = (cb_ref[...] + cw_ref[0:1, :] * cs[0:1, :] + cw_ref[1:2, :] * cs[1:2, :]
            + cw_ref[2:3, :] * cs[2:3, :] + cw_ref[3:4, :] * x_raw)
    xc = _silu(conv)
    nc_ref[0:1, :] = cs[1:2, :]
    nc_ref[1:2, :] = cs[2:3, :]
    nc_ref[2:3, :] = x_raw

    dt = _softplus(dt_ref[...] + dtb_ref[...])
    da = dt * (-jnp.exp(alog_ref[...]))
    e = e_ref[...]
    dt_e = _dot3_right(jnp.broadcast_to(dt, (8, LANE)), e)[0:1, :]
    da_e = _dot3_right(jnp.broadcast_to(da, (8, LANE)), e)[0:1, :]
    xs = xc[:, :A_WIDTH]
    sub = lax.broadcasted_iota(jnp.int32, (8, 1), 0)
    rows = jnp.where(sub == 0, xs * dt_e, jnp.where(sub == 1, jnp.exp(da_e), 0.0))
    cols = jnp.concatenate([rows, jnp.zeros((LANE - 8, A_WIDTH), F32)], axis=0).T
    xdt_col = cols[:, 0:1]
    dec_col = cols[:, 1:2]
    ys = []
    for g in range(A_GROUPS):
        bg = xc[:, A_WIDTH + g * A_STATE:A_WIDTH + (g + 1) * A_STATE]
        cg = xc[:, A_WIDTH + (A_GROUPS + g) * A_STATE:A_WIDTH + (A_GROUPS + g + 1) * A_STATE]
        sl = slice(g * 512, (g + 1) * 512)
        hn = dec_col[sl, :] * h0_ref[sl, :] + xdt_col[sl, :] * bg
        hn_ref[sl, :] = hn
        c8 = jnp.broadcast_to(cg, (8, A_STATE)).astype(BF16)
        ys.append(lax.dot_general(c8, hn.astype(BF16), NT_DIMS, preferred_element_type=F32)[0:1, :])
    y = jnp.concatenate(ys, axis=1) + dsk_ref[...] * xs
    y_ref[...] = _rms(y * _silu(z_ref[...]), gnw_ref[...])


def _ssd_step(xbc, cs, dt, z, h0, cw, cb, dtb, alog, e, dsk, gnw):
    nb = xbc.shape[0]
    per_b = lambda n: pl.BlockSpec((None, 1, n), lambda b: (b, 0, 0))
    full = lambda shp: pl.BlockSpec(shp, lambda b: (0, 0))
    return pl.pallas_call(
        _ssd_step_kernel,
        grid=(nb,),
        in_specs=[per_b(A_CONV_DIM), pl.BlockSpec((None, 3, A_CONV_DIM), lambda b: (b, 0, 0)),
                  per_b(LANE), per_b(A_WIDTH),
                  pl.BlockSpec((None, A_WIDTH, A_STATE), lambda b: (b, 0, 0)),
                  full((A_CONV_W, A_CONV_DIM)), full((1, A_CONV_DIM)), full((1, LANE)),
                  full((1, LANE)), full((LANE, A_WIDTH)), full((1, A_WIDTH)), full((1, A_WIDTH))],
        out_specs=[per_b(A_WIDTH),
                   pl.BlockSpec((None, A_WIDTH, A_STATE), lambda b: (b, 0, 0)),
                   pl.BlockSpec((None, 3, A_CONV_DIM), lambda b: (b, 0, 0))],
        out_shape=[jax.ShapeDtypeStruct((nb, 1, A_WIDTH), F32),
                   jax.ShapeDtypeStruct((nb, A_WIDTH, A_STATE), F32),
                   jax.ShapeDtypeStruct((nb, 3, A_CONV_DIM), F32)],
        compiler_params=_cp("arbitrary"),
    )(xbc, cs, dt, z, h0, cw, cb, dtb, alog, e, dsk, gnw)


def _bias_tile(slope, dil):
    i = lax.broadcasted_iota(jnp.int32, (LANE, 2 * LANE), 0)
    c = lax.broadcasted_iota(jnp.int32, (LANE, 2 * LANE), 1)
    j = LANE + i - c
    valid = (j >= 0) & (j <= LANE)
    return jnp.where(valid, -(slope * (dil * j).astype(F32)), NEG), c


def _attn_tile(q, kcat, vcat, bias):
    s = lax.dot_general(q, kcat, NT_DIMS, preferred_element_type=F32) * ATT_SCALE + bias
    m = jnp.max(s, axis=1, keepdims=True)
    p = jnp.exp(s - m)
    l = jnp.sum(p, axis=1, keepdims=True)
    acc = jnp.dot(p.astype(BF16), vcat, preferred_element_type=F32)
    return m, l, acc


def _rows(start, stride):
    return pl.ds(start, LANE, stride=stride) if stride > 1 else pl.ds(start, LANE)


def _shift_cache_block(step, steps_per_seq, x_ref, nxt_ref, kvn_ref, o_ref):
    R = x_ref.shape[0]
    o_ref[pl.ds(0, R - 1)] = x_ref[pl.ds(1, R - 1)]
    last = step % steps_per_seq == steps_per_seq - 1
    o_ref[R - 1] = jnp.where(last, kvn_ref[...], nxt_ref[0])


def _attn_kernel(steps_per_seq, slope_ref, q_ref, k_ref, v_ref, kp_ref, vp_ref, gate_ref,
                 x_ref, nxt_ref, kvn_ref, y_ref, oc_ref, m_scr, l_scr, acc_scr):
    step = ((pl.program_id(0) * pl.num_programs(1) + pl.program_id(1)) * pl.num_programs(2)
            + pl.program_id(2))
    _shift_cache_block(step, steps_per_seq, x_ref, nxt_ref, kvn_ref, oc_ref)
    first_block = pl.program_id(2) == 0
    slope = slope_ref[0:1, 0:1]
    for _, dil in B_PATTERNS:
        bias_full, c = _bias_tile(slope, dil)
        bias0 = jnp.where((c >= LANE) | jnp.logical_not(first_block), bias_full, NEG)
        nsub = ATT_BLOCK // dil // LANE
        for r in range(dil):
            kc = vc = None
            for sub in range(nsub):
                rows = _rows(sub * LANE * dil + r, dil)
                if sub == 0:
                    prow = _rows((nsub - 1) * LANE * dil + r, dil)
                    kp, vp, bias = kp_ref[prow, :], vp_ref[prow, :], bias0
                else:
                    kp, vp, bias = kc, vc, bias_full
                kc, vc = k_ref[rows, :], v_ref[rows, :]
                kcat = jnp.concatenate([kp, kc], axis=0).astype(BF16)
                vcat = jnp.concatenate([vp, vc], axis=0).astype(BF16)
                m, l, acc = _attn_tile(q_ref[rows, :].astype(BF16), kcat, vcat, bias)
                if dil == 1:
                    m_scr[rows, :] = jnp.broadcast_to(m, (LANE, LANE))
                    l_scr[rows, :] = jnp.broadcast_to(l, (LANE, LANE))
                    acc_scr[rows, :] = acc
                else:
                    m_old = m_scr[rows, :]
                    m_new = jnp.maximum(m_old, m)
                    a_old = jnp.exp(m_old - m_new)
                    a_new = jnp.exp(m - m_new)
                    m_scr[rows, :] = m_new
                    l_scr[rows, :] = a_old * l_scr[rows, :] + a_new * l
                    acc_scr[rows, :] = a_old * acc_scr[rows, :] + a_new * acc
    for sub in range(ATT_BLOCK // LANE):
        rs = pl.ds(sub * LANE, LANE)
        o = acc_scr[rs, :] / l_scr[rs, :]
        y_ref[rs, :] = (o * _silu(gate_ref[rs, :])).astype(y_ref.dtype)


def _attn(qkv, slopes, a_arr, cache6, kvn):
    B, H3, L, dh = qkv.shape
    H = H3 // 3
    R = ATT_BLOCK
    nblk = L // R
    cur = lambda part: pl.BlockSpec((None, None, R, dh), lambda b, h, i: (b, part * H + h, i, 0))
    prev = lambda part: pl.BlockSpec((None, None, R, dh),
                                     lambda b, h, i: (b, part * H + h, jnp.maximum(i - 1, 0), 0))
    slope = pl.BlockSpec((None, 8, LANE), lambda b, h, i: (h, 0, 0))
    gate = pl.BlockSpec((R, LANE), lambda b, h, i: (b * nblk + i, 10 * TN // LANE + h))

    nb, W = cache6.shape[1], cache6.shape[2]
    steps = B * H * nblk
    per = steps // nb
    rows = W // per
    assert per * nb == steps and rows * per == W, (steps, nb, W)
    tail = (2, B_HEADS, B_HEAD_DIM)
    step = lambda b, h, i: (b * H + h) * nblk + i
    cblk = pl.BlockSpec((None, None, rows) + tail,
                        lambda b, h, i: (0, step(b, h, i) // per, step(b, h, i) % per, 0, 0, 0))
    cnext = pl.BlockSpec((None, None, 1) + tail,
                         lambda b, h, i: (0, step(b, h, i) // per,
                                          jnp.minimum((step(b, h, i) % per + 1) * rows, W - 1), 0, 0, 0))
    cnew = pl.BlockSpec((None,) + tail, lambda b, h, i: (step(b, h, i) // per, 0, 0, 0))
    return pl.pallas_call(
        functools.partial(_attn_kernel, per),
        grid=(B, H, nblk),
        in_specs=[slope, cur(0), cur(1), cur(2), prev(1), prev(2), gate, cblk, cnext, cnew],
        out_specs=[pl.BlockSpec((R, LANE), lambda b, h, i: (b * nblk + i, h)), cblk],
        out_shape=[jax.ShapeDtypeStruct((B * L, H * dh), BF16),
                   jax.ShapeDtypeStruct(cache6.shape, F32)],
        scratch_shapes=[pltpu.VMEM((R, LANE), F32)] * 3,
        compiler_params=_cp("arbitrary", "arbitrary", "arbitrary"),
    )(slopes, qkv, qkv, qkv, qkv, qkv, a_arr, cache6, cache6, kvn)


def _attn_step_kernel(bias_ref, q_ref, kn_ref, vn_ref, gate_ref, c1_ref, c4_ref, c16_ref, y_ref):
    q, kn, vn = q_ref[...], kn_ref[...], vn_ref[...]
    s_self = jnp.sum(q * kn, axis=-1, keepdims=True) * ATT_SCALE
    blocks = (c1_ref, c4_ref, c16_ref)
    scores = []
    for p, c_ref in enumerate(blocks):
        kblk = c_ref[:, 0]
        scores.append(jnp.sum(kblk * q[None], axis=-1, keepdims=True) * ATT_SCALE - bias_ref[p])
    m = s_self
    for s in scores:
        m = jnp.maximum(m, jnp.max(s, axis=0))
    p_self = float(len(B_PATTERNS)) * jnp.exp(s_self - m)
    l = p_self
    o = p_self * vn
    for s, c_ref in zip(scores, blocks):
        p = jnp.exp(s - m[None])
        l = l + jnp.sum(p, axis=0)
        o = o + jnp.sum(p * c_ref[:, 1], axis=0)
    y_ref[...] = (o / l) * _silu(gate_ref[...])


def _attn_step(q, kn, vn, gate, cache6, bias):
    nb = q.shape[0]
    W = B_MAX_WINDOW
    hspec = pl.BlockSpec((None, B_HEADS, B_HEAD_DIM), lambda b: (b, 0, 0))
    tail = (2, B_HEADS, B_HEAD_DIM)
    views, specs = [], []
    for _, dil in B_PATTERNS:
        if dil == 1:
            views.append(cache6)
            specs.append(pl.BlockSpec((None, None, LANE) + tail,
                                      lambda b: (0, b, W // LANE - 1, 0, 0, 0)))
        else:
            views.append(cache6.reshape((1, nb, W // dil, dil) + tail))
            specs.append(pl.BlockSpec((None, None, LANE, None) + tail,
                                      lambda b, dil=dil: (0, b, W // dil // LANE - 1, 0, 0, 0, 0)))
    return pl.pallas_call(
        _attn_step_kernel,
        grid=(nb,),
        in_specs=[pl.BlockSpec(bias.shape, lambda b: (0, 0, 0, 0)), hspec, hspec, hspec, hspec]
                 + specs,
        out_specs=hspec,
        out_shape=jax.ShapeDtypeStruct((nb, B_HEADS, B_HEAD_DIM), F32),
        compiler_params=_cp("arbitrary"),
    )(bias, q, kn, vn, gate, *views)


def _cmul(ar, ai, br, bi):
    return ar * br - ai * bi, ar * bi + ai * br


def _s5_prep_kernel(are_ref, aim_ref, ldt_ref, bre_ref, bim_ref, ctre_ref, ctim_ref,
                    t_ref, wb_ref, cb_ref, lam_ref, lamq_ref, cpl_ref):
    Q = S5_Q
    are, aim = are_ref[...], aim_ref[...]
    dt = jnp.exp(ldt_ref[...])
    mag = jnp.exp(dt * are)
    l_re = mag * jnp.cos(dt * aim)
    l_im = mag * jnp.sin(dt * aim)
    den = are * are + aim * aim
    f_re = ((l_re - 1.0) * are + l_im * aim) / den
    f_im = (l_im * are - (l_re - 1.0) * aim) / den
    lr, li = l_re[0:1, :], l_im[0:1, :]
    bb_re, bb_im = _cmul(f_re[0:1, :], f_im[0:1, :], bre_ref[...], bim_ref[...])
    ctre, ctim = ctre_ref[...], ctim_ref[...]
    cplain = jnp.concatenate([ctre, -ctim], axis=0).astype(BF16)
    cpl_ref[...] = cplain
    stk = jnp.concatenate([l_re, l_im, jnp.zeros((LANE - 16, S5_NS), F32)], axis=0).T
    lc_re, lc_im = stk[:, 0:1], stk[:, 8:9]
    p_re, p_im = jnp.ones((1, S5_NS), F32), jnp.zeros((1, S5_NS), F32)
    pc_re, pc_im = lc_re, lc_im
    dts = []
    for tau in range(Q):
        e_re, e_im = _cmul(bb_re, bb_im, p_re, p_im)
        wp = jnp.concatenate([e_re, e_im], axis=1).astype(BF16)
        s = Q - 1 - tau
        wb_ref[s * LANE:(s + 1) * LANE, :] = wp
        dts.append(jnp.dot(wp, cplain, preferred_element_type=F32).astype(BF16))
        c_re, c_im = _cmul(ctre, ctim, pc_re, pc_im)
        cb_ref[:, tau * LANE:(tau + 1) * LANE] = jnp.concatenate([c_re, -c_im], axis=0).astype(BF16)
        p_re, p_im = _cmul(p_re, p_im, lr, li)
        pc_re, pc_im = _cmul(pc_re, pc_im, lc_re, lc_im)
    zero = jnp.zeros((LANE, LANE), BF16)
    for s in range(Q):
        for q in range(Q):
            t_ref[s * LANE:(s + 1) * LANE, q * LANE:(q + 1) * LANE] = dts[q - s] if q >= s else zero
    lam_ref[...] = jnp.concatenate([l_re, l_im], axis=1)
    sub = lax.broadcasted_iota(jnp.int32, (8, 1), 0)
    q_re, q_im = p_re, p_im
    tab_re, tab_im = jnp.zeros((8, S5_NS), F32), jnp.zeros((8, S5_NS), F32)
    for r in range(8):
        tab_re = jnp.where(sub == r, q_re, tab_re)
        tab_im = jnp.where(sub == r, q_im, tab_im)
        q_re, q_im = _cmul(q_re, q_im, p_re, p_im)
    lamq_ref[...] = jnp.concatenate([tab_re, tab_im], axis=1)


def _s5_prep(are, aim, ldt, bre, bim, ctre, ctim):
    Q = S5_Q
    row = pl.BlockSpec((None, 8, S5_NS), lambda i: (i, 0, 0))
    blk = lambda a, b: pl.BlockSpec((None, a, b), lambda i: (i, 0, 0))
    return pl.pallas_call(
        _s5_prep_kernel,
        grid=(S5_NBLK,),
        in_specs=[row, row, row, blk(LANE, S5_NS), blk(LANE, S5_NS), blk(S5_NS, LANE), blk(S5_NS, LANE)],
        out_specs=[blk(Q * LANE, Q * LANE), blk(Q * LANE, 2 * S5_NS), blk(2 * S5_NS, Q * LANE),
                   blk(8, 2 * S5_NS), blk(8, 2 * S5_NS), blk(2 * S5_NS, LANE)],
        out_shape=[jax.ShapeDtypeStruct((S5_NBLK, Q * LANE, Q * LANE), BF16),
                   jax.ShapeDtypeStruct((S5_NBLK, Q * LANE, 2 * S5_NS), BF16),
                   jax.ShapeDtypeStruct((S5_NBLK, 2 * S5_NS, Q * LANE), BF16),
                   jax.ShapeDtypeStruct((S5_NBLK, 8, 2 * S5_NS), F32),
                   jax.ShapeDtypeStruct((S5_NBLK, 8, 2 * S5_NS), F32),
                   jax.ShapeDtypeStruct((S5_NBLK, 2 * S5_NS, LANE), BF16)],
        compiler_params=_cp("arbitrary"),
    )(are, aim, ldt, bre, bim, ctre, ctim)


def _s5_main_kernel(nb, u_ref, t_ref, wb_ref, cb_ref, lamq_ref, dsk_ref, g_ref, hfin_ref,
                    lhs_scr, s_scr, hp_scr, gs_scr):
    Q = S5_Q
    NS = S5_NS
    nrow = u_ref.shape[0] // Q
    nch = nrow // nb
    for s in range(Q):
        lhs_scr[:, s * LANE:(s + 1) * LANE] = u_ref[pl.ds(s, nrow, stride=Q), :].astype(BF16)
    lhs = lhs_scr[...]
    s_scr[...] = jnp.dot(lhs, wb_ref[...], preferred_element_type=F32)
    tab = lamq_ref[...]
    tab_re, tab_im = tab[:, :NS], tab[:, NS:]
    sub = lax.broadcasted_iota(jnp.int32, (8, 1), 0)

    def tile_body(t, carry):
        base = t * 8
        new = []
        for b in range(nb):
            h_re, h_im = carry[2 * b], carry[2 * b + 1]
            st = s_scr[pl.ds(b * nch + base, 8), :]
            x_re, x_im = st[:, :NS], st[:, NS:]
            for sh in (1, 2, 4):
                y_re = jnp.where(sub >= sh, pltpu.roll(x_re, sh, 0), 0.0)
                y_im = jnp.where(sub >= sh, pltpu.roll(x_im, sh, 0), 0.0)
                d_re, d_im = _cmul(tab_re[sh - 1:sh, :], tab_im[sh - 1:sh, :], y_re, y_im)
                x_re, x_im = x_re + d_re, x_im + d_im
            c_re, c_im = _cmul(tab_re, tab_im, h_re, h_im)
            x_re, x_im = x_re + c_re, x_im + c_im
            p_re = jnp.where(sub == 0, h_re, pltpu.roll(x_re, 1, 0))
            p_im = jnp.where(sub == 0, h_im, pltpu.roll(x_im, 1, 0))
            hp_scr[pl.ds(b * nch + base, 8), :] = jnp.concatenate([p_re, p_im], axis=1)
            new += [x_re[7:8, :], x_im[7:8, :]]
        return tuple(new)

    fin = (jnp.zeros((1, NS), F32),) * (2 * nb)
    for t in range(nch // 8):
        fin = tile_body(t, fin)
    for b in range(nb):
        hfin_ref[b:b + 1, :] = jnp.concatenate([fin[2 * b], fin[2 * b + 1]], axis=1)

    y = (jnp.dot(lhs, t_ref[...], preferred_element_type=F32)
         + jnp.dot(hp_scr[...].astype(BF16), cb_ref[...], preferred_element_type=F32))
    for q in range(Q):
        rows = pl.ds(q, nrow, stride=Q)
        yq = y[:, q * LANE:(q + 1) * LANE] + dsk_ref[...] * u_ref[rows, :]
        gs_scr[rows, :] = jax.nn.gelu(yq)
    g_ref[...] = gs_scr[...].astype(g_ref.dtype)


def _s5_main(u_gate, nb, t, wb, cb, lamq, dsk):
    Q = S5_Q
    M = u_gate.shape[0]
    nrow = M // Q
    blk = lambda a, b: pl.BlockSpec((None, a, b), lambda i: (i, 0, 0))
    return pl.pallas_call(
        functools.partial(_s5_main_kernel, nb),
        grid=(S5_NBLK,),
        in_specs=[pl.BlockSpec((M, LANE), lambda i: (0, i)),
                  blk(Q * LANE, Q * LANE), blk(Q * LANE, 2 * S5_NS), blk(2 * S5_NS, Q * LANE),
                  blk(8, 2 * S5_NS), pl.BlockSpec((1, LANE), lambda i: (0, i))],
        out_specs=[pl.BlockSpec((M, LANE), lambda i: (0, i)), blk(nb, 2 * S5_NS)],
        out_shape=[jax.ShapeDtypeStruct((M, E1), BF16),
                   jax.ShapeDtypeStruct((S5_NBLK, nb, 2 * S5_NS), F32)],
        scratch_shapes=[pltpu.VMEM((nrow, Q * LANE), BF16), pltpu.VMEM((nrow, 2 * S5_NS), F32),
                        pltpu.VMEM((nrow, 2 * S5_NS), F32), pltpu.VMEM((M, LANE), F32)],
        compiler_params=_cp("arbitrary"),
    )(u_gate, t, wb, cb, lamq, dsk)


def _s5_step_kernel(u_ref, h0_ref, bb_ref, cpl_ref, lam_ref, dsk_ref, g_ref, hn_ref):
    NS = S5_NS
    u = u_ref[...]
    s = jnp.dot(u.astype(BF16), bb_ref[...], preferred_element_type=F32)
    lam = lam_ref[0:1, :]
    h0 = h0_ref[...]
    n_re, n_im = _cmul(lam[:, :NS], lam[:, NS:], h0[:, :NS], h0[:, NS:])
    hn = jnp.concatenate([n_re + s[:, :NS], n_im + s[:, NS:]], axis=1)
    hn_ref[...] = hn
    y = jnp.dot(hn.astype(BF16), cpl_ref[...], preferred_element_type=F32) + dsk_ref[...] * u
    g_ref[...] = jax.nn.gelu(y)


def _s5_step(u_gate, h0, wb, cpl, lam, dsk):
    nb = u_gate.shape[0]
    blk = lambda a, b: pl.BlockSpec((None, a, b), lambda i: (i, 0, 0))
    return pl.pallas_call(
        _s5_step_kernel,
        grid=(S5_NBLK,),
        in_specs=[pl.BlockSpec((nb, LANE), lambda i: (0, i)), blk(nb, 2 * S5_NS),
                  pl.BlockSpec((None, LANE, 2 * S5_NS), lambda i: (i, S5_Q - 1, 0)),
                  blk(2 * S5_NS, LANE), blk(8, 2 * S5_NS), pl.BlockSpec((1, LANE), lambda i: (0, i))],
        out_specs=[pl.BlockSpec((nb, LANE), lambda i: (0, i)), blk(nb, 2 * S5_NS)],
        out_shape=[jax.ShapeDtypeStruct((nb, E1), F32),
                   jax.ShapeDtypeStruct((S5_NBLK, nb, 2 * S5_NS), F32)],
        compiler_params=_cp("arbitrary"),
    )(u_gate, h0, wb, cpl, lam, dsk)


def _state_to_blocks(st):
    nb = st.shape[0]
    x = st.reshape(nb, S5_NBLK, S5_NS, 2)
    return jnp.transpose(x, (1, 0, 3, 2)).reshape(S5_NBLK, nb, 2 * S5_NS)


def _blocks_to_state(h):
    nb = h.shape[1]
    x = h.reshape(S5_NBLK, nb, 2, S5_NS)
    return jnp.transpose(x, (1, 0, 3, 2)).reshape(nb, C_GROUPS, C_STATE, 2)


def kernel(x_prompt, x_sample, state_a_ssm, state_a_conv, cache_b_kv, state_c, norm_w, final_norm_w, even_w_in, a_conv_w, a_conv_b, a_dt_bias, a_A_log, a_D, a_norm_w, even_w_out, odd_w_in, c_A_re, c_A_im, c_B_re, c_B_im, c_C_re, c_C_im, c_D, c_log_dt, c_glu_w, c_glu_b, odd_w_out):
    B, L, D = x_prompt.shape
    nS = x_sample.shape[0]
    M = B * L
    Q = S5_Q

    w0 = even_w_in[0]
    s1 = A_WIDTH + A_CONV_DIM
    s2 = s1 + A_HEADS
    seg = lambda a, b: w0[:, a:b].astype(BF16)
    w_dt = jnp.pad(seg(s1, s2), ((0, 0), (0, A_TILE - A_HEADS)))
    w_a = jnp.concatenate([seg(A_WIDTH, s1), seg(0, A_WIDTH), seg(s2 + 6144, s2 + 8192), w_dt], axis=1)
    w_qkv = seg(s2, s2 + 6144)
    w_out0 = even_w_out[0].astype(BF16)
    w_in1 = odd_w_in[0].astype(BF16)
    w_glu = c_glu_w[0].astype(BF16)
    w_out1 = odd_w_out[0].astype(BF16)
    nw0 = norm_w[0][None, :]
    nw1 = norm_w[1][None, :]
    fnw = final_norm_w[None, :]
    cw, cb = a_conv_w[0], a_conv_b[0][None, :]
    pad_h = lambda v: jnp.pad(v, (0, LANE - A_HEADS))[None, :]
    dtb, alog = pad_h(a_dt_bias[0]), pad_h(a_A_log[0])
    dsk_a = jnp.repeat(a_D[0], A_HEAD_DIM)[None, :]
    gnw = a_norm_w[0][None, :]
    head_expand = np.kron(np.eye(A_HEADS, dtype=np.float32), np.ones((1, A_HEAD_DIM), np.float32))
    head_expand = jnp.asarray(np.pad(head_expand, ((0, LANE - A_HEADS), (0, 0)))).astype(BF16)
    slopes_np = np.asarray(2.0 ** (-8.0 * np.arange(1, B_HEADS + 1) / B_HEADS), np.float32)
    slopes = jnp.asarray(np.broadcast_to(slopes_np[:, None, None], (B_HEADS, 8, LANE)))
    dist = np.stack([dil * (LANE - np.arange(LANE)) for _, dil in B_PATTERNS]).astype(np.float32)
    step_bias = jnp.asarray((dist[:, :, None] * slopes_np[None, None, :])[..., None])

    eye = jnp.eye(S5_BLK, dtype=F32)
    rows8 = lambda p: jnp.broadcast_to(p.reshape(S5_NBLK, 1, S5_NS), (S5_NBLK, 8, S5_NS))
    bexp = lambda p: jnp.einsum("kgnc,gh->kgchn", p.reshape(S5_NBLK, S5_BLK, C_STATE, C_GROUP),
                                eye).reshape(S5_NBLK, LANE, S5_NS)
    cexp = lambda p: jnp.einsum("kgcn,gh->kgnhc", p.reshape(S5_NBLK, S5_BLK, C_GROUP, C_STATE),
                                eye).reshape(S5_NBLK, S5_NS, LANE)
    ldt = jnp.repeat(c_log_dt[0], C_STATE).reshape(S5_NBLK, S5_NS)
    t_m, wb_m, cb_m, lam, lamq, cpl = _s5_prep(rows8(c_A_re[0]), rows8(c_A_im[0]), rows8(ldt),
                                                bexp(c_B_re[0]), bexp(c_B_im[0]),
                                                cexp(c_C_re[0]), cexp(c_C_im[0]))
    dsk_c = c_D[0][None, :]
    glu_b = c_glu_b[0][None, :]

    xs = x_sample.reshape(nS, D)
    a_s, xs_n = _mm_norm(xs, nw0, w_a, nS, 2048, keep_xn=True)
    (qkv_s,) = _proj_qkv(xs_n, w_qkv, 1, nS, nS, window=False)
    col = lambda a, b: a_s[:, a:b]
    hv = lambda part: jnp.transpose(qkv_s[0, part * B_HEADS:(part + 1) * B_HEADS], (1, 0, 2))
    gate_s = col(5120, 7168).reshape(nS, B_HEADS, B_HEAD_DIM)
    kvn = jnp.stack([hv(1), hv(2)], axis=1)

    xp = x_prompt.reshape(M, D)
    a_arr, xp_n = _mm_norm(xp, nw0, w_a, 1024, A_TILE, keep_xn=True)
    qkv, kv_keep = _proj_qkv(xp_n, w_qkv, B, L, 1024, window=True)
    y_a, ssm_p, tail_p = _ssd(a_arr, B, L, cw, cb, dtb, alog, dsk_a, gnw)
    y_b, new_cache = _attn(qkv, slopes, a_arr, cache_b_kv, kvn)
    hp1 = _outproj0(y_a, y_b, w_out0, xp, 1024)
    u_gate = _mm_norm(hp1, nw1, w_in1, 1024, 1024)
    g, hfin_p = _s5_main(u_gate, B, t_m, wb_m, cb_m, lamq, dsk_c)
    y1 = _glu(g, w_glu, glu_b, u_gate, 1024, BF16)
    y_prompt = _outproj1(y1, w_out1, hp1, fnw, 1024).reshape(B, L, D)

    y_as, ssm_s, conv_s = _ssd_step(col(0, 3072)[:, None, :], state_a_conv[0],
                                    col(7168, 7296)[:, None, :], col(3072, 5120)[:, None, :],
                                    state_a_ssm[0].reshape(nS, A_WIDTH, A_STATE),
                                    cw, cb, dtb, alog, head_expand, dsk_a, gnw)
    y_bs = _attn_step(hv(0), hv(1), hv(2), gate_s, cache_b_kv, step_bias)
    hs1 = _outproj0(y_as.reshape(nS, A_WIDTH), y_bs.reshape(nS, A_WIDTH), w_out0, xs, nS, 1024)
    u_gate_s = _mm_norm(hs1, nw1, w_in1, nS, 2048)
    g_s, hnew_s = _s5_step(u_gate_s, _state_to_blocks(state_c[0]), wb_m, cpl, lam, dsk_c)
    y1_s = _glu(g_s, w_glu, glu_b, u_gate_s, nS, F32, 1024)
    y_sample = _outproj1(y1_s, w_out1, hs1, fnw, nS).reshape(nS, 1, D)

    return (y_prompt, y_sample,
            ssm_p.reshape(1, B, A_HEADS, A_HEAD_DIM, A_STATE),
            tail_p[:, 5:8, :][None],
            kv_keep[None],
            _blocks_to_state(hfin_p)[None],
            ssm_s.reshape(1, nS, A_HEADS, A_HEAD_DIM, A_STATE),
            conv_s[None],
            new_cache,
            _blocks_to_state(hnew_s)[None])
```

```python
import functools

import numpy as np
import jax
import jax.numpy as jnp
from jax import lax
from jax.experimental import pallas as pl
from jax.experimental.pallas import tpu as pltpu

F32 = jnp.float32
BF16 = jnp.bfloat16

D_MODEL = 2048
EPS = 1e-5
A_WIDTH = 2048
A_HEAD_DIM = 64
A_HEADS = 32
A_GROUPS = 4
A_STATE = 128
A_CONV_W = 4
A_CONV_DIM = 3072
SSD_CHUNK = 128
B_HEADS = 16
B_HEAD_DIM = 128
B_PATTERNS = ((128, 1), (512, 4), (2048, 16))
B_MAX_WINDOW = 2048
C_GROUP = 16
C_GROUPS = 256
C_STATE = 64
E1 = 4096

LANE = 128
S5_Q = 8
S5_BLK = 8
S5_NBLK = C_GROUPS // S5_BLK
S5_NS = S5_BLK * C_STATE
TN = 512
A_TILE = 1024
ATT_BLOCK = 2048
NEG = -1e30
ATT_SCALE = B_HEAD_DIM ** -0.5
VMEM_LIMIT = 56 * 1024 * 1024

NT_DIMS = (((1,), (1,)), ((), ()))


def _cp(*sem):
    return pltpu.CompilerParams(dimension_semantics=sem, vmem_limit_bytes=VMEM_LIMIT)


def _silu(x):
    return x * jax.nn.sigmoid(x)


def _softplus(x):
    return jnp.maximum(x, 0.0) + jnp.log1p(jnp.exp(-jnp.abs(x)))


def _split3(a):
    hi = a.astype(BF16)
    r = a - hi.astype(F32)
    mid = r.astype(BF16)
    lo = (r - mid.astype(F32)).astype(BF16)
    return hi, mid, lo


def _dot3_left(m, a):
    hi, mid, lo = _split3(a)
    d = lambda p: jnp.dot(m, p, preferred_element_type=F32)
    return (d(lo) + d(mid)) + d(hi)


def _dot3_right(a, m):
    hi, mid, lo = _split3(a)
    d = lambda p: jnp.dot(p, m, preferred_element_type=F32)
    return (d(lo) + d(mid)) + d(hi)


def _rms(x, w):
    ms = jnp.mean(x * x, axis=-1, keepdims=True)
    return (x * lax.rsqrt(ms + EPS)) * w


QKV_HEADS = 8
CAST_ROWS = 64


def _cast_specs(w, nsteps, step_of):
    rows, cols = w.shape
    per = CAST_ROWS
    while rows // per > nsteps:
        per *= 2
    nblk = rows // per
    assert nblk * per == rows, (w.shape, nsteps)
    spec = pl.BlockSpec((per, cols), lambda *ids: (jnp.minimum(step_of(*ids), nblk - 1), 0))
    return spec, jax.ShapeDtypeStruct(w.shape, BF16)


def _proj_qkv_kernel(rpb, nskip, window, cast, *refs):
    refs = list(refs)
    x_ref, w_ref = refs[:2]
    if cast:
        refs[-1][...] = refs[2][...].astype(BF16)
    qkv_ref = refs[3 if cast else 2]
    j = pl.program_id(1)
    acc = jnp.dot(x_ref[...].astype(BF16), w_ref[...], preferred_element_type=F32)
    for hh in range(QKV_HEADS):
        qkv_ref[hh] = acc[:, hh * LANE:(hh + 1) * LANE]
    if window:
        kv_ref = refs[4 if cast else 3]
        tiles_per = B_HEADS // QKV_HEADS
        kept = pl.program_id(0) % rpb >= nskip
        for part in range(tiles_per):
            @pl.when(kept & (j >= tiles_per) & (j % tiles_per == part))
            def _(part=part):
                for hh in range(QKV_HEADS):
                    kv_ref[:, part * QKV_HEADS + hh, :] = acc[:, hh * LANE:(hh + 1) * LANE]


def _proj_qkv(xn, w, B, L, tm, window, cast=None):
    M = B * L
    tn = QKV_HEADS * LANE
    rpb = L // tm
    keep = min(B_MAX_WINDOW, L)
    nskip = (L - keep) // tm
    tiles_per = B_HEADS // QKV_HEADS

    def kv_map(i, j):
        ri = i % rpb
        return (i // rpb, jnp.maximum(ri - nskip, 0),
                jnp.where((ri < nskip) | (j < 2 * tiles_per), 0, 1), 0, 0)

    out_specs = [pl.BlockSpec((None, QKV_HEADS, tm, LANE), lambda i, j: (i // rpb, j, i % rpb, 0))]
    out_shape = [jax.ShapeDtypeStruct((B, 3 * B_HEADS, L, B_HEAD_DIM), F32)]
    if window:
        out_specs.append(pl.BlockSpec((None, tm, None, B_HEADS, B_HEAD_DIM), kv_map))
        out_shape.append(jax.ShapeDtypeStruct((B, keep, 2, B_HEADS, B_HEAD_DIM), F32))
    nj = w.shape[1] // tn
    in_specs = [pl.BlockSpec((tm, D_MODEL), lambda i, j: (i, 0)),
                pl.BlockSpec((D_MODEL, tn), lambda i, j: (0, j))]
    operands = [xn, w]
    if cast is not None:
        spec, shape = _cast_specs(cast, (M // tm) * nj, lambda i, j: i * nj + j)
        in_specs.append(spec)
        operands.append(cast)
        out_specs.append(spec)
        out_shape.append(shape)
    return pl.pallas_call(
        functools.partial(_proj_qkv_kernel, rpb, nskip, window, cast is not None),
        grid=(M // tm, nj),
        in_specs=in_specs,
        out_specs=out_specs,
        out_shape=out_shape,
        compiler_params=_cp("arbitrary", "arbitrary"),
    )(*operands)


def _mm_norm_kernel(cast, *refs):
    refs = list(refs)
    x_ref, nw_ref, w_ref = refs[:3]
    if cast:
        refs[-1][...] = refs[3][...].astype(BF16)
    o_ref, xn_ref = refs[4:6] if cast else refs[3:5]

    @pl.when(pl.program_id(1) == 0)
    def _():
        xn_ref[...] = _rms(x_ref[...], nw_ref[...]).astype(xn_ref.dtype)

    o_ref[...] = jnp.dot(xn_ref[...].astype(BF16), w_ref[...], preferred_element_type=F32)


def _mm_norm(x, nw, w, tm, tn=TN, keep_xn=False, cast=None):
    M = x.shape[0]
    K, N = w.shape
    nj = N // tn
    in_specs = [pl.BlockSpec((tm, K), lambda i, j: (i, 0)),
                pl.BlockSpec((1, K), lambda i, j: (0, 0)),
                pl.BlockSpec((K, tn), lambda i, j: (0, j))]
    operands = [x, nw, w]
    out_specs = [pl.BlockSpec((tm, tn), lambda i, j: (i, j))]
    out_shape = [jax.ShapeDtypeStruct((M, N), F32)]
    if keep_xn:
        out_specs.append(pl.BlockSpec((tm, K), lambda i, j: (i, 0)))
        out_shape.append(jax.ShapeDtypeStruct((M, K), BF16 if tm % 16 == 0 else F32))
    if cast is not None:
        assert keep_xn
        spec, shape = _cast_specs(cast, (M // tm) * nj, lambda i, j: i * nj + j)
        in_specs.append(spec)
        operands.append(cast)
        out_specs.append(spec)
        out_shape.append(shape)
    out = pl.pallas_call(
        functools.partial(_mm_norm_kernel, cast is not None),
        grid=(M // tm, nj),
        in_specs=in_specs,
        out_specs=out_specs,
        out_shape=out_shape,
        scratch_shapes=[] if keep_xn else [pltpu.VMEM((tm, K), BF16)],
        compiler_params=_cp("arbitrary", "arbitrary"),
    )(*operands)
    return out if len(out) > 1 else out[0]


def _outproj0_kernel(ya_ref, yb_ref, wa_ref, wb_ref, res_ref, o_ref):
    o_ref[...] = (res_ref[...]
                  + jnp.dot(ya_ref[...].astype(BF16), wa_ref[...], preferred_element_type=F32)
                  + jnp.dot(yb_ref[...].astype(BF16), wb_ref[...], preferred_element_type=F32))


def _outproj0(ya, yb, w, res, tm, tn=TN):
    M = ya.shape[0]
    K = ya.shape[1]
    return pl.pallas_call(
        _outproj0_kernel,
        grid=(M // tm, D_MODEL // tn),
        in_specs=[pl.BlockSpec((tm, K), lambda i, j: (i, 0)),
                  pl.BlockSpec((tm, K), lambda i, j: (i, 0)),
                  pl.BlockSpec((K, tn), lambda i, j: (0, j)),
                  pl.BlockSpec((K, tn), lambda i, j: (1, j)),
                  pl.BlockSpec((tm, tn), lambda i, j: (i, j))],
        out_specs=pl.BlockSpec((tm, tn), lambda i, j: (i, j)),
        out_shape=jax.ShapeDtypeStruct((M, D_MODEL), F32),
        compiler_params=_cp("arbitrary", "arbitrary"),
    )(ya, yb, w, w, res)


def _glu_kernel(g_ref, wa_ref, wb_ref, ba_ref, bb_ref, gate_ref, o_ref):
    tm = g_ref.shape[0]
    hr = min(tm, 512)
    for r in range(tm // hr):
        rs = pl.ds(r * hr, hr)
        g = g_ref[rs, :].astype(BF16)
        ga = jnp.dot(g, wa_ref[...], preferred_element_type=F32) + ba_ref[...]
        gb = jnp.dot(g, wb_ref[...], preferred_element_type=F32) + bb_ref[...]
        o_ref[rs, :] = (ga * jax.nn.sigmoid(gb) * _silu(gate_ref[rs, :])).astype(o_ref.dtype)


def _glu(g, w, bias, u_gate, tm, out_dtype, tn=TN):
    M = g.shape[0]
    nb = E1 // tn
    return pl.pallas_call(
        _glu_kernel,
        grid=(M // tm, nb),
        in_specs=[pl.BlockSpec((tm, E1), lambda i, j: (i, 0)),
                  pl.BlockSpec((E1, tn), lambda i, j: (0, j)),
                  pl.BlockSpec((E1, tn), lambda i, j: (0, j + nb)),
                  pl.BlockSpec((1, tn), lambda i, j: (0, j)),
                  pl.BlockSpec((1, tn), lambda i, j: (0, j + nb)),
                  pl.BlockSpec((tm, tn), lambda i, j: (i, j + nb))],
        out_specs=pl.BlockSpec((tm, tn), lambda i, j: (i, j)),
        out_shape=jax.ShapeDtypeStruct((M, E1), out_dtype),
        compiler_params=_cp("arbitrary", "arbitrary"),
    )(g, w, w, bias, bias, u_gate)


def _outproj1_kernel(y_ref, w_ref, res_ref, fnw_ref, o_ref):
    j = pl.program_id(1)
    nj = D_MODEL // TN
    h = res_ref[...] + jnp.dot(y_ref[...].astype(BF16), w_ref[...], preferred_element_type=F32)
    for jj in range(nj):
        @pl.when(j == jj)
        def _(jj=jj):
            o_ref[:, jj * TN:(jj + 1) * TN] = h

    @pl.when(j == nj - 1)
    def _():
        o_ref[...] = _rms(o_ref[...], fnw_ref[...])


def _outproj1(y, w, res, fnw, tm):
    M, K = y.shape
    return pl.pallas_call(
        _outproj1_kernel,
        grid=(M // tm, D_MODEL // TN),
        in_specs=[pl.BlockSpec((tm, K), lambda i, j: (i, 0)),
                  pl.BlockSpec((K, TN), lambda i, j: (0, j)),
                  pl.BlockSpec((tm, TN), lambda i, j: (i, j)),
                  pl.BlockSpec((1, D_MODEL), lambda i, j: (0, 0))],
        out_specs=pl.BlockSpec((tm, D_MODEL), lambda i, j: (i, 0)),
        out_shape=jax.ShapeDtypeStruct((M, D_MODEL), F32),
        compiler_params=_cp("arbitrary", "arbitrary"),
    )(y, w, res, fnw)


def _ssd_kernel(xbc_ref, dt_ref, z0_ref, z1_ref, cw_ref, cb_ref, dtb_ref, alog_ref, dsk_ref, gnw_ref,
                y_ref, st_ref, tail_ref, h_scr, tail_scr, y_scr):
    c = pl.program_id(1)
    Q = SSD_CHUNK

    @pl.when(c == 0)
    def _():
        h_scr[...] = jnp.zeros_like(h_scr)
        tail_scr[...] = jnp.zeros_like(tail_scr)

    x_raw = xbc_ref[...]
    tail8 = tail_scr[...]
    row8 = lax.broadcasted_iota(jnp.int32, (8, 1), 0)
    conv = cb_ref[...] + cw_ref[3:4, :] * x_raw
    for k in (1, 2, 3):
        xr = pltpu.roll(x_raw, k, 0)
        first = jnp.where(row8 < k, pltpu.roll(tail8, k, 0), xr[0:8, :])
        conv = conv + cw_ref[3 - k:4 - k, :] * jnp.concatenate([first, xr[8:, :]], axis=0)
    tail_scr[...] = x_raw[Q - 8:Q, :]
    xc = _silu(conv)

    dt = _softplus(dt_ref[...] + dtb_ref[...])
    a = dt * (-jnp.exp(alog_ref[...]))
    qi = lax.broadcasted_iota(jnp.int32, (Q, Q), 0)
    si = lax.broadcasted_iota(jnp.int32, (Q, Q), 1)
    tri = qi >= si
    a_cum = _dot3_left(tri.astype(BF16), a)
    a_cum_t = a_cum.T
    lo_half = si < A_HEAD_DIM

    for g in range(A_GROUPS):
        bg = xc[:, A_WIDTH + g * A_STATE:A_WIDTH + (g + 1) * A_STATE].astype(BF16)
        cg = xc[:, A_WIDTH + (A_GROUPS + g) * A_STATE:A_WIDTH + (A_GROUPS + g + 1) * A_STATE].astype(BF16)
        cbm = lax.dot_general(cg, bg, NT_DIMS, preferred_element_type=F32)
        hprev = h_scr[g * 512:(g + 1) * 512, :]
        yoff = lax.dot_general(cg, hprev.astype(BF16), NT_DIMS, preferred_element_type=F32)
        for jj in range(4):
            p = g * 4 + jj
            h0, h1 = 2 * p, 2 * p + 1
            dtp = jnp.where(lo_half, dt[:, h0:h0 + 1], dt[:, h1:h1 + 1])
            ap = jnp.where(lo_half, a_cum[:, h0:h0 + 1], a_cum[:, h1:h1 + 1])
            x_p = xc[:, p * LANE:(p + 1) * LANE]
            xdt = x_p * dtp
            g0 = cbm * jnp.exp(jnp.where(tri, a_cum[:, h0:h0 + 1] - a_cum_t[h0:h0 + 1, :], NEG))
            g1 = cbm * jnp.exp(jnp.where(tri, a_cum[:, h1:h1 + 1] - a_cum_t[h1:h1 + 1, :], NEG))
            lhs = jnp.concatenate([g0, g1], axis=1).astype(BF16)
            rhs = jnp.concatenate([jnp.where(lo_half, xdt, 0.0), jnp.where(lo_half, 0.0, xdt)],
                                  axis=0).astype(BF16)
            ydiag = jnp.dot(lhs, rhs, preferred_element_type=F32)
            y_scr[:, p * LANE:(p + 1) * LANE] = (
                ydiag + yoff[:, jj * LANE:(jj + 1) * LANE] * jnp.exp(ap)
                + dsk_ref[:, p * LANE:(p + 1) * LANE] * x_p)
            xd = xdt * jnp.exp(ap[Q - 1:Q, :] - ap)
            st = jnp.dot(xd.T.astype(BF16), bg, preferred_element_type=F32)
            cd = jnp.exp(jnp.where(qi < A_HEAD_DIM, a_cum_t[h0:h0 + 1, Q - 1:Q],
                                   a_cum_t[h1:h1 + 1, Q - 1:Q]))
            h_scr[p * LANE:(p + 1) * LANE, :] = cd * hprev[jj * LANE:(jj + 1) * LANE, :] + st

    z = jnp.concatenate([z0_ref[...], z1_ref[...]], axis=1)
    y_ref[...] = _rms(y_scr[...] * _silu(z), gnw_ref[...]).astype(y_ref.dtype)

    @pl.when(c == pl.num_programs(1) - 1)
    def _():
        st_ref[...] = h_scr[...]
        tail_ref[...] = x_raw[Q - 8:Q, :]


def _ssd(a_arr, B, L, cw, cb, dtb, alog, dsk, gnw):
    nc = L // SSD_CHUNK
    Q = SSD_CHUNK
    row = lambda b, c: b * nc + c
    full = lambda shp: pl.BlockSpec(shp, lambda b, c: (0, 0))
    return pl.pallas_call(
        _ssd_kernel,
        grid=(B, nc),
        in_specs=[pl.BlockSpec((Q, A_CONV_DIM), lambda b, c: (row(b, c), 0)),
                  pl.BlockSpec((Q, LANE), lambda b, c: (row(b, c), 14 * TN // LANE)),
                  pl.BlockSpec((Q, A_WIDTH // 2), lambda b, c: (row(b, c), 3)),
                  pl.BlockSpec((Q, A_WIDTH // 2), lambda b, c: (row(b, c), 4)),
                  full((A_CONV_W, A_CONV_DIM)), full((1, A_CONV_DIM)), full((1, LANE)),
                  full((1, LANE)), full((1, A_WIDTH)), full((1, A_WIDTH))],
        out_specs=[pl.BlockSpec((Q, A_WIDTH), lambda b, c: (row(b, c), 0)),
                   pl.BlockSpec((None, A_WIDTH, A_STATE), lambda b, c: (b, 0, 0)),
                   pl.BlockSpec((None, 8, A_CONV_DIM), lambda b, c: (b, 0, 0))],
        out_shape=[jax.ShapeDtypeStruct((B * L, A_WIDTH), BF16),
                   jax.ShapeDtypeStruct((B, A_WIDTH, A_STATE), F32),
                   jax.ShapeDtypeStruct((B, 8, A_CONV_DIM), F32)],
        scratch_shapes=[pltpu.VMEM((A_WIDTH, A_STATE), F32), pltpu.VMEM((8, A_CONV_DIM), F32),
                        pltpu.VMEM((Q, A_WIDTH), F32)],
        compiler_params=_cp("arbitrary", "arbitrary"),
    )(a_arr, a_arr, a_arr, a_arr, cw, cb, dtb, alog, dsk, gnw)


def _ssd_step_kernel(xbc_ref, cs_ref, dt_ref, z_ref, h0_ref, cw_ref, cb_ref, dtb_ref, alog_ref,
                     e_ref, dsk_ref, gnw_ref, y_ref, hn_ref, nc_ref):
    x_raw = xbc_ref[...]
    cs = cs_ref[...]
    conv = (cb_ref[...] + cw_ref[0:1, :] * cs[0:1, :] + cw_ref[1:2, :] * cs[1:2, :]
            + cw_ref[2:3, :] * cs[2:3, :] + cw_ref[3:4, :] * x_raw)
    xc = _silu(conv)
    nc_ref[0:1, :] = cs[1:2, :]
    nc_ref[1:2, :] = cs[2:3, :]
    nc_ref[2:3, :] = x_raw

    dt = _softplus(dt_ref[...] + dtb_ref[...])
    da = dt * (-jnp.exp(alog_ref[...]))
    e = e_ref[...]
    dt_e = _dot3_right(jnp.broadcast_to(dt, (8, LANE)), e)[0:1, :]
    da_e = _dot3_right(jnp.broadcast_to(da, (8, LANE)), e)[0:1, :]
    xs = xc[:, :A_WIDTH]
    sub = lax.broadcasted_iota(jnp.int32, (8, 1), 0)
    rows = jnp.where(sub == 0, xs * dt_e, jnp.where(sub == 1, jnp.exp(da_e), 0.0))
    cols = jnp.concatenate([rows, jnp.zeros((LANE - 8, A_WIDTH), F32)], axis=0).T
    xdt_col = cols[:, 0:1]
    dec_col = cols[:, 1:2]
    ys = []
    for g in range(A_GROUPS):
        bg = xc[:, A_WIDTH + g * A_STATE:A_WIDTH + (g + 1) * A_STATE]
        cg = xc[:, A_WIDTH + (A_GROUPS + g) * A_STATE:A_WIDTH + (A_GROUPS + g + 1) * A_STATE]
        sl = slice(g * 512, (g + 1) * 512)
        hn = dec_col[sl, :] * h0_ref[sl, :] + xdt_col[sl, :] * bg
        hn_ref[sl, :] = hn
        c8 = jnp.broadcast_to(cg, (8, A_STATE)).astype(BF16)
        ys.append(lax.dot_general(c8, hn.astype(BF16), NT_DIMS, preferred_element_type=F32)[0:1, :])
    y = jnp.concatenate(ys, axis=1) + dsk_ref[...] * xs
    y_ref[...] = _rms(y * _silu(z_ref[...]), gnw_ref[...])


def _ssd_step(xbc, cs, dt, z, h0, cw, cb, dtb, alog, e, dsk, gnw):
    nb = xbc.shape[0]
    per_b = lambda n: pl.BlockSpec((None, 1, n), lambda b: (b, 0, 0))
    full = lambda shp: pl.BlockSpec(shp, lambda b: (0, 0))
    return pl.pallas_call(
        _ssd_step_kernel,
        grid=(nb,),
        in_specs=[per_b(A_CONV_DIM), pl.BlockSpec((None, 3, A_CONV_DIM), lambda b: (b, 0, 0)),
                  per_b(LANE), per_b(A_WIDTH),
                  pl.BlockSpec((None, A_WIDTH, A_STATE), lambda b: (b, 0, 0)),
                  full((A_CONV_W, A_CONV_DIM)), full((1, A_CONV_DIM)), full((1, LANE)),
                  full((1, LANE)), full((LANE, A_WIDTH)), full((1, A_WIDTH)), full((1, A_WIDTH))],
        out_specs=[per_b(A_WIDTH),
                   pl.BlockSpec((None, A_WIDTH, A_STATE), lambda b: (b, 0, 0)),
                   pl.BlockSpec((None, 3, A_CONV_DIM), lambda b: (b, 0, 0))],
        out_shape=[jax.ShapeDtypeStruct((nb, 1, A_WIDTH), F32),
                   jax.ShapeDtypeStruct((nb, A_WIDTH, A_STATE), F32),
                   jax.ShapeDtypeStruct((nb, 3, A_CONV_DIM), F32)],
        compiler_params=_cp("arbitrary"),
    )(xbc, cs, dt, z, h0, cw, cb, dtb, alog, e, dsk, gnw)


def _bias_tile(slope, dil):
    i = lax.broadcasted_iota(jnp.int32, (LANE, 2 * LANE), 0)
    c = lax.broadcasted_iota(jnp.int32, (LANE, 2 * LANE), 1)
    j = LANE + i - c
    valid = (j >= 0) & (j <= LANE)
    return jnp.where(valid, -(slope * (dil * j).astype(F32)), NEG), c


def _attn_tile(q, kcat, vcat, bias):
    s = lax.dot_general(q, kcat, NT_DIMS, preferred_element_type=F32) * ATT_SCALE + bias
    m = jnp.max(s, axis=1, keepdims=True)
    p = jnp.exp(s - m)
    l = jnp.sum(p, axis=1, keepdims=True)
    acc = jnp.dot(p.astype(BF16), vcat, preferred_element_type=F32)
    return m, l, acc


def _rows(start, stride):
    return pl.ds(start, LANE, stride=stride) if stride > 1 else pl.ds(start, LANE)


def _shift_cache_block(step, steps_per_seq, x_ref, nxt_ref, kvn_ref, o_ref):
    R = x_ref.shape[0]
    o_ref[pl.ds(0, R - 1)] = x_ref[pl.ds(1, R - 1)]
    last = step % steps_per_seq == steps_per_seq - 1
    o_ref[R - 1] = jnp.where(last, kvn_ref[...], nxt_ref[0])


def _attn_kernel(steps_per_seq, slope_ref, q_ref, k_ref, v_ref, kp_ref, vp_ref, gate_ref,
                 x_ref, nxt_ref, kvn_ref, wa_ref, wb_ref, y_ref, oc_ref, wa_out, wb_out,
                 m_scr, l_scr, acc_scr):
    step = ((pl.program_id(0) * pl.num_programs(1) + pl.program_id(1)) * pl.num_programs(2)
            + pl.program_id(2))
    _shift_cache_block(step, steps_per_seq, x_ref, nxt_ref, kvn_ref, oc_ref)
    wa_out[...] = wa_ref[...].astype(BF16)
    wb_out[...] = wb_ref[...].astype(BF16)
    first_block = pl.program_id(2) == 0
    slope = slope_ref[0:1, 0:1]
    for _, dil in B_PATTERNS:
        bias_full, c = _bias_tile(slope, dil)
        bias0 = jnp.where((c >= LANE) | jnp.logical_not(first_block), bias_full, NEG)
        nsub = ATT_BLOCK // dil // LANE
        for r in range(dil):
            kc = vc = None
            for sub in range(nsub):
                rows = _rows(sub * LANE * dil + r, dil)
                if sub == 0:
                    prow = _rows((nsub - 1) * LANE * dil + r, dil)
                    kp, vp, bias = kp_ref[prow, :], vp_ref[prow, :], bias0
                else:
                    kp, vp, bias = kc, vc, bias_full
                kc, vc = k_ref[rows, :], v_ref[rows, :]
                kcat = jnp.concatenate([kp, kc], axis=0).astype(BF16)
                vcat = jnp.concatenate([vp, vc], axis=0).astype(BF16)
                m, l, acc = _attn_tile(q_ref[rows, :].astype(BF16), kcat, vcat, bias)
                if dil == 1:
                    m_scr[rows, :] = jnp.broadcast_to(m, (LANE, LANE))
                    l_scr[rows, :] = jnp.broadcast_to(l, (LANE, LANE))
                    acc_scr[rows, :] = acc
                else:
                    m_old = m_scr[rows, :]
                    m_new = jnp.maximum(m_old, m)
                    a_old = jnp.exp(m_old - m_new)
                    a_new = jnp.exp(m - m_new)
                    m_scr[rows, :] = m_new
                    l_scr[rows, :] = a_old * l_scr[rows, :] + a_new * l
                    acc_scr[rows, :] = a_old * acc_scr[rows, :] + a_new * acc
    for sub in range(ATT_BLOCK // LANE):
        rs = pl.ds(sub * LANE, LANE)
        o = acc_scr[rs, :] / l_scr[rs, :]
        y_ref[rs, :] = (o * _silu(gate_ref[rs, :])).astype(y_ref.dtype)


def _attn(qkv, slopes, a_arr, cache6, kvn, wa, wb):
    B, H3, L, dh = qkv.shape
    H = H3 // 3
    R = ATT_BLOCK
    nblk = L // R
    cur = lambda part: pl.BlockSpec((None, None, R, dh), lambda b, h, i: (b, part * H + h, i, 0))
    prev = lambda part: pl.BlockSpec((None, None, R, dh),
                                     lambda b, h, i: (b, part * H + h, jnp.maximum(i - 1, 0), 0))
    slope = pl.BlockSpec((None, 8, LANE), lambda b, h, i: (h, 0, 0))
    gate = pl.BlockSpec((R, LANE), lambda b, h, i: (b * nblk + i, 10 * TN // LANE + h))

    nb, W = cache6.shape[1], cache6.shape[2]
    steps = B * H * nblk
    per = steps // nb
    rows = W // per
    assert per * nb == steps and rows * per == W, (steps, nb, W)
    tail = (2, B_HEADS, B_HEAD_DIM)
    step = lambda b, h, i: (b * H + h) * nblk + i
    cblk = pl.BlockSpec((None, None, rows) + tail,
                        lambda b, h, i: (0, step(b, h, i) // per, step(b, h, i) % per, 0, 0, 0))
    cnext = pl.BlockSpec((None, None, 1) + tail,
                         lambda b, h, i: (0, step(b, h, i) // per,
                                          jnp.minimum((step(b, h, i) % per + 1) * rows, W - 1), 0, 0, 0))
    cnew = pl.BlockSpec((None,) + tail, lambda b, h, i: (step(b, h, i) // per, 0, 0, 0))
    wa_spec, wa_shape = _cast_specs(wa, steps, step)
    wb_spec, wb_shape = _cast_specs(wb, steps, step)
    return pl.pallas_call(
        functools.partial(_attn_kernel, per),
        grid=(B, H, nblk),
        in_specs=[slope, cur(0), cur(1), cur(2), prev(1), prev(2), gate, cblk, cnext, cnew,
                  wa_spec, wb_spec],
        out_specs=[pl.BlockSpec((R, LANE), lambda b, h, i: (b * nblk + i, h)), cblk,
                   wa_spec, wb_spec],
        out_shape=[jax.ShapeDtypeStruct((B * L, H * dh), BF16),
                   jax.ShapeDtypeStruct(cache6.shape, F32), wa_shape, wb_shape],
        scratch_shapes=[pltpu.VMEM((R, LANE), F32)] * 3,
        compiler_params=_cp("arbitrary", "arbitrary", "arbitrary"),
    )(slopes, qkv, qkv, qkv, qkv, qkv, a_arr, cache6, cache6, kvn, wa, wb)


def _attn_step_kernel(bias_ref, q_ref, kn_ref, vn_ref, gate_ref, c1_ref, c4_ref, c16_ref, y_ref):
    q, kn, vn = q_ref[...], kn_ref[...], vn_ref[...]
    s_self = jnp.sum(q * kn, axis=-1, keepdims=True) * ATT_SCALE
    blocks = (c1_ref, c4_ref, c16_ref)
    scores = []
    for p, c_ref in enumerate(blocks):
        kblk = c_ref[:, 0]
        scores.append(jnp.sum(kblk * q[None], axis=-1, keepdims=True) * ATT_SCALE - bias_ref[p])
    m = s_self
    for s in scores:
        m = jnp.maximum(m, jnp.max(s, axis=0))
    p_self = float(len(B_PATTERNS)) * jnp.exp(s_self - m)
    l = p_self
    o = p_self * vn
    for s, c_ref in zip(scores, blocks):
        p = jnp.exp(s - m[None])
        l = l + jnp.sum(p, axis=0)
        o = o + jnp.sum(p * c_ref[:, 1], axis=0)
    y_ref[...] = (o / l) * _silu(gate_ref[...])


def _attn_step(q, kn, vn, gate, cache6, bias):
    nb = q.shape[0]
    W = B_MAX_WINDOW
    hspec = pl.BlockSpec((None, B_HEADS, B_HEAD_DIM), lambda b: (b, 0, 0))
    tail = (2, B_HEADS, B_HEAD_DIM)
    views, specs = [], []
    for _, dil in B_PATTERNS:
        if dil == 1:
            views.append(cache6)
            specs.append(pl.BlockSpec((None, None, LANE) + tail,
                                      lambda b: (0, b, W // LANE - 1, 0, 0, 0)))
        else:
            views.append(cache6.reshape((1, nb, W // dil, dil) + tail))
            specs.append(pl.BlockSpec((None, None, LANE, None) + tail,
                                      lambda b, dil=dil: (0, b, W // dil // LANE - 1, 0, 0, 0, 0)))
    return pl.pallas_call(
        _attn_step_kernel,
        grid=(nb,),
        in_specs=[pl.BlockSpec(bias.shape, lambda b: (0, 0, 0, 0)), hspec, hspec, hspec, hspec]
                 + specs,
        out_specs=hspec,
        out_shape=jax.ShapeDtypeStruct((nb, B_HEADS, B_HEAD_DIM), F32),
        compiler_params=_cp("arbitrary"),
    )(bias, q, kn, vn, gate, *views)


def _cmul(ar, ai, br, bi):
    return ar * br - ai * bi, ar * bi + ai * br


def _s5_prep_kernel(are_ref, aim_ref, ldt_ref, bre_ref, bim_ref, ctre_ref, ctim_ref,
                    t_ref, wb_ref, cb_ref, lam_ref, lamq_ref, cpl_ref):
    Q = S5_Q
    are, aim = are_ref[...], aim_ref[...]
    dt = jnp.exp(ldt_ref[...])
    mag = jnp.exp(dt * are)
    l_re = mag * jnp.cos(dt * aim)
    l_im = mag * jnp.sin(dt * aim)
    den = are * are + aim * aim
    f_re = ((l_re - 1.0) * are + l_im * aim) / den
    f_im = (l_im * are - (l_re - 1.0) * aim) / den
    lr, li = l_re[0:1, :], l_im[0:1, :]
    bb_re, bb_im = _cmul(f_re[0:1, :], f_im[0:1, :], bre_ref[...], bim_ref[...])
    ctre, ctim = ctre_ref[...], ctim_ref[...]
    cplain = jnp.concatenate([ctre, -ctim], axis=0).astype(BF16)
    cpl_ref[...] = cplain
    stk = jnp.concatenate([l_re, l_im, jnp.zeros((LANE - 16, S5_NS), F32)], axis=0).T
    lc_re, lc_im = stk[:, 0:1], stk[:, 8:9]
    p_re, p_im = jnp.ones((1, S5_NS), F32), jnp.zeros((1, S5_NS), F32)
    pc_re, pc_im = lc_re, lc_im
    dts = []
    for tau in range(Q):
        e_re, e_im = _cmul(bb_re, bb_im, p_re, p_im)
        wp = jnp.concatenate([e_re, e_im], axis=1).astype(BF16)
        s = Q - 1 - tau
        wb_ref[s * LANE:(s + 1) * LANE, :] = wp
        dts.append(jnp.dot(wp, cplain, preferred_element_type=F32).astype(BF16))
        c_re, c_im = _cmul(ctre, ctim, pc_re, pc_im)
        cb_ref[:, tau * LANE:(tau + 1) * LANE] = jnp.concatenate([c_re, -c_im], axis=0).astype(BF16)
        p_re, p_im = _cmul(p_re, p_im, lr, li)
        pc_re, pc_im = _cmul(pc_re, pc_im, lc_re, lc_im)
    zero = jnp.zeros((LANE, LANE), BF16)
    for s in range(Q):
        for q in range(Q):
            t_ref[s * LANE:(s + 1) * LANE, q * LANE:(q + 1) * LANE] = dts[q - s] if q >= s else zero
    lam_ref[...] = jnp.concatenate([l_re, l_im], axis=1)
    sub = lax.broadcasted_iota(jnp.int32, (8, 1), 0)
    q_re, q_im = p_re, p_im
    tab_re, tab_im = jnp.zeros((8, S5_NS), F32), jnp.zeros((8, S5_NS), F32)
    for r in range(8):
        tab_re = jnp.where(sub == r, q_re, tab_re)
        tab_im = jnp.where(sub == r, q_im, tab_im)
        q_re, q_im = _cmul(q_re, q_im, p_re, p_im)
    lamq_ref[...] = jnp.concatenate([tab_re, tab_im], axis=1)


def _s5_prep(are, aim, ldt, bre, bim, ctre, ctim):
    Q = S5_Q
    row = pl.BlockSpec((None, 8, S5_NS), lambda i: (i, 0, 0))
    blk = lambda a, b: pl.BlockSpec((None, a, b), lambda i: (i, 0, 0))
    return pl.pallas_call(
        _s5_prep_kernel,
        grid=(S5_NBLK,),
        in_specs=[row, row, row, blk(LANE, S5_NS), blk(LANE, S5_NS), blk(S5_NS, LANE), blk(S5_NS, LANE)],
        out_specs=[blk(Q * LANE, Q * LANE), blk(Q * LANE, 2 * S5_NS), blk(2 * S5_NS, Q * LANE),
                   blk(8, 2 * S5_NS), blk(8, 2 * S5_NS), blk(2 * S5_NS, LANE)],
        out_shape=[jax.ShapeDtypeStruct((S5_NBLK, Q * LANE, Q * LANE), BF16),
                   jax.ShapeDtypeStruct((S5_NBLK, Q * LANE, 2 * S5_NS), BF16),
                   jax.ShapeDtypeStruct((S5_NBLK, 2 * S5_NS, Q * LANE), BF16),
                   jax.ShapeDtypeStruct((S5_NBLK, 8, 2 * S5_NS), F32),
                   jax.ShapeDtypeStruct((S5_NBLK, 8, 2 * S5_NS), F32),
                   jax.ShapeDtypeStruct((S5_NBLK, 2 * S5_NS, LANE), BF16)],
        compiler_params=_cp("arbitrary"),
    )(are, aim, ldt, bre, bim, ctre, ctim)


def _s5_main_kernel(nb, u_ref, t_ref, wb_ref, cb_ref, lamq_ref, dsk_ref, g_ref, hfin_ref,
                    lhs_scr, s_scr, hp_scr, gs_scr):
    Q = S5_Q
    NS = S5_NS
    nrow = u_ref.shape[0] // Q
    nch = nrow // nb
    for s in range(Q):
        lhs_scr[:, s * LANE:(s + 1) * LANE] = u_ref[pl.ds(s, nrow, stride=Q), :].astype(BF16)
    lhs = lhs_scr[...]
    s_scr[...] = jnp.dot(lhs, wb_ref[...], preferred_element_type=F32)
    tab = lamq_ref[...]
    tab_re, tab_im = tab[:, :NS], tab[:, NS:]
    sub = lax.broadcasted_iota(jnp.int32, (8, 1), 0)

    def tile_body(t, carry):
        base = t * 8
        new = []
        for b in range(nb):
            h_re, h_im = carry[2 * b], carry[2 * b + 1]
            st = s_scr[pl.ds(b * nch + base, 8), :]
            x_re, x_im = st[:, :NS], st[:, NS:]
            for sh in (1, 2, 4):
                y_re = jnp.where(sub >= sh, pltpu.roll(x_re, sh, 0), 0.0)
                y_im = jnp.where(sub >= sh, pltpu.roll(x_im, sh, 0), 0.0)
                d_re, d_im = _cmul(tab_re[sh - 1:sh, :], tab_im[sh - 1:sh, :], y_re, y_im)
                x_re, x_im = x_re + d_re, x_im + d_im
            c_re, c_im = _cmul(tab_re, tab_im, h_re, h_im)
            x_re, x_im = x_re + c_re, x_im + c_im
            p_re = jnp.where(sub == 0, h_re, pltpu.roll(x_re, 1, 0))
            p_im = jnp.where(sub == 0, h_im, pltpu.roll(x_im, 1, 0))
            hp_scr[pl.ds(b * nch + base, 8), :] = jnp.concatenate([p_re, p_im], axis=1)
            new += [x_re[7:8, :], x_im[7:8, :]]
        return tuple(new)

    fin = (jnp.zeros((1, NS), F32),) * (2 * nb)
    for t in range(nch // 8):
        fin = tile_body(t, fin)
    for b in range(nb):
        hfin_ref[b:b + 1, :] = jnp.concatenate([fin[2 * b], fin[2 * b + 1]], axis=1)

    y = (jnp.dot(lhs, t_ref[...], preferred_element_type=F32)
         + jnp.dot(hp_scr[...].astype(BF16), cb_ref[...], preferred_element_type=F32))
    for q in range(Q):
        rows = pl.ds(q, nrow, stride=Q)
        yq = y[:, q * LANE:(q + 1) * LANE] + dsk_ref[...] * u_ref[rows, :]
        gs_scr[rows, :] = jax.nn.gelu(yq)
    g_ref[...] = gs_scr[...].astype(g_ref.dtype)


def _s5_main(u_gate, nb, t, wb, cb, lamq, dsk):
    Q = S5_Q
    M = u_gate.shape[0]
    nrow = M // Q
    blk = lambda a, b: pl.BlockSpec((None, a, b), lambda i: (i, 0, 0))
    return pl.pallas_call(
        functools.partial(_s5_main_kernel, nb),
        grid=(S5_NBLK,),
        in_specs=[pl.BlockSpec((M, LANE), lambda i: (0, i)),
                  blk(Q * LANE, Q * LANE), blk(Q * LANE, 2 * S5_NS), blk(2 * S5_NS, Q * LANE),
                  blk(8, 2 * S5_NS), pl.BlockSpec((1, LANE), lambda i: (0, i))],
        out_specs=[pl.BlockSpec((M, LANE), lambda i: (0, i)), blk(nb, 2 * S5_NS)],
        out_shape=[jax.ShapeDtypeStruct((M, E1), BF16),
                   jax.ShapeDtypeStruct((S5_NBLK, nb, 2 * S5_NS), F32)],
        scratch_shapes=[pltpu.VMEM((nrow, Q * LANE), BF16), pltpu.VMEM((nrow, 2 * S5_NS), F32),
                        pltpu.VMEM((nrow, 2 * S5_NS), F32), pltpu.VMEM((M, LANE), F32)],
        compiler_params=_cp("arbitrary"),
    )(u_gate, t, wb, cb, lamq, dsk)


def _s5_step_kernel(u_ref, h0_ref, bb_ref, cpl_ref, lam_ref, dsk_ref, g_ref, hn_ref):
    NS = S5_NS
    u = u_ref[...]
    s = jnp.dot(u.astype(BF16), bb_ref[...], preferred_element_type=F32)
    lam = lam_ref[0:1, :]
    h0 = h0_ref[...]
    n_re, n_im = _cmul(lam[:, :NS], lam[:, NS:], h0[:, :NS], h0[:, NS:])
    hn = jnp.concatenate([n_re + s[:, :NS], n_im + s[:, NS:]], axis=1)
    hn_ref[...] = hn
    y = jnp.dot(hn.astype(BF16), cpl_ref[...], preferred_element_type=F32) + dsk_ref[...] * u
    g_ref[...] = jax.nn.gelu(y)


def _s5_step(u_gate, h0, wb, cpl, lam, dsk):
    nb = u_gate.shape[0]
    blk = lambda a, b: pl.BlockSpec((None, a, b), lambda i: (i, 0, 0))
    return pl.pallas_call(
        _s5_step_kernel,
        grid=(S5_NBLK,),
        in_specs=[pl.BlockSpec((nb, LANE), lambda i: (0, i)), blk(nb, 2 * S5_NS),
                  pl.BlockSpec((None, LANE, 2 * S5_NS), lambda i: (i, S5_Q - 1, 0)),
                  blk(2 * S5_NS, LANE), blk(8, 2 * S5_NS), pl.BlockSpec((1, LANE), lambda i: (0, i))],
        out_specs=[pl.BlockSpec((nb, LANE), lambda i: (0, i)), blk(nb, 2 * S5_NS)],
        out_shape=[jax.ShapeDtypeStruct((nb, E1), F32),
                   jax.ShapeDtypeStruct((S5_NBLK, nb, 2 * S5_NS), F32)],
        compiler_params=_cp("arbitrary"),
    )(u_gate, h0, wb, cpl, lam, dsk)


def _state_to_blocks(st):
    nb = st.shape[0]
    x = st.reshape(nb, S5_NBLK, S5_NS, 2)
    return jnp.transpose(x, (1, 0, 3, 2)).reshape(S5_NBLK, nb, 2 * S5_NS)


def _blocks_to_state(h):
    nb = h.shape[1]
    x = h.reshape(S5_NBLK, nb, 2, S5_NS)
    return jnp.transpose(x, (1, 0, 3, 2)).reshape(nb, C_GROUPS, C_STATE, 2)


def kernel(x_prompt, x_sample, state_a_ssm, state_a_conv, cache_b_kv, state_c, norm_w, final_norm_w, even_w_in, a_conv_w, a_conv_b, a_dt_bias, a_A_log, a_D, a_norm_w, even_w_out, odd_w_in, c_A_re, c_A_im, c_B_re, c_B_im, c_C_re, c_C_im, c_D, c_log_dt, c_glu_w, c_glu_b, odd_w_out):
    B, L, D = x_prompt.shape
    nS = x_sample.shape[0]
    M = B * L
    Q = S5_Q

    w0 = even_w_in[0]
    s1 = A_WIDTH + A_CONV_DIM
    s2 = s1 + A_HEADS
    seg = lambda a, b: w0[:, a:b].astype(BF16)
    w_dt = jnp.pad(seg(s1, s2), ((0, 0), (0, A_TILE - A_HEADS)))
    w_a = jnp.concatenate([seg(A_WIDTH, s1), seg(0, A_WIDTH), seg(s2 + 6144, s2 + 8192), w_dt], axis=1)
    w_qkv = seg(s2, s2 + 6144)
    nw0 = norm_w[0][None, :]
    nw1 = norm_w[1][None, :]
    fnw = final_norm_w[None, :]
    cw, cb = a_conv_w[0], a_conv_b[0][None, :]
    pad_h = lambda v: jnp.pad(v, (0, LANE - A_HEADS))[None, :]
    dtb, alog = pad_h(a_dt_bias[0]), pad_h(a_A_log[0])
    dsk_a = jnp.repeat(a_D[0], A_HEAD_DIM)[None, :]
    gnw = a_norm_w[0][None, :]
    head_expand = np.kron(np.eye(A_HEADS, dtype=np.float32), np.ones((1, A_HEAD_DIM), np.float32))
    head_expand = jnp.asarray(np.pad(head_expand, ((0, LANE - A_HEADS), (0, 0)))).astype(BF16)
    slopes_np = np.asarray(2.0 ** (-8.0 * np.arange(1, B_HEADS + 1) / B_HEADS), np.float32)
    slopes = jnp.asarray(np.broadcast_to(slopes_np[:, None, None], (B_HEADS, 8, LANE)))
    dist = np.stack([dil * (LANE - np.arange(LANE)) for _, dil in B_PATTERNS]).astype(np.float32)
    step_bias = jnp.asarray((dist[:, :, None] * slopes_np[None, None, :])[..., None])

    eye = jnp.eye(S5_BLK, dtype=F32)
    rows8 = lambda p: jnp.broadcast_to(p.reshape(S5_NBLK, 1, S5_NS), (S5_NBLK, 8, S5_NS))
    bexp = lambda p: jnp.einsum("kgnc,gh->kgchn", p.reshape(S5_NBLK, S5_BLK, C_STATE, C_GROUP),
                                eye).reshape(S5_NBLK, LANE, S5_NS)
    cexp = lambda p: jnp.einsum("kgcn,gh->kgnhc", p.reshape(S5_NBLK, S5_BLK, C_GROUP, C_STATE),
                                eye).reshape(S5_NBLK, S5_NS, LANE)
    ldt = jnp.repeat(c_log_dt[0], C_STATE).reshape(S5_NBLK, S5_NS)
    t_m, wb_m, cb_m, lam, lamq, cpl = _s5_prep(rows8(c_A_re[0]), rows8(c_A_im[0]), rows8(ldt),
                                                bexp(c_B_re[0]), bexp(c_B_im[0]),
                                                cexp(c_C_re[0]), cexp(c_C_im[0]))
    dsk_c = c_D[0][None, :]
    glu_b = c_glu_b[0][None, :]

    xs = x_sample.reshape(nS, D)
    a_s, xs_n = _mm_norm(xs, nw0, w_a, nS, 2048, keep_xn=True)
    (qkv_s,) = _proj_qkv(xs_n, w_qkv, 1, nS, nS, window=False)
    col = lambda a, b: a_s[:, a:b]
    hv = lambda part: jnp.transpose(qkv_s[0, part * B_HEADS:(part + 1) * B_HEADS], (1, 0, 2))
    gate_s = col(5120, 7168).reshape(nS, B_HEADS, B_HEAD_DIM)
    kvn = jnp.stack([hv(1), hv(2)], axis=1)

    xp = x_prompt.reshape(M, D)
    a_arr, xp_n, w_glu = _mm_norm(xp, nw0, w_a, 1024, A_TILE, keep_xn=True, cast=c_glu_w[0])
    qkv, kv_keep, w_in1 = _proj_qkv(xp_n, w_qkv, B, L, 1024, window=True, cast=odd_w_in[0])
    y_a, ssm_p, tail_p = _ssd(a_arr, B, L, cw, cb, dtb, alog, dsk_a, gnw)
    y_b, new_cache, w_out0, w_out1 = _attn(qkv, slopes, a_arr, cache_b_kv, kvn,
                                           even_w_out[0], odd_w_out[0])
    hp1 = _outproj0(y_a, y_b, w_out0, xp, 1024)
    u_gate = _mm_norm(hp1, nw1, w_in1, 1024, 1024)
    g, hfin_p = _s5_main(u_gate, B, t_m, wb_m, cb_m, lamq, dsk_c)
    y1 = _glu(g, w_glu, glu_b, u_gate, 1024, BF16)
    y_prompt = _outproj1(y1, w_out1, hp1, fnw, 1024).reshape(B, L, D)

    y_as, ssm_s, conv_s = _ssd_step(col(0, 3072)[:, None, :], state_a_conv[0],
                                    col(7168, 7296)[:, None, :], col(3072, 5120)[:, None, :],
                                    state_a_ssm[0].reshape(nS, A_WIDTH, A_STATE),
                                    cw, cb, dtb, alog, head_expand, dsk_a, gnw)
    y_bs = _attn_step(hv(0), hv(1), hv(2), gate_s, cache_b_kv, step_bias)
    hs1 = _outproj0(y_as.reshape(nS, A_WIDTH), y_bs.reshape(nS, A_WIDTH), w_out0, xs, nS, 1024)
    u_gate_s = _mm_norm(hs1, nw1, w_in1, nS, 2048)
    g_s, hnew_s = _s5_step(u_gate_s, _state_to_blocks(state_c[0]), wb_m, cpl, lam, dsk_c)
    y1_s = _glu(g_s, w_glu, glu_b, u_gate_s, nS, F32, 1024)
    y_sample = _outproj1(y1_s, w_out1, hs1, fnw, nS).reshape(nS, 1, D)

    return (y_prompt, y_sample,
            ssm_p.reshape(1, B, A_HEADS, A_HEAD_DIM, A_STATE),
            tail_p[:, 5:8, :][None],
            kv_keep[None],
            _blocks_to_state(hfin_p)[None],
            ssm_s.reshape(1, nS, A_HEADS, A_HEAD_DIM, A_STATE),
            conv_s[None],
            new_cache,
            _blocks_to_state(hnew_s)[None])
```

```python
import functools

import numpy as np
import jax
import jax.numpy as jnp
from jax import lax
from jax.experimental import pallas as pl
from jax.experimental.pallas import tpu as pltpu

F32 = jnp.float32
BF16 = jnp.bfloat16

D_MODEL = 2048
EPS = 1e-5
A_WIDTH = 2048
A_HEAD_DIM = 64
A_HEADS = 32
A_GROUPS = 4
A_STATE = 128
A_CONV_W = 4
A_CONV_DIM = 3072
SSD_CHUNK = 128
B_HEADS = 16
B_HEAD_DIM = 128
B_PATTERNS = ((128, 1), (512, 4), (2048, 16))
B_MAX_WINDOW = 2048
C_GROUP = 16
C_GROUPS = 256
C_STATE = 64
E1 = 4096

LANE = 128
S5_Q = 8
S5_BLK = 8
S5_NBLK = C_GROUPS // S5_BLK
S5_NS = S5_BLK * C_STATE
TN = 512
A_TILE = 1024
ATT_BLOCK = 2048
NEG = -1e30
ATT_SCALE = B_HEAD_DIM ** -0.5
VMEM_LIMIT = 56 * 1024 * 1024

NT_DIMS = (((1,), (1,)), ((), ()))


def _cp(*sem):
    return pltpu.CompilerParams(dimension_semantics=sem, vmem_limit_bytes=VMEM_LIMIT)


def _silu(x):
    return x * jax.nn.sigmoid(x)


def _softplus(x):
    return jnp.maximum(x, 0.0) + jnp.log1p(jnp.exp(-jnp.abs(x)))


def _split3(a):
    hi = a.astype(BF16)
    r = a - hi.astype(F32)
    mid = r.astype(BF16)
    lo = (r - mid.astype(F32)).astype(BF16)
    return hi, mid, lo


def _dot3_left(m, a):
    hi, mid, lo = _split3(a)
    d = lambda p: jnp.dot(m, p, preferred_element_type=F32)
    return (d(lo) + d(mid)) + d(hi)


def _dot3_right(a, m):
    hi, mid, lo = _split3(a)
    d = lambda p: jnp.dot(p, m, preferred_element_type=F32)
    return (d(lo) + d(mid)) + d(hi)


def _rms(x, w):
    ms = jnp.mean(x * x, axis=-1, keepdims=True)
    return (x * lax.rsqrt(ms + EPS)) * w


QKV_HEADS = 8
CAST_ROWS = 64


def _cast_specs(w, nsteps, step_of):
    rows, cols = w.shape
    per = CAST_ROWS
    while rows // per > nsteps:
        per *= 2
    nblk = rows // per
    assert nblk * per == rows, (w.shape, nsteps)
    spec = pl.BlockSpec((per, cols), lambda *ids: (jnp.minimum(step_of(*ids), nblk - 1), 0))
    return spec, jax.ShapeDtypeStruct(w.shape, BF16)


def _proj_qkv_kernel(rpb, nskip, window, cast, *refs):
    refs = list(refs)
    x_ref, w_ref = refs[:2]
    if cast:
        refs[-1][...] = refs[2][...].astype(BF16)
    qkv_ref = refs[3 if cast else 2]
    j = pl.program_id(1)
    acc = jnp.dot(x_ref[...].astype(BF16), w_ref[...], preferred_element_type=F32)
    for hh in range(QKV_HEADS):
        qkv_ref[hh] = acc[:, hh * LANE:(hh + 1) * LANE]
    if window:
        kv_ref = refs[4 if cast else 3]
        tiles_per = B_HEADS // QKV_HEADS
        kept = pl.program_id(0) % rpb >= nskip
        for part in range(tiles_per):
            @pl.when(kept & (j >= tiles_per) & (j % tiles_per == part))
            def _(part=part):
                for hh in range(QKV_HEADS):
                    kv_ref[:, part * QKV_HEADS + hh, :] = acc[:, hh * LANE:(hh + 1) * LANE]


def _proj_qkv(xn, w, B, L, tm, window, cast=None):
    M = B * L
    tn = QKV_HEADS * LANE
    rpb = L // tm
    keep = min(B_MAX_WINDOW, L)
    nskip = (L - keep) // tm
    tiles_per = B_HEADS // QKV_HEADS

    def kv_map(i, j):
        ri = i % rpb
        return (i // rpb, jnp.maximum(ri - nskip, 0),
                jnp.where((ri < nskip) | (j < 2 * tiles_per), 0, 1), 0, 0)

    out_specs = [pl.BlockSpec((None, QKV_HEADS, tm, LANE), lambda i, j: (i // rpb, j, i % rpb, 0))]
    out_shape = [jax.ShapeDtypeStruct((B, 3 * B_HEADS, L, B_HEAD_DIM), F32)]
    if window:
        out_specs.append(pl.BlockSpec((None, tm, None, B_HEADS, B_HEAD_DIM), kv_map))
        out_shape.append(jax.ShapeDtypeStruct((B, keep, 2, B_HEADS, B_HEAD_DIM), F32))
    nj = w.shape[1] // tn
    in_specs = [pl.BlockSpec((tm, D_MODEL), lambda i, j: (i, 0)),
                pl.BlockSpec((D_MODEL, tn), lambda i, j: (0, j))]
    operands = [xn, w]
    if cast is not None:
        spec, shape = _cast_specs(cast, (M // tm) * nj, lambda i, j: i * nj + j)
        in_specs.append(spec)
        operands.append(cast)
        out_specs.append(spec)
        out_shape.append(shape)
    return pl.pallas_call(
        functools.partial(_proj_qkv_kernel, rpb, nskip, window, cast is not None),
        grid=(M // tm, nj),
        in_specs=in_specs,
        out_specs=out_specs,
        out_shape=out_shape,
        compiler_params=_cp("arbitrary", "arbitrary"),
    )(*operands)


def _mm_norm_kernel(cast, *refs):
    refs = list(refs)
    x_ref, nw_ref, w_ref = refs[:3]
    if cast:
        refs[-1][...] = refs[3][...].astype(BF16)
    o_ref, xn_ref = refs[4:6] if cast else refs[3:5]

    @pl.when(pl.program_id(1) == 0)
    def _():
        xn_ref[...] = _rms(x_ref[...], nw_ref[...]).astype(xn_ref.dtype)

    o_ref[...] = jnp.dot(xn_ref[...].astype(BF16), w_ref[...], preferred_element_type=F32)


def _mm_norm(x, nw, w, tm, tn=TN, keep_xn=False, cast=None):
    M = x.shape[0]
    K, N = w.shape
    nj = N // tn
    in_specs = [pl.BlockSpec((tm, K), lambda i, j: (i, 0)),
                pl.BlockSpec((1, K), lambda i, j: (0, 0)),
                pl.BlockSpec((K, tn), lambda i, j: (0, j))]
    operands = [x, nw, w]
    out_specs = [pl.BlockSpec((tm, tn), lambda i, j: (i, j))]
    out_shape = [jax.ShapeDtypeStruct((M, N), F32)]
    if keep_xn:
        out_specs.append(pl.BlockSpec((tm, K), lambda i, j: (i, 0)))
        out_shape.append(jax.ShapeDtypeStruct((M, K), BF16 if tm % 16 == 0 else F32))
    if cast is not None:
        assert keep_xn
        spec, shape = _cast_specs(cast, (M // tm) * nj, lambda i, j: i * nj + j)
        in_specs.append(spec)
        operands.append(cast)
        out_specs.append(spec)
        out_shape.append(shape)
    out = pl.pallas_call(
        functools.partial(_mm_norm_kernel, cast is not None),
        grid=(M // tm, nj),
        in_specs=in_specs,
        out_specs=out_specs,
        out_shape=out_shape,
        scratch_shapes=[] if keep_xn else [pltpu.VMEM((tm, K), BF16)],
        compiler_params=_cp("arbitrary", "arbitrary"),
    )(*operands)
    return out if len(out) > 1 else out[0]


def _outproj0_kernel(ya_ref, yb_ref, wa_ref, wb_ref, res_ref, o_ref):
    o_ref[...] = (res_ref[...]
                  + jnp.dot(ya_ref[...].astype(BF16), wa_ref[...], preferred_element_type=F32)
                  + jnp.dot(yb_ref[...].astype(BF16), wb_ref[...], preferred_element_type=F32))


def _outproj0(ya, yb, w, res, tm, tn=TN):
    M = ya.shape[0]
    K = ya.shape[1]
    return pl.pallas_call(
        _outproj0_kernel,
        grid=(M // tm, D_MODEL // tn),
        in_specs=[pl.BlockSpec((tm, K), lambda i, j: (i, 0)),
                  pl.BlockSpec((tm, K), lambda i, j: (i, 0)),
                  pl.BlockSpec((K, tn), lambda i, j: (0, j)),
                  pl.BlockSpec((K, tn), lambda i, j: (1, j)),
                  pl.BlockSpec((tm, tn), lambda i, j: (i, j))],
        out_specs=pl.BlockSpec((tm, tn), lambda i, j: (i, j)),
        out_shape=jax.ShapeDtypeStruct((M, D_MODEL), F32),
        compiler_params=_cp("arbitrary", "arbitrary"),
    )(ya, yb, w, w, res)


def _glu_kernel(g_ref, wa_ref, wb_ref, ba_ref, bb_ref, gate_ref, o_ref):
    tm = g_ref.shape[0]
    hr = min(tm, 512)
    for r in range(tm // hr):
        rs = pl.ds(r * hr, hr)
        g = g_ref[rs, :].astype(BF16)
        ga = jnp.dot(g, wa_ref[...], preferred_element_type=F32) + ba_ref[...]
        gb = jnp.dot(g, wb_ref[...], preferred_element_type=F32) + bb_ref[...]
        o_ref[rs, :] = (ga * jax.nn.sigmoid(gb) * _silu(gate_ref[rs, :])).astype(o_ref.dtype)


def _glu(g, w, bias, u_gate, tm, out_dtype, tn=TN):
    M = g.shape[0]
    nb = E1 // tn
    return pl.pallas_call(
        _glu_kernel,
        grid=(M // tm, nb),
        in_specs=[pl.BlockSpec((tm, E1), lambda i, j: (i, 0)),
                  pl.BlockSpec((E1, tn), lambda i, j: (0, j)),
                  pl.BlockSpec((E1, tn), lambda i, j: (0, j + nb)),
                  pl.BlockSpec((1, tn), lambda i, j: (0, j)),
                  pl.BlockSpec((1, tn), lambda i, j: (0, j + nb)),
                  pl.BlockSpec((tm, tn), lambda i, j: (i, j + nb))],
        out_specs=pl.BlockSpec((tm, tn), lambda i, j: (i, j)),
        out_shape=jax.ShapeDtypeStruct((M, E1), out_dtype),
        compiler_params=_cp("arbitrary", "arbitrary"),
    )(g, w, w, bias, bias, u_gate)


def _outproj1_kernel(y_ref, w_ref, res_ref, fnw_ref, o_ref):
    j = pl.program_id(1)
    nj = D_MODEL // TN
    h = res_ref[...] + jnp.dot(y_ref[...].astype(BF16), w_ref[...], preferred_element_type=F32)
    for jj in range(nj):
        @pl.when(j == jj)
        def _(jj=jj):
            o_ref[:, jj * TN:(jj + 1) * TN] = h

    @pl.when(j == nj - 1)
    def _():
        o_ref[...] = _rms(o_ref[...], fnw_ref[...])


def _outproj1(y, w, res, fnw, tm):
    M, K = y.shape
    return pl.pallas_call(
        _outproj1_kernel,
        grid=(M // tm, D_MODEL // TN),
        in_specs=[pl.BlockSpec((tm, K), lambda i, j: (i, 0)),
                  pl.BlockSpec((K, TN), lambda i, j: (0, j)),
                  pl.BlockSpec((tm, TN), lambda i, j: (i, j)),
                  pl.BlockSpec((1, D_MODEL), lambda i, j: (0, 0))],
        out_specs=pl.BlockSpec((tm, D_MODEL), lambda i, j: (i, 0)),
        out_shape=jax.ShapeDtypeStruct((M, D_MODEL), F32),
        compiler_params=_cp("arbitrary", "arbitrary"),
    )(y, w, res, fnw)


def _ssd_kernel(xbc_ref, dt_ref, z0_ref, z1_ref, cw_ref, cb_ref, dtb_ref, alog_ref, dsk_ref, gnw_ref,
                y_ref, st_ref, tail_ref, h_scr, tail_scr, y_scr):
    c = pl.program_id(1)
    Q = SSD_CHUNK

    @pl.when(c == 0)
    def _():
        h_scr[...] = jnp.zeros_like(h_scr)
        tail_scr[...] = jnp.zeros_like(tail_scr)

    x_raw = xbc_ref[...]
    tail8 = tail_scr[...]
    row8 = lax.broadcasted_iota(jnp.int32, (8, 1), 0)
    conv = cb_ref[...] + cw_ref[3:4, :] * x_raw
    for k in (1, 2, 3):
        xr = pltpu.roll(x_raw, k, 0)
        first = jnp.where(row8 < k, pltpu.roll(tail8, k, 0), xr[0:8, :])
        conv = conv + cw_ref[3 - k:4 - k, :] * jnp.concatenate([first, xr[8:, :]], axis=0)
    tail_scr[...] = x_raw[Q - 8:Q, :]
    xc = _silu(conv)

    dt = _softplus(dt_ref[...] + dtb_ref[...])
    a = dt * (-jnp.exp(alog_ref[...]))
    qi = lax.broadcasted_iota(jnp.int32, (Q, Q), 0)
    si = lax.broadcasted_iota(jnp.int32, (Q, Q), 1)
    tri = qi >= si
    a_cum = _dot3_left(tri.astype(BF16), a)
    a_cum_t = a_cum.T
    lo_half = si < A_HEAD_DIM

    for g in range(A_GROUPS):
        bg = xc[:, A_WIDTH + g * A_STATE:A_WIDTH + (g + 1) * A_STATE].astype(BF16)
        cg = xc[:, A_WIDTH + (A_GROUPS + g) * A_STATE:A_WIDTH + (A_GROUPS + g + 1) * A_STATE].astype(BF16)
        cbm = lax.dot_general(cg, bg, NT_DIMS, preferred_element_type=F32)
        hprev = h_scr[g * 512:(g + 1) * 512, :]
        yoff = lax.dot_general(cg, hprev.astype(BF16), NT_DIMS, preferred_element_type=F32)
        for jj in range(4):
            p = g * 4 + jj
            h0, h1 = 2 * p, 2 * p + 1
            dtp = jnp.where(lo_half, dt[:, h0:h0 + 1], dt[:, h1:h1 + 1])
            ap = jnp.where(lo_half, a_cum[:, h0:h0 + 1], a_cum[:, h1:h1 + 1])
            x_p = xc[:, p * LANE:(p + 1) * LANE]
            xdt = x_p * dtp
            g0 = cbm * jnp.exp(jnp.where(tri, a_cum[:, h0:h0 + 1] - a_cum_t[h0:h0 + 1, :], NEG))
            g1 = cbm * jnp.exp(jnp.where(tri, a_cum[:, h1:h1 + 1] - a_cum_t[h1:h1 + 1, :], NEG))
            lhs = jnp.concatenate([g0, g1], axis=1).astype(BF16)
            rhs = jnp.concatenate([jnp.where(lo_half, xdt, 0.0), jnp.where(lo_half, 0.0, xdt)],
                                  axis=0).astype(BF16)
            ydiag = jnp.dot(lhs, rhs, preferred_element_type=F32)
            y_scr[:, p * LANE:(p + 1) * LANE] = (
                ydiag + yoff[:, jj * LANE:(jj + 1) * LANE] * jnp.exp(ap)
                + dsk_ref[:, p * LANE:(p + 1) * LANE] * x_p)
            xd = xdt * jnp.exp(ap[Q - 1:Q, :] - ap)
            st = jnp.dot(xd.T.astype(BF16), bg, preferred_element_type=F32)
            cd = jnp.exp(jnp.where(qi < A_HEAD_DIM, a_cum_t[h0:h0 + 1, Q - 1:Q],
                                   a_cum_t[h1:h1 + 1, Q - 1:Q]))
            h_scr[p * LANE:(p + 1) * LANE, :] = cd * hprev[jj * LANE:(jj + 1) * LANE, :] + st

    z = jnp.concatenate([z0_ref[...], z1_ref[...]], axis=1)
    y_ref[...] = _rms(y_scr[...] * _silu(z), gnw_ref[...]).astype(y_ref.dtype)

    @pl.when(c == pl.num_programs(1) - 1)
    def _():
        st_ref[...] = h_scr[...]
        tail_ref[...] = x_raw[Q - 8:Q, :]


def _ssd(a_arr, B, L, cw, cb, dtb, alog, dsk, gnw):
    nc = L // SSD_CHUNK
    Q = SSD_CHUNK
    row = lambda b, c: b * nc + c
    full = lambda shp: pl.BlockSpec(shp, lambda b, c: (0, 0))
    return pl.pallas_call(
        _ssd_kernel,
        grid=(B, nc),
        in_specs=[pl.BlockSpec((Q, A_CONV_DIM), lambda b, c: (row(b, c), 0)),
                  pl.BlockSpec((Q, LANE), lambda b, c: (row(b, c), 14 * TN // LANE)),
                  pl.BlockSpec((Q, A_WIDTH // 2), lambda b, c: (row(b, c), 3)),
                  pl.BlockSpec((Q, A_WIDTH // 2), lambda b, c: (row(b, c), 4)),
                  full((A_CONV_W, A_CONV_DIM)), full((1, A_CONV_DIM)), full((1, LANE)),
                  full((1, LANE)), full((1, A_WIDTH)), full((1, A_WIDTH))],
        out_specs=[pl.BlockSpec((Q, A_WIDTH), lambda b, c: (row(b, c), 0)),
                   pl.BlockSpec((None, A_WIDTH, A_STATE), lambda b, c: (b, 0, 0)),
                   pl.BlockSpec((None, 8, A_CONV_DIM), lambda b, c: (b, 0, 0))],
        out_shape=[jax.ShapeDtypeStruct((B * L, A_WIDTH), BF16),
                   jax.ShapeDtypeStruct((B, A_WIDTH, A_STATE), F32),
                   jax.ShapeDtypeStruct((B, 8, A_CONV_DIM), F32)],
        scratch_shapes=[pltpu.VMEM((A_WIDTH, A_STATE), F32), pltpu.VMEM((8, A_CONV_DIM), F32),
                        pltpu.VMEM((Q, A_WIDTH), F32)],
        compiler_params=_cp("arbitrary", "arbitrary"),
    )(a_arr, a_arr, a_arr, a_arr, cw, cb, dtb, alog, dsk, gnw)


def _ssd_step_kernel(xbc_ref, cs_ref, dt_ref, z_ref, h0_ref, cw_ref, cb_ref, dtb_ref, alog_ref,
                     e_ref, dsk_ref, gnw_ref, y_ref, hn_ref, nc_ref):
    x_raw = xbc_ref[...]
    cs = cs_ref[...]
    conv = (cb_ref[...] + cw_ref[0:1, :] * cs[0:1, :] + cw_ref[1:2, :] * cs[1:2, :]
            + cw_ref[2:3, :] * cs[2:3, :] + cw_ref[3:4, :] * x_raw)
    xc = _silu(conv)
    nc_ref[0:1, :] = cs[1:2, :]
    nc_ref[1:2, :] = cs[2:3, :]
    nc_ref[2:3, :] = x_raw

    dt = _softplus(dt_ref[...] + dtb_ref[...])
    da = dt * (-jnp.exp(alog_ref[...]))
    e = e_ref[...]
    dt_e = _dot3_right(jnp.broadcast_to(dt, (8, LANE)), e)[0:1, :]
    da_e = _dot3_right(jnp.broadcast_to(da, (8, LANE)), e)[0:1, :]
    xs = xc[:, :A_WIDTH]
    sub = lax.broadcasted_iota(jnp.int32, (8, 1), 0)
    rows = jnp.where(sub == 0, xs * dt_e, jnp.where(sub == 1, jnp.exp(da_e), 0.0))
    cols = jnp.concatenate([rows, jnp.zeros((LANE - 8, A_WIDTH), F32)], axis=0).T
    xdt_col = cols[:, 0:1]
    dec_col = cols[:, 1:2]
    ys = []
    for g in range(A_GROUPS):
        bg = xc[:, A_WIDTH + g * A_STATE:A_WIDTH + (g + 1) * A_STATE]
        cg = xc[:, A_WIDTH + (A_GROUPS + g) * A_STATE:A_WIDTH + (A_GROUPS + g + 1) * A_STATE]
        sl = slice(g * 512, (g + 1) * 512)
        hn = dec_col[sl, :] * h0_ref[sl, :] + xdt_col[sl, :] * bg
        hn_ref[sl, :] = hn
        c8 = jnp.broadcast_to(cg, (8, A_STATE)).astype(BF16)
        ys.append(lax.dot_general(c8, hn.astype(BF16), NT_DIMS, preferred_element_type=F32)[0:1, :])
    y = jnp.concatenate(ys, axis=1) + dsk_ref[...] * xs
    y_ref[...] = _rms(y * _silu(z_ref[...]), gnw_ref[...])


def _ssd_step(xbc, cs, dt, z, h0, cw, cb, dtb, alog, e, dsk, gnw):
    nb = xbc.shape[0]
    per_b = lambda n: pl.BlockSpec((None, 1, n), lambda b: (b, 0, 0))
    full = lambda shp: pl.BlockSpec(shp, lambda b: (0, 0))
    return pl.pallas_call(
        _ssd_step_kernel,
        grid=(nb,),
        in_specs=[per_b(A_CONV_DIM), pl.BlockSpec((None, 3, A_CONV_DIM), lambda b: (b, 0, 0)),
                  per_b(LANE), per_b(A_WIDTH),
                  pl.BlockSpec((None, A_WIDTH, A_STATE), lambda b: (b, 0, 0)),
                  full((A_CONV_W, A_CONV_DIM)), full((1, A_CONV_DIM)), full((1, LANE)),
                  full((1, LANE)), full((LANE, A_WIDTH)), full((1, A_WIDTH)), full((1, A_WIDTH))],
        out_specs=[per_b(A_WIDTH),
                   pl.BlockSpec((None, A_WIDTH, A_STATE), lambda b: (b, 0, 0)),
                   pl.BlockSpec((None, 3, A_CONV_DIM), lambda b: (b, 0, 0))],
        out_shape=[jax.ShapeDtypeStruct((nb, 1, A_WIDTH), F32),
                   jax.ShapeDtypeStruct((nb, A_WIDTH, A_STATE), F32),
                   jax.ShapeDtypeStruct((nb, 3, A_CONV_DIM), F32)],
        compiler_params=_cp("arbitrary"),
    )(xbc, cs, dt, z, h0, cw, cb, dtb, alog, e, dsk, gnw)


def _bias_tile(slope, dil):
    i = lax.broadcasted_iota(jnp.int32, (LANE, 2 * LANE), 0)
    c = lax.broadcasted_iota(jnp.int32, (LANE, 2 * LANE), 1)
    j = LANE + i - c
    valid = (j >= 0) & (j <= LANE)
    return jnp.where(valid, -(slope * (dil * j).astype(F32)), NEG), c


def _attn_tile(q, kcat, vcat, bias):
    s = lax.dot_general(q, kcat, NT_DIMS, preferred_element_type=F32) * ATT_SCALE + bias
    m = jnp.max(s, axis=1, keepdims=True)
    p = jnp.exp(s - m)
    l = jnp.sum(p, axis=1, keepdims=True)
    acc = jnp.dot(p.astype(BF16), vcat, preferred_element_type=F32)
    return m, l, acc


def _rows(start, stride):
    return pl.ds(start, LANE, stride=stride) if stride > 1 else pl.ds(start, LANE)


def _shift_cache_block(step, steps_per_seq, x_ref, nxt_ref, kvn_ref, o_ref):
    R = x_ref.shape[0]
    o_ref[pl.ds(0, R - 1)] = x_ref[pl.ds(1, R - 1)]
    last = step % steps_per_seq == steps_per_seq - 1
    o_ref[R - 1] = jnp.where(last, kvn_ref[...], nxt_ref[0])


def _attn_kernel(steps_per_seq, slope_ref, q_ref, k_ref, v_ref, kp_ref, vp_ref, gate_ref,
                 x_ref, nxt_ref, kvn_ref, wa_ref, wb_ref, y_ref, oc_ref, wa_out, wb_out,
                 m_scr, l_scr, acc_scr):
    step = ((pl.program_id(0) * pl.num_programs(1) + pl.program_id(1)) * pl.num_programs(2)
            + pl.program_id(2))
    _shift_cache_block(step, steps_per_seq, x_ref, nxt_ref, kvn_ref, oc_ref)
    wa_out[...] = wa_ref[...].astype(BF16)
    wb_out[...] = wb_ref[...].astype(BF16)
    first_block = pl.program_id(2) == 0
    slope = slope_ref[0:1, 0:1]
    for _, dil in B_PATTERNS:
        bias_full, c = _bias_tile(slope, dil)
        bias0 = jnp.where((c >= LANE) | jnp.logical_not(first_block), bias_full, NEG)
        nsub = ATT_BLOCK // dil // LANE
        for r in range(dil):
            kc = vc = None
            for sub in range(nsub):
                rows = _rows(sub * LANE * dil + r, dil)
                if sub == 0:
                    prow = _rows((nsub - 1) * LANE * dil + r, dil)
                    kp, vp, bias = kp_ref[prow, :], vp_ref[prow, :], bias0
                else:
                    kp, vp, bias = kc, vc, bias_full
                kc, vc = k_ref[rows, :], v_ref[rows, :]
                kcat = jnp.concatenate([kp, kc], axis=0).astype(BF16)
                vcat = jnp.concatenate([vp, vc], axis=0).astype(BF16)
                m, l, acc = _attn_tile(q_ref[rows, :].astype(BF16), kcat, vcat, bias)
                if dil == 1:
                    m_scr[rows, :] = jnp.broadcast_to(m, (LANE, LANE))
                    l_scr[rows, :] = jnp.broadcast_to(l, (LANE, LANE))
                    acc_scr[rows, :] = acc
                else:
                    m_old = m_scr[rows, :]
                    m_new = jnp.maximum(m_old, m)
                    a_old = jnp.exp(m_old - m_new)
                    a_new = jnp.exp(m - m_new)
                    m_scr[rows, :] = m_new
                    l_scr[rows, :] = a_old * l_scr[rows, :] + a_new * l
                    acc_scr[rows, :] = a_old * acc_scr[rows, :] + a_new * acc
    for sub in range(ATT_BLOCK // LANE):
        rs = pl.ds(sub * LANE, LANE)
        o = acc_scr[rs, :] / l_scr[rs, :]
        y_ref[rs, :] = (o * _silu(gate_ref[rs, :])).astype(y_ref.dtype)


def _attn(qkv, slopes, a_arr, cache6, kvn, wa, wb):
    B, H3, L, dh = qkv.shape
    H = H3 // 3
    R = ATT_BLOCK
    nblk = L // R
    cur = lambda part: pl.BlockSpec((None, None, R, dh), lambda b, h, i: (b, part * H + h, i, 0))
    prev = lambda part: pl.BlockSpec((None, None, R, dh),
                                     lambda b, h, i: (b, part * H + h, jnp.maximum(i - 1, 0), 0))
    slope = pl.BlockSpec((None, 8, LANE), lambda b, h, i: (h, 0, 0))
    gate = pl.BlockSpec((R, LANE), lambda b, h, i: (b * nblk + i, 10 * TN // LANE + h))

    nb, W = cache6.shape[1], cache6.shape[2]
    steps = B * H * nblk
    per = steps // nb
    rows = W // per
    assert per * nb == steps and rows * per == W, (steps, nb, W)
    tail = (2, B_HEADS, B_HEAD_DIM)
    step = lambda b, h, i: (b * H + h) * nblk + i
    cblk = pl.BlockSpec((None, None, rows) + tail,
                        lambda b, h, i: (0, step(b, h, i) // per, step(b, h, i) % per, 0, 0, 0))
    cnext = pl.BlockSpec((None, None, 1) + tail,
                         lambda b, h, i: (0, step(b, h, i) // per,
                                          jnp.minimum((step(b, h, i) % per + 1) * rows, W - 1), 0, 0, 0))
    cnew = pl.BlockSpec((None,) + tail, lambda b, h, i: (step(b, h, i) // per, 0, 0, 0))
    wa_spec, wa_shape = _cast_specs(wa, steps, step)
    wb_spec, wb_shape = _cast_specs(wb, steps, step)
    return pl.pallas_call(
        functools.partial(_attn_kernel, per),
        grid=(B, H, nblk),
        in_specs=[slope, cur(0), cur(1), cur(2), prev(1), prev(2), gate, cblk, cnext, cnew,
                  wa_spec, wb_spec],
        out_specs=[pl.BlockSpec((R, LANE), lambda b, h, i: (b * nblk + i, h)), cblk,
                   wa_spec, wb_spec],
        out_shape=[jax.ShapeDtypeStruct((B * L, H * dh), BF16),
                   jax.ShapeDtypeStruct(cache6.shape, F32), wa_shape, wb_shape],
        scratch_shapes=[pltpu.VMEM((R, LANE), F32)] * 3,
        compiler_params=_cp("arbitrary", "arbitrary", "arbitrary"),
    )(slopes, qkv, qkv, qkv, qkv, qkv, a_arr, cache6, cache6, kvn, wa, wb)


def _attn_step_kernel(bias_ref, q_ref, kn_ref, vn_ref, gate_ref, c1_ref, c4_ref, c16_ref, y_ref):
    q, kn, vn = q_ref[...], kn_ref[...], vn_ref[...]
    s_self = jnp.sum(q * kn, axis=-1, keepdims=True) * ATT_SCALE
    blocks = (c1_ref, c4_ref, c16_ref)
    scores = []
    for p, c_ref in enumerate(blocks):
        kblk = c_ref[:, 0]
        scores.append(jnp.sum(kblk * q[None], axis=-1, keepdims=True) * ATT_SCALE - bias_ref[p])
    m = s_self
    for s in scores:
        m = jnp.maximum(m, jnp.max(s, axis=0))
    p_self = float(len(B_PATTERNS)) * jnp.exp(s_self - m)
    l = p_self
    o = p_self * vn
    for s, c_ref in zip(scores, blocks):
        p = jnp.exp(s - m[None])
        l = l + jnp.sum(p, axis=0)
        o = o + jnp.sum(p * c_ref[:, 1], axis=0)
    y_ref[...] = (o / l) * _silu(gate_ref[...])


def _attn_step(q, kn, vn, gate, cache6, bias):
    nb = q.shape[0]
    W = B_MAX_WINDOW
    hspec = pl.BlockSpec((None, B_HEADS, B_HEAD_DIM), lambda b: (b, 0, 0))
    tail = (2, B_HEADS, B_HEAD_DIM)
    views, specs = [], []
    for _, dil in B_PATTERNS:
        if dil == 1:
            views.append(cache6)
            specs.append(pl.BlockSpec((None, None, LANE) + tail,
                                      lambda b: (0, b, W // LANE - 1, 0, 0, 0)))
        else:
            views.append(cache6.reshape((1, nb, W // dil, dil) + tail))
            specs.append(pl.BlockSpec((None, None, LANE, None) + tail,
                                      lambda b, dil=dil: (0, b, W // dil // LANE - 1, 0, 0, 0, 0)))
    return pl.pallas_call(
        _attn_step_kernel,
        grid=(nb,),
        in_specs=[pl.BlockSpec(bias.shape, lambda b: (0, 0, 0, 0)), hspec, hspec, hspec, hspec]
                 + specs,
        out_specs=hspec,
        out_shape=jax.ShapeDtypeStruct((nb, B_HEADS, B_HEAD_DIM), F32),
        compiler_params=_cp("arbitrary"),
    )(bias, q, kn, vn, gate, *views)


def _cmul(ar, ai, br, bi):
    return ar * br - ai * bi, ar * bi + ai * br


def _blockdiag_rows(x):
    hi = pltpu.roll(x, C_STATE, 1)
    g = lax.broadcasted_iota(jnp.int32, (LANE, LANE), 0) // C_GROUP
    return jnp.concatenate([jnp.where(g == 2 * b, x, jnp.where(g == 2 * b + 1, hi, 0.0))
                            for b in range(S5_NS // LANE)], axis=1)


def _blockdiag_cols(x):
    piece = lambda g: x[g * C_STATE:(g + 1) * C_STATE, :]
    return jnp.concatenate([piece(0)] + [pltpu.roll(piece(g), C_GROUP * g, 1)
                                         for g in range(1, S5_BLK)], axis=0)


def _lanes_from(ref, tail_ref, col0, nblk):
    lane = lax.broadcasted_iota(jnp.int32, (ref.shape[0], LANE), 1)
    blk = lambda c: tail_ref[...] if c + LANE > ref.shape[1] else ref[:, c:c + LANE]
    out = []
    cur = pltpu.roll(blk(col0), LANE - 32, 1)
    for m in range(nblk):
        nxt = pltpu.roll(blk(col0 + (m + 1) * LANE), LANE - 32, 1)
        out.append(jnp.where(lane < LANE - 32, cur, nxt))
        cur = nxt
    return jnp.concatenate(out, axis=1)


def _layer0_weights(w_ref, tail_ref, wa_ref, wqkv_ref):
    s1 = A_WIDTH + A_CONV_DIM
    lane = lax.broadcasted_iota(jnp.int32, (w_ref.shape[0], A_TILE), 1)
    dt = jnp.where(lane < A_HEADS, w_ref[:, s1:s1 + A_TILE], 0.0)
    gate = _lanes_from(w_ref, tail_ref, s1 + 3 * 2048, 2048 // LANE)
    wa_ref[...] = jnp.concatenate([w_ref[:, A_WIDTH:s1], w_ref[:, :A_WIDTH], gate, dt],
                                  axis=1).astype(BF16)
    wqkv_ref[...] = _lanes_from(w_ref, tail_ref, s1, 3 * 2048 // LANE).astype(BF16)


def _s5_prep_kernel(are_ref, aim_ref, ldt_ref, bre_ref, bim_ref, ctre_ref, ctim_ref,
                    w_ref, tail_ref,
                    t_ref, wb_ref, cb_ref, lam_ref, lamq_ref, cpl_ref, wa_ref, wqkv_ref):
    _layer0_weights(w_ref, tail_ref, wa_ref, wqkv_ref)
    Q = S5_Q
    are, aim = are_ref[...], aim_ref[...]
    dt = jnp.exp(ldt_ref[...])
    mag = jnp.exp(dt * are)
    l_re = mag * jnp.cos(dt * aim)
    l_im = mag * jnp.sin(dt * aim)
    den = are * are + aim * aim
    f_re = ((l_re - 1.0) * are + l_im * aim) / den
    f_im = (l_im * are - (l_re - 1.0) * aim) / den
    lr, li = l_re[0:1, :], l_im[0:1, :]
    bb_re, bb_im = _cmul(f_re[0:1, :], f_im[0:1, :], _blockdiag_rows(bre_ref[...]),
                         _blockdiag_rows(bim_ref[...]))
    ctre, ctim = _blockdiag_cols(ctre_ref[...]), _blockdiag_cols(ctim_ref[...])
    cplain = jnp.concatenate([ctre, -ctim], axis=0).astype(BF16)
    cpl_ref[...] = cplain
    stk = jnp.concatenate([l_re, l_im, jnp.zeros((LANE - 16, S5_NS), F32)], axis=0).T
    lc_re, lc_im = stk[:, 0:1], stk[:, 8:9]
    p_re, p_im = jnp.ones((1, S5_NS), F32), jnp.zeros((1, S5_NS), F32)
    pc_re, pc_im = lc_re, lc_im
    dts = []
    for tau in range(Q):
        e_re, e_im = _cmul(bb_re, bb_im, p_re, p_im)
        wp = jnp.concatenate([e_re, e_im], axis=1).astype(BF16)
        s = Q - 1 - tau
        wb_ref[s * LANE:(s + 1) * LANE, :] = wp
        dts.append(jnp.dot(wp, cplain, preferred_element_type=F32).astype(BF16))
        c_re, c_im = _cmul(ctre, ctim, pc_re, pc_im)
        cb_ref[:, tau * LANE:(tau + 1) * LANE] = jnp.concatenate([c_re, -c_im], axis=0).astype(BF16)
        p_re, p_im = _cmul(p_re, p_im, lr, li)
        pc_re, pc_im = _cmul(pc_re, pc_im, lc_re, lc_im)
    zero = jnp.zeros((LANE, LANE), BF16)
    for s in range(Q):
        for q in range(Q):
            t_ref[s * LANE:(s + 1) * LANE, q * LANE:(q + 1) * LANE] = dts[q - s] if q >= s else zero
    lam_ref[...] = jnp.concatenate([l_re, l_im], axis=1)
    sub = lax.broadcasted_iota(jnp.int32, (8, 1), 0)
    q_re, q_im = p_re, p_im
    tab_re, tab_im = jnp.zeros((8, S5_NS), F32), jnp.zeros((8, S5_NS), F32)
    for r in range(8):
        tab_re = jnp.where(sub == r, q_re, tab_re)
        tab_im = jnp.where(sub == r, q_im, tab_im)
        q_re, q_im = _cmul(q_re, q_im, p_re, p_im)
    lamq_ref[...] = jnp.concatenate([tab_re, tab_im], axis=1)


def _s5_prep(are, aim, ldt, bre, bim, ctre, ctim, w0):
    Q = S5_Q
    rows, cols = w0.shape
    wr = rows // S5_NBLK
    assert wr * S5_NBLK == rows and wr % 16 == 0, w0.shape
    row = pl.BlockSpec((None, 8, S5_NS), lambda i: (i, 0, 0))
    blk = lambda a, b: pl.BlockSpec((None, a, b), lambda i: (i, 0, 0))
    wrow = lambda n: pl.BlockSpec((wr, n), lambda i: (i, 0))
    return pl.pallas_call(
        _s5_prep_kernel,
        grid=(S5_NBLK,),
        in_specs=[row, row, row, blk(LANE, LANE), blk(LANE, LANE), blk(S5_NS, LANE), blk(S5_NS, LANE),
                  wrow(cols), pl.BlockSpec((wr, LANE), lambda i: (i, cols // LANE))],
        out_specs=[blk(Q * LANE, Q * LANE), blk(Q * LANE, 2 * S5_NS), blk(2 * S5_NS, Q * LANE),
                   blk(8, 2 * S5_NS), blk(8, 2 * S5_NS), blk(2 * S5_NS, LANE),
                   wrow(A_WIDTH + A_CONV_DIM + 2048 + A_TILE), wrow(3 * 2048)],
        out_shape=[jax.ShapeDtypeStruct((S5_NBLK, Q * LANE, Q * LANE), BF16),
                   jax.ShapeDtypeStruct((S5_NBLK, Q * LANE, 2 * S5_NS), BF16),
                   jax.ShapeDtypeStruct((S5_NBLK, 2 * S5_NS, Q * LANE), BF16),
                   jax.ShapeDtypeStruct((S5_NBLK, 8, 2 * S5_NS), F32),
                   jax.ShapeDtypeStruct((S5_NBLK, 8, 2 * S5_NS), F32),
                   jax.ShapeDtypeStruct((S5_NBLK, 2 * S5_NS, LANE), BF16),
                   jax.ShapeDtypeStruct((rows, A_WIDTH + A_CONV_DIM + 2048 + A_TILE), BF16),
                   jax.ShapeDtypeStruct((rows, 3 * 2048), BF16)],
        compiler_params=_cp("arbitrary"),
    )(are, aim, ldt, bre, bim, ctre, ctim, w0, w0)


def _s5_main_kernel(nb, u_ref, t_ref, wb_ref, cb_ref, lamq_ref, dsk_ref, g_ref, hfin_ref,
                    lhs_scr, s_scr, hp_scr, gs_scr):
    Q = S5_Q
    NS = S5_NS
    nrow = u_ref.shape[0] // Q
    nch = nrow // nb
    for s in range(Q):
        lhs_scr[:, s * LANE:(s + 1) * LANE] = u_ref[pl.ds(s, nrow, stride=Q), :].astype(BF16)
    lhs = lhs_scr[...]
    s_scr[...] = jnp.dot(lhs, wb_ref[...], preferred_element_type=F32)
    tab = lamq_ref[...]
    tab_re, tab_im = tab[:, :NS], tab[:, NS:]
    sub = lax.broadcasted_iota(jnp.int32, (8, 1), 0)

    def tile_body(t, carry):
        base = t * 8
        new = []
        for b in range(nb):
            h_re, h_im = carry[2 * b], carry[2 * b + 1]
            st = s_scr[pl.ds(b * nch + base, 8), :]
            x_re, x_im = st[:, :NS], st[:, NS:]
            for sh in (1, 2, 4):
                y_re = jnp.where(sub >= sh, pltpu.roll(x_re, sh, 0), 0.0)
                y_im = jnp.where(sub >= sh, pltpu.roll(x_im, sh, 0), 0.0)
                d_re, d_im = _cmul(tab_re[sh - 1:sh, :], tab_im[sh - 1:sh, :], y_re, y_im)
                x_re, x_im = x_re + d_re, x_im + d_im
            c_re, c_im = _cmul(tab_re, tab_im, h_re, h_im)
            x_re, x_im = x_re + c_re, x_im + c_im
            p_re = jnp.where(sub == 0, h_re, pltpu.roll(x_re, 1, 0))
            p_im = jnp.where(sub == 0, h_im, pltpu.roll(x_im, 1, 0))
            hp_scr[pl.ds(b * nch + base, 8), :] = jnp.concatenate([p_re, p_im], axis=1)
            new += [x_re[7:8, :], x_im[7:8, :]]
        return tuple(new)

    fin = (jnp.zeros((1, NS), F32),) * (2 * nb)
    for t in range(nch // 8):
        fin = tile_body(t, fin)
    for b in range(nb):
        hfin_ref[b:b + 1, :] = jnp.concatenate([fin[2 * b], fin[2 * b + 1]], axis=1)

    y = (jnp.dot(lhs, t_ref[...], preferred_element_type=F32)
         + jnp.dot(hp_scr[...].astype(BF16), cb_ref[...], preferred_element_type=F32))
    for q in range(Q):
        rows = pl.ds(q, nrow, stride=Q)
        yq = y[:, q * LANE:(q + 1) * LANE] + dsk_ref[...] * u_ref[rows, :]
        gs_scr[rows, :] = jax.nn.gelu(yq)
    g_ref[...] = gs_scr[...].astype(g_ref.dtype)


def _s5_main(u_gate, nb, t, wb, cb, lamq, dsk):
    Q = S5_Q
    M = u_gate.shape[0]
    nrow = M // Q
    blk = lambda a, b: pl.BlockSpec((None, a, b), lambda i: (i, 0, 0))
    return pl.pallas_call(
        functools.partial(_s5_main_kernel, nb),
        grid=(S5_NBLK,),
        in_specs=[pl.BlockSpec((M, LANE), lambda i: (0, i)),
                  blk(Q * LANE, Q * LANE), blk(Q * LANE, 2 * S5_NS), blk(2 * S5_NS, Q * LANE),
                  blk(8, 2 * S5_NS), pl.BlockSpec((1, LANE), lambda i: (0, i))],
        out_specs=[pl.BlockSpec((M, LANE), lambda i: (0, i)), blk(nb, 2 * S5_NS)],
        out_shape=[jax.ShapeDtypeStruct((M, E1), BF16),
                   jax.ShapeDtypeStruct((S5_NBLK, nb, 2 * S5_NS), F32)],
        scratch_shapes=[pltpu.VMEM((nrow, Q * LANE), BF16), pltpu.VMEM((nrow, 2 * S5_NS), F32),
                        pltpu.VMEM((nrow, 2 * S5_NS), F32), pltpu.VMEM((M, LANE), F32)],
        compiler_params=_cp("arbitrary"),
    )(u_gate, t, wb, cb, lamq, dsk)


def _s5_step_kernel(u_ref, h0_ref, bb_ref, cpl_ref, lam_ref, dsk_ref, g_ref, hn_ref):
    NS = S5_NS
    u = u_ref[...]
    s = jnp.dot(u.astype(BF16), bb_ref[...], preferred_element_type=F32)
    lam = lam_ref[0:1, :]
    h0 = h0_ref[...]
    n_re, n_im = _cmul(lam[:, :NS], lam[:, NS:], h0[:, :NS], h0[:, NS:])
    hn = jnp.concatenate([n_re + s[:, :NS], n_im + s[:, NS:]], axis=1)
    hn_ref[...] = hn
    y = jnp.dot(hn.astype(BF16), cpl_ref[...], preferred_element_type=F32) + dsk_ref[...] * u
    g_ref[...] = jax.nn.gelu(y)


def _s5_step(u_gate, h0, wb, cpl, lam, dsk):
    nb = u_gate.shape[0]
    blk = lambda a, b: pl.BlockSpec((None, a, b), lambda i: (i, 0, 0))
    return pl.pallas_call(
        _s5_step_kernel,
        grid=(S5_NBLK,),
        in_specs=[pl.BlockSpec((nb, LANE), lambda i: (0, i)), blk(nb, 2 * S5_NS),
                  pl.BlockSpec((None, LANE, 2 * S5_NS), lambda i: (i, S5_Q - 1, 0)),
                  blk(2 * S5_NS, LANE), blk(8, 2 * S5_NS), pl.BlockSpec((1, LANE), lambda i: (0, i))],
        out_specs=[pl.BlockSpec((nb, LANE), lambda i: (0, i)), blk(nb, 2 * S5_NS)],
        out_shape=[jax.ShapeDtypeStruct((nb, E1), F32),
                   jax.ShapeDtypeStruct((S5_NBLK, nb, 2 * S5_NS), F32)],
        compiler_params=_cp("arbitrary"),
    )(u_gate, h0, wb, cpl, lam, dsk)


def _state_to_blocks(st):
    nb = st.shape[0]
    x = st.reshape(nb, S5_NBLK, S5_NS, 2)
    return jnp.transpose(x, (1, 0, 3, 2)).reshape(S5_NBLK, nb, 2 * S5_NS)


def _blocks_to_state(h):
    nb = h.shape[1]
    x = h.reshape(S5_NBLK, nb, 2, S5_NS)
    return jnp.transpose(x, (1, 0, 3, 2)).reshape(nb, C_GROUPS, C_STATE, 2)


def kernel(x_prompt, x_sample, state_a_ssm, state_a_conv, cache_b_kv, state_c, norm_w, final_norm_w, even_w_in, a_conv_w, a_conv_b, a_dt_bias, a_A_log, a_D, a_norm_w, even_w_out, odd_w_in, c_A_re, c_A_im, c_B_re, c_B_im, c_C_re, c_C_im, c_D, c_log_dt, c_glu_w, c_glu_b, odd_w_out):
    B, L, D = x_prompt.shape
    nS = x_sample.shape[0]
    M = B * L
    Q = S5_Q

    nw0 = norm_w[0][None, :]
    nw1 = norm_w[1][None, :]
    fnw = final_norm_w[None, :]
    cw, cb = a_conv_w[0], a_conv_b[0][None, :]
    pad_h = lambda v: jnp.pad(v, (0, LANE - A_HEADS))[None, :]
    dtb, alog = pad_h(a_dt_bias[0]), pad_h(a_A_log[0])
    dsk_a = jnp.repeat(a_D[0], A_HEAD_DIM)[None, :]
    gnw = a_norm_w[0][None, :]
    head_expand = np.kron(np.eye(A_HEADS, dtype=np.float32), np.ones((1, A_HEAD_DIM), np.float32))
    head_expand = jnp.asarray(np.pad(head_expand, ((0, LANE - A_HEADS), (0, 0)))).astype(BF16)
    slopes_np = np.asarray(2.0 ** (-8.0 * np.arange(1, B_HEADS + 1) / B_HEADS), np.float32)
    slopes = jnp.asarray(np.broadcast_to(slopes_np[:, None, None], (B_HEADS, 8, LANE)))
    dist = np.stack([dil * (LANE - np.arange(LANE)) for _, dil in B_PATTERNS]).astype(np.float32)
    step_bias = jnp.asarray((dist[:, :, None] * slopes_np[None, None, :])[..., None])

    rows8 = lambda p: jnp.broadcast_to(p.reshape(S5_NBLK, 1, S5_NS), (S5_NBLK, 8, S5_NS))
    lane_pad = lambda p: jnp.pad(p, ((0, 0), (0, 0), (0, LANE - p.shape[-1])))
    b_rows = lambda p: lane_pad(jnp.transpose(p, (0, 2, 1)).reshape(S5_NBLK, LANE, C_STATE))
    c_rows = lambda p: lane_pad(jnp.transpose(p, (0, 2, 1)).reshape(S5_NBLK, S5_NS, C_GROUP))
    ldt = jnp.repeat(c_log_dt[0], C_STATE).reshape(S5_NBLK, S5_NS)
    t_m, wb_m, cb_m, lam, lamq, cpl, w_a, w_qkv = _s5_prep(
        rows8(c_A_re[0]), rows8(c_A_im[0]), rows8(ldt), b_rows(c_B_re[0]), b_rows(c_B_im[0]),
        c_rows(c_C_re[0]), c_rows(c_C_im[0]), even_w_in[0])
    dsk_c = c_D[0][None, :]
    glu_b = c_glu_b[0][None, :]

    xs = x_sample.reshape(nS, D)
    a_s, xs_n = _mm_norm(xs, nw0, w_a, nS, 2048, keep_xn=True)
    (qkv_s,) = _proj_qkv(xs_n, w_qkv, 1, nS, nS, window=False)
    col = lambda a, b: a_s[:, a:b]
    hv = lambda part: jnp.transpose(qkv_s[0, part * B_HEADS:(part + 1) * B_HEADS], (1, 0, 2))
    gate_s = col(5120, 7168).reshape(nS, B_HEADS, B_HEAD_DIM)
    kvn = jnp.stack([hv(1), hv(2)], axis=1)

    xp = x_prompt.reshape(M, D)
    a_arr, xp_n, w_glu = _mm_norm(xp, nw0, w_a, 1024, A_TILE, keep_xn=True, cast=c_glu_w[0])
    qkv, kv_keep, w_in1 = _proj_qkv(xp_n, w_qkv, B, L, 1024, window=True, cast=odd_w_in[0])
    y_a, ssm_p, tail_p = _ssd(a_arr, B, L, cw, cb, dtb, alog, dsk_a, gnw)
    y_b, new_cache, w_out0, w_out1 = _attn(qkv, slopes, a_arr, cache_b_kv, kvn,
                                           even_w_out[0], odd_w_out[0])
    hp1 = _outproj0(y_a, y_b, w_out0, xp, 1024)
    u_gate = _mm_norm(hp1, nw1, w_in1, 1024, 1024)
    g, hfin_p = _s5_main(u_gate, B, t_m, wb_m, cb_m, lamq, dsk_c)
    y1 = _glu(g, w_glu, glu_b, u_gate, 1024, BF16)
    y_prompt = _outproj1(y1, w_out1, hp1, fnw, 1024).reshape(B, L, D)

    y_as, ssm_s, conv_s = _ssd_step(col(0, 3072)[:, None, :], state_a_conv[0],
                                    col(7168, 7296)[:, None, :], col(3072, 5120)[:, None, :],
                                    state_a_ssm[0].reshape(nS, A_WIDTH, A_STATE),
                                    cw, cb, dtb, alog, head_expand, dsk_a, gnw)
    y_bs = _attn_step(hv(0), hv(1), hv(2), gate_s, cache_b_kv, step_bias)
    hs1 = _outproj0(y_as.reshape(nS, A_WIDTH), y_bs.reshape(nS, A_WIDTH), w_out0, xs, nS, 1024)
    u_gate_s = _mm_norm(hs1, nw1, w_in1, nS, 2048)
    g_s, hnew_s = _s5_step(u_gate_s, _state_to_blocks(state_c[0]), wb_m, cpl, lam, dsk_c)
    y1_s = _glu(g_s, w_glu, glu_b, u_gate_s, nS, F32, 1024)
    y_sample = _outproj1(y1_s, w_out1, hs1, fnw, nS).reshape(nS, 1, D)

    return (y_prompt, y_sample,
            ssm_p.reshape(1, B, A_HEADS, A_HEAD_DIM, A_STATE),
            tail_p[:, 5:8, :][None],
            kv_keep[None],
            _blocks_to_state(hfin_p)[None],
            ssm_s.reshape(1, nS, A_HEADS, A_HEAD_DIM, A_STATE),
            conv_s[None],
            new_cache,
            _blocks_to_state(hnew_s)[None])
```

```python
import functools

import numpy as np
import jax
import jax.numpy as jnp
from jax import lax
from jax.experimental import pallas as pl
from jax.experimental.pallas import tpu as pltpu

F32 = jnp.float32
BF16 = jnp.bfloat16

D_MODEL = 2048
EPS = 1e-5
A_WIDTH = 2048
A_HEAD_DIM = 64
A_HEADS = 32
A_GROUPS = 4
A_STATE = 128
A_CONV_W = 4
A_CONV_DIM = 3072
SSD_CHUNK = 128
B_HEADS = 16
B_HEAD_DIM = 128
B_PATTERNS = ((128, 1), (512, 4), (2048, 16))
B_MAX_WINDOW = 2048
C_GROUP = 16
C_GROUPS = 256
C_STATE = 64
E1 = 4096

LANE = 128
S5_Q = 8
S5_BLK = 8
S5_NBLK = C_GROUPS // S5_BLK
S5_NS = S5_BLK * C_STATE
TN = 512
A_TILE = 1024
ATT_BLOCK = 2048
NEG = -1e30
ATT_SCALE = B_HEAD_DIM ** -0.5
VMEM_LIMIT = 56 * 1024 * 1024

NT_DIMS = (((1,), (1,)), ((), ()))


def _cp(*sem):
    return pltpu.CompilerParams(dimension_semantics=sem, vmem_limit_bytes=VMEM_LIMIT)


def _silu(x):
    return x * jax.nn.sigmoid(x)


def _softplus(x):
    return jnp.maximum(x, 0.0) + jnp.log1p(jnp.exp(-jnp.abs(x)))


def _split3(a):
    hi = a.astype(BF16)
    r = a - hi.astype(F32)
    mid = r.astype(BF16)
    lo = (r - mid.astype(F32)).astype(BF16)
    return hi, mid, lo


def _dot3_left(m, a):
    hi, mid, lo = _split3(a)
    d = lambda p: jnp.dot(m, p, preferred_element_type=F32)
    return (d(lo) + d(mid)) + d(hi)


def _dot3_right(a, m):
    hi, mid, lo = _split3(a)
    d = lambda p: jnp.dot(p, m, preferred_element_type=F32)
    return (d(lo) + d(mid)) + d(hi)


def _rms(x, w):
    ms = jnp.mean(x * x, axis=-1, keepdims=True)
    return (x * lax.rsqrt(ms + EPS)) * w


QKV_HEADS = 8
CAST_ROWS = 64


def _cast_specs(w, nsteps, step_of):
    rows, cols = w.shape
    per = CAST_ROWS
    while rows // per > nsteps:
        per *= 2
    nblk = rows // per
    assert nblk * per == rows, (w.shape, nsteps)
    spec = pl.BlockSpec((per, cols), lambda *ids: (jnp.minimum(step_of(*ids), nblk - 1), 0))
    return spec, jax.ShapeDtypeStruct(w.shape, BF16)


def _proj_qkv_kernel(rpb, nskip, window, cast, *refs):
    refs = list(refs)
    x_ref, w_ref = refs[:2]
    if cast:
        refs[-1][...] = refs[2][...].astype(BF16)
    qkv_ref = refs[3 if cast else 2]
    j = pl.program_id(1)
    acc = _dot_w(x_ref[...].astype(BF16), w_ref[...], True)
    for hh in range(QKV_HEADS):
        qkv_ref[hh] = acc[:, hh * LANE:(hh + 1) * LANE]
    if window:
        kv_ref = refs[4 if cast else 3]
        tiles_per = B_HEADS // QKV_HEADS
        kept = pl.program_id(0) % rpb >= nskip
        for part in range(tiles_per):
            @pl.when(kept & (j >= tiles_per) & (j % tiles_per == part))
            def _(part=part):
                for hh in range(QKV_HEADS):
                    kv_ref[:, part * QKV_HEADS + hh, :] = acc[:, hh * LANE:(hh + 1) * LANE]


def _proj_qkv(xn, w, B, L, tm, window, cast=None):
    M = B * L
    tn = QKV_HEADS * LANE
    rpb = L // tm
    keep = min(B_MAX_WINDOW, L)
    nskip = (L - keep) // tm
    tiles_per = B_HEADS // QKV_HEADS

    def kv_map(i, j):
        ri = i % rpb
        return (i // rpb, jnp.maximum(ri - nskip, 0),
                jnp.where((ri < nskip) | (j < 2 * tiles_per), 0, 1), 0, 0)

    out_specs = [pl.BlockSpec((None, QKV_HEADS, tm, LANE), lambda i, j: (i // rpb, j, i % rpb, 0))]
    out_shape = [jax.ShapeDtypeStruct((B, 3 * B_HEADS, L, B_HEAD_DIM), F32)]
    if window:
        out_specs.append(pl.BlockSpec((None, tm, None, B_HEADS, B_HEAD_DIM), kv_map))
        out_shape.append(jax.ShapeDtypeStruct((B, keep, 2, B_HEADS, B_HEAD_DIM), F32))
    nj = w.shape[0] // tn
    in_specs = [pl.BlockSpec((tm, D_MODEL), lambda i, j: (i, 0)),
                pl.BlockSpec((tn, D_MODEL), lambda i, j: (j, 0))]
    operands = [xn, w]
    if cast is not None:
        spec, shape = _cast_specs(cast, (M // tm) * nj, lambda i, j: i * nj + j)
        in_specs.append(spec)
        operands.append(cast)
        out_specs.append(spec)
        out_shape.append(shape)
    return pl.pallas_call(
        functools.partial(_proj_qkv_kernel, rpb, nskip, window, cast is not None),
        grid=(M // tm, nj),
        in_specs=in_specs,
        out_specs=out_specs,
        out_shape=out_shape,
        compiler_params=_cp("arbitrary", "arbitrary"),
    )(*operands)


def _dot_w(x, w, wt):
    if wt:
        return lax.dot_general(x, w, NT_DIMS, preferred_element_type=F32)
    return jnp.dot(x, w, preferred_element_type=F32)


def _mm_norm_kernel(cast, wt, *refs):
    refs = list(refs)
    x_ref, nw_ref, w_ref = refs[:3]
    if cast:
        refs[-1][...] = refs[3][...].astype(BF16)
    o_ref, xn_ref = refs[4:6] if cast else refs[3:5]

    @pl.when(pl.program_id(1) == 0)
    def _():
        xn_ref[...] = _rms(x_ref[...], nw_ref[...]).astype(xn_ref.dtype)

    o_ref[...] = _dot_w(xn_ref[...].astype(BF16), w_ref[...], wt)


def _mm_norm(x, nw, w, tm, tn=TN, keep_xn=False, cast=None, wt=False):
    M = x.shape[0]
    N, K = w.shape if wt else w.shape[::-1]
    nj = N // tn
    in_specs = [pl.BlockSpec((tm, K), lambda i, j: (i, 0)),
                pl.BlockSpec((1, K), lambda i, j: (0, 0)),
                pl.BlockSpec((tn, K), lambda i, j: (j, 0)) if wt else
                pl.BlockSpec((K, tn), lambda i, j: (0, j))]
    operands = [x, nw, w]
    out_specs = [pl.BlockSpec((tm, tn), lambda i, j: (i, j))]
    out_shape = [jax.ShapeDtypeStruct((M, N), F32)]
    if keep_xn:
        out_specs.append(pl.BlockSpec((tm, K), lambda i, j: (i, 0)))
        out_shape.append(jax.ShapeDtypeStruct((M, K), BF16 if tm % 16 == 0 else F32))
    if cast is not None:
        assert keep_xn
        spec, shape = _cast_specs(cast, (M // tm) * nj, lambda i, j: i * nj + j)
        in_specs.append(spec)
        operands.append(cast)
        out_specs.append(spec)
        out_shape.append(shape)
    out = pl.pallas_call(
        functools.partial(_mm_norm_kernel, cast is not None, wt),
        grid=(M // tm, nj),
        in_specs=in_specs,
        out_specs=out_specs,
        out_shape=out_shape,
        scratch_shapes=[] if keep_xn else [pltpu.VMEM((tm, K), BF16)],
        compiler_params=_cp("arbitrary", "arbitrary"),
    )(*operands)
    return out if len(out) > 1 else out[0]


def _outproj0_kernel(ya_ref, yb_ref, wa_ref, wb_ref, res_ref, o_ref):
    o_ref[...] = (res_ref[...]
                  + jnp.dot(ya_ref[...].astype(BF16), wa_ref[...], preferred_element_type=F32)
                  + jnp.dot(yb_ref[...].astype(BF16), wb_ref[...], preferred_element_type=F32))


def _outproj0(ya, yb, w, res, tm, tn=TN):
    M = ya.shape[0]
    K = ya.shape[1]
    return pl.pallas_call(
        _outproj0_kernel,
        grid=(M // tm, D_MODEL // tn),
        in_specs=[pl.BlockSpec((tm, K), lambda i, j: (i, 0)),
                  pl.BlockSpec((tm, K), lambda i, j: (i, 0)),
                  pl.BlockSpec((K, tn), lambda i, j: (0, j)),
                  pl.BlockSpec((K, tn), lambda i, j: (1, j)),
                  pl.BlockSpec((tm, tn), lambda i, j: (i, j))],
        out_specs=pl.BlockSpec((tm, tn), lambda i, j: (i, j)),
        out_shape=jax.ShapeDtypeStruct((M, D_MODEL), F32),
        compiler_params=_cp("arbitrary", "arbitrary"),
    )(ya, yb, w, w, res)


def _glu_kernel(g_ref, wa_ref, wb_ref, ba_ref, bb_ref, gate_ref, o_ref):
    tm = g_ref.shape[0]
    hr = min(tm, 512)
    for r in range(tm // hr):
        rs = pl.ds(r * hr, hr)
        g = g_ref[rs, :].astype(BF16)
        ga = jnp.dot(g, wa_ref[...], preferred_element_type=F32) + ba_ref[...]
        gb = jnp.dot(g, wb_ref[...], preferred_element_type=F32) + bb_ref[...]
        o_ref[rs, :] = (ga * jax.nn.sigmoid(gb) * _silu(gate_ref[rs, :])).astype(o_ref.dtype)


def _glu(g, w, bias, u_gate, tm, out_dtype, tn=TN):
    M = g.shape[0]
    nb = E1 // tn
    return pl.pallas_call(
        _glu_kernel,
        grid=(M // tm, nb),
        in_specs=[pl.BlockSpec((tm, E1), lambda i, j: (i, 0)),
                  pl.BlockSpec((E1, tn), lambda i, j: (0, j)),
                  pl.BlockSpec((E1, tn), lambda i, j: (0, j + nb)),
                  pl.BlockSpec((1, tn), lambda i, j: (0, j)),
                  pl.BlockSpec((1, tn), lambda i, j: (0, j + nb)),
                  pl.BlockSpec((tm, tn), lambda i, j: (i, j + nb))],
        out_specs=pl.BlockSpec((tm, tn), lambda i, j: (i, j)),
        out_shape=jax.ShapeDtypeStruct((M, E1), out_dtype),
        compiler_params=_cp("arbitrary", "arbitrary"),
    )(g, w, w, bias, bias, u_gate)


def _outproj1_kernel(y_ref, w_ref, res_ref, fnw_ref, o_ref):
    j = pl.program_id(1)
    nj = D_MODEL // TN
    h = res_ref[...] + jnp.dot(y_ref[...].astype(BF16), w_ref[...], preferred_element_type=F32)
    for jj in range(nj):
        @pl.when(j == jj)
        def _(jj=jj):
            o_ref[:, jj * TN:(jj + 1) * TN] = h

    @pl.when(j == nj - 1)
    def _():
        o_ref[...] = _rms(o_ref[...], fnw_ref[...])


def _outproj1(y, w, res, fnw, tm):
    M, K = y.shape
    return pl.pallas_call(
        _outproj1_kernel,
        grid=(M // tm, D_MODEL // TN),
        in_specs=[pl.BlockSpec((tm, K), lambda i, j: (i, 0)),
                  pl.BlockSpec((K, TN), lambda i, j: (0, j)),
                  pl.BlockSpec((tm, TN), lambda i, j: (i, j)),
                  pl.BlockSpec((1, D_MODEL), lambda i, j: (0, 0))],
        out_specs=pl.BlockSpec((tm, D_MODEL), lambda i, j: (i, 0)),
        out_shape=jax.ShapeDtypeStruct((M, D_MODEL), F32),
        compiler_params=_cp("arbitrary", "arbitrary"),
    )(y, w, res, fnw)


def _ssd_kernel(xbc_ref, dt_ref, z0_ref, z1_ref, cw_ref, cb_ref, dtb_ref, alog_ref, dsk_ref, gnw_ref,
                y_ref, st_ref, tail_ref, h_scr, tail_scr, y_scr):
    c = pl.program_id(1)
    Q = SSD_CHUNK

    @pl.when(c == 0)
    def _():
        h_scr[...] = jnp.zeros_like(h_scr)
        tail_scr[...] = jnp.zeros_like(tail_scr)

    x_raw = xbc_ref[...]
    tail8 = tail_scr[...]
    row8 = lax.broadcasted_iota(jnp.int32, (8, 1), 0)
    conv = cb_ref[...] + cw_ref[3:4, :] * x_raw
    for k in (1, 2, 3):
        xr = pltpu.roll(x_raw, k, 0)
        first = jnp.where(row8 < k, pltpu.roll(tail8, k, 0), xr[0:8, :])
        conv = conv + cw_ref[3 - k:4 - k, :] * jnp.concatenate([first, xr[8:, :]], axis=0)
    tail_scr[...] = x_raw[Q - 8:Q, :]
    xc = _silu(conv)

    dt = _softplus(dt_ref[...] + dtb_ref[...])
    a = dt * (-jnp.exp(alog_ref[...]))
    qi = lax.broadcasted_iota(jnp.int32, (Q, Q), 0)
    si = lax.broadcasted_iota(jnp.int32, (Q, Q), 1)
    tri = qi >= si
    a_cum = _dot3_left(tri.astype(BF16), a)
    a_cum_t = a_cum.T
    lo_half = si < A_HEAD_DIM

    for g in range(A_GROUPS):
        bg = xc[:, A_WIDTH + g * A_STATE:A_WIDTH + (g + 1) * A_STATE].astype(BF16)
        cg = xc[:, A_WIDTH + (A_GROUPS + g) * A_STATE:A_WIDTH + (A_GROUPS + g + 1) * A_STATE].astype(BF16)
        cbm = lax.dot_general(cg, bg, NT_DIMS, preferred_element_type=F32)
        hprev = h_scr[g * 512:(g + 1) * 512, :]
        yoff = lax.dot_general(cg, hprev.astype(BF16), NT_DIMS, preferred_element_type=F32)
        for jj in range(4):
            p = g * 4 + jj
            h0, h1 = 2 * p, 2 * p + 1
            dtp = jnp.where(lo_half, dt[:, h0:h0 + 1], dt[:, h1:h1 + 1])
            ap = jnp.where(lo_half, a_cum[:, h0:h0 + 1], a_cum[:, h1:h1 + 1])
            x_p = xc[:, p * LANE:(p + 1) * LANE]
            xdt = x_p * dtp
            g0 = cbm * jnp.exp(jnp.where(tri, a_cum[:, h0:h0 + 1] - a_cum_t[h0:h0 + 1, :], NEG))
            g1 = cbm * jnp.exp(jnp.where(tri, a_cum[:, h1:h1 + 1] - a_cum_t[h1:h1 + 1, :], NEG))
            lhs = jnp.concatenate([g0, g1], axis=1).astype(BF16)
            rhs = jnp.concatenate([jnp.where(lo_half, xdt, 0.0), jnp.where(lo_half, 0.0, xdt)],
                                  axis=0).astype(BF16)
            ydiag = jnp.dot(lhs, rhs, preferred_element_type=F32)
            y_scr[:, p * LANE:(p + 1) * LANE] = (
                ydiag + yoff[:, jj * LANE:(jj + 1) * LANE] * jnp.exp(ap)
                + dsk_ref[:, p * LANE:(p + 1) * LANE] * x_p)
            xd = xdt * jnp.exp(ap[Q - 1:Q, :] - ap)
            st = jnp.dot(xd.T.astype(BF16), bg, preferred_element_type=F32)
            cd = jnp.exp(jnp.where(qi < A_HEAD_DIM, a_cum_t[h0:h0 + 1, Q - 1:Q],
                                   a_cum_t[h1:h1 + 1, Q - 1:Q]))
            h_scr[p * LANE:(p + 1) * LANE, :] = cd * hprev[jj * LANE:(jj + 1) * LANE, :] + st

    z = jnp.concatenate([z0_ref[...], z1_ref[...]], axis=1)
    y_ref[...] = _rms(y_scr[...] * _silu(z), gnw_ref[...]).astype(y_ref.dtype)

    @pl.when(c == pl.num_programs(1) - 1)
    def _():
        st_ref[...] = h_scr[...]
        tail_ref[...] = x_raw[Q - 8:Q, :]


def _ssd(a_arr, B, L, cw, cb, dtb, alog, dsk, gnw):
    nc = L // SSD_CHUNK
    Q = SSD_CHUNK
    row = lambda b, c: b * nc + c
    full = lambda shp: pl.BlockSpec(shp, lambda b, c: (0, 0))
    return pl.pallas_call(
        _ssd_kernel,
        grid=(B, nc),
        in_specs=[pl.BlockSpec((Q, A_CONV_DIM), lambda b, c: (row(b, c), 0)),
                  pl.BlockSpec((Q, LANE), lambda b, c: (row(b, c), 14 * TN // LANE)),
                  pl.BlockSpec((Q, A_WIDTH // 2), lambda b, c: (row(b, c), 3)),
                  pl.BlockSpec((Q, A_WIDTH // 2), lambda b, c: (row(b, c), 4)),
                  full((A_CONV_W, A_CONV_DIM)), full((1, A_CONV_DIM)), full((1, LANE)),
                  full((1, LANE)), full((1, A_WIDTH)), full((1, A_WIDTH))],
        out_specs=[pl.BlockSpec((Q, A_WIDTH), lambda b, c: (row(b, c), 0)),
                   pl.BlockSpec((None, A_WIDTH, A_STATE), lambda b, c: (b, 0, 0)),
                   pl.BlockSpec((None, 8, A_CONV_DIM), lambda b, c: (b, 0, 0))],
        out_shape=[jax.ShapeDtypeStruct((B * L, A_WIDTH), BF16),
                   jax.ShapeDtypeStruct((B, A_WIDTH, A_STATE), F32),
                   jax.ShapeDtypeStruct((B, 8, A_CONV_DIM), F32)],
        scratch_shapes=[pltpu.VMEM((A_WIDTH, A_STATE), F32), pltpu.VMEM((8, A_CONV_DIM), F32),
                        pltpu.VMEM((Q, A_WIDTH), F32)],
        compiler_params=_cp("arbitrary", "arbitrary"),
    )(a_arr, a_arr, a_arr, a_arr, cw, cb, dtb, alog, dsk, gnw)


def _ssd_step_kernel(xbc_ref, cs_ref, dt_ref, z_ref, h0_ref, cw_ref, cb_ref, dtb_ref, alog_ref,
                     e_ref, dsk_ref, gnw_ref, y_ref, hn_ref, nc_ref):
    x_raw = xbc_ref[...]
    cs = cs_ref[...]
    conv = (cb_ref[...] + cw_ref[0:1, :] * cs[0:1, :] + cw_ref[1:2, :] * cs[1:2, :]
            + cw_ref[2:3, :] * cs[2:3, :] + cw_ref[3:4, :] * x_raw)
    xc = _silu(conv)
    nc_ref[0:1, :] = cs[1:2, :]
    nc_ref[1:2, :] = cs[2:3, :]
    nc_ref[2:3, :] = x_raw

    dt = _softplus(dt_ref[...] + dtb_ref[...])
    da = dt * (-jnp.exp(alog_ref[...]))
    e = e_ref[...]
    dt_e = _dot3_right(jnp.broadcast_to(dt, (8, LANE)), e)[0:1, :]
    da_e = _dot3_right(jnp.broadcast_to(da, (8, LANE)), e)[0:1, :]
    xs = xc[:, :A_WIDTH]
    sub = lax.broadcasted_iota(jnp.int32, (8, 1), 0)
    rows = jnp.where(sub == 0, xs * dt_e, jnp.where(sub == 1, jnp.exp(da_e), 0.0))
    cols = jnp.concatenate([rows, jnp.zeros((LANE - 8, A_WIDTH), F32)], axis=0).T
    xdt_col = cols[:, 0:1]
    dec_col = cols[:, 1:2]
    ys = []
    for g in range(A_GROUPS):
        bg = xc[:, A_WIDTH + g * A_STATE:A_WIDTH + (g + 1) * A_STATE]
        cg = xc[:, A_WIDTH + (A_GROUPS + g) * A_STATE:A_WIDTH + (A_GROUPS + g + 1) * A_STATE]
        sl = slice(g * 512, (g + 1) * 512)
        hn = dec_col[sl, :] * h0_ref[sl, :] + xdt_col[sl, :] * bg
        hn_ref[sl, :] = hn
        c8 = jnp.broadcast_to(cg, (8, A_STATE)).astype(BF16)
        ys.append(lax.dot_general(c8, hn.astype(BF16), NT_DIMS, preferred_element_type=F32)[0:1, :])
    y = jnp.concatenate(ys, axis=1) + dsk_ref[...] * xs
    y_ref[...] = _rms(y * _silu(z_ref[...]), gnw_ref[...])


def _ssd_step(xbc, cs, dt, z, h0, cw, cb, dtb, alog, e, dsk, gnw):
    nb = xbc.shape[0]
    per_b = lambda n: pl.BlockSpec((None, 1, n), lambda b: (b, 0, 0))
    full = lambda shp: pl.BlockSpec(shp, lambda b: (0, 0))
    return pl.pallas_call(
        _ssd_step_kernel,
        grid=(nb,),
        in_specs=[per_b(A_CONV_DIM), pl.BlockSpec((None, 3, A_CONV_DIM), lambda b: (b, 0, 0)),
                  per_b(LANE), per_b(A_WIDTH),
                  pl.BlockSpec((None, A_WIDTH, A_STATE), lambda b: (b, 0, 0)),
                  full((A_CONV_W, A_CONV_DIM)), full((1, A_CONV_DIM)), full((1, LANE)),
                  full((1, LANE)), full((LANE, A_WIDTH)), full((1, A_WIDTH)), full((1, A_WIDTH))],
        out_specs=[per_b(A_WIDTH),
                   pl.BlockSpec((None, A_WIDTH, A_STATE), lambda b: (b, 0, 0)),
                   pl.BlockSpec((None, 3, A_CONV_DIM), lambda b: (b, 0, 0))],
        out_shape=[jax.ShapeDtypeStruct((nb, 1, A_WIDTH), F32),
                   jax.ShapeDtypeStruct((nb, A_WIDTH, A_STATE), F32),
                   jax.ShapeDtypeStruct((nb, 3, A_CONV_DIM), F32)],
        compiler_params=_cp("arbitrary"),
    )(xbc, cs, dt, z, h0, cw, cb, dtb, alog, e, dsk, gnw)


def _bias_tile(slope, dil):
    i = lax.broadcasted_iota(jnp.int32, (LANE, 2 * LANE), 0)
    c = lax.broadcasted_iota(jnp.int32, (LANE, 2 * LANE), 1)
    j = LANE + i - c
    valid = (j >= 0) & (j <= LANE)
    return jnp.where(valid, -(slope * (dil * j).astype(F32)), NEG), c


def _attn_tile(q, kcat, vcat, bias):
    s = lax.dot_general(q, kcat, NT_DIMS, preferred_element_type=F32) * ATT_SCALE + bias
    m = jnp.max(s, axis=1, keepdims=True)
    p = jnp.exp(s - m)
    l = jnp.sum(p, axis=1, keepdims=True)
    acc = jnp.dot(p.astype(BF16), vcat, preferred_element_type=F32)
    return m, l, acc


def _rows(start, stride):
    return pl.ds(start, LANE, stride=stride) if stride > 1 else pl.ds(start, LANE)


def _shift_cache_block(step, steps_per_seq, x_ref, nxt_ref, kvn_ref, o_ref):
    R = x_ref.shape[0]
    o_ref[pl.ds(0, R - 1)] = x_ref[pl.ds(1, R - 1)]
    last = step % steps_per_seq == steps_per_seq - 1
    o_ref[R - 1] = jnp.where(last, kvn_ref[...], nxt_ref[0])


def _attn_kernel(steps_per_seq, slope_ref, q_ref, k_ref, v_ref, kp_ref, vp_ref, gate_ref,
                 x_ref, nxt_ref, kvn_ref, wa_ref, wb_ref, y_ref, oc_ref, wa_out, wb_out,
                 m_scr, l_scr, acc_scr):
    step = ((pl.program_id(0) * pl.num_programs(1) + pl.program_id(1)) * pl.num_programs(2)
            + pl.program_id(2))
    _shift_cache_block(step, steps_per_seq, x_ref, nxt_ref, kvn_ref, oc_ref)
    wa_out[...] = wa_ref[...].astype(BF16)
    wb_out[...] = wb_ref[...].astype(BF16)
    first_block = pl.program_id(2) == 0
    slope = slope_ref[0:1, 0:1]
    for _, dil in B_PATTERNS:
        bias_full, c = _bias_tile(slope, dil)
        bias0 = jnp.where((c >= LANE) | jnp.logical_not(first_block), bias_full, NEG)
        nsub = ATT_BLOCK // dil // LANE
        for r in range(dil):
            kc = vc = None
            for sub in range(nsub):
                rows = _rows(sub * LANE * dil + r, dil)
                if sub == 0:
                    prow = _rows((nsub - 1) * LANE * dil + r, dil)
                    kp, vp, bias = kp_ref[prow, :], vp_ref[prow, :], bias0
                else:
                    kp, vp, bias = kc, vc, bias_full
                kc, vc = k_ref[rows, :], v_ref[rows, :]
                kcat = jnp.concatenate([kp, kc], axis=0).astype(BF16)
                vcat = jnp.concatenate([vp, vc], axis=0).astype(BF16)
                m, l, acc = _attn_tile(q_ref[rows, :].astype(BF16), kcat, vcat, bias)
                if dil == 1:
                    m_scr[rows, :] = jnp.broadcast_to(m, (LANE, LANE))
                    l_scr[rows, :] = jnp.broadcast_to(l, (LANE, LANE))
                    acc_scr[rows, :] = acc
                else:
                    m_old = m_scr[rows, :]
                    m_new = jnp.maximum(m_old, m)
                    a_old = jnp.exp(m_old - m_new)
                    a_new = jnp.exp(m - m_new)
                    m_scr[rows, :] = m_new
                    l_scr[rows, :] = a_old * l_scr[rows, :] + a_new * l
                    acc_scr[rows, :] = a_old * acc_scr[rows, :] + a_new * acc
    for sub in range(ATT_BLOCK // LANE):
        rs = pl.ds(sub * LANE, LANE)
        o = acc_scr[rs, :] / l_scr[rs, :]
        y_ref[rs, :] = (o * _silu(gate_ref[rs, :])).astype(y_ref.dtype)


def _attn(qkv, slopes, a_arr, cache6, kvn, wa, wb):
    B, H3, L, dh = qkv.shape
    H = H3 // 3
    R = ATT_BLOCK
    nblk = L // R
    cur = lambda part: pl.BlockSpec((None, None, R, dh), lambda b, h, i: (b, part * H + h, i, 0))
    prev = lambda part: pl.BlockSpec((None, None, R, dh),
                                     lambda b, h, i: (b, part * H + h, jnp.maximum(i - 1, 0), 0))
    slope = pl.BlockSpec((None, 8, LANE), lambda b, h, i: (h, 0, 0))
    gate = pl.BlockSpec((R, LANE), lambda b, h, i: (b * nblk + i, 10 * TN // LANE + h))

    nb, W = cache6.shape[1], cache6.shape[2]
    steps = B * H * nblk
    per = steps // nb
    rows = W // per
    assert per * nb == steps and rows * per == W, (steps, nb, W)
    tail = (2, B_HEADS, B_HEAD_DIM)
    step = lambda b, h, i: (b * H + h) * nblk + i
    cblk = pl.BlockSpec((None, None, rows) + tail,
                        lambda b, h, i: (0, step(b, h, i) // per, step(b, h, i) % per, 0, 0, 0))
    cnext = pl.BlockSpec((None, None, 1) + tail,
                         lambda b, h, i: (0, step(b, h, i) // per,
                                          jnp.minimum((step(b, h, i) % per + 1) * rows, W - 1), 0, 0, 0))
    cnew = pl.BlockSpec((None,) + tail, lambda b, h, i: (step(b, h, i) // per, 0, 0, 0))
    wa_spec, wa_shape = _cast_specs(wa, steps, step)
    wb_spec, wb_shape = _cast_specs(wb, steps, step)
    return pl.pallas_call(
        functools.partial(_attn_kernel, per),
        grid=(B, H, nblk),
        in_specs=[slope, cur(0), cur(1), cur(2), prev(1), prev(2), gate, cblk, cnext, cnew,
                  wa_spec, wb_spec],
        out_specs=[pl.BlockSpec((R, LANE), lambda b, h, i: (b * nblk + i, h)), cblk,
                   wa_spec, wb_spec],
        out_shape=[jax.ShapeDtypeStruct((B * L, H * dh), BF16),
                   jax.ShapeDtypeStruct(cache6.shape, F32), wa_shape, wb_shape],
        scratch_shapes=[pltpu.VMEM((R, LANE), F32)] * 3,
        compiler_params=_cp("arbitrary", "arbitrary", "arbitrary"),
    )(slopes, qkv, qkv, qkv, qkv, qkv, a_arr, cache6, cache6, kvn, wa, wb)


def _attn_step_kernel(bias_ref, q_ref, kn_ref, vn_ref, gate_ref, c1_ref, c4_ref, c16_ref, y_ref):
    q, kn, vn = q_ref[...], kn_ref[...], vn_ref[...]
    s_self = jnp.sum(q * kn, axis=-1, keepdims=True) * ATT_SCALE
    blocks = (c1_ref, c4_ref, c16_ref)
    scores = []
    for p, c_ref in enumerate(blocks):
        kblk = c_ref[:, 0]
        scores.append(jnp.sum(kblk * q[None], axis=-1, keepdims=True) * ATT_SCALE - bias_ref[p])
    m = s_self
    for s in scores:
        m = jnp.maximum(m, jnp.max(s, axis=0))
    p_self = float(len(B_PATTERNS)) * jnp.exp(s_self - m)
    l = p_self
    o = p_self * vn
    for s, c_ref in zip(scores, blocks):
        p = jnp.exp(s - m[None])
        l = l + jnp.sum(p, axis=0)
        o = o + jnp.sum(p * c_ref[:, 1], axis=0)
    y_ref[...] = (o / l) * _silu(gate_ref[...])


def _attn_step(q, kn, vn, gate, cache6, bias):
    nb = q.shape[0]
    W = B_MAX_WINDOW
    hspec = pl.BlockSpec((None, B_HEADS, B_HEAD_DIM), lambda b: (b, 0, 0))
    tail = (2, B_HEADS, B_HEAD_DIM)
    views, specs = [], []
    for _, dil in B_PATTERNS:
        if dil == 1:
            views.append(cache6)
            specs.append(pl.BlockSpec((None, None, LANE) + tail,
                                      lambda b: (0, b, W // LANE - 1, 0, 0, 0)))
        else:
            views.append(cache6.reshape((1, nb, W // dil, dil) + tail))
            specs.append(pl.BlockSpec((None, None, LANE, None) + tail,
                                      lambda b, dil=dil: (0, b, W // dil // LANE - 1, 0, 0, 0, 0)))
    return pl.pallas_call(
        _attn_step_kernel,
        grid=(nb,),
        in_specs=[pl.BlockSpec(bias.shape, lambda b: (0, 0, 0, 0)), hspec, hspec, hspec, hspec]
                 + specs,
        out_specs=hspec,
        out_shape=jax.ShapeDtypeStruct((nb, B_HEADS, B_HEAD_DIM), F32),
        compiler_params=_cp("arbitrary"),
    )(bias, q, kn, vn, gate, *views)


def _cmul(ar, ai, br, bi):
    return ar * br - ai * bi, ar * bi + ai * br


def _blockdiag_rows(x):
    hi = pltpu.roll(x, C_STATE, 1)
    g = lax.broadcasted_iota(jnp.int32, (LANE, LANE), 0) // C_GROUP
    return jnp.concatenate([jnp.where(g == 2 * b, x, jnp.where(g == 2 * b + 1, hi, 0.0))
                            for b in range(S5_NS // LANE)], axis=1)


def _blockdiag_cols(x):
    piece = lambda g: x[g * C_STATE:(g + 1) * C_STATE, :]
    return jnp.concatenate([piece(0)] + [pltpu.roll(piece(g), C_GROUP * g, 1)
                                         for g in range(1, S5_BLK)], axis=0)


def _s5_prep_kernel(are_ref, aim_ref, ldt_ref, bre_ref, bim_ref, ctre_ref, ctim_ref,
                    t_ref, wb_ref, cb_ref, lam_ref, lamq_ref, cpl_ref):
    Q = S5_Q
    are, aim = are_ref[...], aim_ref[...]
    dt = jnp.exp(ldt_ref[...])
    mag = jnp.exp(dt * are)
    l_re = mag * jnp.cos(dt * aim)
    l_im = mag * jnp.sin(dt * aim)
    den = are * are + aim * aim
    f_re = ((l_re - 1.0) * are + l_im * aim) / den
    f_im = (l_im * are - (l_re - 1.0) * aim) / den
    lr, li = l_re[0:1, :], l_im[0:1, :]
    bb_re, bb_im = _cmul(f_re[0:1, :], f_im[0:1, :], _blockdiag_rows(bre_ref[...]),
                         _blockdiag_rows(bim_ref[...]))
    ctre, ctim = _blockdiag_cols(ctre_ref[...]), _blockdiag_cols(ctim_ref[...])
    cplain = jnp.concatenate([ctre, -ctim], axis=0).astype(BF16)
    cpl_ref[...] = cplain
    stk = jnp.concatenate([l_re, l_im, jnp.zeros((LANE - 16, S5_NS), F32)], axis=0).T
    lc_re, lc_im = stk[:, 0:1], stk[:, 8:9]
    p_re, p_im = jnp.ones((1, S5_NS), F32), jnp.zeros((1, S5_NS), F32)
    pc_re, pc_im = lc_re, lc_im
    dts = []
    for tau in range(Q):
        e_re, e_im = _cmul(bb_re, bb_im, p_re, p_im)
        wp = jnp.concatenate([e_re, e_im], axis=1).astype(BF16)
        s = Q - 1 - tau
        wb_ref[s * LANE:(s + 1) * LANE, :] = wp
        dts.append(jnp.dot(wp, cplain, preferred_element_type=F32).astype(BF16))
        c_re, c_im = _cmul(ctre, ctim, pc_re, pc_im)
        cb_ref[:, tau * LANE:(tau + 1) * LANE] = jnp.concatenate([c_re, -c_im], axis=0).astype(BF16)
        p_re, p_im = _cmul(p_re, p_im, lr, li)
        pc_re, pc_im = _cmul(pc_re, pc_im, lc_re, lc_im)
    zero = jnp.zeros((LANE, LANE), BF16)
    for s in range(Q):
        for q in range(Q):
            t_ref[s * LANE:(s + 1) * LANE, q * LANE:(q + 1) * LANE] = dts[q - s] if q >= s else zero
    lam_ref[...] = jnp.concatenate([l_re, l_im], axis=1)
    sub = lax.broadcasted_iota(jnp.int32, (8, 1), 0)
    q_re, q_im = p_re, p_im
    tab_re, tab_im = jnp.zeros((8, S5_NS), F32), jnp.zeros((8, S5_NS), F32)
    for r in range(8):
        tab_re = jnp.where(sub == r, q_re, tab_re)
        tab_im = jnp.where(sub == r, q_im, tab_im)
        q_re, q_im = _cmul(q_re, q_im, p_re, p_im)
    lamq_ref[...] = jnp.concatenate([tab_re, tab_im], axis=1)


def _s5_prep(are, aim, ldt, bre, bim, ctre, ctim):
    Q = S5_Q
    row = pl.BlockSpec((None, 8, S5_NS), lambda i: (i, 0, 0))
    blk = lambda a, b: pl.BlockSpec((None, a, b), lambda i: (i, 0, 0))
    return pl.pallas_call(
        _s5_prep_kernel,
        grid=(S5_NBLK,),
        in_specs=[row, row, row, blk(LANE, LANE), blk(LANE, LANE), blk(S5_NS, LANE), blk(S5_NS, LANE)],
        out_specs=[blk(Q * LANE, Q * LANE), blk(Q * LANE, 2 * S5_NS), blk(2 * S5_NS, Q * LANE),
                   blk(8, 2 * S5_NS), blk(8, 2 * S5_NS), blk(2 * S5_NS, LANE)],
        out_shape=[jax.ShapeDtypeStruct((S5_NBLK, Q * LANE, Q * LANE), BF16),
                   jax.ShapeDtypeStruct((S5_NBLK, Q * LANE, 2 * S5_NS), BF16),
                   jax.ShapeDtypeStruct((S5_NBLK, 2 * S5_NS, Q * LANE), BF16),
                   jax.ShapeDtypeStruct((S5_NBLK, 8, 2 * S5_NS), F32),
                   jax.ShapeDtypeStruct((S5_NBLK, 8, 2 * S5_NS), F32),
                   jax.ShapeDtypeStruct((S5_NBLK, 2 * S5_NS, LANE), BF16)],
        compiler_params=_cp("arbitrary"),
    )(are, aim, ldt, bre, bim, ctre, ctim)


def _s5_main_kernel(nb, u_ref, t_ref, wb_ref, cb_ref, lamq_ref, dsk_ref, g_ref, hfin_ref,
                    lhs_scr, s_scr, hp_scr, gs_scr):
    Q = S5_Q
    NS = S5_NS
    nrow = u_ref.shape[0] // Q
    nch = nrow // nb
    for s in range(Q):
        lhs_scr[:, s * LANE:(s + 1) * LANE] = u_ref[pl.ds(s, nrow, stride=Q), :].astype(BF16)
    lhs = lhs_scr[...]
    s_scr[...] = jnp.dot(lhs, wb_ref[...], preferred_element_type=F32)
    tab = lamq_ref[...]
    tab_re, tab_im = tab[:, :NS], tab[:, NS:]
    sub = lax.broadcasted_iota(jnp.int32, (8, 1), 0)

    def tile_body(t, carry):
        base = t * 8
        new = []
        for b in range(nb):
            h_re, h_im = carry[2 * b], carry[2 * b + 1]
            st = s_scr[pl.ds(b * nch + base, 8), :]
            x_re, x_im = st[:, :NS], st[:, NS:]
            for sh in (1, 2, 4):
                y_re = jnp.where(sub >= sh, pltpu.roll(x_re, sh, 0), 0.0)
                y_im = jnp.where(sub >= sh, pltpu.roll(x_im, sh, 0), 0.0)
                d_re, d_im = _cmul(tab_re[sh - 1:sh, :], tab_im[sh - 1:sh, :], y_re, y_im)
                x_re, x_im = x_re + d_re, x_im + d_im
            c_re, c_im = _cmul(tab_re, tab_im, h_re, h_im)
            x_re, x_im = x_re + c_re, x_im + c_im
            p_re = jnp.where(sub == 0, h_re, pltpu.roll(x_re, 1, 0))
            p_im = jnp.where(sub == 0, h_im, pltpu.roll(x_im, 1, 0))
            hp_scr[pl.ds(b * nch + base, 8), :] = jnp.concatenate([p_re, p_im], axis=1)
            new += [x_re[7:8, :], x_im[7:8, :]]
        return tuple(new)

    fin = (jnp.zeros((1, NS), F32),) * (2 * nb)
    for t in range(nch // 8):
        fin = tile_body(t, fin)
    for b in range(nb):
        hfin_ref[b:b + 1, :] = jnp.concatenate([fin[2 * b], fin[2 * b + 1]], axis=1)

    y = (jnp.dot(lhs, t_ref[...], preferred_element_type=F32)
         + jnp.dot(hp_scr[...].astype(BF16), cb_ref[...], preferred_element_type=F32))
    for q in range(Q):
        rows = pl.ds(q, nrow, stride=Q)
        yq = y[:, q * LANE:(q + 1) * LANE] + dsk_ref[...] * u_ref[rows, :]
        gs_scr[rows, :] = jax.nn.gelu(yq)
    g_ref[...] = gs_scr[...].astype(g_ref.dtype)


def _s5_main(u_gate, nb, t, wb, cb, lamq, dsk):
    Q = S5_Q
    M = u_gate.shape[0]
    nrow = M // Q
    blk = lambda a, b: pl.BlockSpec((None, a, b), lambda i: (i, 0, 0))
    return pl.pallas_call(
        functools.partial(_s5_main_kernel, nb),
        grid=(S5_NBLK,),
        in_specs=[pl.BlockSpec((M, LANE), lambda i: (0, i)),
                  blk(Q * LANE, Q * LANE), blk(Q * LANE, 2 * S5_NS), blk(2 * S5_NS, Q * LANE),
                  blk(8, 2 * S5_NS), pl.BlockSpec((1, LANE), lambda i: (0, i))],
        out_specs=[pl.BlockSpec((M, LANE), lambda i: (0, i)), blk(nb, 2 * S5_NS)],
        out_shape=[jax.ShapeDtypeStruct((M, E1), BF16),
                   jax.ShapeDtypeStruct((S5_NBLK, nb, 2 * S5_NS), F32)],
        scratch_shapes=[pltpu.VMEM((nrow, Q * LANE), BF16), pltpu.VMEM((nrow, 2 * S5_NS), F32),
                        pltpu.VMEM((nrow, 2 * S5_NS), F32), pltpu.VMEM((M, LANE), F32)],
        compiler_params=_cp("arbitrary"),
    )(u_gate, t, wb, cb, lamq, dsk)


def _s5_step_kernel(u_ref, h0_ref, bb_ref, cpl_ref, lam_ref, dsk_ref, g_ref, hn_ref):
    NS = S5_NS
    u = u_ref[...]
    s = jnp.dot(u.astype(BF16), bb_ref[...], preferred_element_type=F32)
    lam = lam_ref[0:1, :]
    h0 = h0_ref[...]
    n_re, n_im = _cmul(lam[:, :NS], lam[:, NS:], h0[:, :NS], h0[:, NS:])
    hn = jnp.concatenate([n_re + s[:, :NS], n_im + s[:, NS:]], axis=1)
    hn_ref[...] = hn
    y = jnp.dot(hn.astype(BF16), cpl_ref[...], preferred_element_type=F32) + dsk_ref[...] * u
    g_ref[...] = jax.nn.gelu(y)


def _s5_step(u_gate, h0, wb, cpl, lam, dsk):
    nb = u_gate.shape[0]
    blk = lambda a, b: pl.BlockSpec((None, a, b), lambda i: (i, 0, 0))
    return pl.pallas_call(
        _s5_step_kernel,
        grid=(S5_NBLK,),
        in_specs=[pl.BlockSpec((nb, LANE), lambda i: (0, i)), blk(nb, 2 * S5_NS),
                  pl.BlockSpec((None, LANE, 2 * S5_NS), lambda i: (i, S5_Q - 1, 0)),
                  blk(2 * S5_NS, LANE), blk(8, 2 * S5_NS), pl.BlockSpec((1, LANE), lambda i: (0, i))],
        out_specs=[pl.BlockSpec((nb, LANE), lambda i: (0, i)), blk(nb, 2 * S5_NS)],
        out_shape=[jax.ShapeDtypeStruct((nb, E1), F32),
                   jax.ShapeDtypeStruct((S5_NBLK, nb, 2 * S5_NS), F32)],
        compiler_params=_cp("arbitrary"),
    )(u_gate, h0, wb, cpl, lam, dsk)


def _state_to_blocks(st):
    nb = st.shape[0]
    x = st.reshape(nb, S5_NBLK, S5_NS, 2)
    return jnp.transpose(x, (1, 0, 3, 2)).reshape(S5_NBLK, nb, 2 * S5_NS)


def _blocks_to_state(h):
    nb = h.shape[1]
    x = h.reshape(S5_NBLK, nb, 2, S5_NS)
    return jnp.transpose(x, (1, 0, 3, 2)).reshape(nb, C_GROUPS, C_STATE, 2)


def kernel(x_prompt, x_sample, state_a_ssm, state_a_conv, cache_b_kv, state_c, norm_w, final_norm_w, even_w_in, a_conv_w, a_conv_b, a_dt_bias, a_A_log, a_D, a_norm_w, even_w_out, odd_w_in, c_A_re, c_A_im, c_B_re, c_B_im, c_C_re, c_C_im, c_D, c_log_dt, c_glu_w, c_glu_b, odd_w_out):
    B, L, D = x_prompt.shape
    nS = x_sample.shape[0]
    M = B * L
    Q = S5_Q

    nw0 = norm_w[0][None, :]
    nw1 = norm_w[1][None, :]
    fnw = final_norm_w[None, :]
    cw, cb = a_conv_w[0], a_conv_b[0][None, :]
    pad_h = lambda v: jnp.pad(v, (0, LANE - A_HEADS))[None, :]
    dtb, alog = pad_h(a_dt_bias[0]), pad_h(a_A_log[0])
    dsk_a = jnp.repeat(a_D[0], A_HEAD_DIM)[None, :]
    gnw = a_norm_w[0][None, :]
    head_expand = np.kron(np.eye(A_HEADS, dtype=np.float32), np.ones((1, A_HEAD_DIM), np.float32))
    head_expand = jnp.asarray(np.pad(head_expand, ((0, LANE - A_HEADS), (0, 0)))).astype(BF16)
    slopes_np = np.asarray(2.0 ** (-8.0 * np.arange(1, B_HEADS + 1) / B_HEADS), np.float32)
    slopes = jnp.asarray(np.broadcast_to(slopes_np[:, None, None], (B_HEADS, 8, LANE)))
    dist = np.stack([dil * (LANE - np.arange(LANE)) for _, dil in B_PATTERNS]).astype(np.float32)
    step_bias = jnp.asarray((dist[:, :, None] * slopes_np[None, None, :])[..., None])

    rows8 = lambda p: jnp.broadcast_to(p.reshape(S5_NBLK, 1, S5_NS), (S5_NBLK, 8, S5_NS))
    lane_pad = lambda p: jnp.pad(p, ((0, 0), (0, 0), (0, LANE - p.shape[-1])))
    b_rows = lambda p: lane_pad(jnp.transpose(p, (0, 2, 1)).reshape(S5_NBLK, LANE, C_STATE))
    c_rows = lambda p: lane_pad(jnp.transpose(p, (0, 2, 1)).reshape(S5_NBLK, S5_NS, C_GROUP))
    ldt = jnp.repeat(c_log_dt[0], C_STATE).reshape(S5_NBLK, S5_NS)
    t_m, wb_m, cb_m, lam, lamq, cpl = _s5_prep(
        rows8(c_A_re[0]), rows8(c_A_im[0]), rows8(ldt), b_rows(c_B_re[0]), b_rows(c_B_im[0]),
        c_rows(c_C_re[0]), c_rows(c_C_im[0]))
    w0t = jnp.transpose(even_w_in[0])
    s1 = A_WIDTH + A_CONV_DIM
    s2 = s1 + A_HEADS
    w_a = jnp.concatenate([w0t[A_WIDTH:s1], w0t[:A_WIDTH], w0t[s2 + 6144:], w0t[s1:s2],
                           jnp.zeros((A_TILE - A_HEADS, D), F32)], axis=0).astype(BF16)
    w_qkv = w0t[s2:s2 + 6144].astype(BF16)
    dsk_c = c_D[0][None, :]
    glu_b = c_glu_b[0][None, :]

    xs = x_sample.reshape(nS, D)
    a_s, xs_n = _mm_norm(xs, nw0, w_a, nS, 2048, keep_xn=True, wt=True)
    (qkv_s,) = _proj_qkv(xs_n, w_qkv, 1, nS, nS, window=False)
    col = lambda a, b: a_s[:, a:b]
    hv = lambda part: jnp.transpose(qkv_s[0, part * B_HEADS:(part + 1) * B_HEADS], (1, 0, 2))
    gate_s = col(5120, 7168).reshape(nS, B_HEADS, B_HEAD_DIM)
    kvn = jnp.stack([hv(1), hv(2)], axis=1)

    xp = x_prompt.reshape(M, D)
    a_arr, xp_n, w_glu = _mm_norm(xp, nw0, w_a, 1024, A_TILE, keep_xn=True, cast=c_glu_w[0],
                                  wt=True)
    qkv, kv_keep, w_in1 = _proj_qkv(xp_n, w_qkv, B, L, 1024, window=True, cast=odd_w_in[0])
    y_a, ssm_p, tail_p = _ssd(a_arr, B, L, cw, cb, dtb, alog, dsk_a, gnw)
    y_b, new_cache, w_out0, w_out1 = _attn(qkv, slopes, a_arr, cache_b_kv, kvn,
                                           even_w_out[0], odd_w_out[0])
    hp1 = _outproj0(y_a, y_b, w_out0, xp, 1024)
    u_gate = _mm_norm(hp1, nw1, w_in1, 1024, 1024)
    g, hfin_p = _s5_main(u_gate, B, t_m, wb_m, cb_m, lamq, dsk_c)
    y1 = _glu(g, w_glu, glu_b, u_gate, 1024, BF16)
    y_prompt = _outproj1(y1, w_out1, hp1, fnw, 1024).reshape(B, L, D)

    y_as, ssm_s, conv_s = _ssd_step(col(0, 3072)[:, None, :], state_a_conv[0],
                                    col(7168, 7296)[:, None, :], col(3072, 5120)[:, None, :],
                                    state_a_ssm[0].reshape(nS, A_WIDTH, A_STATE),
                                    cw, cb, dtb, alog, head_expand, dsk_a, gnw)
    y_bs = _attn_step(hv(0), hv(1), hv(2), gate_s, cache_b_kv, step_bias)
    hs1 = _outproj0(y_as.reshape(nS, A_WIDTH), y_bs.reshape(nS, A_WIDTH), w_out0, xs, nS, 1024)
    u_gate_s = _mm_norm(hs1, nw1, w_in1, nS, 2048)
    g_s, hnew_s = _s5_step(u_gate_s, _state_to_blocks(state_c[0]), wb_m, cpl, lam, dsk_c)
    y1_s = _glu(g_s, w_glu, glu_b, u_gate_s, nS, F32, 1024)
    y_sample = _outproj1(y1_s, w_out1, hs1, fnw, nS).reshape(nS, 1, D)

    return (y_prompt, y_sample,
            ssm_p.reshape(1, B, A_HEADS, A_HEAD_DIM, A_STATE),
            tail_p[:, 5:8, :][None],
            kv_keep[None],
            _blocks_to_state(hfin_p)[None],
            ssm_s.reshape(1, nS, A_HEADS, A_HEAD_DIM, A_STATE),
            conv_s[None],
            new_cache,
            _blocks_to_state(hnew_s)[None])
```

```python
import functools

import numpy as np
import jax
import jax.numpy as jnp
from jax import lax
from jax.experimental import pallas as pl
from jax.experimental.pallas import tpu as pltpu

F32 = jnp.float32
BF16 = jnp.bfloat16

D_MODEL = 2048
EPS = 1e-5
A_WIDTH = 2048
A_HEAD_DIM = 64
A_HEADS = 32
A_GROUPS = 4
A_STATE = 128
A_CONV_W = 4
A_CONV_DIM = 3072
SSD_CHUNK = 128
B_HEADS = 16
B_HEAD_DIM = 128
B_PATTERNS = ((128, 1), (512, 4), (2048, 16))
B_MAX_WINDOW = 2048
C_GROUP = 16
C_GROUPS = 256
C_STATE = 64
E1 = 4096

LANE = 128
S5_Q = 8
S5_BLK = 8
S5_NBLK = C_GROUPS // S5_BLK
S5_NS = S5_BLK * C_STATE
TN = 512
A_TILE = 1024
ATT_BLOCK = 2048
NEG = -1e30
ATT_SCALE = B_HEAD_DIM ** -0.5
LOG2E = 1.4426950408889634
VMEM_LIMIT = 56 * 1024 * 1024

NT_DIMS = (((1,), (1,)), ((), ()))


def _cp(*sem):
    return pltpu.CompilerParams(dimension_semantics=sem, vmem_limit_bytes=VMEM_LIMIT)


def _silu(x):
    return x * jax.nn.sigmoid(x)


def _softplus(x):
    return jnp.maximum(x, 0.0) + jnp.log1p(jnp.exp(-jnp.abs(x)))


def _split3(a):
    hi = a.astype(BF16)
    r = a - hi.astype(F32)
    mid = r.astype(BF16)
    lo = (r - mid.astype(F32)).astype(BF16)
    return hi, mid, lo


def _dot3_left(m, a):
    hi, mid, lo = _split3(a)
    d = lambda p: jnp.dot(m, p, preferred_element_type=F32)
    return (d(lo) + d(mid)) + d(hi)


def _dot3_right(a, m):
    hi, mid, lo = _split3(a)
    d = lambda p: jnp.dot(p, m, preferred_element_type=F32)
    return (d(lo) + d(mid)) + d(hi)


def _rms(x, w):
    ms = jnp.mean(x * x, axis=-1, keepdims=True)
    return (x * lax.rsqrt(ms + EPS)) * w


QKV_HEADS = 8
CAST_ROWS = 64


def _cast_specs(w, nsteps, step_of):
    rows, cols = w.shape
    per = CAST_ROWS
    while rows // per > nsteps:
        per *= 2
    nblk = rows // per
    assert nblk * per == rows, (w.shape, nsteps)
    spec = pl.BlockSpec((per, cols), lambda *ids: (jnp.minimum(step_of(*ids), nblk - 1), 0))
    return spec, jax.ShapeDtypeStruct(w.shape, BF16)


def _proj_qkv_kernel(rpb, nskip, window, cast, *refs):
    refs = list(refs)
    x_ref, w_ref = refs[:2]
    if cast:
        refs[-1][...] = refs[2][...].astype(BF16)
    qkv_ref = refs[3 if cast else 2]
    j = pl.program_id(1)
    acc = _dot_w(x_ref[...].astype(BF16), w_ref[...], True)
    for hh in range(QKV_HEADS):
        qkv_ref[hh] = acc[:, hh * LANE:(hh + 1) * LANE]
    if window:
        kv_ref = refs[4 if cast else 3]
        tiles_per = B_HEADS // QKV_HEADS
        kept = pl.program_id(0) % rpb >= nskip
        for part in range(tiles_per):
            @pl.when(kept & (j >= tiles_per) & (j % tiles_per == part))
            def _(part=part):
                for hh in range(QKV_HEADS):
                    kv_ref[:, part * QKV_HEADS + hh, :] = acc[:, hh * LANE:(hh + 1) * LANE]


def _proj_qkv(xn, w, B, L, tm, window, cast=None):
    M = B * L
    tn = QKV_HEADS * LANE
    rpb = L // tm
    keep = min(B_MAX_WINDOW, L)
    nskip = (L - keep) // tm
    tiles_per = B_HEADS // QKV_HEADS

    def kv_map(i, j):
        ri = i % rpb
        return (i // rpb, jnp.maximum(ri - nskip, 0),
                jnp.where((ri < nskip) | (j < 2 * tiles_per), 0, 1), 0, 0)

    out_specs = [pl.BlockSpec((None, QKV_HEADS, tm, LANE), lambda i, j: (i // rpb, j, i % rpb, 0))]
    out_shape = [jax.ShapeDtypeStruct((B, 3 * B_HEADS, L, B_HEAD_DIM), F32)]
    if window:
        out_specs.append(pl.BlockSpec((None, tm, None, B_HEADS, B_HEAD_DIM), kv_map))
        out_shape.append(jax.ShapeDtypeStruct((B, keep, 2, B_HEADS, B_HEAD_DIM), F32))
    nj = w.shape[0] // tn
    in_specs = [pl.BlockSpec((tm, D_MODEL), lambda i, j: (i, 0)),
                pl.BlockSpec((tn, D_MODEL), lambda i, j: (j, 0))]
    operands = [xn, w]
    if cast is not None:
        spec, shape = _cast_specs(cast, (M // tm) * nj, lambda i, j: i * nj + j)
        in_specs.append(spec)
        operands.append(cast)
        out_specs.append(spec)
        out_shape.append(shape)
    return pl.pallas_call(
        functools.partial(_proj_qkv_kernel, rpb, nskip, window, cast is not None),
        grid=(M // tm, nj),
        in_specs=in_specs,
        out_specs=out_specs,
        out_shape=out_shape,
        compiler_params=_cp("arbitrary", "arbitrary"),
    )(*operands)


def _dot_w(x, w, wt):
    if wt:
        return lax.dot_general(x, w, NT_DIMS, preferred_element_type=F32)
    return jnp.dot(x, w, preferred_element_type=F32)


def _mm_norm_kernel(cast, wt, *refs):
    refs = list(refs)
    x_ref, nw_ref, w_ref = refs[:3]
    if cast:
        refs[-1][...] = refs[3][...].astype(BF16)
    o_ref, xn_ref = refs[4:6] if cast else refs[3:5]

    @pl.when(pl.program_id(1) == 0)
    def _():
        xn_ref[...] = _rms(x_ref[...], nw_ref[...]).astype(xn_ref.dtype)

    o_ref[...] = _dot_w(xn_ref[...].astype(BF16), w_ref[...], wt)


def _mm_norm(x, nw, w, tm, tn=TN, keep_xn=False, cast=None, wt=False):
    M = x.shape[0]
    N, K = w.shape if wt else w.shape[::-1]
    nj = N // tn
    in_specs = [pl.BlockSpec((tm, K), lambda i, j: (i, 0)),
                pl.BlockSpec((1, K), lambda i, j: (0, 0)),
                pl.BlockSpec((tn, K), lambda i, j: (j, 0)) if wt else
                pl.BlockSpec((K, tn), lambda i, j: (0, j))]
    operands = [x, nw, w]
    out_specs = [pl.BlockSpec((tm, tn), lambda i, j: (i, j))]
    out_shape = [jax.ShapeDtypeStruct((M, N), F32)]
    if keep_xn:
        out_specs.append(pl.BlockSpec((tm, K), lambda i, j: (i, 0)))
        out_shape.append(jax.ShapeDtypeStruct((M, K), BF16 if tm % 16 == 0 else F32))
    if cast is not None:
        assert keep_xn
        spec, shape = _cast_specs(cast, (M // tm) * nj, lambda i, j: i * nj + j)
        in_specs.append(spec)
        operands.append(cast)
        out_specs.append(spec)
        out_shape.append(shape)
    out = pl.pallas_call(
        functools.partial(_mm_norm_kernel, cast is not None, wt),
        grid=(M // tm, nj),
        in_specs=in_specs,
        out_specs=out_specs,
        out_shape=out_shape,
        scratch_shapes=[] if keep_xn else [pltpu.VMEM((tm, K), BF16)],
        compiler_params=_cp("arbitrary", "arbitrary"),
    )(*operands)
    return out if len(out) > 1 else out[0]


def _outproj0_kernel(ya_ref, yb_ref, wa_ref, wb_ref, res_ref, o_ref):
    o_ref[...] = (res_ref[...]
                  + jnp.dot(ya_ref[...].astype(BF16), wa_ref[...], preferred_element_type=F32)
                  + jnp.dot(yb_ref[...].astype(BF16), wb_ref[...], preferred_element_type=F32))


def _outproj0(ya, yb, w, res, tm, tn=TN):
    M = ya.shape[0]
    K = ya.shape[1]
    return pl.pallas_call(
        _outproj0_kernel,
        grid=(M // tm, D_MODEL // tn),
        in_specs=[pl.BlockSpec((tm, K), lambda i, j: (i, 0)),
                  pl.BlockSpec((tm, K), lambda i, j: (i, 0)),
                  pl.BlockSpec((K, tn), lambda i, j: (0, j)),
                  pl.BlockSpec((K, tn), lambda i, j: (1, j)),
                  pl.BlockSpec((tm, tn), lambda i, j: (i, j))],
        out_specs=pl.BlockSpec((tm, tn), lambda i, j: (i, j)),
        out_shape=jax.ShapeDtypeStruct((M, D_MODEL), F32),
        compiler_params=_cp("arbitrary", "arbitrary"),
    )(ya, yb, w, w, res)


def _glu_kernel(g_ref, wa_ref, wb_ref, ba_ref, bb_ref, gate_ref, o_ref):
    tm = g_ref.shape[0]
    hr = min(tm, 512)
    for r in range(tm // hr):
        rs = pl.ds(r * hr, hr)
        g = g_ref[rs, :].astype(BF16)
        ga = jnp.dot(g, wa_ref[...], preferred_element_type=F32) + ba_ref[...]
        gb = jnp.dot(g, wb_ref[...], preferred_element_type=F32) + bb_ref[...]
        o_ref[rs, :] = (ga * jax.nn.sigmoid(gb) * _silu(gate_ref[rs, :])).astype(o_ref.dtype)


def _glu(g, w, bias, u_gate, tm, out_dtype, tn=TN):
    M = g.shape[0]
    nb = E1 // tn
    return pl.pallas_call(
        _glu_kernel,
        grid=(M // tm, nb),
        in_specs=[pl.BlockSpec((tm, E1), lambda i, j: (i, 0)),
                  pl.BlockSpec((E1, tn), lambda i, j: (0, j)),
                  pl.BlockSpec((E1, tn), lambda i, j: (0, j + nb)),
                  pl.BlockSpec((1, tn), lambda i, j: (0, j)),
                  pl.BlockSpec((1, tn), lambda i, j: (0, j + nb)),
                  pl.BlockSpec((tm, tn), lambda i, j: (i, j + nb))],
        out_specs=pl.BlockSpec((tm, tn), lambda i, j: (i, j)),
        out_shape=jax.ShapeDtypeStruct((M, E1), out_dtype),
        compiler_params=_cp("arbitrary", "arbitrary"),
    )(g, w, w, bias, bias, u_gate)


def _outproj1_kernel(y_ref, w_ref, res_ref, fnw_ref, o_ref):
    j = pl.program_id(1)
    nj = D_MODEL // TN
    h = res_ref[...] + jnp.dot(y_ref[...].astype(BF16), w_ref[...], preferred_element_type=F32)
    for jj in range(nj):
        @pl.when(j == jj)
        def _(jj=jj):
            o_ref[:, jj * TN:(jj + 1) * TN] = h

    @pl.when(j == nj - 1)
    def _():
        o_ref[...] = _rms(o_ref[...], fnw_ref[...])


def _outproj1(y, w, res, fnw, tm):
    M, K = y.shape
    return pl.pallas_call(
        _outproj1_kernel,
        grid=(M // tm, D_MODEL // TN),
        in_specs=[pl.BlockSpec((tm, K), lambda i, j: (i, 0)),
                  pl.BlockSpec((K, TN), lambda i, j: (0, j)),
                  pl.BlockSpec((tm, TN), lambda i, j: (i, j)),
                  pl.BlockSpec((1, D_MODEL), lambda i, j: (0, 0))],
        out_specs=pl.BlockSpec((tm, D_MODEL), lambda i, j: (i, 0)),
        out_shape=jax.ShapeDtypeStruct((M, D_MODEL), F32),
        compiler_params=_cp("arbitrary", "arbitrary"),
    )(y, w, res, fnw)


def _ssd_kernel(xbc_ref, dt_ref, z0_ref, z1_ref, cw_ref, cb_ref, dtb_ref, alog_ref, dsk_ref, gnw_ref,
                y_ref, st_ref, tail_ref, h_scr, ext_scr, y_scr):
    c = pl.program_id(1)
    Q = SSD_CHUNK

    @pl.when(c == 0)
    def _():
        h_scr[...] = jnp.zeros_like(h_scr)
        ext_scr[0:8, :] = jnp.zeros((8, A_CONV_DIM), F32)

    x_raw = xbc_ref[...]
    ext_scr[8:8 + Q, :] = x_raw
    conv = cb_ref[...] + cw_ref[3:4, :] * x_raw
    for k in (1, 2, 3):
        conv = conv + cw_ref[3 - k:4 - k, :] * ext_scr[pl.ds(8 - k, Q), :]
    ext_scr[0:8, :] = x_raw[Q - 8:Q, :]
    xc = _silu(conv)

    dt = _softplus(dt_ref[...] + dtb_ref[...])
    a = dt * (-jnp.exp(alog_ref[...]))
    qi = lax.broadcasted_iota(jnp.int32, (Q, Q), 0)
    si = lax.broadcasted_iota(jnp.int32, (Q, Q), 1)
    tri = qi >= si
    a_cum = _dot3_left(tri.astype(BF16), a)
    a_cum_t = a_cum.T
    lo_half = si < A_HEAD_DIM

    for g in range(A_GROUPS):
        bg = xc[:, A_WIDTH + g * A_STATE:A_WIDTH + (g + 1) * A_STATE].astype(BF16)
        cg = xc[:, A_WIDTH + (A_GROUPS + g) * A_STATE:A_WIDTH + (A_GROUPS + g + 1) * A_STATE].astype(BF16)
        cbm = lax.dot_general(cg, bg, NT_DIMS, preferred_element_type=F32)
        hprev = h_scr[g * 512:(g + 1) * 512, :]
        yoff = lax.dot_general(cg, hprev.astype(BF16), NT_DIMS, preferred_element_type=F32)
        for jj in range(4):
            p = g * 4 + jj
            h0, h1 = 2 * p, 2 * p + 1
            dtp = jnp.where(lo_half, dt[:, h0:h0 + 1], dt[:, h1:h1 + 1])
            ap = jnp.where(lo_half, a_cum[:, h0:h0 + 1], a_cum[:, h1:h1 + 1])
            x_p = xc[:, p * LANE:(p + 1) * LANE]
            xdt = x_p * dtp
            g0 = cbm * jnp.exp(jnp.where(tri, a_cum[:, h0:h0 + 1] - a_cum_t[h0:h0 + 1, :], NEG))
            g1 = cbm * jnp.exp(jnp.where(tri, a_cum[:, h1:h1 + 1] - a_cum_t[h1:h1 + 1, :], NEG))
            lhs = jnp.concatenate([g0, g1], axis=1).astype(BF16)
            rhs = jnp.concatenate([jnp.where(lo_half, xdt, 0.0), jnp.where(lo_half, 0.0, xdt)],
                                  axis=0).astype(BF16)
            ydiag = jnp.dot(lhs, rhs, preferred_element_type=F32)
            y_scr[:, p * LANE:(p + 1) * LANE] = (
                ydiag + yoff[:, jj * LANE:(jj + 1) * LANE] * jnp.exp(ap)
                + dsk_ref[:, p * LANE:(p + 1) * LANE] * x_p)
            xd = xdt * jnp.exp(ap[Q - 1:Q, :] - ap)
            st = jnp.dot(xd.T.astype(BF16), bg, preferred_element_type=F32)
            cd = jnp.exp(jnp.where(qi < A_HEAD_DIM, a_cum_t[h0:h0 + 1, Q - 1:Q],
                                   a_cum_t[h1:h1 + 1, Q - 1:Q]))
            h_scr[p * LANE:(p + 1) * LANE, :] = cd * hprev[jj * LANE:(jj + 1) * LANE, :] + st

    z = jnp.concatenate([z0_ref[...], z1_ref[...]], axis=1)
    y_ref[...] = _rms(y_scr[...] * _silu(z), gnw_ref[...]).astype(y_ref.dtype)

    @pl.when(c == pl.num_programs(1) - 1)
    def _():
        st_ref[...] = h_scr[...]
        tail_ref[...] = x_raw[Q - 8:Q, :]


def _ssd(a_arr, B, L, cw, cb, dtb, alog, dsk, gnw):
    nc = L // SSD_CHUNK
    Q = SSD_CHUNK
    row = lambda b, c: b * nc + c
    full = lambda shp: pl.BlockSpec(shp, lambda b, c: (0, 0))
    return pl.pallas_call(
        _ssd_kernel,
        grid=(B, nc),
        in_specs=[pl.BlockSpec((Q, A_CONV_DIM), lambda b, c: (row(b, c), 0)),
                  pl.BlockSpec((Q, LANE), lambda b, c: (row(b, c), 14 * TN // LANE)),
                  pl.BlockSpec((Q, A_WIDTH // 2), lambda b, c: (row(b, c), 3)),
                  pl.BlockSpec((Q, A_WIDTH // 2), lambda b, c: (row(b, c), 4)),
                  full((A_CONV_W, A_CONV_DIM)), full((1, A_CONV_DIM)), full((1, LANE)),
                  full((1, LANE)), full((1, A_WIDTH)), full((1, A_WIDTH))],
        out_specs=[pl.BlockSpec((Q, A_WIDTH), lambda b, c: (row(b, c), 0)),
                   pl.BlockSpec((None, A_WIDTH, A_STATE), lambda b, c: (b, 0, 0)),
                   pl.BlockSpec((None, 8, A_CONV_DIM), lambda b, c: (b, 0, 0))],
        out_shape=[jax.ShapeDtypeStruct((B * L, A_WIDTH), BF16),
                   jax.ShapeDtypeStruct((B, A_WIDTH, A_STATE), F32),
                   jax.ShapeDtypeStruct((B, 8, A_CONV_DIM), F32)],
        scratch_shapes=[pltpu.VMEM((A_WIDTH, A_STATE), F32), pltpu.VMEM((8 + Q, A_CONV_DIM), F32),
                        pltpu.VMEM((Q, A_WIDTH), F32)],
        compiler_params=_cp("arbitrary", "arbitrary"),
    )(a_arr, a_arr, a_arr, a_arr, cw, cb, dtb, alog, dsk, gnw)


def _ssd_step_kernel(xbc_ref, cs_ref, dt_ref, z_ref, h0_ref, cw_ref, cb_ref, dtb_ref, alog_ref,
                     e_ref, dsk_ref, gnw_ref, y_ref, hn_ref, nc_ref):
    x_raw = xbc_ref[...]
    cs = cs_ref[...]
    conv = (cb_ref[...] + cw_ref[0:1, :] * cs[0:1, :] + cw_ref[1:2, :] * cs[1:2, :]
            + cw_ref[2:3, :] * cs[2:3, :] + cw_ref[3:4, :] * x_raw)
    xc = _silu(conv)
    nc_ref[0:1, :] = cs[1:2, :]
    nc_ref[1:2, :] = cs[2:3, :]
    nc_ref[2:3, :] = x_raw

    dt = _softplus(dt_ref[...] + dtb_ref[...])
    da = dt * (-jnp.exp(alog_ref[...]))
    e = e_ref[...]
    dt_e = _dot3_right(jnp.broadcast_to(dt, (8, LANE)), e)[0:1, :]
    da_e = _dot3_right(jnp.broadcast_to(da, (8, LANE)), e)[0:1, :]
    xs = xc[:, :A_WIDTH]
    sub = lax.broadcasted_iota(jnp.int32, (8, 1), 0)
    rows = jnp.where(sub == 0, xs * dt_e, jnp.where(sub == 1, jnp.exp(da_e), 0.0))
    cols = jnp.concatenate([rows, jnp.zeros((LANE - 8, A_WIDTH), F32)], axis=0).T
    xdt_col = cols[:, 0:1]
    dec_col = cols[:, 1:2]
    ys = []
    for g in range(A_GROUPS):
        bg = xc[:, A_WIDTH + g * A_STATE:A_WIDTH + (g + 1) * A_STATE]
        cg = xc[:, A_WIDTH + (A_GROUPS + g) * A_STATE:A_WIDTH + (A_GROUPS + g + 1) * A_STATE]
        sl = slice(g * 512, (g + 1) * 512)
        hn = dec_col[sl, :] * h0_ref[sl, :] + xdt_col[sl, :] * bg
        hn_ref[sl, :] = hn
        c8 = jnp.broadcast_to(cg, (8, A_STATE)).astype(BF16)
        ys.append(lax.dot_general(c8, hn.astype(BF16), NT_DIMS, preferred_element_type=F32)[0:1, :])
    y = jnp.concatenate(ys, axis=1) + dsk_ref[...] * xs
    y_ref[...] = _rms(y * _silu(z_ref[...]), gnw_ref[...])


def _ssd_step(xbc, cs, dt, z, h0, cw, cb, dtb, alog, e, dsk, gnw):
    nb = xbc.shape[0]
    per_b = lambda n: pl.BlockSpec((None, 1, n), lambda b: (b, 0, 0))
    full = lambda shp: pl.BlockSpec(shp, lambda b: (0, 0))
    return pl.pallas_call(
        _ssd_step_kernel,
        grid=(nb,),
        in_specs=[per_b(A_CONV_DIM), pl.BlockSpec((None, 3, A_CONV_DIM), lambda b: (b, 0, 0)),
                  per_b(LANE), per_b(A_WIDTH),
                  pl.BlockSpec((None, A_WIDTH, A_STATE), lambda b: (b, 0, 0)),
                  full((A_CONV_W, A_CONV_DIM)), full((1, A_CONV_DIM)), full((1, LANE)),
                  full((1, LANE)), full((LANE, A_WIDTH)), full((1, A_WIDTH)), full((1, A_WIDTH))],
        out_specs=[per_b(A_WIDTH),
                   pl.BlockSpec((None, A_WIDTH, A_STATE), lambda b: (b, 0, 0)),
                   pl.BlockSpec((None, 3, A_CONV_DIM), lambda b: (b, 0, 0))],
        out_shape=[jax.ShapeDtypeStruct((nb, 1, A_WIDTH), F32),
                   jax.ShapeDtypeStruct((nb, A_WIDTH, A_STATE), F32),
                   jax.ShapeDtypeStruct((nb, 3, A_CONV_DIM), F32)],
        compiler_params=_cp("arbitrary"),
    )(xbc, cs, dt, z, h0, cw, cb, dtb, alog, e, dsk, gnw)


def _bias_tile(slope, dil):
    i = lax.broadcasted_iota(jnp.int32, (LANE, 2 * LANE), 0)
    c = lax.broadcasted_iota(jnp.int32, (LANE, 2 * LANE), 1)
    j = LANE + i - c
    valid = (j >= 0) & (j <= LANE)
    return jnp.where(valid, -(slope * (dil * j).astype(F32)), NEG), c


def _attn_tile(q, kcat, vcat, bias2):
    s = lax.dot_general(q, kcat, NT_DIMS, preferred_element_type=F32) * (ATT_SCALE * LOG2E) + bias2
    m = jnp.max(s, axis=1, keepdims=True)
    p = jnp.exp2(s - m)
    l = jnp.sum(p, axis=1, keepdims=True)
    acc = jnp.dot(p.astype(BF16), vcat, preferred_element_type=F32)
    return m, l, acc


def _rows(start, stride):
    return pl.ds(start, LANE, stride=stride) if stride > 1 else pl.ds(start, LANE)


def _shift_cache_block(step, steps_per_seq, x_ref, nxt_ref, kvn_ref, o_ref):
    R = x_ref.shape[0]
    o_ref[pl.ds(0, R - 1)] = x_ref[pl.ds(1, R - 1)]
    last = step % steps_per_seq == steps_per_seq - 1
    o_ref[R - 1] = jnp.where(last, kvn_ref[...], nxt_ref[0])


def _attn_kernel(steps_per_seq, slope_ref, q_ref, k_ref, v_ref, kp_ref, vp_ref, gate_ref,
                 x_ref, nxt_ref, kvn_ref, wa_ref, wb_ref, y_ref, oc_ref, wa_out, wb_out, *scr):
    step = ((pl.program_id(0) * pl.num_programs(1) + pl.program_id(1)) * pl.num_programs(2)
            + pl.program_id(2))
    _shift_cache_block(step, steps_per_seq, x_ref, nxt_ref, kvn_ref, oc_ref)
    wa_out[...] = wa_ref[...].astype(BF16)
    wb_out[...] = wb_ref[...].astype(BF16)
    first_block = pl.program_id(2) == 0
    slope = slope_ref[0:1, 0:1]
    npat = len(B_PATTERNS)
    m_scr, l_scr, acc_scr = scr[:npat], scr[npat:2 * npat], scr[2 * npat:]
    for pat, (_, dil) in enumerate(B_PATTERNS):
        bias_full, c = _bias_tile(slope, dil)
        bias_full = jnp.where(bias_full > NEG, bias_full * LOG2E, NEG)
        bias0 = jnp.where((c >= LANE) | jnp.logical_not(first_block), bias_full, NEG)
        nsub = ATT_BLOCK // dil // LANE
        for r in range(dil):
            kc = vc = None
            for sub in range(nsub):
                rows = _rows(sub * LANE * dil + r, dil)
                if sub == 0:
                    prow = _rows((nsub - 1) * LANE * dil + r, dil)
                    kp, vp, bias = kp_ref[prow, :], vp_ref[prow, :], bias0
                else:
                    kp, vp, bias = kc, vc, bias_full
                kc, vc = k_ref[rows, :], v_ref[rows, :]
                kcat = jnp.concatenate([kp, kc], axis=0).astype(BF16)
                vcat = jnp.concatenate([vp, vc], axis=0).astype(BF16)
                m, l, acc = _attn_tile(q_ref[rows, :].astype(BF16), kcat, vcat, bias)
                m_scr[pat][rows, :] = jnp.broadcast_to(m, (LANE, LANE))
                l_scr[pat][rows, :] = jnp.broadcast_to(l, (LANE, LANE))
                acc_scr[pat][rows, :] = acc
    for sub in range(ATT_BLOCK // LANE):
        rs = pl.ds(sub * LANE, LANE)
        ms = [m_scr[p][rs, :] for p in range(npat)]
        m = functools.reduce(jnp.maximum, ms)
        l = acc = 0.0
        for p in range(npat):
            a = jnp.exp2(ms[p] - m)
            l = l + a * l_scr[p][rs, :]
            acc = acc + a * acc_scr[p][rs, :]
        y_ref[rs, :] = ((acc / l) * _silu(gate_ref[rs, :])).astype(y_ref.dtype)


def _attn(qkv, slopes, a_arr, cache6, kvn, wa, wb):
    B, H3, L, dh = qkv.shape
    H = H3 // 3
    R = ATT_BLOCK
    nblk = L // R
    cur = lambda part: pl.BlockSpec((None, None, R, dh), lambda b, h, i: (b, part * H + h, i, 0))
    prev = lambda part: pl.BlockSpec((None, None, R, dh),
                                     lambda b, h, i: (b, part * H + h, jnp.maximum(i - 1, 0), 0))
    slope = pl.BlockSpec((None, 8, LANE), lambda b, h, i: (h, 0, 0))
    gate = pl.BlockSpec((R, LANE), lambda b, h, i: (b * nblk + i, 10 * TN // LANE + h))

    nb, W = cache6.shape[1], cache6.shape[2]
    steps = B * H * nblk
    per = steps // nb
    rows = W // per
    assert per * nb == steps and rows * per == W, (steps, nb, W)
    tail = (2, B_HEADS, B_HEAD_DIM)
    step = lambda b, h, i: (b * H + h) * nblk + i
    cblk = pl.BlockSpec((None, None, rows) + tail,
                        lambda b, h, i: (0, step(b, h, i) // per, step(b, h, i) % per, 0, 0, 0))
    cnext = pl.BlockSpec((None, None, 1) + tail,
                         lambda b, h, i: (0, step(b, h, i) // per,
                                          jnp.minimum((step(b, h, i) % per + 1) * rows, W - 1), 0, 0, 0))
    cnew = pl.BlockSpec((None,) + tail, lambda b, h, i: (step(b, h, i) // per, 0, 0, 0))
    wa_spec, wa_shape = _cast_specs(wa, steps, step)
    wb_spec, wb_shape = _cast_specs(wb, steps, step)
    return pl.pallas_call(
        functools.partial(_attn_kernel, per),
        grid=(B, H, nblk),
        in_specs=[slope, cur(0), cur(1), cur(2), prev(1), prev(2), gate, cblk, cnext, cnew,
                  wa_spec, wb_spec],
        out_specs=[pl.BlockSpec((R, LANE), lambda b, h, i: (b * nblk + i, h)), cblk,
                   wa_spec, wb_spec],
        out_shape=[jax.ShapeDtypeStruct((B * L, H * dh), BF16),
                   jax.ShapeDtypeStruct(cache6.shape, F32), wa_shape, wb_shape],
        scratch_shapes=[pltpu.VMEM((R, LANE), F32)] * (3 * len(B_PATTERNS)),
        compiler_params=_cp("arbitrary", "arbitrary", "arbitrary"),
    )(slopes, qkv, qkv, qkv, qkv, qkv, a_arr, cache6, cache6, kvn, wa, wb)


def _attn_step_kernel(bias_ref, q_ref, kn_ref, vn_ref, gate_ref, c1_ref, c4_ref, c16_ref, y_ref):
    q, kn, vn = q_ref[...], kn_ref[...], vn_ref[...]
    s_self = jnp.sum(q * kn, axis=-1, keepdims=True) * ATT_SCALE
    blocks = (c1_ref, c4_ref, c16_ref)
    scores = []
    for p, c_ref in enumerate(blocks):
        kblk = c_ref[:, 0]
        scores.append(jnp.sum(kblk * q[None], axis=-1, keepdims=True) * ATT_SCALE - bias_ref[p])
    m = s_self
    for s in scores:
        m = jnp.maximum(m, jnp.max(s, axis=0))
    p_self = float(len(B_PATTERNS)) * jnp.exp(s_self - m)
    l = p_self
    o = p_self * vn
    for s, c_ref in zip(scores, blocks):
        p = jnp.exp(s - m[None])
        l = l + jnp.sum(p, axis=0)
        o = o + jnp.sum(p * c_ref[:, 1], axis=0)
    y_ref[...] = (o / l) * _silu(gate_ref[...])


def _attn_step(q, kn, vn, gate, cache6, bias):
    nb = q.shape[0]
    W = B_MAX_WINDOW
    hspec = pl.BlockSpec((None, B_HEADS, B_HEAD_DIM), lambda b: (b, 0, 0))
    tail = (2, B_HEADS, B_HEAD_DIM)
    views, specs = [], []
    for _, dil in B_PATTERNS:
        if dil == 1:
            views.append(cache6)
            specs.append(pl.BlockSpec((None, None, LANE) + tail,
                                      lambda b: (0, b, W // LANE - 1, 0, 0, 0)))
        else:
            views.append(cache6.reshape((1, nb, W // dil, dil) + tail))
            specs.append(pl.BlockSpec((None, None, LANE, None) + tail,
                                      lambda b, dil=dil: (0, b, W // dil // LANE - 1, 0, 0, 0, 0)))
    return pl.pallas_call(
        _attn_step_kernel,
        grid=(nb,),
        in_specs=[pl.BlockSpec(bias.shape, lambda b: (0, 0, 0, 0)), hspec, hspec, hspec, hspec]
                 + specs,
        out_specs=hspec,
        out_shape=jax.ShapeDtypeStruct((nb, B_HEADS, B_HEAD_DIM), F32),
        compiler_params=_cp("arbitrary"),
    )(bias, q, kn, vn, gate, *views)


def _cmul(ar, ai, br, bi):
    return ar * br - ai * bi, ar * bi + ai * br


def _blockdiag_rows(x):
    hi = pltpu.roll(x, C_STATE, 1)
    g = lax.broadcasted_iota(jnp.int32, (LANE, LANE), 0) // C_GROUP
    return jnp.concatenate([jnp.where(g == 2 * b, x, jnp.where(g == 2 * b + 1, hi, 0.0))
                            for b in range(S5_NS // LANE)], axis=1)


def _blockdiag_cols(x):
    piece = lambda g: x[g * C_STATE:(g + 1) * C_STATE, :]
    return jnp.concatenate([piece(0)] + [pltpu.roll(piece(g), C_GROUP * g, 1)
                                         for g in range(1, S5_BLK)], axis=0)


def _s5_prep_kernel(are_ref, aim_ref, ldt_ref, bre_ref, bim_ref, ctre_ref, ctim_ref,
                    t_ref, wb_ref, cb_ref, lam_ref, lamq_ref, cpl_ref):
    Q = S5_Q
    are, aim = are_ref[...], aim_ref[...]
    dt = jnp.exp(ldt_ref[...])
    mag = jnp.exp(dt * are)
    l_re = mag * jnp.cos(dt * aim)
    l_im = mag * jnp.sin(dt * aim)
    den = are * are + aim * aim
    f_re = ((l_re - 1.0) * are + l_im * aim) / den
    f_im = (l_im * are - (l_re - 1.0) * aim) / den
    lr, li = l_re[0:1, :], l_im[0:1, :]
    bb_re, bb_im = _cmul(f_re[0:1, :], f_im[0:1, :], _blockdiag_rows(bre_ref[...]),
                         _blockdiag_rows(bim_ref[...]))
    ctre, ctim = _blockdiag_cols(ctre_ref[...]), _blockdiag_cols(ctim_ref[...])
    cplain = jnp.concatenate([ctre, -ctim], axis=0).astype(BF16)
    cpl_ref[...] = cplain
    stk = jnp.concatenate([l_re, l_im, jnp.zeros((LANE - 16, S5_NS), F32)], axis=0).T
    lc_re, lc_im = stk[:, 0:1], stk[:, 8:9]
    p_re, p_im = jnp.ones((1, S5_NS), F32), jnp.zeros((1, S5_NS), F32)
    pc_re, pc_im = lc_re, lc_im
    dts = []
    for tau in range(Q):
        e_re, e_im = _cmul(bb_re, bb_im, p_re, p_im)
        wp = jnp.concatenate([e_re, e_im], axis=1).astype(BF16)
        s = Q - 1 - tau
        wb_ref[s * LANE:(s + 1) * LANE, :] = wp
        dts.append(jnp.dot(wp, cplain, preferred_element_type=F32).astype(BF16))
        c_re, c_im = _cmul(ctre, ctim, pc_re, pc_im)
        cb_ref[:, tau * LANE:(tau + 1) * LANE] = jnp.concatenate([c_re, -c_im], axis=0).astype(BF16)
        p_re, p_im = _cmul(p_re, p_im, lr, li)
        pc_re, pc_im = _cmul(pc_re, pc_im, lc_re, lc_im)
    zero = jnp.zeros((LANE, LANE), BF16)
    for s in range(Q):
        for q in range(Q):
            t_ref[s * LANE:(s + 1) * LANE, q * LANE:(q + 1) * LANE] = dts[q - s] if q >= s else zero
    lam_ref[...] = jnp.concatenate([l_re, l_im], axis=1)
    sub = lax.broadcasted_iota(jnp.int32, (8, 1), 0)
    q_re, q_im = p_re, p_im
    tab_re, tab_im = jnp.zeros((8, S5_NS), F32), jnp.zeros((8, S5_NS), F32)
    for r in range(8):
        tab_re = jnp.where(sub == r, q_re, tab_re)
        tab_im = jnp.where(sub == r, q_im, tab_im)
        q_re, q_im = _cmul(q_re, q_im, p_re, p_im)
    lamq_ref[...] = jnp.concatenate([tab_re, tab_im], axis=1)


def _s5_prep(are, aim, ldt, bre, bim, ctre, ctim):
    Q = S5_Q
    row = pl.BlockSpec((None, 8, S5_NS), lambda i: (i, 0, 0))
    blk = lambda a, b: pl.BlockSpec((None, a, b), lambda i: (i, 0, 0))
    return pl.pallas_call(
        _s5_prep_kernel,
        grid=(S5_NBLK,),
        in_specs=[row, row, row, blk(LANE, LANE), blk(LANE, LANE), blk(S5_NS, LANE), blk(S5_NS, LANE)],
        out_specs=[blk(Q * LANE, Q * LANE), blk(Q * LANE, 2 * S5_NS), blk(2 * S5_NS, Q * LANE),
                   blk(8, 2 * S5_NS), blk(8, 2 * S5_NS), blk(2 * S5_NS, LANE)],
        out_shape=[jax.ShapeDtypeStruct((S5_NBLK, Q * LANE, Q * LANE), BF16),
                   jax.ShapeDtypeStruct((S5_NBLK, Q * LANE, 2 * S5_NS), BF16),
                   jax.ShapeDtypeStruct((S5_NBLK, 2 * S5_NS, Q * LANE), BF16),
                   jax.ShapeDtypeStruct((S5_NBLK, 8, 2 * S5_NS), F32),
                   jax.ShapeDtypeStruct((S5_NBLK, 8, 2 * S5_NS), F32),
                   jax.ShapeDtypeStruct((S5_NBLK, 2 * S5_NS, LANE), BF16)],
        compiler_params=_cp("arbitrary"),
    )(are, aim, ldt, bre, bim, ctre, ctim)


def _s5_main_kernel(nb, u_ref, t_ref, wb_ref, cb_ref, lamq_ref, dsk_ref, g_ref, hfin_ref,
                    lhs_scr, s_scr, hp_scr, gs_scr):
    Q = S5_Q
    NS = S5_NS
    nrow = u_ref.shape[0] // Q
    nch = nrow // nb
    for s in range(Q):
        lhs_scr[:, s * LANE:(s + 1) * LANE] = u_ref[pl.ds(s, nrow, stride=Q), :].astype(BF16)
    lhs = lhs_scr[...]
    s_scr[...] = jnp.dot(lhs, wb_ref[...], preferred_element_type=F32)
    tab = lamq_ref[...]
    tab_re, tab_im = tab[:, :NS], tab[:, NS:]
    sub = lax.broadcasted_iota(jnp.int32, (8, 1), 0)

    def tile_body(t, carry):
        base = t * 8
        new = []
        for b in range(nb):
            h_re, h_im = carry[2 * b], carry[2 * b + 1]
            st = s_scr[pl.ds(b * nch + base, 8), :]
            x_re, x_im = st[:, :NS], st[:, NS:]
            for sh in (1, 2, 4):
                y_re = jnp.where(sub >= sh, pltpu.roll(x_re, sh, 0), 0.0)
                y_im = jnp.where(sub >= sh, pltpu.roll(x_im, sh, 0), 0.0)
                d_re, d_im = _cmul(tab_re[sh - 1:sh, :], tab_im[sh - 1:sh, :], y_re, y_im)
                x_re, x_im = x_re + d_re, x_im + d_im
            c_re, c_im = _cmul(tab_re, tab_im, h_re, h_im)
            x_re, x_im = x_re + c_re, x_im + c_im
            p_re = jnp.where(sub == 0, h_re, pltpu.roll(x_re, 1, 0))
            p_im = jnp.where(sub == 0, h_im, pltpu.roll(x_im, 1, 0))
            hp_scr[pl.ds(b * nch + base, 8), :] = jnp.concatenate([p_re, p_im], axis=1)
            new += [x_re[7:8, :], x_im[7:8, :]]
        return tuple(new)

    fin = (jnp.zeros((1, NS), F32),) * (2 * nb)
    for t in range(nch // 8):
        fin = tile_body(t, fin)
    for b in range(nb):
        hfin_ref[b:b + 1, :] = jnp.concatenate([fin[2 * b], fin[2 * b + 1]], axis=1)

    y = (jnp.dot(lhs, t_ref[...], preferred_element_type=F32)
         + jnp.dot(hp_scr[...].astype(BF16), cb_ref[...], preferred_element_type=F32))
    for q in range(Q):
        rows = pl.ds(q, nrow, stride=Q)
        yq = y[:, q * LANE:(q + 1) * LANE] + dsk_ref[...] * u_ref[rows, :]
        gs_scr[rows, :] = jax.nn.gelu(yq)
    g_ref[...] = gs_scr[...].astype(g_ref.dtype)


def _s5_main(u_gate, nb, t, wb, cb, lamq, dsk):
    Q = S5_Q
    M = u_gate.shape[0]
    nrow = M // Q
    blk = lambda a, b: pl.BlockSpec((None, a, b), lambda i: (i, 0, 0))
    return pl.pallas_call(
        functools.partial(_s5_main_kernel, nb),
        grid=(S5_NBLK,),
        in_specs=[pl.BlockSpec((M, LANE), lambda i: (0, i)),
                  blk(Q * LANE, Q * LANE), blk(Q * LANE, 2 * S5_NS), blk(2 * S5_NS, Q * LANE),
                  blk(8, 2 * S5_NS), pl.BlockSpec((1, LANE), lambda i: (0, i))],
        out_specs=[pl.BlockSpec((M, LANE), lambda i: (0, i)), blk(nb, 2 * S5_NS)],
        out_shape=[jax.ShapeDtypeStruct((M, E1), BF16),
                   jax.ShapeDtypeStruct((S5_NBLK, nb, 2 * S5_NS), F32)],
        scratch_shapes=[pltpu.VMEM((nrow, Q * LANE), BF16), pltpu.VMEM((nrow, 2 * S5_NS), F32),
                        pltpu.VMEM((nrow, 2 * S5_NS), F32), pltpu.VMEM((M, LANE), F32)],
        compiler_params=_cp("arbitrary"),
    )(u_gate, t, wb, cb, lamq, dsk)


def _s5_step_kernel(u_ref, h0_ref, bb_ref, cpl_ref, lam_ref, dsk_ref, g_ref, hn_ref):
    NS = S5_NS
    u = u_ref[...]
    s = jnp.dot(u.astype(BF16), bb_ref[...], preferred_element_type=F32)
    lam = lam_ref[0:1, :]
    h0 = h0_ref[...]
    n_re, n_im = _cmul(lam[:, :NS], lam[:, NS:], h0[:, :NS], h0[:, NS:])
    hn = jnp.concatenate([n_re + s[:, :NS], n_im + s[:, NS:]], axis=1)
    hn_ref[...] = hn
    y = jnp.dot(hn.astype(BF16), cpl_ref[...], preferred_element_type=F32) + dsk_ref[...] * u
    g_ref[...] = jax.nn.gelu(y)


def _s5_step(u_gate, h0, wb, cpl, lam, dsk):
    nb = u_gate.shape[0]
    blk = lambda a, b: pl.BlockSpec((None, a, b), lambda i: (i, 0, 0))
    return pl.pallas_call(
        _s5_step_kernel,
        grid=(S5_NBLK,),
        in_specs=[pl.BlockSpec((nb, LANE), lambda i: (0, i)), blk(nb, 2 * S5_NS),
                  pl.BlockSpec((None, LANE, 2 * S5_NS), lambda i: (i, S5_Q - 1, 0)),
                  blk(2 * S5_NS, LANE), blk(8, 2 * S5_NS), pl.BlockSpec((1, LANE), lambda i: (0, i))],
        out_specs=[pl.BlockSpec((nb, LANE), lambda i: (0, i)), blk(nb, 2 * S5_NS)],
        out_shape=[jax.ShapeDtypeStruct((nb, E1), F32),
                   jax.ShapeDtypeStruct((S5_NBLK, nb, 2 * S5_NS), F32)],
        compiler_params=_cp("arbitrary"),
    )(u_gate, h0, wb, cpl, lam, dsk)


def _state_to_blocks(st):
    nb = st.shape[0]
    x = st.reshape(nb, S5_NBLK, S5_NS, 2)
    return jnp.transpose(x, (1, 0, 3, 2)).reshape(S5_NBLK, nb, 2 * S5_NS)


def _blocks_to_state(h):
    nb = h.shape[1]
    x = h.reshape(S5_NBLK, nb, 2, S5_NS)
    return jnp.transpose(x, (1, 0, 3, 2)).reshape(nb, C_GROUPS, C_STATE, 2)


def kernel(x_prompt, x_sample, state_a_ssm, state_a_conv, cache_b_kv, state_c, norm_w, final_norm_w, even_w_in, a_conv_w, a_conv_b, a_dt_bias, a_A_log, a_D, a_norm_w, even_w_out, odd_w_in, c_A_re, c_A_im, c_B_re, c_B_im, c_C_re, c_C_im, c_D, c_log_dt, c_glu_w, c_glu_b, odd_w_out):
    B, L, D = x_prompt.shape
    nS = x_sample.shape[0]
    M = B * L
    Q = S5_Q

    nw0 = norm_w[0][None, :]
    nw1 = norm_w[1][None, :]
    fnw = final_norm_w[None, :]
    cw, cb = a_conv_w[0], a_conv_b[0][None, :]
    pad_h = lambda v: jnp.pad(v, (0, LANE - A_HEADS))[None, :]
    dtb, alog = pad_h(a_dt_bias[0]), pad_h(a_A_log[0])
    dsk_a = jnp.repeat(a_D[0], A_HEAD_DIM)[None, :]
    gnw = a_norm_w[0][None, :]
    head_expand = np.kron(np.eye(A_HEADS, dtype=np.float32), np.ones((1, A_HEAD_DIM), np.float32))
    head_expand = jnp.asarray(np.pad(head_expand, ((0, LANE - A_HEADS), (0, 0)))).astype(BF16)
    slopes_np = np.asarray(2.0 ** (-8.0 * np.arange(1, B_HEADS + 1) / B_HEADS), np.float32)
    slopes = jnp.asarray(np.broadcast_to(slopes_np[:, None, None], (B_HEADS, 8, LANE)))
    dist = np.stack([dil * (LANE - np.arange(LANE)) for _, dil in B_PATTERNS]).astype(np.float32)
    step_bias = jnp.asarray((dist[:, :, None] * slopes_np[None, None, :])[..., None])

    rows8 = lambda p: jnp.broadcast_to(p.reshape(S5_NBLK, 1, S5_NS), (S5_NBLK, 8, S5_NS))
    lane_pad = lambda p: jnp.pad(p, ((0, 0), (0, 0), (0, LANE - p.shape[-1])))
    b_rows = lambda p: lane_pad(jnp.transpose(p, (0, 2, 1)).reshape(S5_NBLK, LANE, C_STATE))
    c_rows = lambda p: lane_pad(jnp.transpose(p, (0, 2, 1)).reshape(S5_NBLK, S5_NS, C_GROUP))
    ldt = jnp.repeat(c_log_dt[0], C_STATE).reshape(S5_NBLK, S5_NS)
    t_m, wb_m, cb_m, lam, lamq, cpl = _s5_prep(
        rows8(c_A_re[0]), rows8(c_A_im[0]), rows8(ldt), b_rows(c_B_re[0]), b_rows(c_B_im[0]),
        c_rows(c_C_re[0]), c_rows(c_C_im[0]))
    w0t = jnp.transpose(even_w_in[0])
    s1 = A_WIDTH + A_CONV_DIM
    s2 = s1 + A_HEADS
    w_a = jnp.concatenate([w0t[A_WIDTH:s1], w0t[:A_WIDTH], w0t[s2 + 6144:], w0t[s1:s2],
                           jnp.zeros((A_TILE - A_HEADS, D), F32)], axis=0).astype(BF16)
    w_qkv = w0t[s2:s2 + 6144].astype(BF16)
    dsk_c = c_D[0][None, :]
    glu_b = c_glu_b[0][None, :]

    xs = x_sample.reshape(nS, D)
    a_s, xs_n = _mm_norm(xs, nw0, w_a, nS, 2048, keep_xn=True, wt=True)
    (qkv_s,) = _proj_qkv(xs_n, w_qkv, 1, nS, nS, window=False)
    col = lambda a, b: a_s[:, a:b]
    hv = lambda part: jnp.transpose(qkv_s[0, part * B_HEADS:(part + 1) * B_HEADS], (1, 0, 2))
    gate_s = col(5120, 7168).reshape(nS, B_HEADS, B_HEAD_DIM)
    kvn = jnp.stack([hv(1), hv(2)], axis=1)

    xp = x_prompt.reshape(M, D)
    a_arr, xp_n, w_glu = _mm_norm(xp, nw0, w_a, 1024, A_TILE, keep_xn=True, cast=c_glu_w[0],
                                  wt=True)
    qkv, kv_keep, w_in1 = _proj_qkv(xp_n, w_qkv, B, L, 1024, window=True, cast=odd_w_in[0])
    y_a, ssm_p, tail_p = _ssd(a_arr, B, L, cw, cb, dtb, alog, dsk_a, gnw)
    y_b, new_cache, w_out0, w_out1 = _attn(qkv, slopes, a_arr, cache_b_kv, kvn,
                                           even_w_out[0], odd_w_out[0])
    hp1 = _outproj0(y_a, y_b, w_out0, xp, 1024)
    u_gate = _mm_norm(hp1, nw1, w_in1, 1024, 1024)
    g, hfin_p = _s5_main(u_gate, B, t_m, wb_m, cb_m, lamq, dsk_c)
    y1 = _glu(g, w_glu, glu_b, u_gate, 1024, BF16)
    y_prompt = _outproj1(y1, w_out1, hp1, fnw, 1024).reshape(B, L, D)

    y_as, ssm_s, conv_s = _ssd_step(col(0, 3072)[:, None, :], state_a_conv[0],
                                    col(7168, 7296)[:, None, :], col(3072, 5120)[:, None, :],
                                    state_a_ssm[0].reshape(nS, A_WIDTH, A_STATE),
                                    cw, cb, dtb, alog, head_expand, dsk_a, gnw)
    y_bs = _attn_step(hv(0), hv(1), hv(2), gate_s, cache_b_kv, step_bias)
    hs1 = _outproj0(y_as.reshape(nS, A_WIDTH), y_bs.reshape(nS, A_WIDTH), w_out0, xs, nS, 1024)
    u_gate_s = _mm_norm(hs1, nw1, w_in1, nS, 2048)
    g_s, hnew_s = _s5_step(u_gate_s, _state_to_blocks(state_c[0]), wb_m, cpl, lam, dsk_c)
    y1_s = _glu(g_s, w_glu, glu_b, u_gate_s, nS, F32, 1024)
    y_sample = _outproj1(y1_s, w_out1, hs1, fnw, nS).reshape(nS, 1, D)

    return (y_prompt, y_sample,
            ssm_p.reshape(1, B, A_HEADS, A_HEAD_DIM, A_STATE),
            tail_p[:, 5:8, :][None],
            kv_keep[None],
            _blocks_to_state(hfin_p)[None],
            ssm_s.reshape(1, nS, A_HEADS, A_HEAD_DIM, A_STATE),
            conv_s[None],
            new_cache,
            _blocks_to_state(hnew_s)[None])
```

```python
import functools

import numpy as np
import jax
import jax.numpy as jnp
from jax import lax
from jax.experimental import pallas as pl
from jax.experimental.pallas import tpu as pltpu

F32 = jnp.float32
BF16 = jnp.bfloat16

D_MODEL = 2048
EPS = 1e-5
A_WIDTH = 2048
A_HEAD_DIM = 64
A_HEADS = 32
A_GROUPS = 4
A_STATE = 128
A_CONV_W = 4
A_CONV_DIM = 3072
SSD_CHUNK = 128
B_HEADS = 16
B_HEAD_DIM = 128
B_PATTERNS = ((128, 1), (512, 4), (2048, 16))
B_MAX_WINDOW = 2048
C_GROUP = 16
C_GROUPS = 256
C_STATE = 64
E1 = 4096

LANE = 128
S5_Q = 8
S5_BLK = 8
S5_NBLK = C_GROUPS // S5_BLK
S5_NS = S5_BLK * C_STATE
TN = 512
A_TILE = 1024
ATT_BLOCK = 2048
NEG = -1e30
ATT_SCALE = B_HEAD_DIM ** -0.5
LOG2E = 1.4426950408889634
VMEM_LIMIT = 56 * 1024 * 1024

NT_DIMS = (((1,), (1,)), ((), ()))


def _cp(*sem):
    return pltpu.CompilerParams(dimension_semantics=sem, vmem_limit_bytes=VMEM_LIMIT)


def _silu(x):
    return x * jax.nn.sigmoid(x)


def _softplus(x):
    return jnp.maximum(x, 0.0) + jnp.log1p(jnp.exp(-jnp.abs(x)))


def _split3(a):
    hi = a.astype(BF16)
    r = a - hi.astype(F32)
    mid = r.astype(BF16)
    lo = (r - mid.astype(F32)).astype(BF16)
    return hi, mid, lo


def _dot3_left(m, a):
    hi, mid, lo = _split3(a)
    d = lambda p: jnp.dot(m, p, preferred_element_type=F32)
    return (d(lo) + d(mid)) + d(hi)


def _dot3_right(a, m):
    hi, mid, lo = _split3(a)
    d = lambda p: jnp.dot(p, m, preferred_element_type=F32)
    return (d(lo) + d(mid)) + d(hi)


def _rms(x, w):
    ms = jnp.mean(x * x, axis=-1, keepdims=True)
    return (x * lax.rsqrt(ms + EPS)) * w


QKV_HEADS = 8
CAST_ROWS = 64


def _cast_specs(w, nsteps, step_of):
    rows, cols = w.shape
    per = CAST_ROWS
    while rows // per > nsteps:
        per *= 2
    nblk = rows // per
    assert nblk * per == rows, (w.shape, nsteps)
    spec = pl.BlockSpec((per, cols), lambda *ids: (jnp.minimum(step_of(*ids), nblk - 1), 0))
    return spec, jax.ShapeDtypeStruct(w.shape, BF16)


def _proj_qkv_kernel(rpb, nskip, x_ref, w_ref, src_ref, xs_ref, qkv_ref, kv_ref, dst_ref, qkvs_ref):
    i, j = pl.program_id(0), pl.program_id(1)
    dst_ref[...] = src_ref[...].astype(BF16)

    def heads(x, out_ref, kv_heads=None):
        acc = _dot_w(x.astype(BF16), w_ref[...], True)
        for hh in range(QKV_HEADS):
            out_ref[hh] = acc[:, hh * LANE:(hh + 1) * LANE]
            if kv_heads is not None:
                kv_ref[:, kv_heads + hh, :] = acc[:, hh * LANE:(hh + 1) * LANE]

    pl.when(i == 0)(lambda: heads(xs_ref[...], qkvs_ref))
    tiles_per = B_HEADS // QKV_HEADS
    in_window = (i % rpb >= nskip) & (j >= tiles_per)
    pl.when(jnp.logical_not(in_window))(lambda: heads(x_ref[...], qkv_ref))
    for part in range(tiles_per):
        pl.when(in_window & (j % tiles_per == part))(
            lambda part=part: heads(x_ref[...], qkv_ref, part * QKV_HEADS))


def _proj_qkv(xn, w, B, L, tm, cast, xs_n):
    M = B * L
    ms = xs_n.shape[0]
    tn = QKV_HEADS * LANE
    rpb = L // tm
    keep = min(B_MAX_WINDOW, L)
    nskip = (L - keep) // tm
    tiles_per = B_HEADS // QKV_HEADS
    nj = w.shape[0] // tn
    rcol = _rider_col(nj)

    def kv_map(i, j):
        ri = i % rpb
        return (i // rpb, jnp.maximum(ri - nskip, 0),
                jnp.where((ri < nskip) | (j < 2 * tiles_per), 0, 1), 0, 0)

    cast_spec, cast_shape = _cast_specs(cast, (M // tm) * nj, lambda i, j: i * nj + j)
    return pl.pallas_call(
        functools.partial(_proj_qkv_kernel, rpb, nskip),
        grid=(M // tm, nj),
        in_specs=[pl.BlockSpec((tm, D_MODEL), lambda i, j: (i, 0)),
                  pl.BlockSpec((tn, D_MODEL), lambda i, j: (j, 0)),
                  cast_spec,
                  pl.BlockSpec((ms, D_MODEL), lambda i, j: (0, 0))],
        out_specs=[pl.BlockSpec((None, QKV_HEADS, tm, LANE), lambda i, j: (i // rpb, j, i % rpb, 0)),
                   pl.BlockSpec((None, tm, None, B_HEADS, B_HEAD_DIM), kv_map),
                   cast_spec,
                   pl.BlockSpec((QKV_HEADS, ms, LANE), lambda i, j: (rcol(i, j), 0, 0))],
        out_shape=[jax.ShapeDtypeStruct((B, 3 * B_HEADS, L, B_HEAD_DIM), F32),
                   jax.ShapeDtypeStruct((B, keep, 2, B_HEADS, B_HEAD_DIM), F32),
                   cast_shape,
                   jax.ShapeDtypeStruct((3 * B_HEADS, ms, B_HEAD_DIM), F32)],
        compiler_params=_cp("arbitrary", "arbitrary"),
    )(xn, w, cast, xs_n)


def _dot_w(x, w, wt):
    if wt:
        return lax.dot_general(x, w, NT_DIMS, preferred_element_type=F32)
    return jnp.dot(x, w, preferred_element_type=F32)


def _rider_col(nj):
    return lambda i, j: jnp.where(i == 0, j, nj - 1)


def _mm_norm_kernel(names, wt, *refs):
    r = dict(zip(names, refs))
    j = pl.program_id(1)
    if "cast_src" in r:
        r["cast_out"][...] = r["cast_src"][...].astype(BF16)

    def rows(x_ref, xn_ref, o_ref):
        @pl.when(j == 0)
        def _():
            xn_ref[...] = _rms(x_ref[...], r["nw"][...]).astype(xn_ref.dtype)

        o_ref[...] = _dot_w(xn_ref[...].astype(BF16), r["w"][...], wt)

    rows(r["x"], r["xn"], r["o"])
    if "xs" in r:
        pl.when(pl.program_id(0) == 0)(lambda: rows(r["xs"], r["xs_n"], r["os"]))


def _mm_norm(x, nw, w, tm, tn=TN, keep_xn=False, cast=None, wt=False, rider=None):
    M = x.shape[0]
    N, K = w.shape if wt else w.shape[::-1]
    nj = N // tn
    ins = {"x": (x, pl.BlockSpec((tm, K), lambda i, j: (i, 0))),
           "nw": (nw, pl.BlockSpec((1, K), lambda i, j: (0, 0))),
           "w": (w, pl.BlockSpec((tn, K), lambda i, j: (j, 0)) if wt else
                 pl.BlockSpec((K, tn), lambda i, j: (0, j)))}
    outs = {"o": (jax.ShapeDtypeStruct((M, N), F32), pl.BlockSpec((tm, tn), lambda i, j: (i, j)))}
    scratch = {}
    xn_spec = pl.BlockSpec((tm, K), lambda i, j: (i, 0))
    if keep_xn:
        outs["xn"] = (jax.ShapeDtypeStruct((M, K), BF16), xn_spec)
    else:
        scratch["xn"] = pltpu.VMEM((tm, K), BF16)
    if cast is not None:
        spec, shape = _cast_specs(cast, (M // tm) * nj, lambda i, j: i * nj + j)
        ins["cast_src"] = (cast, spec)
        outs["cast_out"] = (shape, spec)
    if rider is not None:
        ms = rider.shape[0]
        rcol = _rider_col(nj)
        ins["xs"] = (rider, pl.BlockSpec((ms, K), lambda i, j: (0, 0)))
        outs["os"] = (jax.ShapeDtypeStruct((ms, N), F32),
                      pl.BlockSpec((ms, tn), lambda i, j: (0, rcol(i, j))))
        if keep_xn:
            outs["xs_n"] = (jax.ShapeDtypeStruct((ms, K), F32), pl.BlockSpec((ms, K), lambda i, j: (0, 0)))
        else:
            scratch["xs_n"] = pltpu.VMEM((ms, K), F32)
    out = pl.pallas_call(
        functools.partial(_mm_norm_kernel, tuple(ins) + tuple(outs) + tuple(scratch), wt),
        grid=(M // tm, nj),
        in_specs=[s for _, s in ins.values()],
        out_specs=[s for _, s in outs.values()],
        out_shape=[a for a, _ in outs.values()],
        scratch_shapes=list(scratch.values()),
        compiler_params=_cp("arbitrary", "arbitrary"),
    )(*[a for a, _ in ins.values()])
    return dict(zip(outs, out))


def _outproj0_kernel(ya_ref, yb_ref, wa_ref, wb_ref, res_ref, yas_ref, ybs_ref, ress_ref,
                     o_ref, os_ref):
    def rows(ya, yb, res, o):
        o[...] = (res[...] + jnp.dot(ya[...].astype(BF16), wa_ref[...], preferred_element_type=F32)
                  + jnp.dot(yb[...].astype(BF16), wb_ref[...], preferred_element_type=F32))

    rows(ya_ref, yb_ref, res_ref, o_ref)
    pl.when(pl.program_id(0) == 0)(lambda: rows(yas_ref, ybs_ref, ress_ref, os_ref))


def _outproj0(ya, yb, w, res, tm, rider):
    M, K = ya.shape
    ms = rider[0].shape[0]
    nj = D_MODEL // TN
    rcol = _rider_col(nj)
    row = pl.BlockSpec((tm, K), lambda i, j: (i, 0))
    srow = pl.BlockSpec((ms, K), lambda i, j: (0, 0))
    stile = pl.BlockSpec((ms, TN), lambda i, j: (0, rcol(i, j)))
    return pl.pallas_call(
        _outproj0_kernel,
        grid=(M // tm, nj),
        in_specs=[row, row,
                  pl.BlockSpec((K, TN), lambda i, j: (0, j)),
                  pl.BlockSpec((K, TN), lambda i, j: (1, j)),
                  pl.BlockSpec((tm, TN), lambda i, j: (i, j)), srow, srow, stile],
        out_specs=[pl.BlockSpec((tm, TN), lambda i, j: (i, j)), stile],
        out_shape=[jax.ShapeDtypeStruct((M, D_MODEL), F32), jax.ShapeDtypeStruct((ms, D_MODEL), F32)],
        compiler_params=_cp("arbitrary", "arbitrary"),
    )(ya, yb, w, w, res, *rider)


def _glu_kernel(g_ref, wa_ref, wb_ref, ba_ref, bb_ref, gate_ref, gs_ref, gates_ref, o_ref, os_ref):
    def rows(g_ref, gate_ref, o_ref):
        tm = g_ref.shape[0]
        hr = min(tm, 512)
        for r in range(tm // hr):
            rs = pl.ds(r * hr, hr)
            g = g_ref[rs, :].astype(BF16)
            ga = jnp.dot(g, wa_ref[...], preferred_element_type=F32) + ba_ref[...]
            gb = jnp.dot(g, wb_ref[...], preferred_element_type=F32) + bb_ref[...]
            o_ref[rs, :] = (ga * jax.nn.sigmoid(gb) * _silu(gate_ref[rs, :])).astype(o_ref.dtype)

    rows(g_ref, gate_ref, o_ref)
    pl.when(pl.program_id(0) == 0)(lambda: rows(gs_ref, gates_ref, os_ref))


def _glu(g, w, bias, u_gate, tm, rider):
    M = g.shape[0]
    gs, ugs = rider
    ms = gs.shape[0]
    nb = E1 // TN
    rcol = _rider_col(nb)
    return pl.pallas_call(
        _glu_kernel,
        grid=(M // tm, nb),
        in_specs=[pl.BlockSpec((tm, E1), lambda i, j: (i, 0)),
                  pl.BlockSpec((E1, TN), lambda i, j: (0, j)),
                  pl.BlockSpec((E1, TN), lambda i, j: (0, j + nb)),
                  pl.BlockSpec((1, TN), lambda i, j: (0, j)),
                  pl.BlockSpec((1, TN), lambda i, j: (0, j + nb)),
                  pl.BlockSpec((tm, TN), lambda i, j: (i, j + nb)),
                  pl.BlockSpec((ms, E1), lambda i, j: (0, 0)),
                  pl.BlockSpec((ms, TN), lambda i, j: (0, nb + rcol(i, j)))],
        out_specs=[pl.BlockSpec((tm, TN), lambda i, j: (i, j)),
                   pl.BlockSpec((ms, TN), lambda i, j: (0, rcol(i, j)))],
        out_shape=[jax.ShapeDtypeStruct((M, E1), BF16), jax.ShapeDtypeStruct((ms, E1), F32)],
        compiler_params=_cp("arbitrary", "arbitrary"),
    )(g, w, w, bias, bias, u_gate, gs, ugs)


def _outproj1_kernel(y_ref, w_ref, res_ref, fnw_ref, ys_ref, ress_ref, o_ref, os_ref):
    j = pl.program_id(1)
    nj = D_MODEL // TN

    def rows(y_ref, res_ref, o_ref):
        h = res_ref[...] + jnp.dot(y_ref[...].astype(BF16), w_ref[...], preferred_element_type=F32)
        for jj in range(nj):
            @pl.when(j == jj)
            def _(jj=jj):
                o_ref[:, jj * TN:(jj + 1) * TN] = h

        @pl.when(j == nj - 1)
        def _():
            o_ref[...] = _rms(o_ref[...], fnw_ref[...])

    rows(y_ref, res_ref, o_ref)
    pl.when(pl.program_id(0) == 0)(lambda: rows(ys_ref, ress_ref, os_ref))


def _outproj1(y, w, res, fnw, tm, rider):
    M, K = y.shape
    ys, ress = rider
    ms = ys.shape[0]
    nj = D_MODEL // TN
    rcol = _rider_col(nj)
    return pl.pallas_call(
        _outproj1_kernel,
        grid=(M // tm, nj),
        in_specs=[pl.BlockSpec((tm, K), lambda i, j: (i, 0)),
                  pl.BlockSpec((K, TN), lambda i, j: (0, j)),
                  pl.BlockSpec((tm, TN), lambda i, j: (i, j)),
                  pl.BlockSpec((1, D_MODEL), lambda i, j: (0, 0)),
                  pl.BlockSpec((ms, K), lambda i, j: (0, 0)),
                  pl.BlockSpec((ms, TN), lambda i, j: (0, rcol(i, j)))],
        out_specs=[pl.BlockSpec((tm, D_MODEL), lambda i, j: (i, 0)),
                   pl.BlockSpec((ms, D_MODEL), lambda i, j: (0, 0))],
        out_shape=[jax.ShapeDtypeStruct((M, D_MODEL), F32), jax.ShapeDtypeStruct((ms, D_MODEL), F32)],
        compiler_params=_cp("arbitrary", "arbitrary"),
    )(y, w, res, fnw, ys, ress)


def _ssd_kernel(xbc_ref, dt_ref, z0_ref, z1_ref, cw_ref, cb_ref, dtb_ref, alog_ref, dsk_ref, gnw_ref,
                y_ref, st_ref, tail_ref, h_scr, ext_scr, y_scr):
    c = pl.program_id(1)
    Q = SSD_CHUNK

    @pl.when(c == 0)
    def _():
        h_scr[...] = jnp.zeros_like(h_scr)
        ext_scr[0:8, :] = jnp.zeros((8, A_CONV_DIM), F32)

    x_raw = xbc_ref[...]
    ext_scr[8:8 + Q, :] = x_raw
    conv = cb_ref[...] + cw_ref[3:4, :] * x_raw
    for k in (1, 2, 3):
        conv = conv + cw_ref[3 - k:4 - k, :] * ext_scr[pl.ds(8 - k, Q), :]
    ext_scr[0:8, :] = x_raw[Q - 8:Q, :]
    xc = _silu(conv)

    dt = _softplus(dt_ref[...] + dtb_ref[...])
    a = dt * (-jnp.exp(alog_ref[...]))
    qi = lax.broadcasted_iota(jnp.int32, (Q, Q), 0)
    si = lax.broadcasted_iota(jnp.int32, (Q, Q), 1)
    tri = qi >= si
    a_cum = _dot3_left(tri.astype(BF16), a)
    a_cum_t = a_cum.T
    lo_half = si < A_HEAD_DIM

    for g in range(A_GROUPS):
        bg = xc[:, A_WIDTH + g * A_STATE:A_WIDTH + (g + 1) * A_STATE].astype(BF16)
        cg = xc[:, A_WIDTH + (A_GROUPS + g) * A_STATE:A_WIDTH + (A_GROUPS + g + 1) * A_STATE].astype(BF16)
        cbm = lax.dot_general(cg, bg, NT_DIMS, preferred_element_type=F32)
        hprev = h_scr[g * 512:(g + 1) * 512, :]
        yoff = lax.dot_general(cg, hprev.astype(BF16), NT_DIMS, preferred_element_type=F32)
        for jj in range(4):
            p = g * 4 + jj
            h0, h1 = 2 * p, 2 * p + 1
            dtp = jnp.where(lo_half, dt[:, h0:h0 + 1], dt[:, h1:h1 + 1])
            ap = jnp.where(lo_half, a_cum[:, h0:h0 + 1], a_cum[:, h1:h1 + 1])
            x_p = xc[:, p * LANE:(p + 1) * LANE]
            xdt = x_p * dtp
            g0 = cbm * jnp.exp(jnp.where(tri, a_cum[:, h0:h0 + 1] - a_cum_t[h0:h0 + 1, :], NEG))
            g1 = cbm * jnp.exp(jnp.where(tri, a_cum[:, h1:h1 + 1] - a_cum_t[h1:h1 + 1, :], NEG))
            lhs = jnp.concatenate([g0, g1], axis=1).astype(BF16)
            rhs = jnp.concatenate([jnp.where(lo_half, xdt, 0.0), jnp.where(lo_half, 0.0, xdt)],
                                  axis=0).astype(BF16)
            ydiag = jnp.dot(lhs, rhs, preferred_element_type=F32)
            y_scr[:, p * LANE:(p + 1) * LANE] = (
                ydiag + yoff[:, jj * LANE:(jj + 1) * LANE] * jnp.exp(ap)
                + dsk_ref[:, p * LANE:(p + 1) * LANE] * x_p)
            xd = xdt * jnp.exp(ap[Q - 1:Q, :] - ap)
            st = jnp.dot(xd.T.astype(BF16), bg, preferred_element_type=F32)
            cd = jnp.exp(jnp.where(qi < A_HEAD_DIM, a_cum_t[h0:h0 + 1, Q - 1:Q],
                                   a_cum_t[h1:h1 + 1, Q - 1:Q]))
            h_scr[p * LANE:(p + 1) * LANE, :] = cd * hprev[jj * LANE:(jj + 1) * LANE, :] + st

    z = jnp.concatenate([z0_ref[...], z1_ref[...]], axis=1)
    y_ref[...] = _rms(y_scr[...] * _silu(z), gnw_ref[...]).astype(y_ref.dtype)

    @pl.when(c == pl.num_programs(1) - 1)
    def _():
        st_ref[...] = h_scr[...]
        tail_ref[...] = x_raw[Q - 8:Q, :]


def _ssd(a_arr, B, L, cw, cb, dtb, alog, dsk, gnw):
    nc = L // SSD_CHUNK
    Q = SSD_CHUNK
    row = lambda b, c: b * nc + c
    full = lambda shp: pl.BlockSpec(shp, lambda b, c: (0, 0))
    return pl.pallas_call(
        _ssd_kernel,
        grid=(B, nc),
        in_specs=[pl.BlockSpec((Q, A_CONV_DIM), lambda b, c: (row(b, c), 0)),
                  pl.BlockSpec((Q, LANE), lambda b, c: (row(b, c), 14 * TN // LANE)),
                  pl.BlockSpec((Q, A_WIDTH // 2), lambda b, c: (row(b, c), 3)),
                  pl.BlockSpec((Q, A_WIDTH // 2), lambda b, c: (row(b, c), 4)),
                  full((A_CONV_W, A_CONV_DIM)), full((1, A_CONV_DIM)), full((1, LANE)),
                  full((1, LANE)), full((1, A_WIDTH)), full((1, A_WIDTH))],
        out_specs=[pl.BlockSpec((Q, A_WIDTH), lambda b, c: (row(b, c), 0)),
                   pl.BlockSpec((None, A_WIDTH, A_STATE), lambda b, c: (b, 0, 0)),
                   pl.BlockSpec((None, 8, A_CONV_DIM), lambda b, c: (b, 0, 0))],
        out_shape=[jax.ShapeDtypeStruct((B * L, A_WIDTH), BF16),
                   jax.ShapeDtypeStruct((B, A_WIDTH, A_STATE), F32),
                   jax.ShapeDtypeStruct((B, 8, A_CONV_DIM), F32)],
        scratch_shapes=[pltpu.VMEM((A_WIDTH, A_STATE), F32), pltpu.VMEM((8 + Q, A_CONV_DIM), F32),
                        pltpu.VMEM((Q, A_WIDTH), F32)],
        compiler_params=_cp("arbitrary", "arbitrary"),
    )(a_arr, a_arr, a_arr, a_arr, cw, cb, dtb, alog, dsk, gnw)


def _ssd_step_kernel(xbc_ref, cs_ref, dt_ref, z_ref, h0_ref, cw_ref, cb_ref, dtb_ref, alog_ref,
                     e_ref, dsk_ref, gnw_ref, y_ref, hn_ref, nc_ref):
    x_raw = xbc_ref[...]
    cs = cs_ref[...]
    conv = (cb_ref[...] + cw_ref[0:1, :] * cs[0:1, :] + cw_ref[1:2, :] * cs[1:2, :]
            + cw_ref[2:3, :] * cs[2:3, :] + cw_ref[3:4, :] * x_raw)
    xc = _silu(conv)
    nc_ref[0:1, :] = cs[1:2, :]
    nc_ref[1:2, :] = cs[2:3, :]
    nc_ref[2:3, :] = x_raw

    dt = _softplus(dt_ref[...] + dtb_ref[...])
    da = dt * (-jnp.exp(alog_ref[...]))
    e = e_ref[...]
    dt_e = _dot3_right(jnp.broadcast_to(dt, (8, LANE)), e)[0:1, :]
    da_e = _dot3_right(jnp.broadcast_to(da, (8, LANE)), e)[0:1, :]
    xs = xc[:, :A_WIDTH]
    sub = lax.broadcasted_iota(jnp.int32, (8, 1), 0)
    rows = jnp.where(sub == 0, xs * dt_e, jnp.where(sub == 1, jnp.exp(da_e), 0.0))
    cols = jnp.concatenate([rows, jnp.zeros((LANE - 8, A_WIDTH), F32)], axis=0).T
    xdt_col = cols[:, 0:1]
    dec_col = cols[:, 1:2]
    ys = []
    for g in range(A_GROUPS):
        bg = xc[:, A_WIDTH + g * A_STATE:A_WIDTH + (g + 1) * A_STATE]
        cg = xc[:, A_WIDTH + (A_GROUPS + g) * A_STATE:A_WIDTH + (A_GROUPS + g + 1) * A_STATE]
        sl = slice(g * 512, (g + 1) * 512)
        hn = dec_col[sl, :] * h0_ref[sl, :] + xdt_col[sl, :] * bg
        hn_ref[sl, :] = hn
        c8 = jnp.broadcast_to(cg, (8, A_STATE)).astype(BF16)
        ys.append(lax.dot_general(c8, hn.astype(BF16), NT_DIMS, preferred_element_type=F32)[0:1, :])
    y = jnp.concatenate(ys, axis=1) + dsk_ref[...] * xs
    y_ref[...] = _rms(y * _silu(z_ref[...]), gnw_ref[...])


def _ssd_step(xbc, cs, dt, z, h0, cw, cb, dtb, alog, e, dsk, gnw):
    nb = xbc.shape[0]
    per_b = lambda n: pl.BlockSpec((None, 1, n), lambda b: (b, 0, 0))
    full = lambda shp: pl.BlockSpec(shp, lambda b: (0, 0))
    return pl.pallas_call(
        _ssd_step_kernel,
        grid=(nb,),
        in_specs=[per_b(A_CONV_DIM), pl.BlockSpec((None, 3, A_CONV_DIM), lambda b: (b, 0, 0)),
                  per_b(LANE), per_b(A_WIDTH),
                  pl.BlockSpec((None, A_WIDTH, A_STATE), lambda b: (b, 0, 0)),
                  full((A_CONV_W, A_CONV_DIM)), full((1, A_CONV_DIM)), full((1, LANE)),
                  full((1, LANE)), full((LANE, A_WIDTH)), full((1, A_WIDTH)), full((1, A_WIDTH))],
        out_specs=[per_b(A_WIDTH),
                   pl.BlockSpec((None, A_WIDTH, A_STATE), lambda b: (b, 0, 0)),
                   pl.BlockSpec((None, 3, A_CONV_DIM), lambda b: (b, 0, 0))],
        out_shape=[jax.ShapeDtypeStruct((nb, 1, A_WIDTH), F32),
                   jax.ShapeDtypeStruct((nb, A_WIDTH, A_STATE), F32),
                   jax.ShapeDtypeStruct((nb, 3, A_CONV_DIM), F32)],
        compiler_params=_cp("arbitrary"),
    )(xbc, cs, dt, z, h0, cw, cb, dtb, alog, e, dsk, gnw)


def _bias_tile(slope, dil):
    i = lax.broadcasted_iota(jnp.int32, (LANE, 2 * LANE), 0)
    c = lax.broadcasted_iota(jnp.int32, (LANE, 2 * LANE), 1)
    j = LANE + i - c
    valid = (j >= 0) & (j <= LANE)
    return jnp.where(valid, -(slope * (dil * j).astype(F32)), NEG), c


def _attn_tile(q, kcat, vcat, bias2):
    s = lax.dot_general(q, kcat, NT_DIMS, preferred_element_type=F32) * (ATT_SCALE * LOG2E) + bias2
    m = jnp.max(s, axis=1, keepdims=True)
    p = jnp.exp2(s - m)
    l = jnp.sum(p, axis=1, keepdims=True)
    acc = jnp.dot(p.astype(BF16), vcat, preferred_element_type=F32)
    return m, l, acc


def _rows(start, stride):
    return pl.ds(start, LANE, stride=stride) if stride > 1 else pl.ds(start, LANE)


def _shift_cache_block(step, steps_per_seq, x_ref, nxt_ref, kvn_ref, o_ref):
    R = x_ref.shape[0]
    o_ref[pl.ds(0, R - 1)] = x_ref[pl.ds(1, R - 1)]
    last = step % steps_per_seq == steps_per_seq - 1
    o_ref[R - 1] = jnp.where(last, kvn_ref[...], nxt_ref[0])


def _attn_kernel(steps_per_seq, slope_ref, q_ref, k_ref, v_ref, kp_ref, vp_ref, gate_ref,
                 x_ref, nxt_ref, kvn_ref, wa_ref, wb_ref, y_ref, oc_ref, wa_out, wb_out, *scr):
    step = ((pl.program_id(0) * pl.num_programs(1) + pl.program_id(1)) * pl.num_programs(2)
            + pl.program_id(2))
    _shift_cache_block(step, steps_per_seq, x_ref, nxt_ref, kvn_ref, oc_ref)
    wa_out[...] = wa_ref[...].astype(BF16)
    wb_out[...] = wb_ref[...].astype(BF16)
    first_block = pl.program_id(2) == 0
    slope = slope_ref[0:1, 0:1]
    npat = len(B_PATTERNS)
    m_scr, l_scr, acc_scr = scr[:npat], scr[npat:2 * npat], scr[2 * npat:]
    for pat, (_, dil) in enumerate(B_PATTERNS):
        bias_full, c = _bias_tile(slope, dil)
        bias_full = jnp.where(bias_full > NEG, bias_full * LOG2E, NEG)
        bias0 = jnp.where((c >= LANE) | jnp.logical_not(first_block), bias_full, NEG)
        nsub = ATT_BLOCK // dil // LANE
        for r in range(dil):
            kc = vc = None
            for sub in range(nsub):
                rows = _rows(sub * LANE * dil + r, dil)
                if sub == 0:
                    prow = _rows((nsub - 1) * LANE * dil + r, dil)
                    kp, vp, bias = kp_ref[prow, :], vp_ref[prow, :], bias0
                else:
                    kp, vp, bias = kc, vc, bias_full
                kc, vc = k_ref[rows, :], v_ref[rows, :]
                kcat = jnp.concatenate([kp, kc], axis=0).astype(BF16)
                vcat = jnp.concatenate([vp, vc], axis=0).astype(BF16)
                m, l, acc = _attn_tile(q_ref[rows, :].astype(BF16), kcat, vcat, bias)
                m_scr[pat][rows, :] = jnp.broadcast_to(m, (LANE, LANE))
                l_scr[pat][rows, :] = jnp.broadcast_to(l, (LANE, LANE))
                acc_scr[pat][rows, :] = acc
    for sub in range(ATT_BLOCK // LANE):
        rs = pl.ds(sub * LANE, LANE)
        ms = [m_scr[p][rs, :] for p in range(npat)]
        m = functools.reduce(jnp.maximum, ms)
        l = acc = 0.0
        for p in range(npat):
            a = jnp.exp2(ms[p] - m)
            l = l + a * l_scr[p][rs, :]
            acc = acc + a * acc_scr[p][rs, :]
        y_ref[rs, :] = ((acc / l) * _silu(gate_ref[rs, :])).astype(y_ref.dtype)


def _attn(qkv, slopes, a_arr, cache6, kvn, wa, wb):
    B, H3, L, dh = qkv.shape
    H = H3 // 3
    R = ATT_BLOCK
    nblk = L // R
    cur = lambda part: pl.BlockSpec((None, None, R, dh), lambda b, h, i: (b, part * H + h, i, 0))
    prev = lambda part: pl.BlockSpec((None, None, R, dh),
                                     lambda b, h, i: (b, part * H + h, jnp.maximum(i - 1, 0), 0))
    slope = pl.BlockSpec((None, 8, LANE), lambda b, h, i: (h, 0, 0))
    gate = pl.BlockSpec((R, LANE), lambda b, h, i: (b * nblk + i, 10 * TN // LANE + h))

    nb, W = cache6.shape[1], cache6.shape[2]
    steps = B * H * nblk
    per = steps // nb
    rows = W // per
    assert per * nb == steps and rows * per == W, (steps, nb, W)
    tail = (2, B_HEADS, B_HEAD_DIM)
    step = lambda b, h, i: (b * H + h) * nblk + i
    cblk = pl.BlockSpec((None, None, rows) + tail,
                        lambda b, h, i: (0, step(b, h, i) // per, step(b, h, i) % per, 0, 0, 0))
    cnext = pl.BlockSpec((None, None, 1) + tail,
                         lambda b, h, i: (0, step(b, h, i) // per,
                                          jnp.minimum((step(b, h, i) % per + 1) * rows, W - 1), 0, 0, 0))
    cnew = pl.BlockSpec((None,) + tail, lambda b, h, i: (step(b, h, i) // per, 0, 0, 0))
    wa_spec, wa_shape = _cast_specs(wa, steps, step)
    wb_spec, wb_shape = _cast_specs(wb, steps, step)
    return pl.pallas_call(
        functools.partial(_attn_kernel, per),
        grid=(B, H, nblk),
        in_specs=[slope, cur(0), cur(1), cur(2), prev(1), prev(2), gate, cblk, cnext, cnew,
                  wa_spec, wb_spec],
        out_specs=[pl.BlockSpec((R, LANE), lambda b, h, i: (b * nblk + i, h)), cblk,
                   wa_spec, wb_spec],
        out_shape=[jax.ShapeDtypeStruct((B * L, H * dh), BF16),
                   jax.ShapeDtypeStruct(cache6.shape, F32), wa_shape, wb_shape],
        scratch_shapes=[pltpu.VMEM((R, LANE), F32)] * (3 * len(B_PATTERNS)),
        compiler_params=_cp("arbitrary", "arbitrary", "arbitrary"),
    )(slopes, qkv, qkv, qkv, qkv, qkv, a_arr, cache6, cache6, kvn, wa, wb)


def _attn_step_kernel(bias_ref, q_ref, kn_ref, vn_ref, gate_ref, c1_ref, c4_ref, c16_ref, y_ref):
    q, kn, vn = q_ref[...], kn_ref[...], vn_ref[...]
    s_self = jnp.sum(q * kn, axis=-1, keepdims=True) * ATT_SCALE
    blocks = (c1_ref, c4_ref, c16_ref)
    scores = []
    for p, c_ref in enumerate(blocks):
        kblk = c_ref[:, 0]
        scores.append(jnp.sum(kblk * q[None], axis=-1, keepdims=True) * ATT_SCALE - bias_ref[p])
    m = s_self
    for s in scores:
        m = jnp.maximum(m, jnp.max(s, axis=0))
    p_self = float(len(B_PATTERNS)) * jnp.exp(s_self - m)
    l = p_self
    o = p_self * vn
    for s, c_ref in zip(scores, blocks):
        p = jnp.exp(s - m[None])
        l = l + jnp.sum(p, axis=0)
        o = o + jnp.sum(p * c_ref[:, 1], axis=0)
    y_ref[...] = (o / l) * _silu(gate_ref[...])


def _attn_step(q, kn, vn, gate, cache6, bias):
    nb = q.shape[0]
    W = B_MAX_WINDOW
    hspec = pl.BlockSpec((None, B_HEADS, B_HEAD_DIM), lambda b: (b, 0, 0))
    tail = (2, B_HEADS, B_HEAD_DIM)
    views, specs = [], []
    for _, dil in B_PATTERNS:
        if dil == 1:
            views.append(cache6)
            specs.append(pl.BlockSpec((None, None, LANE) + tail,
                                      lambda b: (0, b, W // LANE - 1, 0, 0, 0)))
        else:
            views.append(cache6.reshape((1, nb, W // dil, dil) + tail))
            specs.append(pl.BlockSpec((None, None, LANE, None) + tail,
                                      lambda b, dil=dil: (0, b, W // dil // LANE - 1, 0, 0, 0, 0)))
    return pl.pallas_call(
        _attn_step_kernel,
        grid=(nb,),
        in_specs=[pl.BlockSpec(bias.shape, lambda b: (0, 0, 0, 0)), hspec, hspec, hspec, hspec]
                 + specs,
        out_specs=hspec,
        out_shape=jax.ShapeDtypeStruct((nb, B_HEADS, B_HEAD_DIM), F32),
        compiler_params=_cp("arbitrary"),
    )(bias, q, kn, vn, gate, *views)


def _cmul(ar, ai, br, bi):
    return ar * br - ai * bi, ar * bi + ai * br


def _blockdiag_rows(x):
    hi = pltpu.roll(x, C_STATE, 1)
    g = lax.broadcasted_iota(jnp.int32, (LANE, LANE), 0) // C_GROUP
    return jnp.concatenate([jnp.where(g == 2 * b, x, jnp.where(g == 2 * b + 1, hi, 0.0))
                            for b in range(S5_NS // LANE)], axis=1)


def _blockdiag_cols(x):
    piece = lambda g: x[g * C_STATE:(g + 1) * C_STATE, :]
    return jnp.concatenate([piece(0)] + [pltpu.roll(piece(g), C_GROUP * g, 1)
                                         for g in range(1, S5_BLK)], axis=0)


def _s5_prep_kernel(are_ref, aim_ref, ldt_ref, bre_ref, bim_ref, ctre_ref, ctim_ref,
                    t_ref, wb_ref, cb_ref, lam_ref, lamq_ref, cpl_ref):
    Q = S5_Q
    are, aim = are_ref[...], aim_ref[...]
    dt = jnp.exp(ldt_ref[...])
    mag = jnp.exp(dt * are)
    l_re = mag * jnp.cos(dt * aim)
    l_im = mag * jnp.sin(dt * aim)
    den = are * are + aim * aim
    f_re = ((l_re - 1.0) * are + l_im * aim) / den
    f_im = (l_im * are - (l_re - 1.0) * aim) / den
    lr, li = l_re[0:1, :], l_im[0:1, :]
    bb_re, bb_im = _cmul(f_re[0:1, :], f_im[0:1, :], _blockdiag_rows(bre_ref[...]),
                         _blockdiag_rows(bim_ref[...]))
    ctre, ctim = _blockdiag_cols(ctre_ref[...]), _blockdiag_cols(ctim_ref[...])
    cplain = jnp.concatenate([ctre, -ctim], axis=0).astype(BF16)
    cpl_ref[...] = cplain
    stk = jnp.concatenate([l_re, l_im, jnp.zeros((LANE - 16, S5_NS), F32)], axis=0).T
    lc_re, lc_im = stk[:, 0:1], stk[:, 8:9]
    p_re, p_im = jnp.ones((1, S5_NS), F32), jnp.zeros((1, S5_NS), F32)
    pc_re, pc_im = lc_re, lc_im
    dts = []
    for tau in range(Q):
        e_re, e_im = _cmul(bb_re, bb_im, p_re, p_im)
        wp = jnp.concatenate([e_re, e_im], axis=1).astype(BF16)
        s = Q - 1 - tau
        wb_ref[s * LANE:(s + 1) * LANE, :] = wp
        dts.append(jnp.dot(wp, cplain, preferred_element_type=F32).astype(BF16))
        c_re, c_im = _cmul(ctre, ctim, pc_re, pc_im)
        cb_ref[:, tau * LANE:(tau + 1) * LANE] = jnp.concatenate([c_re, -c_im], axis=0).astype(BF16)
        p_re, p_im = _cmul(p_re, p_im, lr, li)
        pc_re, pc_im = _cmul(pc_re, pc_im, lc_re, lc_im)
    zero = jnp.zeros((LANE, LANE), BF16)
    for s in range(Q):
        for q in range(Q):
            t_ref[s * LANE:(s + 1) * LANE, q * LANE:(q + 1) * LANE] = dts[q - s] if q >= s else zero
    lam_ref[...] = jnp.concatenate([l_re, l_im], axis=1)
    sub = lax.broadcasted_iota(jnp.int32, (8, 1), 0)
    q_re, q_im = p_re, p_im
    tab_re, tab_im = jnp.zeros((8, S5_NS), F32), jnp.zeros((8, S5_NS), F32)
    for r in range(8):
        tab_re = jnp.where(sub == r, q_re, tab_re)
        tab_im = jnp.where(sub == r, q_im, tab_im)
        q_re, q_im = _cmul(q_re, q_im, p_re, p_im)
    lamq_ref[...] = jnp.concatenate([tab_re, tab_im], axis=1)


def _s5_prep(are, aim, ldt, bre, bim, ctre, ctim):
    Q = S5_Q
    row = pl.BlockSpec((None, 8, S5_NS), lambda i: (i, 0, 0))
    blk = lambda a, b: pl.BlockSpec((None, a, b), lambda i: (i, 0, 0))
    return pl.pallas_call(
        _s5_prep_kernel,
        grid=(S5_NBLK,),
        in_specs=[row, row, row, blk(LANE, LANE), blk(LANE, LANE), blk(S5_NS, LANE), blk(S5_NS, LANE)],
        out_specs=[blk(Q * LANE, Q * LANE), blk(Q * LANE, 2 * S5_NS), blk(2 * S5_NS, Q * LANE),
                   blk(8, 2 * S5_NS), blk(8, 2 * S5_NS), blk(2 * S5_NS, LANE)],
        out_shape=[jax.ShapeDtypeStruct((S5_NBLK, Q * LANE, Q * LANE), BF16),
                   jax.ShapeDtypeStruct((S5_NBLK, Q * LANE, 2 * S5_NS), BF16),
                   jax.ShapeDtypeStruct((S5_NBLK, 2 * S5_NS, Q * LANE), BF16),
                   jax.ShapeDtypeStruct((S5_NBLK, 8, 2 * S5_NS), F32),
                   jax.ShapeDtypeStruct((S5_NBLK, 8, 2 * S5_NS), F32),
                   jax.ShapeDtypeStruct((S5_NBLK, 2 * S5_NS, LANE), BF16)],
        compiler_params=_cp("arbitrary"),
    )(are, aim, ldt, bre, bim, ctre, ctim)


def _s5_main_kernel(nb, u_ref, t_ref, wb_ref, cb_ref, lamq_ref, dsk_ref, g_ref, hfin_ref,
                    lhs_scr, gs_scr, *seq_scr):
    Q = S5_Q
    NS = S5_NS
    nrow = u_ref.shape[0] // Q
    nch = nrow // nb
    for s in range(Q):
        lhs_scr[:, s * LANE:(s + 1) * LANE] = u_ref[pl.ds(s, nrow, stride=Q), :].astype(BF16)
    tab = lamq_ref[...]
    tab_re, tab_im = tab[:, :NS], tab[:, NS:]
    sub = lax.broadcasted_iota(jnp.int32, (8, 1), 0)

    def scan_tile(s_scr, hp_scr, row0, h_re, h_im):
        st = s_scr[pl.ds(row0, 8), :]
        x_re, x_im = st[:, :NS], st[:, NS:]
        for sh in (1, 2, 4):
            y_re = jnp.where(sub >= sh, pltpu.roll(x_re, sh, 0), 0.0)
            y_im = jnp.where(sub >= sh, pltpu.roll(x_im, sh, 0), 0.0)
            d_re, d_im = _cmul(tab_re[sh - 1:sh, :], tab_im[sh - 1:sh, :], y_re, y_im)
            x_re, x_im = x_re + d_re, x_im + d_im
        c_re, c_im = _cmul(tab_re, tab_im, h_re, h_im)
        x_re, x_im = x_re + c_re, x_im + c_im
        p_re = jnp.where(sub == 0, h_re, pltpu.roll(x_re, 1, 0))
        p_im = jnp.where(sub == 0, h_im, pltpu.roll(x_im, 1, 0))
        hp_scr[pl.ds(row0, 8), :] = jnp.concatenate([p_re, p_im], axis=1)
        return x_re[7:8, :], x_im[7:8, :]

    for b in range(nb):
        s_scr, hp_scr = seq_scr[2 * b], seq_scr[2 * b + 1]
        lhs = lhs_scr[pl.ds(b * nch, nch), :]
        s_scr[...] = jnp.dot(lhs, wb_ref[...], preferred_element_type=F32)
        h_re = h_im = jnp.zeros((1, NS), F32)
        for t in range(nch // 8):
            h_re, h_im = scan_tile(s_scr, hp_scr, t * 8, h_re, h_im)
        hfin_ref[b:b + 1, :] = jnp.concatenate([h_re, h_im], axis=1)
        y = (jnp.dot(lhs, t_ref[...], preferred_element_type=F32)
             + jnp.dot(hp_scr[...].astype(BF16), cb_ref[...], preferred_element_type=F32))
        for q in range(Q):
            rows = pl.ds(b * nch * Q + q, nch, stride=Q)
            yq = y[:, q * LANE:(q + 1) * LANE] + dsk_ref[...] * u_ref[rows, :]
            gs_scr[rows, :] = jax.nn.gelu(yq)
    g_ref[...] = gs_scr[...].astype(g_ref.dtype)


def _s5_main(u_gate, nb, t, wb, cb, lamq, dsk):
    Q = S5_Q
    M = u_gate.shape[0]
    nrow = M // Q
    blk = lambda a, b: pl.BlockSpec((None, a, b), lambda i: (i, 0, 0))
    return pl.pallas_call(
        functools.partial(_s5_main_kernel, nb),
        grid=(S5_NBLK,),
        in_specs=[pl.BlockSpec((M, LANE), lambda i: (0, i)),
                  blk(Q * LANE, Q * LANE), blk(Q * LANE, 2 * S5_NS), blk(2 * S5_NS, Q * LANE),
                  blk(8, 2 * S5_NS), pl.BlockSpec((1, LANE), lambda i: (0, i))],
        out_specs=[pl.BlockSpec((M, LANE), lambda i: (0, i)), blk(nb, 2 * S5_NS)],
        out_shape=[jax.ShapeDtypeStruct((M, E1), BF16),
                   jax.ShapeDtypeStruct((S5_NBLK, nb, 2 * S5_NS), F32)],
        scratch_shapes=[pltpu.VMEM((nrow, Q * LANE), BF16), pltpu.VMEM((M, LANE), F32)]
                       + [pltpu.VMEM((nrow // nb, 2 * S5_NS), F32)] * (2 * nb),
        compiler_params=_cp("arbitrary"),
    )(u_gate, t, wb, cb, lamq, dsk)


def _s5_step_kernel(u_ref, h0_ref, bb_ref, cpl_ref, lam_ref, dsk_ref, g_ref, hn_ref):
    NS = S5_NS
    u = u_ref[...]
    s = jnp.dot(u.astype(BF16), bb_ref[...], preferred_element_type=F32)
    lam = lam_ref[0:1, :]
    h0 = h0_ref[...]
    n_re, n_im = _cmul(lam[:, :NS], lam[:, NS:], h0[:, :NS], h0[:, NS:])
    hn = jnp.concatenate([n_re + s[:, :NS], n_im + s[:, NS:]], axis=1)
    hn_ref[...] = hn
    y = jnp.dot(hn.astype(BF16), cpl_ref[...], preferred_element_type=F32) + dsk_ref[...] * u
    g_ref[...] = jax.nn.gelu(y)


def _s5_step(u_gate, h0, wb, cpl, lam, dsk):
    nb = u_gate.shape[0]
    blk = lambda a, b: pl.BlockSpec((None, a, b), lambda i: (i, 0, 0))
    return pl.pallas_call(
        _s5_step_kernel,
        grid=(S5_NBLK,),
        in_specs=[pl.BlockSpec((nb, LANE), lambda i: (0, i)), blk(nb, 2 * S5_NS),
                  pl.BlockSpec((None, LANE, 2 * S5_NS), lambda i: (i, S5_Q - 1, 0)),
                  blk(2 * S5_NS, LANE), blk(8, 2 * S5_NS), pl.BlockSpec((1, LANE), lambda i: (0, i))],
        out_specs=[pl.BlockSpec((nb, LANE), lambda i: (0, i)), blk(nb, 2 * S5_NS)],
        out_shape=[jax.ShapeDtypeStruct((nb, E1), F32),
                   jax.ShapeDtypeStruct((S5_NBLK, nb, 2 * S5_NS), F32)],
        compiler_params=_cp("arbitrary"),
    )(u_gate, h0, wb, cpl, lam, dsk)


def _state_to_blocks(st):
    nb = st.shape[0]
    x = st.reshape(nb, S5_NBLK, S5_NS, 2)
    return jnp.transpose(x, (1, 0, 3, 2)).reshape(S5_NBLK, nb, 2 * S5_NS)


def _blocks_to_state(h):
    nb = h.shape[1]
    x = h.reshape(S5_NBLK, nb, 2, S5_NS)
    return jnp.transpose(x, (1, 0, 3, 2)).reshape(nb, C_GROUPS, C_STATE, 2)


def kernel(x_prompt, x_sample, state_a_ssm, state_a_conv, cache_b_kv, state_c, norm_w, final_norm_w, even_w_in, a_conv_w, a_conv_b, a_dt_bias, a_A_log, a_D, a_norm_w, even_w_out, odd_w_in, c_A_re, c_A_im, c_B_re, c_B_im, c_C_re, c_C_im, c_D, c_log_dt, c_glu_w, c_glu_b, odd_w_out):
    B, L, D = x_prompt.shape
    nS = x_sample.shape[0]
    M = B * L
    Q = S5_Q

    nw0 = norm_w[0][None, :]
    nw1 = norm_w[1][None, :]
    fnw = final_norm_w[None, :]
    cw, cb = a_conv_w[0], a_conv_b[0][None, :]
    pad_h = lambda v: jnp.pad(v, (0, LANE - A_HEADS))[None, :]
    dtb, alog = pad_h(a_dt_bias[0]), pad_h(a_A_log[0])
    dsk_a = jnp.repeat(a_D[0], A_HEAD_DIM)[None, :]
    gnw = a_norm_w[0][None, :]
    head_expand = np.kron(np.eye(A_HEADS, dtype=np.float32), np.ones((1, A_HEAD_DIM), np.float32))
    head_expand = jnp.asarray(np.pad(head_expand, ((0, LANE - A_HEADS), (0, 0)))).astype(BF16)
    slopes_np = np.asarray(2.0 ** (-8.0 * np.arange(1, B_HEADS + 1) / B_HEADS), np.float32)
    slopes = jnp.asarray(np.broadcast_to(slopes_np[:, None, None], (B_HEADS, 8, LANE)))
    dist = np.stack([dil * (LANE - np.arange(LANE)) for _, dil in B_PATTERNS]).astype(np.float32)
    step_bias = jnp.asarray((dist[:, :, None] * slopes_np[None, None, :])[..., None])

    rows8 = lambda p: jnp.broadcast_to(p.reshape(S5_NBLK, 1, S5_NS), (S5_NBLK, 8, S5_NS))
    lane_pad = lambda p: jnp.pad(p, ((0, 0), (0, 0), (0, LANE - p.shape[-1])))
    b_rows = lambda p: lane_pad(jnp.transpose(p, (0, 2, 1)).reshape(S5_NBLK, LANE, C_STATE))
    c_rows = lambda p: lane_pad(jnp.transpose(p, (0, 2, 1)).reshape(S5_NBLK, S5_NS, C_GROUP))
    ldt = jnp.repeat(c_log_dt[0], C_STATE).reshape(S5_NBLK, S5_NS)
    t_m, wb_m, cb_m, lam, lamq, cpl = _s5_prep(
        rows8(c_A_re[0]), rows8(c_A_im[0]), rows8(ldt), b_rows(c_B_re[0]), b_rows(c_B_im[0]),
        c_rows(c_C_re[0]), c_rows(c_C_im[0]))
    w0t = jnp.transpose(even_w_in[0])
    s1 = A_WIDTH + A_CONV_DIM
    s2 = s1 + A_HEADS
    w_a = jnp.concatenate([w0t[A_WIDTH:s1], w0t[:A_WIDTH], w0t[s2 + 6144:], w0t[s1:s2],
                           jnp.zeros((A_TILE - A_HEADS, D), F32)], axis=0).astype(BF16)
    w_qkv = w0t[s2:s2 + 6144].astype(BF16)
    dsk_c = c_D[0][None, :]
    glu_b = c_glu_b[0][None, :]

    xs = x_sample.reshape(nS, D)
    xp = x_prompt.reshape(M, D)

    p1 = _mm_norm(xp, nw0, w_a, 1024, A_TILE, keep_xn=True, cast=c_glu_w[0], wt=True, rider=xs)
    a_arr, a_s, w_glu = p1["o"], p1["os"], p1["cast_out"]
    qkv, kv_keep, w_in1, qkv_s = _proj_qkv(p1["xn"], w_qkv, B, L, 1024, odd_w_in[0], p1["xs_n"])
    col = lambda a, b: a_s[:, a:b]
    hv = lambda part: jnp.transpose(qkv_s[part * B_HEADS:(part + 1) * B_HEADS], (1, 0, 2))
    gate_s = col(5120, 7168).reshape(nS, B_HEADS, B_HEAD_DIM)
    kvn = jnp.stack([hv(1), hv(2)], axis=1)
    y_a, ssm_p, tail_p = _ssd(a_arr, B, L, cw, cb, dtb, alog, dsk_a, gnw)
    y_b, new_cache, w_out0, w_out1 = _attn(qkv, slopes, a_arr, cache_b_kv, kvn,
                                           even_w_out[0], odd_w_out[0])
    y_as, ssm_s, conv_s = _ssd_step(col(0, 3072)[:, None, :], state_a_conv[0],
                                    col(7168, 7296)[:, None, :], col(3072, 5120)[:, None, :],
                                    state_a_ssm[0].reshape(nS, A_WIDTH, A_STATE),
                                    cw, cb, dtb, alog, head_expand, dsk_a, gnw)
    y_bs = _attn_step(hv(0), hv(1), hv(2), gate_s, cache_b_kv, step_bias)
    hp1, hs1 = _outproj0(y_a, y_b, w_out0, xp, 1024,
                         (y_as.reshape(nS, A_WIDTH), y_bs.reshape(nS, A_WIDTH), xs))

    p3 = _mm_norm(hp1, nw1, w_in1, 1024, 1024, rider=hs1)
    u_gate, u_gate_s = p3["o"], p3["os"]
    g, hfin_p = _s5_main(u_gate, B, t_m, wb_m, cb_m, lamq, dsk_c)
    g_s, hnew_s = _s5_step(u_gate_s, _state_to_blocks(state_c[0]), wb_m, cpl, lam, dsk_c)
    y1, y1_s = _glu(g, w_glu, glu_b, u_gate, 1024, (g_s, u_gate_s))
    y_prompt, y_sample = _outproj1(y1, w_out1, hp1, fnw, 1024, (y1_s, hs1))
    y_prompt = y_prompt.reshape(B, L, D)
    y_sample = y_sample.reshape(nS, 1, D)

    return (y_prompt, y_sample,
            ssm_p.reshape(1, B, A_HEADS, A_HEAD_DIM, A_STATE),
            tail_p[:, 5:8, :][None],
            kv_keep[None],
            _blocks_to_state(hfin_p)[None],
            ssm_s.reshape(1, nS, A_HEADS, A_HEAD_DIM, A_STATE),
            conv_s[None],
            new_cache,
            _blocks_to_state(hnew_s)[None])
```

```python
import functools

import numpy as np
import jax
import jax.numpy as jnp
from jax import lax
from jax.experimental import pallas as pl
from jax.experimental.pallas import tpu as pltpu

F32 = jnp.float32
BF16 = jnp.bfloat16

D_MODEL = 2048
EPS = 1e-5
A_WIDTH = 2048
A_HEAD_DIM = 64
A_HEADS = 32
A_GROUPS = 4
A_STATE = 128
A_CONV_W = 4
A_CONV_DIM = 3072
SSD_CHUNK = 128
B_HEADS = 16
B_HEAD_DIM = 128
B_PATTERNS = ((128, 1), (512, 4), (2048, 16))
B_MAX_WINDOW = 2048
C_GROUP = 16
C_GROUPS = 256
C_STATE = 64
E1 = 4096

LANE = 128
S5_Q = 8
S5_BLK = 8
S5_NBLK = C_GROUPS // S5_BLK
S5_NS = S5_BLK * C_STATE
TN = 512
A_TILE = 1024
ATT_BLOCK = 2048
NEG = -1e30
ATT_SCALE = B_HEAD_DIM ** -0.5
LOG2E = 1.4426950408889634
VMEM_LIMIT = 56 * 1024 * 1024

NT_DIMS = (((1,), (1,)), ((), ()))


def _cp(*sem):
    return pltpu.CompilerParams(dimension_semantics=sem, vmem_limit_bytes=VMEM_LIMIT)


def _silu(x):
    return x * jax.nn.sigmoid(x)


def _softplus(x):
    return jnp.maximum(x, 0.0) + jnp.log1p(jnp.exp(-jnp.abs(x)))


def _split3(a):
    hi = a.astype(BF16)
    r = a - hi.astype(F32)
    mid = r.astype(BF16)
    lo = (r - mid.astype(F32)).astype(BF16)
    return hi, mid, lo


def _dot3_left(m, a):
    hi, mid, lo = _split3(a)
    d = lambda p: jnp.dot(m, p, preferred_element_type=F32)
    return (d(lo) + d(mid)) + d(hi)


def _dot3_right(a, m):
    hi, mid, lo = _split3(a)
    d = lambda p: jnp.dot(p, m, preferred_element_type=F32)
    return (d(lo) + d(mid)) + d(hi)


def _rms(x, w):
    ms = jnp.mean(x * x, axis=-1, keepdims=True)
    return (x * lax.rsqrt(ms + EPS)) * w


QKV_HEADS = 8
CAST_ROWS = 64


def _cast_specs(w, nsteps, step_of):
    rows, cols = w.shape
    per = CAST_ROWS
    while rows // per > nsteps:
        per *= 2
    nblk = rows // per
    assert nblk * per == rows, (w.shape, nsteps)
    spec = pl.BlockSpec((per, cols), lambda *ids: (jnp.minimum(step_of(*ids), nblk - 1), 0))
    return spec, jax.ShapeDtypeStruct(w.shape, BF16)


def _proj_qkv_kernel(rpb, nskip, x_ref, w_ref, src_ref, xs_ref, qkv_ref, kv_ref, dst_ref, qkvs_ref):
    i, j = pl.program_id(0), pl.program_id(1)
    dst_ref[...] = src_ref[...].astype(BF16)

    def heads(x, out_ref, kv_heads=None):
        acc = _dot_w(x.astype(BF16), w_ref[...], True)
        for hh in range(QKV_HEADS):
            out_ref[hh] = acc[:, hh * LANE:(hh + 1) * LANE]
            if kv_heads is not None:
                kv_ref[:, kv_heads + hh, :] = acc[:, hh * LANE:(hh + 1) * LANE]

    pl.when(i == 0)(lambda: heads(xs_ref[...], qkvs_ref))
    tiles_per = B_HEADS // QKV_HEADS
    in_window = (i % rpb >= nskip) & (j >= tiles_per)
    pl.when(jnp.logical_not(in_window))(lambda: heads(x_ref[...], qkv_ref))
    for part in range(tiles_per):
        pl.when(in_window & (j % tiles_per == part))(
            lambda part=part: heads(x_ref[...], qkv_ref, part * QKV_HEADS))


def _proj_qkv(xn, w, B, L, tm, cast, xs_n):
    M = B * L
    ms = xs_n.shape[0]
    tn = QKV_HEADS * LANE
    rpb = L // tm
    keep = min(B_MAX_WINDOW, L)
    nskip = (L - keep) // tm
    tiles_per = B_HEADS // QKV_HEADS
    nj = w.shape[0] // tn
    rcol = _rider_col(nj)

    def kv_map(i, j):
        ri = i % rpb
        return (i // rpb, jnp.maximum(ri - nskip, 0),
                jnp.where((ri < nskip) | (j < 2 * tiles_per), 0, 1), 0, 0)

    cast_spec, cast_shape = _cast_specs(cast, (M // tm) * nj, lambda i, j: i * nj + j)
    return pl.pallas_call(
        functools.partial(_proj_qkv_kernel, rpb, nskip),
        grid=(M // tm, nj),
        in_specs=[pl.BlockSpec((tm, D_MODEL), lambda i, j: (i, 0)),
                  pl.BlockSpec((tn, D_MODEL), lambda i, j: (j, 0)),
                  cast_spec,
                  pl.BlockSpec((ms, D_MODEL), lambda i, j: (0, 0))],
        out_specs=[pl.BlockSpec((None, QKV_HEADS, tm, LANE), lambda i, j: (i // rpb, j, i % rpb, 0)),
                   pl.BlockSpec((None, tm, None, B_HEADS, B_HEAD_DIM), kv_map),
                   cast_spec,
                   pl.BlockSpec((QKV_HEADS, ms, LANE), lambda i, j: (rcol(i, j), 0, 0))],
        out_shape=[jax.ShapeDtypeStruct((B, 3 * B_HEADS, L, B_HEAD_DIM), F32),
                   jax.ShapeDtypeStruct((B, keep, 2, B_HEADS, B_HEAD_DIM), F32),
                   cast_shape,
                   jax.ShapeDtypeStruct((3 * B_HEADS, ms, B_HEAD_DIM), F32)],
        compiler_params=_cp("arbitrary", "arbitrary"),
    )(xn, w, cast, xs_n)


def _dot_w(x, w, wt):
    if wt:
        return lax.dot_general(x, w, NT_DIMS, preferred_element_type=F32)
    return jnp.dot(x, w, preferred_element_type=F32)


def _rider_col(nj):
    return lambda i, j: jnp.where(i == 0, j, nj - 1)


def _mm_norm_kernel(names, wt, *refs):
    r = dict(zip(names, refs))
    j = pl.program_id(1)
    if "cast_src" in r:
        r["cast_out"][...] = r["cast_src"][...].astype(BF16)

    def rows(x_ref, xn_ref, o_ref):
        @pl.when(j == 0)
        def _():
            xn_ref[...] = _rms(x_ref[...], r["nw"][...]).astype(xn_ref.dtype)

        o_ref[...] = _dot_w(xn_ref[...].astype(BF16), r["w"][...], wt)

    rows(r["x"], r["xn"], r["o"])
    if "xs" in r:
        pl.when(pl.program_id(0) == 0)(lambda: rows(r["xs"], r["xs_n"], r["os"]))


def _mm_norm(x, nw, w, tm, tn=TN, keep_xn=False, cast=None, wt=False, rider=None):
    M = x.shape[0]
    N, K = w.shape if wt else w.shape[::-1]
    nj = N // tn
    ins = {"x": (x, pl.BlockSpec((tm, K), lambda i, j: (i, 0))),
           "nw": (nw, pl.BlockSpec((1, K), lambda i, j: (0, 0))),
           "w": (w, pl.BlockSpec((tn, K), lambda i, j: (j, 0)) if wt else
                 pl.BlockSpec((K, tn), lambda i, j: (0, j)))}
    outs = {"o": (jax.ShapeDtypeStruct((M, N), F32), pl.BlockSpec((tm, tn), lambda i, j: (i, j)))}
    scratch = {}
    xn_spec = pl.BlockSpec((tm, K), lambda i, j: (i, 0))
    if keep_xn:
        outs["xn"] = (jax.ShapeDtypeStruct((M, K), BF16), xn_spec)
    else:
        scratch["xn"] = pltpu.VMEM((tm, K), BF16)
    if cast is not None:
        spec, shape = _cast_specs(cast, (M // tm) * nj, lambda i, j: i * nj + j)
        ins["cast_src"] = (cast, spec)
        outs["cast_out"] = (shape, spec)
    if rider is not None:
        ms = rider.shape[0]
        rcol = _rider_col(nj)
        ins["xs"] = (rider, pl.BlockSpec((ms, K), lambda i, j: (0, 0)))
        outs["os"] = (jax.ShapeDtypeStruct((ms, N), F32),
                      pl.BlockSpec((ms, tn), lambda i, j: (0, rcol(i, j))))
        if keep_xn:
            outs["xs_n"] = (jax.ShapeDtypeStruct((ms, K), F32), pl.BlockSpec((ms, K), lambda i, j: (0, 0)))
        else:
            scratch["xs_n"] = pltpu.VMEM((ms, K), F32)
    out = pl.pallas_call(
        functools.partial(_mm_norm_kernel, tuple(ins) + tuple(outs) + tuple(scratch), wt),
        grid=(M // tm, nj),
        in_specs=[s for _, s in ins.values()],
        out_specs=[s for _, s in outs.values()],
        out_shape=[a for a, _ in outs.values()],
        scratch_shapes=list(scratch.values()),
        compiler_params=_cp("arbitrary", "arbitrary"),
    )(*[a for a, _ in ins.values()])
    return dict(zip(outs, out))


def _outproj0_kernel(ya_ref, yb_ref, wa_ref, wb_ref, res_ref, yas_ref, ybs_ref, ress_ref,
                     o_ref, os_ref):
    def rows(ya, yb, res, o):
        o[...] = (res[...] + jnp.dot(ya[...].astype(BF16), wa_ref[...], preferred_element_type=F32)
                  + jnp.dot(yb[...].astype(BF16), wb_ref[...], preferred_element_type=F32))

    rows(ya_ref, yb_ref, res_ref, o_ref)
    pl.when(pl.program_id(0) == 0)(lambda: rows(yas_ref, ybs_ref, ress_ref, os_ref))


def _outproj0(ya, yb, w, res, tm, rider):
    M, K = ya.shape
    ms = rider[0].shape[0]
    nj = D_MODEL // TN
    rcol = _rider_col(nj)
    row = pl.BlockSpec((tm, K), lambda i, j: (i, 0))
    srow = pl.BlockSpec((ms, K), lambda i, j: (0, 0))
    stile = pl.BlockSpec((ms, TN), lambda i, j: (0, rcol(i, j)))
    return pl.pallas_call(
        _outproj0_kernel,
        grid=(M // tm, nj),
        in_specs=[row, row,
                  pl.BlockSpec((K, TN), lambda i, j: (0, j)),
                  pl.BlockSpec((K, TN), lambda i, j: (1, j)),
                  pl.BlockSpec((tm, TN), lambda i, j: (i, j)), srow, srow, stile],
        out_specs=[pl.BlockSpec((tm, TN), lambda i, j: (i, j)), stile],
        out_shape=[jax.ShapeDtypeStruct((M, D_MODEL), F32), jax.ShapeDtypeStruct((ms, D_MODEL), F32)],
        compiler_params=_cp("arbitrary", "arbitrary"),
    )(ya, yb, w, w, res, *rider)


def _glu_kernel(g_ref, wa_ref, wb_ref, ba_ref, bb_ref, gate_ref, gs_ref, gates_ref, o_ref, os_ref):
    def rows(g_ref, gate_ref, o_ref):
        tm = g_ref.shape[0]
        hr = min(tm, 512)
        for r in range(tm // hr):
            rs = pl.ds(r * hr, hr)
            g = g_ref[rs, :].astype(BF16)
            ga = jnp.dot(g, wa_ref[...], preferred_element_type=F32) + ba_ref[...]
            gb = jnp.dot(g, wb_ref[...], preferred_element_type=F32) + bb_ref[...]
            o_ref[rs, :] = (ga * jax.nn.sigmoid(gb) * _silu(gate_ref[rs, :])).astype(o_ref.dtype)

    rows(g_ref, gate_ref, o_ref)
    pl.when(pl.program_id(0) == 0)(lambda: rows(gs_ref, gates_ref, os_ref))


def _glu(g, w, bias, u_gate, tm, rider):
    M = g.shape[0]
    gs, ugs = rider
    ms = gs.shape[0]
    nb = E1 // TN
    rcol = _rider_col(nb)
    return pl.pallas_call(
        _glu_kernel,
        grid=(M // tm, nb),
        in_specs=[pl.BlockSpec((tm, E1), lambda i, j: (i, 0)),
                  pl.BlockSpec((E1, TN), lambda i, j: (0, j)),
                  pl.BlockSpec((E1, TN), lambda i, j: (0, j + nb)),
                  pl.BlockSpec((1, TN), lambda i, j: (0, j)),
                  pl.BlockSpec((1, TN), lambda i, j: (0, j + nb)),
                  pl.BlockSpec((tm, TN), lambda i, j: (i, j + nb)),
                  pl.BlockSpec((ms, E1), lambda i, j: (0, 0)),
                  pl.BlockSpec((ms, TN), lambda i, j: (0, nb + rcol(i, j)))],
        out_specs=[pl.BlockSpec((tm, TN), lambda i, j: (i, j)),
                   pl.BlockSpec((ms, TN), lambda i, j: (0, rcol(i, j)))],
        out_shape=[jax.ShapeDtypeStruct((M, E1), BF16), jax.ShapeDtypeStruct((ms, E1), F32)],
        compiler_params=_cp("arbitrary", "arbitrary"),
    )(g, w, w, bias, bias, u_gate, gs, ugs)


def _outproj1_kernel(y_ref, w_ref, res_ref, fnw_ref, ys_ref, ress_ref, o_ref, os_ref):
    j = pl.program_id(1)
    nj = D_MODEL // TN

    def rows(y_ref, res_ref, o_ref):
        h = res_ref[...] + jnp.dot(y_ref[...].astype(BF16), w_ref[...], preferred_element_type=F32)
        for jj in range(nj):
            @pl.when(j == jj)
            def _(jj=jj):
                o_ref[:, jj * TN:(jj + 1) * TN] = h

        @pl.when(j == nj - 1)
        def _():
            o_ref[...] = _rms(o_ref[...], fnw_ref[...])

    rows(y_ref, res_ref, o_ref)
    pl.when(pl.program_id(0) == 0)(lambda: rows(ys_ref, ress_ref, os_ref))


def _outproj1(y, w, res, fnw, tm, rider):
    M, K = y.shape
    ys, ress = rider
    ms = ys.shape[0]
    nj = D_MODEL // TN
    rcol = _rider_col(nj)
    return pl.pallas_call(
        _outproj1_kernel,
        grid=(M // tm, nj),
        in_specs=[pl.BlockSpec((tm, K), lambda i, j: (i, 0)),
                  pl.BlockSpec((K, TN), lambda i, j: (0, j)),
                  pl.BlockSpec((tm, TN), lambda i, j: (i, j)),
                  pl.BlockSpec((1, D_MODEL), lambda i, j: (0, 0)),
                  pl.BlockSpec((ms, K), lambda i, j: (0, 0)),
                  pl.BlockSpec((ms, TN), lambda i, j: (0, rcol(i, j)))],
        out_specs=[pl.BlockSpec((tm, D_MODEL), lambda i, j: (i, 0)),
                   pl.BlockSpec((ms, D_MODEL), lambda i, j: (0, 0))],
        out_shape=[jax.ShapeDtypeStruct((M, D_MODEL), F32), jax.ShapeDtypeStruct((ms, D_MODEL), F32)],
        compiler_params=_cp("arbitrary", "arbitrary"),
    )(y, w, res, fnw, ys, ress)


def _ssd_kernel(xbc_ref, dt_ref, z0_ref, z1_ref, cw_ref, cb_ref, dtb_ref, alog_ref, dsk_ref, gnw_ref,
                y_ref, st_ref, tail_ref, h_scr, ext_scr, y_scr):
    c = pl.program_id(1)
    Q = SSD_CHUNK

    @pl.when(c == 0)
    def _():
        h_scr[...] = jnp.zeros_like(h_scr)
        ext_scr[0:8, :] = jnp.zeros((8, A_CONV_DIM), F32)

    x_raw = xbc_ref[...]
    ext_scr[8:8 + Q, :] = x_raw
    conv = cb_ref[...] + cw_ref[3:4, :] * x_raw
    for k in (1, 2, 3):
        conv = conv + cw_ref[3 - k:4 - k, :] * ext_scr[pl.ds(8 - k, Q), :]
    ext_scr[0:8, :] = x_raw[Q - 8:Q, :]
    xc = _silu(conv)

    dt = _softplus(dt_ref[...] + dtb_ref[...])
    a = dt * (-jnp.exp(alog_ref[...]))
    qi = lax.broadcasted_iota(jnp.int32, (Q, Q), 0)
    si = lax.broadcasted_iota(jnp.int32, (Q, Q), 1)
    tri = qi >= si
    a_cum = _dot3_left(tri.astype(BF16), a)
    a_cum_t = a_cum.T
    lo_half = si < A_HEAD_DIM

    for g in range(A_GROUPS):
        bg = xc[:, A_WIDTH + g * A_STATE:A_WIDTH + (g + 1) * A_STATE].astype(BF16)
        cg = xc[:, A_WIDTH + (A_GROUPS + g) * A_STATE:A_WIDTH + (A_GROUPS + g + 1) * A_STATE].astype(BF16)
        cbm = lax.dot_general(cg, bg, NT_DIMS, preferred_element_type=F32)
        hprev = h_scr[g * 512:(g + 1) * 512, :]
        yoff = lax.dot_general(cg, hprev.astype(BF16), NT_DIMS, preferred_element_type=F32)
        for jj in range(4):
            p = g * 4 + jj
            h0, h1 = 2 * p, 2 * p + 1
            dtp = jnp.where(lo_half, dt[:, h0:h0 + 1], dt[:, h1:h1 + 1])
            ap = jnp.where(lo_half, a_cum[:, h0:h0 + 1], a_cum[:, h1:h1 + 1])
            x_p = xc[:, p * LANE:(p + 1) * LANE]
            xdt = x_p * dtp
            g0 = cbm * jnp.exp(jnp.where(tri, a_cum[:, h0:h0 + 1] - a_cum_t[h0:h0 + 1, :], NEG))
            g1 = cbm * jnp.exp(jnp.where(tri, a_cum[:, h1:h1 + 1] - a_cum_t[h1:h1 + 1, :], NEG))
            lhs = jnp.concatenate([g0, g1], axis=1).astype(BF16)
            rhs = jnp.concatenate([jnp.where(lo_half, xdt, 0.0), jnp.where(lo_half, 0.0, xdt)],
                                  axis=0).astype(BF16)
            ydiag = jnp.dot(lhs, rhs, preferred_element_type=F32)
            y_scr[:, p * LANE:(p + 1) * LANE] = (
                ydiag + yoff[:, jj * LANE:(jj + 1) * LANE] * jnp.exp(ap)
                + dsk_ref[:, p * LANE:(p + 1) * LANE] * x_p)
            xd = xdt * jnp.exp(ap[Q - 1:Q, :] - ap)
            st = jnp.dot(xd.T.astype(BF16), bg, preferred_element_type=F32)
            cd = jnp.exp(jnp.where(qi < A_HEAD_DIM, a_cum_t[h0:h0 + 1, Q - 1:Q],
                                   a_cum_t[h1:h1 + 1, Q - 1:Q]))
            h_scr[p * LANE:(p + 1) * LANE, :] = cd * hprev[jj * LANE:(jj + 1) * LANE, :] + st

    z = jnp.concatenate([z0_ref[...], z1_ref[...]], axis=1)
    y_ref[...] = _rms(y_scr[...] * _silu(z), gnw_ref[...]).astype(y_ref.dtype)

    @pl.when(c == pl.num_programs(1) - 1)
    def _():
        st_ref[...] = h_scr[...]
        tail_ref[...] = x_raw[Q - 8:Q, :]


def _ssd(a_arr, B, L, cw, cb, dtb, alog, dsk, gnw):
    nc = L // SSD_CHUNK
    Q = SSD_CHUNK
    row = lambda b, c: b * nc + c
    full = lambda shp: pl.BlockSpec(shp, lambda b, c: (0, 0))
    return pl.pallas_call(
        _ssd_kernel,
        grid=(B, nc),
        in_specs=[pl.BlockSpec((Q, A_CONV_DIM), lambda b, c: (row(b, c), 0)),
                  pl.BlockSpec((Q, LANE), lambda b, c: (row(b, c), 14 * TN // LANE)),
                  pl.BlockSpec((Q, A_WIDTH // 2), lambda b, c: (row(b, c), 3)),
                  pl.BlockSpec((Q, A_WIDTH // 2), lambda b, c: (row(b, c), 4)),
                  full((A_CONV_W, A_CONV_DIM)), full((1, A_CONV_DIM)), full((1, LANE)),
                  full((1, LANE)), full((1, A_WIDTH)), full((1, A_WIDTH))],
        out_specs=[pl.BlockSpec((Q, A_WIDTH), lambda b, c: (row(b, c), 0)),
                   pl.BlockSpec((None, A_WIDTH, A_STATE), lambda b, c: (b, 0, 0)),
                   pl.BlockSpec((None, 8, A_CONV_DIM), lambda b, c: (b, 0, 0))],
        out_shape=[jax.ShapeDtypeStruct((B * L, A_WIDTH), BF16),
                   jax.ShapeDtypeStruct((B, A_WIDTH, A_STATE), F32),
                   jax.ShapeDtypeStruct((B, 8, A_CONV_DIM), F32)],
        scratch_shapes=[pltpu.VMEM((A_WIDTH, A_STATE), F32), pltpu.VMEM((8 + Q, A_CONV_DIM), F32),
                        pltpu.VMEM((Q, A_WIDTH), F32)],
        compiler_params=_cp("arbitrary", "arbitrary"),
    )(a_arr, a_arr, a_arr, a_arr, cw, cb, dtb, alog, dsk, gnw)


def _ssd_step_kernel(xbc_ref, cs_ref, dt_ref, z_ref, h0_ref, cw_ref, cb_ref, dtb_ref, alog_ref,
                     e_ref, dsk_ref, gnw_ref, y_ref, hn_ref, nc_ref):
    x_raw = xbc_ref[...]
    cs = cs_ref[...]
    conv = (cb_ref[...] + cw_ref[0:1, :] * cs[0:1, :] + cw_ref[1:2, :] * cs[1:2, :]
            + cw_ref[2:3, :] * cs[2:3, :] + cw_ref[3:4, :] * x_raw)
    xc = _silu(conv)
    nc_ref[0:1, :] = cs[1:2, :]
    nc_ref[1:2, :] = cs[2:3, :]
    nc_ref[2:3, :] = x_raw

    dt = _softplus(dt_ref[...] + dtb_ref[...])
    da = dt * (-jnp.exp(alog_ref[...]))
    e = e_ref[...]
    dt_e = _dot3_right(jnp.broadcast_to(dt, (8, LANE)), e)[0:1, :]
    da_e = _dot3_right(jnp.broadcast_to(da, (8, LANE)), e)[0:1, :]
    xs = xc[:, :A_WIDTH]
    sub = lax.broadcasted_iota(jnp.int32, (8, 1), 0)
    rows = jnp.where(sub == 0, xs * dt_e, jnp.where(sub == 1, jnp.exp(da_e), 0.0))
    cols = jnp.concatenate([rows, jnp.zeros((LANE - 8, A_WIDTH), F32)], axis=0).T
    xdt_col = cols[:, 0:1]
    dec_col = cols[:, 1:2]
    ys = []
    for g in range(A_GROUPS):
        bg = xc[:, A_WIDTH + g * A_STATE:A_WIDTH + (g + 1) * A_STATE]
        cg = xc[:, A_WIDTH + (A_GROUPS + g) * A_STATE:A_WIDTH + (A_GROUPS + g + 1) * A_STATE]
        sl = slice(g * 512, (g + 1) * 512)
        hn = dec_col[sl, :] * h0_ref[sl, :] + xdt_col[sl, :] * bg
        hn_ref[sl, :] = hn
        c8 = jnp.broadcast_to(cg, (8, A_STATE)).astype(BF16)
        ys.append(lax.dot_general(c8, hn.astype(BF16), NT_DIMS, preferred_element_type=F32)[0:1, :])
    y = jnp.concatenate(ys, axis=1) + dsk_ref[...] * xs
    y_ref[...] = _rms(y * _silu(z_ref[...]), gnw_ref[...])


def _ssd_step(xbc, cs, dt, z, h0, cw, cb, dtb, alog, e, dsk, gnw):
    nb = xbc.shape[0]
    per_b = lambda n: pl.BlockSpec((None, 1, n), lambda b: (b, 0, 0))
    full = lambda shp: pl.BlockSpec(shp, lambda b: (0, 0))
    return pl.pallas_call(
        _ssd_step_kernel,
        grid=(nb,),
        in_specs=[per_b(A_CONV_DIM), pl.BlockSpec((None, 3, A_CONV_DIM), lambda b: (b, 0, 0)),
                  per_b(LANE), per_b(A_WIDTH),
                  pl.BlockSpec((None, A_WIDTH, A_STATE), lambda b: (b, 0, 0)),
                  full((A_CONV_W, A_CONV_DIM)), full((1, A_CONV_DIM)), full((1, LANE)),
                  full((1, LANE)), full((LANE, A_WIDTH)), full((1, A_WIDTH)), full((1, A_WIDTH))],
        out_specs=[per_b(A_WIDTH),
                   pl.BlockSpec((None, A_WIDTH, A_STATE), lambda b: (b, 0, 0)),
                   pl.BlockSpec((None, 3, A_CONV_DIM), lambda b: (b, 0, 0))],
        out_shape=[jax.ShapeDtypeStruct((nb, 1, A_WIDTH), F32),
                   jax.ShapeDtypeStruct((nb, A_WIDTH, A_STATE), F32),
                   jax.ShapeDtypeStruct((nb, 3, A_CONV_DIM), F32)],
        compiler_params=_cp("arbitrary"),
    )(xbc, cs, dt, z, h0, cw, cb, dtb, alog, e, dsk, gnw)


def _bias_tile(slope, dil):
    i = lax.broadcasted_iota(jnp.int32, (LANE, 2 * LANE), 0)
    c = lax.broadcasted_iota(jnp.int32, (LANE, 2 * LANE), 1)
    j = LANE + i - c
    valid = (j >= 0) & (j <= LANE)
    return jnp.where(valid, -(slope * (dil * j).astype(F32)), NEG), c


def _attn_tile(q, kcat, vcat, bias2):
    s = lax.dot_general(q, kcat, NT_DIMS, preferred_element_type=F32) * (ATT_SCALE * LOG2E) + bias2
    m = jnp.max(s, axis=1, keepdims=True)
    p = jnp.exp2(s - m)
    l = jnp.sum(p, axis=1, keepdims=True)
    acc = jnp.dot(p.astype(BF16), vcat, preferred_element_type=F32)
    return m, l, acc


def _rows(start, stride):
    return pl.ds(start, LANE, stride=stride) if stride > 1 else pl.ds(start, LANE)


def _shift_cache_block(step, steps_per_seq, x_ref, nxt_ref, kvn_ref, o_ref):
    R = x_ref.shape[0]
    o_ref[pl.ds(0, R - 1)] = x_ref[pl.ds(1, R - 1)]
    last = step % steps_per_seq == steps_per_seq - 1
    o_ref[R - 1] = jnp.where(last, kvn_ref[...], nxt_ref[0])


def _attn_kernel(steps_per_seq, slope_ref, q_ref, k_ref, v_ref, kp_ref, vp_ref, gate_ref,
                 x_ref, nxt_ref, kvn_ref, wa_ref, wb_ref, y_ref, oc_ref, wa_out, wb_out, *scr):
    step = ((pl.program_id(0) * pl.num_programs(1) + pl.program_id(1)) * pl.num_programs(2)
            + pl.program_id(2))
    _shift_cache_block(step, steps_per_seq, x_ref, nxt_ref, kvn_ref, oc_ref)
    wa_out[...] = wa_ref[...].astype(BF16)
    wb_out[...] = wb_ref[...].astype(BF16)
    first_block = pl.program_id(2) == 0
    slope = slope_ref[0:1, 0:1]
    npat = len(B_PATTERNS)
    m_scr, l_scr, acc_scr = scr[:npat], scr[npat:2 * npat], scr[2 * npat:]
    for pat, (_, dil) in enumerate(B_PATTERNS):
        bias_full, c = _bias_tile(slope, dil)
        bias_full = jnp.where(bias_full > NEG, bias_full * LOG2E, NEG)
        bias0 = jnp.where((c >= LANE) | jnp.logical_not(first_block), bias_full, NEG)
        nsub = ATT_BLOCK // dil // LANE
        for r in range(dil):
            kc = vc = None
            for sub in range(nsub):
                rows = _rows(sub * LANE * dil + r, dil)
                if sub == 0:
                    prow = _rows((nsub - 1) * LANE * dil + r, dil)
                    kp, vp, bias = kp_ref[prow, :], vp_ref[prow, :], bias0
                else:
                    kp, vp, bias = kc, vc, bias_full
                kc, vc = k_ref[rows, :], v_ref[rows, :]
                kcat = jnp.concatenate([kp, kc], axis=0).astype(BF16)
                vcat = jnp.concatenate([vp, vc], axis=0).astype(BF16)
                m, l, acc = _attn_tile(q_ref[rows, :].astype(BF16), kcat, vcat, bias)
                m_scr[pat][rows, :] = jnp.broadcast_to(m, (LANE, LANE))
                l_scr[pat][rows, :] = jnp.broadcast_to(l, (LANE, LANE))
                acc_scr[pat][rows, :] = acc
    for sub in range(ATT_BLOCK // LANE):
        rs = pl.ds(sub * LANE, LANE)
        ms = [m_scr[p][rs, :] for p in range(npat)]
        m = functools.reduce(jnp.maximum, ms)
        l = acc = 0.0
        for p in range(npat):
            a = jnp.exp2(ms[p] - m)
            l = l + a * l_scr[p][rs, :]
            acc = acc + a * acc_scr[p][rs, :]
        y_ref[rs, :] = ((acc / l) * _silu(gate_ref[rs, :])).astype(y_ref.dtype)


def _attn(qkv, slopes, a_arr, cache6, kvn, wa, wb):
    B, H3, L, dh = qkv.shape
    H = H3 // 3
    R = ATT_BLOCK
    nblk = L // R
    cur = lambda part: pl.BlockSpec((None, None, R, dh), lambda b, h, i: (b, part * H + h, i, 0))
    prev = lambda part: pl.BlockSpec((None, None, R, dh),
                                     lambda b, h, i: (b, part * H + h, jnp.maximum(i - 1, 0), 0))
    slope = pl.BlockSpec((None, 8, LANE), lambda b, h, i: (h, 0, 0))
    gate = pl.BlockSpec((R, LANE), lambda b, h, i: (b * nblk + i, 10 * TN // LANE + h))

    nb, W = cache6.shape[1], cache6.shape[2]
    steps = B * H * nblk
    per = steps // nb
    rows = W // per
    assert per * nb == steps and rows * per == W, (steps, nb, W)
    tail = (2, B_HEADS, B_HEAD_DIM)
    step = lambda b, h, i: (b * H + h) * nblk + i
    cblk = pl.BlockSpec((None, None, rows) + tail,
                        lambda b, h, i: (0, step(b, h, i) // per, step(b, h, i) % per, 0, 0, 0))
    cnext = pl.BlockSpec((None, None, 1) + tail,
                         lambda b, h, i: (0, step(b, h, i) // per,
                                          jnp.minimum((step(b, h, i) % per + 1) * rows, W - 1), 0, 0, 0))
    cnew = pl.BlockSpec((None,) + tail, lambda b, h, i: (step(b, h, i) // per, 0, 0, 0))
    wa_spec, wa_shape = _cast_specs(wa, steps, step)
    wb_spec, wb_shape = _cast_specs(wb, steps, step)
    return pl.pallas_call(
        functools.partial(_attn_kernel, per),
        grid=(B, H, nblk),
        in_specs=[slope, cur(0), cur(1), cur(2), prev(1), prev(2), gate, cblk, cnext, cnew,
                  wa_spec, wb_spec],
        out_specs=[pl.BlockSpec((R, LANE), lambda b, h, i: (b * nblk + i, h)), cblk,
                   wa_spec, wb_spec],
        out_shape=[jax.ShapeDtypeStruct((B * L, H * dh), BF16),
                   jax.ShapeDtypeStruct(cache6.shape, F32), wa_shape, wb_shape],
        scratch_shapes=[pltpu.VMEM((R, LANE), F32)] * (3 * len(B_PATTERNS)),
        compiler_params=_cp("arbitrary", "arbitrary", "arbitrary"),
    )(slopes, qkv, qkv, qkv, qkv, qkv, a_arr, cache6, cache6, kvn, wa, wb)


def _attn_step_kernel(bias_ref, q_ref, kn_ref, vn_ref, gate_ref, c1_ref, c4_ref, c16_ref, y_ref):
    q, kn, vn = q_ref[...], kn_ref[...], vn_ref[...]
    s_self = jnp.sum(q * kn, axis=-1, keepdims=True) * ATT_SCALE
    blocks = (c1_ref, c4_ref, c16_ref)
    scores = []
    for p, c_ref in enumerate(blocks):
        kblk = c_ref[:, 0]
        scores.append(jnp.sum(kblk * q[None], axis=-1, keepdims=True) * ATT_SCALE - bias_ref[p])
    m = s_self
    for s in scores:
        m = jnp.maximum(m, jnp.max(s, axis=0))
    p_self = float(len(B_PATTERNS)) * jnp.exp(s_self - m)
    l = p_self
    o = p_self * vn
    for s, c_ref in zip(scores, blocks):
        p = jnp.exp(s - m[None])
        l = l + jnp.sum(p, axis=0)
        o = o + jnp.sum(p * c_ref[:, 1], axis=0)
    y_ref[...] = (o / l) * _silu(gate_ref[...])


def _attn_step(q, kn, vn, gate, cache6, bias):
    nb = q.shape[0]
    W = B_MAX_WINDOW
    hspec = pl.BlockSpec((None, B_HEADS, B_HEAD_DIM), lambda b: (b, 0, 0))
    tail = (2, B_HEADS, B_HEAD_DIM)
    views, specs = [], []
    for _, dil in B_PATTERNS:
        if dil == 1:
            views.append(cache6)
            specs.append(pl.BlockSpec((None, None, LANE) + tail,
                                      lambda b: (0, b, W // LANE - 1, 0, 0, 0)))
        else:
            views.append(cache6.reshape((1, nb, W // dil, dil) + tail))
            specs.append(pl.BlockSpec((None, None, LANE, None) + tail,
                                      lambda b, dil=dil: (0, b, W // dil // LANE - 1, 0, 0, 0, 0)))
    return pl.pallas_call(
        _attn_step_kernel,
        grid=(nb,),
        in_specs=[pl.BlockSpec(bias.shape, lambda b: (0, 0, 0, 0)), hspec, hspec, hspec, hspec]
                 + specs,
        out_specs=hspec,
        out_shape=jax.ShapeDtypeStruct((nb, B_HEADS, B_HEAD_DIM), F32),
        compiler_params=_cp("arbitrary"),
    )(bias, q, kn, vn, gate, *views)


def _cmul(ar, ai, br, bi):
    return ar * br - ai * bi, ar * bi + ai * br


def _blockdiag_rows(x):
    hi = pltpu.roll(x, C_STATE, 1)
    g = lax.broadcasted_iota(jnp.int32, (LANE, LANE), 0) // C_GROUP
    return jnp.concatenate([jnp.where(g == 2 * b, x, jnp.where(g == 2 * b + 1, hi, 0.0))
                            for b in range(S5_NS // LANE)], axis=1)


def _blockdiag_cols(x):
    piece = lambda g: x[g * C_STATE:(g + 1) * C_STATE, :]
    return jnp.concatenate([piece(0)] + [pltpu.roll(piece(g), C_GROUP * g, 1)
                                         for g in range(1, S5_BLK)], axis=0)


def _s5_prep_kernel(are_ref, aim_ref, ldt_ref, bre_ref, bim_ref, ctre_ref, ctim_ref,
                    t_ref, wb_ref, cb_ref, lam_ref, lamq_ref, cpl_ref):
    Q = S5_Q
    are, aim = are_ref[...], aim_ref[...]
    dt = jnp.exp(ldt_ref[...])
    mag = jnp.exp(dt * are)
    l_re = mag * jnp.cos(dt * aim)
    l_im = mag * jnp.sin(dt * aim)
    den = are * are + aim * aim
    f_re = ((l_re - 1.0) * are + l_im * aim) / den
    f_im = (l_im * are - (l_re - 1.0) * aim) / den
    lr, li = l_re[0:1, :], l_im[0:1, :]
    bb_re, bb_im = _cmul(f_re[0:1, :], f_im[0:1, :], _blockdiag_rows(bre_ref[...]),
                         _blockdiag_rows(bim_ref[...]))
    ctre, ctim = _blockdiag_cols(ctre_ref[...]), _blockdiag_cols(ctim_ref[...])
    cplain = jnp.concatenate([ctre, -ctim], axis=0).astype(BF16)
    cpl_ref[...] = cplain
    stk = jnp.concatenate([l_re, l_im, jnp.zeros((LANE - 16, S5_NS), F32)], axis=0).T
    lc_re, lc_im = stk[:, 0:1], stk[:, 8:9]
    p_re, p_im = jnp.ones((1, S5_NS), F32), jnp.zeros((1, S5_NS), F32)
    pc_re, pc_im = lc_re, lc_im
    dts = []
    for tau in range(Q):
        e_re, e_im = _cmul(bb_re, bb_im, p_re, p_im)
        wp = jnp.concatenate([e_re, e_im], axis=1).astype(BF16)
        s = Q - 1 - tau
        wb_ref[s * LANE:(s + 1) * LANE, :] = wp
        dts.append(jnp.dot(wp, cplain, preferred_element_type=F32).astype(BF16))
        c_re, c_im = _cmul(ctre, ctim, pc_re, pc_im)
        cb_ref[:, tau * LANE:(tau + 1) * LANE] = jnp.concatenate([c_re, -c_im], axis=0).astype(BF16)
        p_re, p_im = _cmul(p_re, p_im, lr, li)
        pc_re, pc_im = _cmul(pc_re, pc_im, lc_re, lc_im)
    zero = jnp.zeros((LANE, LANE), BF16)
    for s in range(Q):
        for q in range(Q):
            t_ref[s * LANE:(s + 1) * LANE, q * LANE:(q + 1) * LANE] = dts[q - s] if q >= s else zero
    lam_ref[...] = jnp.concatenate([l_re, l_im], axis=1)
    sub = lax.broadcasted_iota(jnp.int32, (8, 1), 0)
    q_re, q_im = p_re, p_im
    tab_re, tab_im = jnp.zeros((8, S5_NS), F32), jnp.zeros((8, S5_NS), F32)
    for r in range(8):
        tab_re = jnp.where(sub == r, q_re, tab_re)
        tab_im = jnp.where(sub == r, q_im, tab_im)
        q_re, q_im = _cmul(q_re, q_im, p_re, p_im)
    lamq_ref[...] = jnp.concatenate([tab_re, tab_im], axis=1)


def _s5_prep(are, aim, ldt, bre, bim, ctre, ctim):
    Q = S5_Q
    row = pl.BlockSpec((None, 8, S5_NS), lambda i: (i, 0, 0))
    blk = lambda a, b: pl.BlockSpec((None, a, b), lambda i: (i, 0, 0))
    return pl.pallas_call(
        _s5_prep_kernel,
        grid=(S5_NBLK,),
        in_specs=[row, row, row, blk(LANE, LANE), blk(LANE, LANE), blk(S5_NS, LANE), blk(S5_NS, LANE)],
        out_specs=[blk(Q * LANE, Q * LANE), blk(Q * LANE, 2 * S5_NS), blk(2 * S5_NS, Q * LANE),
                   blk(8, 2 * S5_NS), blk(8, 2 * S5_NS), blk(2 * S5_NS, LANE)],
        out_shape=[jax.ShapeDtypeStruct((S5_NBLK, Q * LANE, Q * LANE), BF16),
                   jax.ShapeDtypeStruct((S5_NBLK, Q * LANE, 2 * S5_NS), BF16),
                   jax.ShapeDtypeStruct((S5_NBLK, 2 * S5_NS, Q * LANE), BF16),
                   jax.ShapeDtypeStruct((S5_NBLK, 8, 2 * S5_NS), F32),
                   jax.ShapeDtypeStruct((S5_NBLK, 8, 2 * S5_NS), F32),
                   jax.ShapeDtypeStruct((S5_NBLK, 2 * S5_NS, LANE), BF16)],
        compiler_params=_cp("arbitrary"),
    )(are, aim, ldt, bre, bim, ctre, ctim)


def _s5_main_kernel(nb, u_ref, t_ref, wb_ref, cb_ref, lamq_ref, dsk_ref, g_ref, hfin_ref,
                    lhs_scr, s_scr, hp_scr, gs_scr):
    Q = S5_Q
    NS = S5_NS
    nrow = u_ref.shape[0] // Q
    nch = nrow // nb
    for s in range(Q):
        lhs_scr[:, s * LANE:(s + 1) * LANE] = u_ref[pl.ds(s, nrow, stride=Q), :].astype(BF16)
    tab = lamq_ref[...]
    tab_re, tab_im = tab[:, :NS], tab[:, NS:]
    sub = lax.broadcasted_iota(jnp.int32, (8, 1), 0)

    def scan_tile(row0, h_re, h_im):
        st = s_scr[pl.ds(row0, 8), :]
        x_re, x_im = st[:, :NS], st[:, NS:]
        for sh in (1, 2, 4):
            y_re = jnp.where(sub >= sh, pltpu.roll(x_re, sh, 0), 0.0)
            y_im = jnp.where(sub >= sh, pltpu.roll(x_im, sh, 0), 0.0)
            d_re, d_im = _cmul(tab_re[sh - 1:sh, :], tab_im[sh - 1:sh, :], y_re, y_im)
            x_re, x_im = x_re + d_re, x_im + d_im
        c_re, c_im = _cmul(tab_re, tab_im, h_re, h_im)
        x_re, x_im = x_re + c_re, x_im + c_im
        p_re = jnp.where(sub == 0, h_re, pltpu.roll(x_re, 1, 0))
        p_im = jnp.where(sub == 0, h_im, pltpu.roll(x_im, 1, 0))
        hp_scr[pl.ds(row0, 8), :] = jnp.concatenate([p_re, p_im], axis=1)
        return x_re[7:8, :], x_im[7:8, :]

    lhs = lhs_scr[...]
    s_scr[...] = jnp.dot(lhs, wb_ref[...], preferred_element_type=F32)
    state = [(jnp.zeros((1, NS), F32),) * 2] * nb
    for t in range(nch // 8):
        state = [scan_tile(b * nch + t * 8, *state[b]) for b in range(nb)]
    for b in range(nb):
        hfin_ref[b:b + 1, :] = jnp.concatenate(state[b], axis=1)
    y = (jnp.dot(lhs, t_ref[...], preferred_element_type=F32)
         + jnp.dot(hp_scr[...].astype(BF16), cb_ref[...], preferred_element_type=F32))
    for q in range(Q):
        rows = pl.ds(q, nrow, stride=Q)
        yq = y[:, q * LANE:(q + 1) * LANE] + dsk_ref[...] * u_ref[rows, :]
        gs_scr[rows, :] = jax.nn.gelu(yq)
    g_ref[...] = gs_scr[...].astype(g_ref.dtype)


def _s5_main(u_gate, nb, t, wb, cb, lamq, dsk):
    Q = S5_Q
    M = u_gate.shape[0]
    nrow = M // Q
    blk = lambda a, b: pl.BlockSpec((None, a, b), lambda i: (i, 0, 0))
    return pl.pallas_call(
        functools.partial(_s5_main_kernel, nb),
        grid=(S5_NBLK,),
        in_specs=[pl.BlockSpec((M, LANE), lambda i: (0, i)),
                  blk(Q * LANE, Q * LANE), blk(Q * LANE, 2 * S5_NS), blk(2 * S5_NS, Q * LANE),
                  blk(8, 2 * S5_NS), pl.BlockSpec((1, LANE), lambda i: (0, i))],
        out_specs=[pl.BlockSpec((M, LANE), lambda i: (0, i)), blk(nb, 2 * S5_NS)],
        out_shape=[jax.ShapeDtypeStruct((M, E1), BF16),
                   jax.ShapeDtypeStruct((S5_NBLK, nb, 2 * S5_NS), F32)],
        scratch_shapes=[pltpu.VMEM((nrow, Q * LANE), BF16), pltpu.VMEM((nrow, 2 * S5_NS), F32),
                        pltpu.VMEM((nrow, 2 * S5_NS), F32), pltpu.VMEM((M, LANE), F32)],
        compiler_params=_cp("arbitrary"),
    )(u_gate, t, wb, cb, lamq, dsk)


S5_STEP_BLOCKS = 4


def _s5_step_kernel(u_ref, h0_ref, bb_ref, cpl_ref, lam_ref, dsk_ref, g_ref, hn_ref):
    NS = S5_NS
    for k in range(S5_STEP_BLOCKS):
        cs = slice(k * LANE, (k + 1) * LANE)
        u = u_ref[:, cs]
        s = jnp.dot(u.astype(BF16), bb_ref[k], preferred_element_type=F32)
        lam = lam_ref[k, 0:1, :]
        h0 = h0_ref[k]
        n_re, n_im = _cmul(lam[:, :NS], lam[:, NS:], h0[:, :NS], h0[:, NS:])
        hn = jnp.concatenate([n_re + s[:, :NS], n_im + s[:, NS:]], axis=1)
        hn_ref[k] = hn
        y = jnp.dot(hn.astype(BF16), cpl_ref[k], preferred_element_type=F32) + dsk_ref[:, cs] * u
        g_ref[:, cs] = jax.nn.gelu(y)


def _s5_step(u_gate, h0, wb, cpl, lam, dsk):
    nb = u_gate.shape[0]
    G = S5_STEP_BLOCKS
    blk = lambda a, b: pl.BlockSpec((G, a, b), lambda i: (i, 0, 0))
    return pl.pallas_call(
        _s5_step_kernel,
        grid=(S5_NBLK // G,),
        in_specs=[pl.BlockSpec((nb, G * LANE), lambda i: (0, i)), blk(nb, 2 * S5_NS),
                  pl.BlockSpec((G, LANE, 2 * S5_NS), lambda i: (i, S5_Q - 1, 0)),
                  blk(2 * S5_NS, LANE), blk(8, 2 * S5_NS),
                  pl.BlockSpec((1, G * LANE), lambda i: (0, i))],
        out_specs=[pl.BlockSpec((nb, G * LANE), lambda i: (0, i)), blk(nb, 2 * S5_NS)],
        out_shape=[jax.ShapeDtypeStruct((nb, E1), F32),
                   jax.ShapeDtypeStruct((S5_NBLK, nb, 2 * S5_NS), F32)],
        compiler_params=_cp("arbitrary"),
    )(u_gate, h0, wb, cpl, lam, dsk)


def _state_to_blocks(st):
    nb = st.shape[0]
    x = st.reshape(nb, S5_NBLK, S5_NS, 2)
    return jnp.transpose(x, (1, 0, 3, 2)).reshape(S5_NBLK, nb, 2 * S5_NS)


def _blocks_to_state(h):
    nb = h.shape[1]
    x = h.reshape(S5_NBLK, nb, 2, S5_NS)
    return jnp.transpose(x, (1, 0, 3, 2)).reshape(nb, C_GROUPS, C_STATE, 2)


def kernel(x_prompt, x_sample, state_a_ssm, state_a_conv, cache_b_kv, state_c, norm_w, final_norm_w, even_w_in, a_conv_w, a_conv_b, a_dt_bias, a_A_log, a_D, a_norm_w, even_w_out, odd_w_in, c_A_re, c_A_im, c_B_re, c_B_im, c_C_re, c_C_im, c_D, c_log_dt, c_glu_w, c_glu_b, odd_w_out):
    B, L, D = x_prompt.shape
    nS = x_sample.shape[0]
    M = B * L
    Q = S5_Q

    nw0 = norm_w[0][None, :]
    nw1 = norm_w[1][None, :]
    fnw = final_norm_w[None, :]
    cw, cb = a_conv_w[0], a_conv_b[0][None, :]
    pad_h = lambda v: jnp.pad(v, (0, LANE - A_HEADS))[None, :]
    dtb, alog = pad_h(a_dt_bias[0]), pad_h(a_A_log[0])
    dsk_a = jnp.repeat(a_D[0], A_HEAD_DIM)[None, :]
    gnw = a_norm_w[0][None, :]
    head_expand = np.kron(np.eye(A_HEADS, dtype=np.float32), np.ones((1, A_HEAD_DIM), np.float32))
    head_expand = jnp.asarray(np.pad(head_expand, ((0, LANE - A_HEADS), (0, 0)))).astype(BF16)
    slopes_np = np.asarray(2.0 ** (-8.0 * np.arange(1, B_HEADS + 1) / B_HEADS), np.float32)
    slopes = jnp.asarray(np.broadcast_to(slopes_np[:, None, None], (B_HEADS, 8, LANE)))
    dist = np.stack([dil * (LANE - np.arange(LANE)) for _, dil in B_PATTERNS]).astype(np.float32)
    step_bias = jnp.asarray((dist[:, :, None] * slopes_np[None, None, :])[..., None])

    rows8 = lambda p: jnp.broadcast_to(p.reshape(S5_NBLK, 1, S5_NS), (S5_NBLK, 8, S5_NS))
    lane_pad = lambda p: jnp.pad(p, ((0, 0), (0, 0), (0, LANE - p.shape[-1])))
    b_rows = lambda p: lane_pad(jnp.transpose(p, (0, 2, 1)).reshape(S5_NBLK, LANE, C_STATE))
    c_rows = lambda p: lane_pad(jnp.transpose(p, (0, 2, 1)).reshape(S5_NBLK, S5_NS, C_GROUP))
    ldt = jnp.repeat(c_log_dt[0], C_STATE).reshape(S5_NBLK, S5_NS)
    t_m, wb_m, cb_m, lam, lamq, cpl = _s5_prep(
        rows8(c_A_re[0]), rows8(c_A_im[0]), rows8(ldt), b_rows(c_B_re[0]), b_rows(c_B_im[0]),
        c_rows(c_C_re[0]), c_rows(c_C_im[0]))
    w0t = jnp.transpose(even_w_in[0])
    s1 = A_WIDTH + A_CONV_DIM
    s2 = s1 + A_HEADS
    w_a = jnp.concatenate([w0t[A_WIDTH:s1], w0t[:A_WIDTH], w0t[s2 + 6144:], w0t[s1:s2],
                           jnp.zeros((A_TILE - A_HEADS, D), F32)], axis=0).astype(BF16)
    w_qkv = w0t[s2:s2 + 6144].astype(BF16)
    dsk_c = c_D[0][None, :]
    glu_b = c_glu_b[0][None, :]

    xs = x_sample.reshape(nS, D)
    xp = x_prompt.reshape(M, D)

    p1 = _mm_norm(xp, nw0, w_a, 1024, A_TILE, keep_xn=True, cast=c_glu_w[0], wt=True, rider=xs)
    a_arr, a_s, w_glu = p1["o"], p1["os"], p1["cast_out"]
    qkv, kv_keep, w_in1, qkv_s = _proj_qkv(p1["xn"], w_qkv, B, L, 1024, odd_w_in[0], p1["xs_n"])
    col = lambda a, b: a_s[:, a:b]
    hv = lambda part: jnp.transpose(qkv_s[part * B_HEADS:(part + 1) * B_HEADS], (1, 0, 2))
    gate_s = col(5120, 7168).reshape(nS, B_HEADS, B_HEAD_DIM)
    kvn = jnp.stack([hv(1), hv(2)], axis=1)
    y_a, ssm_p, tail_p = _ssd(a_arr, B, L, cw, cb, dtb, alog, dsk_a, gnw)
    y_b, new_cache, w_out0, w_out1 = _attn(qkv, slopes, a_arr, cache_b_kv, kvn,
                                           even_w_out[0], odd_w_out[0])
    y_as, ssm_s, conv_s = _ssd_step(col(0, 3072)[:, None, :], state_a_conv[0],
                                    col(7168, 7296)[:, None, :], col(3072, 5120)[:, None, :],
                                    state_a_ssm[0].reshape(nS, A_WIDTH, A_STATE),
                                    cw, cb, dtb, alog, head_expand, dsk_a, gnw)
    y_bs = _attn_step(hv(0), hv(1), hv(2), gate_s, cache_b_kv, step_bias)
    hp1, hs1 = _outproj0(y_a, y_b, w_out0, xp, 1024,
                         (y_as.reshape(nS, A_WIDTH), y_bs.reshape(nS, A_WIDTH), xs))

    p3 = _mm_norm(hp1, nw1, w_in1, 1024, 1024, rider=hs1)
    u_gate, u_gate_s = p3["o"], p3["os"]
    g, hfin_p = _s5_main(u_gate, B, t_m, wb_m, cb_m, lamq, dsk_c)
    g_s, hnew_s = _s5_step(u_gate_s, _state_to_blocks(state_c[0]), wb_m, cpl, lam, dsk_c)
    y1, y1_s = _glu(g, w_glu, glu_b, u_gate, 1024, (g_s, u_gate_s))
    y_prompt, y_sample = _outproj1(y1, w_out1, hp1, fnw, 1024, (y1_s, hs1))
    y_prompt = y_prompt.reshape(B, L, D)
    y_sample = y_sample.reshape(nS, 1, D)

    return (y_prompt, y_sample,
            ssm_p.reshape(1, B, A_HEADS, A_HEAD_DIM, A_STATE),
            tail_p[:, 5:8, :][None],
            kv_keep[None],
            _blocks_to_state(hfin_p)[None],
            ssm_s.reshape(1, nS, A_HEADS, A_HEAD_DIM, A_STATE),
            conv_s[None],
            new_cache,
            _blocks_to_state(hnew_s)[None])
```

```python
import functools

import numpy as np
import jax
import jax.numpy as jnp
from jax import lax
from jax.experimental import pallas as pl
from jax.experimental.pallas import tpu as pltpu

F32 = jnp.float32
BF16 = jnp.bfloat16

D_MODEL = 2048
EPS = 1e-5
A_WIDTH = 2048
A_HEAD_DIM = 64
A_HEADS = 32
A_GROUPS = 4
A_STATE = 128
A_CONV_W = 4
A_CONV_DIM = 3072
SSD_CHUNK = 128
B_HEADS = 16
B_HEAD_DIM = 128
B_PATTERNS = ((128, 1), (512, 4), (2048, 16))
B_MAX_WINDOW = 2048
C_GROUP = 16
C_GROUPS = 256
C_STATE = 64
E1 = 4096

LANE = 128
S5_Q = 8
S5_BLK = 8
S5_NBLK = C_GROUPS // S5_BLK
S5_NS = S5_BLK * C_STATE
TN = 512
A_TILE = 1024
ATT_BLOCK = 2048
NEG = -1e30
ATT_SCALE = B_HEAD_DIM ** -0.5
LOG2E = 1.4426950408889634
VMEM_LIMIT = 56 * 1024 * 1024

NT_DIMS = (((1,), (1,)), ((), ()))


def _cp(*sem):
    return pltpu.CompilerParams(dimension_semantics=sem, vmem_limit_bytes=VMEM_LIMIT)


def _silu(x):
    return x * jax.nn.sigmoid(x)


def _softplus(x):
    return jnp.maximum(x, 0.0) + jnp.log1p(jnp.exp(-jnp.abs(x)))


def _split3(a):
    hi = a.astype(BF16)
    r = a - hi.astype(F32)
    mid = r.astype(BF16)
    lo = (r - mid.astype(F32)).astype(BF16)
    return hi, mid, lo


def _dot3_left(m, a):
    hi, mid, lo = _split3(a)
    d = lambda p: jnp.dot(m, p, preferred_element_type=F32)
    return (d(lo) + d(mid)) + d(hi)


def _dot3_right(a, m):
    hi, mid, lo = _split3(a)
    d = lambda p: jnp.dot(p, m, preferred_element_type=F32)
    return (d(lo) + d(mid)) + d(hi)


def _rms(x, w):
    ms = jnp.mean(x * x, axis=-1, keepdims=True)
    return (x * lax.rsqrt(ms + EPS)) * w


QKV_HEADS = 8
CAST_ROWS = 64


def _cast_specs(w, nsteps, step_of):
    rows, cols = w.shape
    per = CAST_ROWS
    while rows // per > nsteps:
        per *= 2
    nblk = rows // per
    assert nblk * per == rows, (w.shape, nsteps)
    spec = pl.BlockSpec((per, cols), lambda *ids: (jnp.minimum(step_of(*ids), nblk - 1), 0))
    return spec, jax.ShapeDtypeStruct(w.shape, BF16)


def _proj_qkv_kernel(rpb, nskip, x_ref, w_ref, src_ref, xs_ref, qkv_ref, kv_ref, dst_ref, qkvs_ref):
    i, j = pl.program_id(0), pl.program_id(1)
    dst_ref[...] = src_ref[...].astype(BF16)

    def heads(x, out_ref, kv_heads=None):
        acc = _dot_w(x.astype(BF16), w_ref[...], True)
        for hh in range(QKV_HEADS):
            out_ref[hh] = acc[:, hh * LANE:(hh + 1) * LANE]
            if kv_heads is not None:
                kv_ref[:, kv_heads + hh, :] = acc[:, hh * LANE:(hh + 1) * LANE]

    pl.when(i == 0)(lambda: heads(xs_ref[...], qkvs_ref))
    tiles_per = B_HEADS // QKV_HEADS
    in_window = (i % rpb >= nskip) & (j >= tiles_per)
    pl.when(jnp.logical_not(in_window))(lambda: heads(x_ref[...], qkv_ref))
    for part in range(tiles_per):
        pl.when(in_window & (j % tiles_per == part))(
            lambda part=part: heads(x_ref[...], qkv_ref, part * QKV_HEADS))


def _proj_qkv(xn, w, B, L, tm, cast, xs_n):
    M = B * L
    ms = xs_n.shape[0]
    tn = QKV_HEADS * LANE
    rpb = L // tm
    keep = min(B_MAX_WINDOW, L)
    nskip = (L - keep) // tm
    tiles_per = B_HEADS // QKV_HEADS
    nj = w.shape[0] // tn
    rcol = _rider_col(nj)

    def kv_map(i, j):
        ri = i % rpb
        return (i // rpb, jnp.maximum(ri - nskip, 0),
                jnp.where((ri < nskip) | (j < 2 * tiles_per), 0, 1), 0, 0)

    cast_spec, cast_shape = _cast_specs(cast, (M // tm) * nj, lambda i, j: i * nj + j)
    return pl.pallas_call(
        functools.partial(_proj_qkv_kernel, rpb, nskip),
        grid=(M // tm, nj),
        in_specs=[pl.BlockSpec((tm, D_MODEL), lambda i, j: (i, 0)),
                  pl.BlockSpec((tn, D_MODEL), lambda i, j: (j, 0)),
                  cast_spec,
                  pl.BlockSpec((ms, D_MODEL), lambda i, j: (0, 0))],
        out_specs=[pl.BlockSpec((None, QKV_HEADS, tm, LANE), lambda i, j: (i // rpb, j, i % rpb, 0)),
                   pl.BlockSpec((None, tm, None, B_HEADS, B_HEAD_DIM), kv_map),
                   cast_spec,
                   pl.BlockSpec((QKV_HEADS, ms, LANE), lambda i, j: (rcol(i, j), 0, 0))],
        out_shape=[jax.ShapeDtypeStruct((B, 3 * B_HEADS, L, B_HEAD_DIM), F32),
                   jax.ShapeDtypeStruct((B, keep, 2, B_HEADS, B_HEAD_DIM), F32),
                   cast_shape,
                   jax.ShapeDtypeStruct((3 * B_HEADS, ms, B_HEAD_DIM), F32)],
        compiler_params=_cp("arbitrary", "arbitrary"),
    )(xn, w, cast, xs_n)


def _dot_w(x, w, wt):
    if wt:
        return lax.dot_general(x, w, NT_DIMS, preferred_element_type=F32)
    return jnp.dot(x, w, preferred_element_type=F32)


def _rider_col(nj):
    return lambda i, j: jnp.where(i == 0, j, nj - 1)


def _mm_norm_kernel(names, wt, *refs):
    r = dict(zip(names, refs))
    j = pl.program_id(1)
    if "cast_src" in r:
        r["cast_out"][...] = r["cast_src"][...].astype(BF16)

    def rows(x_ref, xn_ref, o_ref):
        @pl.when(j == 0)
        def _():
            xn_ref[...] = _rms(x_ref[...], r["nw"][...]).astype(xn_ref.dtype)

        o_ref[...] = _dot_w(xn_ref[...].astype(BF16), r["w"][...], wt)

    rows(r["x"], r["xn"], r["o"])
    if "xs" in r:
        pl.when(pl.program_id(0) == 0)(lambda: rows(r["xs"], r["xs_n"], r["os"]))


def _mm_norm(x, nw, w, tm, tn=TN, keep_xn=False, cast=None, wt=False, rider=None):
    M = x.shape[0]
    N, K = w.shape if wt else w.shape[::-1]
    nj = N // tn
    ins = {"x": (x, pl.BlockSpec((tm, K), lambda i, j: (i, 0))),
           "nw": (nw, pl.BlockSpec((1, K), lambda i, j: (0, 0))),
           "w": (w, pl.BlockSpec((tn, K), lambda i, j: (j, 0)) if wt else
                 pl.BlockSpec((K, tn), lambda i, j: (0, j)))}
    outs = {"o": (jax.ShapeDtypeStruct((M, N), F32), pl.BlockSpec((tm, tn), lambda i, j: (i, j)))}
    scratch = {}
    xn_spec = pl.BlockSpec((tm, K), lambda i, j: (i, 0))
    if keep_xn:
        outs["xn"] = (jax.ShapeDtypeStruct((M, K), BF16), xn_spec)
    else:
        scratch["xn"] = pltpu.VMEM((tm, K), BF16)
    if cast is not None:
        spec, shape = _cast_specs(cast, (M // tm) * nj, lambda i, j: i * nj + j)
        ins["cast_src"] = (cast, spec)
        outs["cast_out"] = (shape, spec)
    if rider is not None:
        ms = rider.shape[0]
        rcol = _rider_col(nj)
        ins["xs"] = (rider, pl.BlockSpec((ms, K), lambda i, j: (0, 0)))
        outs["os"] = (jax.ShapeDtypeStruct((ms, N), F32),
                      pl.BlockSpec((ms, tn), lambda i, j: (0, rcol(i, j))))
        if keep_xn:
            outs["xs_n"] = (jax.ShapeDtypeStruct((ms, K), F32), pl.BlockSpec((ms, K), lambda i, j: (0, 0)))
        else:
            scratch["xs_n"] = pltpu.VMEM((ms, K), F32)
    out = pl.pallas_call(
        functools.partial(_mm_norm_kernel, tuple(ins) + tuple(outs) + tuple(scratch), wt),
        grid=(M // tm, nj),
        in_specs=[s for _, s in ins.values()],
        out_specs=[s for _, s in outs.values()],
        out_shape=[a for a, _ in outs.values()],
        scratch_shapes=list(scratch.values()),
        compiler_params=_cp("arbitrary", "arbitrary"),
    )(*[a for a, _ in ins.values()])
    return dict(zip(outs, out))


def _outproj0_kernel(ya_ref, yb_ref, wa_ref, wb_ref, res_ref, yas_ref, ybs_ref, ress_ref,
                     o_ref, os_ref):
    def rows(ya, yb, res, o):
        o[...] = (res[...] + jnp.dot(ya[...].astype(BF16), wa_ref[...], preferred_element_type=F32)
                  + jnp.dot(yb[...].astype(BF16), wb_ref[...], preferred_element_type=F32))

    rows(ya_ref, yb_ref, res_ref, o_ref)
    pl.when(pl.program_id(0) == 0)(lambda: rows(yas_ref, ybs_ref, ress_ref, os_ref))


def _outproj0(ya, yb, w, res, tm, rider):
    M, K = ya.shape
    ms = rider[0].shape[0]
    nj = D_MODEL // TN
    rcol = _rider_col(nj)
    row = pl.BlockSpec((tm, K), lambda i, j: (i, 0))
    srow = pl.BlockSpec((ms, K), lambda i, j: (0, 0))
    stile = pl.BlockSpec((ms, TN), lambda i, j: (0, rcol(i, j)))
    return pl.pallas_call(
        _outproj0_kernel,
        grid=(M // tm, nj),
        in_specs=[row, row,
                  pl.BlockSpec((K, TN), lambda i, j: (0, j)),
                  pl.BlockSpec((K, TN), lambda i, j: (1, j)),
                  pl.BlockSpec((tm, TN), lambda i, j: (i, j)), srow, srow, stile],
        out_specs=[pl.BlockSpec((tm, TN), lambda i, j: (i, j)), stile],
        out_shape=[jax.ShapeDtypeStruct((M, D_MODEL), F32), jax.ShapeDtypeStruct((ms, D_MODEL), F32)],
        compiler_params=_cp("arbitrary", "arbitrary"),
    )(ya, yb, w, w, res, *rider)


def _glu_kernel(g_ref, wa_ref, wb_ref, ba_ref, bb_ref, gate_ref, gs_ref, gates_ref, o_ref, os_ref):
    def rows(g_ref, gate_ref, o_ref):
        tm = g_ref.shape[0]
        hr = min(tm, 512)
        for r in range(tm // hr):
            rs = pl.ds(r * hr, hr)
            g = g_ref[rs, :].astype(BF16)
            ga = jnp.dot(g, wa_ref[...], preferred_element_type=F32) + ba_ref[...]
            gb = jnp.dot(g, wb_ref[...], preferred_element_type=F32) + bb_ref[...]
            o_ref[rs, :] = (ga * jax.nn.sigmoid(gb) * _silu(gate_ref[rs, :])).astype(o_ref.dtype)

    rows(g_ref, gate_ref, o_ref)
    pl.when(pl.program_id(0) == 0)(lambda: rows(gs_ref, gates_ref, os_ref))


def _glu(g, w, bias, u_gate, tm, rider):
    M = g.shape[0]
    gs, ugs = rider
    ms = gs.shape[0]
    nb = E1 // TN
    rcol = _rider_col(nb)
    return pl.pallas_call(
        _glu_kernel,
        grid=(M // tm, nb),
        in_specs=[pl.BlockSpec((tm, E1), lambda i, j: (i, 0)),
                  pl.BlockSpec((E1, TN), lambda i, j: (0, j)),
                  pl.BlockSpec((E1, TN), lambda i, j: (0, j + nb)),
                  pl.BlockSpec((1, TN), lambda i, j: (0, j)),
                  pl.BlockSpec((1, TN), lambda i, j: (0, j + nb)),
                  pl.BlockSpec((tm, TN), lambda i, j: (i, j + nb)),
                  pl.BlockSpec((ms, E1), lambda i, j: (0, 0)),
                  pl.BlockSpec((ms, TN), lambda i, j: (0, nb + rcol(i, j)))],
        out_specs=[pl.BlockSpec((tm, TN), lambda i, j: (i, j)),
                   pl.BlockSpec((ms, TN), lambda i, j: (0, rcol(i, j)))],
        out_shape=[jax.ShapeDtypeStruct((M, E1), BF16), jax.ShapeDtypeStruct((ms, E1), F32)],
        compiler_params=_cp("arbitrary", "arbitrary"),
    )(g, w, w, bias, bias, u_gate, gs, ugs)


def _outproj1_kernel(y_ref, w_ref, res_ref, fnw_ref, ys_ref, ress_ref, o_ref, os_ref):
    j = pl.program_id(1)
    nj = D_MODEL // TN

    def rows(y_ref, res_ref, o_ref):
        h = res_ref[...] + jnp.dot(y_ref[...].astype(BF16), w_ref[...], preferred_element_type=F32)
        for jj in range(nj):
            @pl.when(j == jj)
            def _(jj=jj):
                o_ref[:, jj * TN:(jj + 1) * TN] = h

        @pl.when(j == nj - 1)
        def _():
            o_ref[...] = _rms(o_ref[...], fnw_ref[...])

    rows(y_ref, res_ref, o_ref)
    pl.when(pl.program_id(0) == 0)(lambda: rows(ys_ref, ress_ref, os_ref))


def _outproj1(y, w, res, fnw, tm, rider):
    M, K = y.shape
    ys, ress = rider
    ms = ys.shape[0]
    nj = D_MODEL // TN
    rcol = _rider_col(nj)
    return pl.pallas_call(
        _outproj1_kernel,
        grid=(M // tm, nj),
        in_specs=[pl.BlockSpec((tm, K), lambda i, j: (i, 0)),
                  pl.BlockSpec((K, TN), lambda i, j: (0, j)),
                  pl.BlockSpec((tm, TN), lambda i, j: (i, j)),
                  pl.BlockSpec((1, D_MODEL), lambda i, j: (0, 0)),
                  pl.BlockSpec((ms, K), lambda i, j: (0, 0)),
                  pl.BlockSpec((ms, TN), lambda i, j: (0, rcol(i, j)))],
        out_specs=[pl.BlockSpec((tm, D_MODEL), lambda i, j: (i, 0)),
                   pl.BlockSpec((ms, D_MODEL), lambda i, j: (0, 0))],
        out_shape=[jax.ShapeDtypeStruct((M, D_MODEL), F32), jax.ShapeDtypeStruct((ms, D_MODEL), F32)],
        compiler_params=_cp("arbitrary", "arbitrary"),
    )(y, w, res, fnw, ys, ress)


def _dt_raw(xn, wdt_ref):
    return lax.dot_general(xn.astype(BF16), wdt_ref[...], NT_DIMS, preferred_element_type=F32)


def _ssd_kernel(xbc_ref, xn_ref, wdt_ref, z0_ref, z1_ref, cw_ref, cb_ref, dtb_ref, alog_ref,
                dsk_ref, gnw_ref, y_ref, st_ref, tail_ref, h_scr, ext_scr, y_scr):
    c = pl.program_id(1)
    Q = SSD_CHUNK

    @pl.when(c == 0)
    def _():
        h_scr[...] = jnp.zeros_like(h_scr)
        ext_scr[0:8, :] = jnp.zeros((8, A_CONV_DIM), F32)

    x_raw = xbc_ref[...]
    ext_scr[8:8 + Q, :] = x_raw
    conv = cb_ref[...] + cw_ref[3:4, :] * x_raw
    for k in (1, 2, 3):
        conv = conv + cw_ref[3 - k:4 - k, :] * ext_scr[pl.ds(8 - k, Q), :]
    ext_scr[0:8, :] = x_raw[Q - 8:Q, :]
    xc = _silu(conv)

    dt = _softplus(_dt_raw(xn_ref[...], wdt_ref) + dtb_ref[...])
    a = dt * (-jnp.exp(alog_ref[...]))
    qi = lax.broadcasted_iota(jnp.int32, (Q, Q), 0)
    si = lax.broadcasted_iota(jnp.int32, (Q, Q), 1)
    tri = qi >= si
    a_cum = _dot3_left(tri.astype(BF16), a)
    a_cum_t = a_cum.T
    lo_half = si < A_HEAD_DIM

    for g in range(A_GROUPS):
        bg = xc[:, A_WIDTH + g * A_STATE:A_WIDTH + (g + 1) * A_STATE].astype(BF16)
        cg = xc[:, A_WIDTH + (A_GROUPS + g) * A_STATE:A_WIDTH + (A_GROUPS + g + 1) * A_STATE].astype(BF16)
        cbm = lax.dot_general(cg, bg, NT_DIMS, preferred_element_type=F32)
        hprev = h_scr[g * 512:(g + 1) * 512, :]
        yoff = lax.dot_general(cg, hprev.astype(BF16), NT_DIMS, preferred_element_type=F32)
        for jj in range(4):
            p = g * 4 + jj
            h0, h1 = 2 * p, 2 * p + 1
            dtp = jnp.where(lo_half, dt[:, h0:h0 + 1], dt[:, h1:h1 + 1])
            ap = jnp.where(lo_half, a_cum[:, h0:h0 + 1], a_cum[:, h1:h1 + 1])
            x_p = xc[:, p * LANE:(p + 1) * LANE]
            xdt = x_p * dtp
            g0 = cbm * jnp.exp(jnp.where(tri, a_cum[:, h0:h0 + 1] - a_cum_t[h0:h0 + 1, :], NEG))
            g1 = cbm * jnp.exp(jnp.where(tri, a_cum[:, h1:h1 + 1] - a_cum_t[h1:h1 + 1, :], NEG))
            lhs = jnp.concatenate([g0, g1], axis=1).astype(BF16)
            rhs = jnp.concatenate([jnp.where(lo_half, xdt, 0.0), jnp.where(lo_half, 0.0, xdt)],
                                  axis=0).astype(BF16)
            ydiag = jnp.dot(lhs, rhs, preferred_element_type=F32)
            y_scr[:, p * LANE:(p + 1) * LANE] = (
                ydiag + yoff[:, jj * LANE:(jj + 1) * LANE] * jnp.exp(ap)
                + dsk_ref[:, p * LANE:(p + 1) * LANE] * x_p)
            xd = xdt * jnp.exp(ap[Q - 1:Q, :] - ap)
            st = jnp.dot(xd.T.astype(BF16), bg, preferred_element_type=F32)
            cd = jnp.exp(jnp.where(qi < A_HEAD_DIM, a_cum_t[h0:h0 + 1, Q - 1:Q],
                                   a_cum_t[h1:h1 + 1, Q - 1:Q]))
            h_scr[p * LANE:(p + 1) * LANE, :] = cd * hprev[jj * LANE:(jj + 1) * LANE, :] + st

    z = jnp.concatenate([z0_ref[...], z1_ref[...]], axis=1)
    y_ref[...] = _rms(y_scr[...] * _silu(z), gnw_ref[...]).astype(y_ref.dtype)

    @pl.when(c == pl.num_programs(1) - 1)
    def _():
        st_ref[...] = h_scr[...]
        tail_ref[...] = x_raw[Q - 8:Q, :]


def _ssd(a_arr, xn, wdt, B, L, cw, cb, dtb, alog, dsk, gnw):
    nc = L // SSD_CHUNK
    Q = SSD_CHUNK
    row = lambda b, c: b * nc + c
    full = lambda shp: pl.BlockSpec(shp, lambda b, c: (0, 0))
    return pl.pallas_call(
        _ssd_kernel,
        grid=(B, nc),
        in_specs=[pl.BlockSpec((Q, A_CONV_DIM), lambda b, c: (row(b, c), 0)),
                  pl.BlockSpec((Q, D_MODEL), lambda b, c: (row(b, c), 0)),
                  full((LANE, D_MODEL)),
                  pl.BlockSpec((Q, A_WIDTH // 2), lambda b, c: (row(b, c), 3)),
                  pl.BlockSpec((Q, A_WIDTH // 2), lambda b, c: (row(b, c), 4)),
                  full((A_CONV_W, A_CONV_DIM)), full((1, A_CONV_DIM)), full((1, LANE)),
                  full((1, LANE)), full((1, A_WIDTH)), full((1, A_WIDTH))],
        out_specs=[pl.BlockSpec((Q, A_WIDTH), lambda b, c: (row(b, c), 0)),
                   pl.BlockSpec((None, A_WIDTH, A_STATE), lambda b, c: (b, 0, 0)),
                   pl.BlockSpec((None, 8, A_CONV_DIM), lambda b, c: (b, 0, 0))],
        out_shape=[jax.ShapeDtypeStruct((B * L, A_WIDTH), BF16),
                   jax.ShapeDtypeStruct((B, A_WIDTH, A_STATE), F32),
                   jax.ShapeDtypeStruct((B, 8, A_CONV_DIM), F32)],
        scratch_shapes=[pltpu.VMEM((A_WIDTH, A_STATE), F32), pltpu.VMEM((8 + Q, A_CONV_DIM), F32),
                        pltpu.VMEM((Q, A_WIDTH), F32)],
        compiler_params=_cp("arbitrary", "arbitrary"),
    )(a_arr, xn, wdt, a_arr, a_arr, cw, cb, dtb, alog, dsk, gnw)


def _ssd_step_kernel(xbc_ref, cs_ref, xn_ref, wdt_ref, z_ref, h0_ref, cw_ref, cb_ref, dtb_ref,
                     alog_ref, e_ref, dsk_ref, gnw_ref, y_ref, hn_ref, nc_ref):
    x_raw = xbc_ref[...]
    cs = cs_ref[...]
    conv = (cb_ref[...] + cw_ref[0:1, :] * cs[0:1, :] + cw_ref[1:2, :] * cs[1:2, :]
            + cw_ref[2:3, :] * cs[2:3, :] + cw_ref[3:4, :] * x_raw)
    xc = _silu(conv)
    nc_ref[0:1, :] = cs[1:2, :]
    nc_ref[1:2, :] = cs[2:3, :]
    nc_ref[2:3, :] = x_raw

    dt_raw = _dt_raw(jnp.broadcast_to(xn_ref[...], (8, D_MODEL)), wdt_ref)[0:1, :]
    dt = _softplus(dt_raw + dtb_ref[...])
    da = dt * (-jnp.exp(alog_ref[...]))
    e = e_ref[...]
    dt_e = _dot3_right(jnp.broadcast_to(dt, (8, LANE)), e)[0:1, :]
    da_e = _dot3_right(jnp.broadcast_to(da, (8, LANE)), e)[0:1, :]
    xs = xc[:, :A_WIDTH]
    sub = lax.broadcasted_iota(jnp.int32, (8, 1), 0)
    rows = jnp.where(sub == 0, xs * dt_e, jnp.where(sub == 1, jnp.exp(da_e), 0.0))
    cols = jnp.concatenate([rows, jnp.zeros((LANE - 8, A_WIDTH), F32)], axis=0).T
    xdt_col = cols[:, 0:1]
    dec_col = cols[:, 1:2]
    ys = []
    for g in range(A_GROUPS):
        bg = xc[:, A_WIDTH + g * A_STATE:A_WIDTH + (g + 1) * A_STATE]
        cg = xc[:, A_WIDTH + (A_GROUPS + g) * A_STATE:A_WIDTH + (A_GROUPS + g + 1) * A_STATE]
        sl = slice(g * 512, (g + 1) * 512)
        hn = dec_col[sl, :] * h0_ref[sl, :] + xdt_col[sl, :] * bg
        hn_ref[sl, :] = hn
        c8 = jnp.broadcast_to(cg, (8, A_STATE)).astype(BF16)
        ys.append(lax.dot_general(c8, hn.astype(BF16), NT_DIMS, preferred_element_type=F32)[0:1, :])
    y = jnp.concatenate(ys, axis=1) + dsk_ref[...] * xs
    y_ref[...] = _rms(y * _silu(z_ref[...]), gnw_ref[...])


def _ssd_step(xbc, cs, xn, wdt, z, h0, cw, cb, dtb, alog, e, dsk, gnw):
    nb = xbc.shape[0]
    per_b = lambda n: pl.BlockSpec((None, 1, n), lambda b: (b, 0, 0))
    full = lambda shp: pl.BlockSpec(shp, lambda b: (0, 0))
    return pl.pallas_call(
        _ssd_step_kernel,
        grid=(nb,),
        in_specs=[per_b(A_CONV_DIM), pl.BlockSpec((None, 3, A_CONV_DIM), lambda b: (b, 0, 0)),
                  per_b(D_MODEL), full((LANE, D_MODEL)), per_b(A_WIDTH),
                  pl.BlockSpec((None, A_WIDTH, A_STATE), lambda b: (b, 0, 0)),
                  full((A_CONV_W, A_CONV_DIM)), full((1, A_CONV_DIM)), full((1, LANE)),
                  full((1, LANE)), full((LANE, A_WIDTH)), full((1, A_WIDTH)), full((1, A_WIDTH))],
        out_specs=[per_b(A_WIDTH),
                   pl.BlockSpec((None, A_WIDTH, A_STATE), lambda b: (b, 0, 0)),
                   pl.BlockSpec((None, 3, A_CONV_DIM), lambda b: (b, 0, 0))],
        out_shape=[jax.ShapeDtypeStruct((nb, 1, A_WIDTH), F32),
                   jax.ShapeDtypeStruct((nb, A_WIDTH, A_STATE), F32),
                   jax.ShapeDtypeStruct((nb, 3, A_CONV_DIM), F32)],
        compiler_params=_cp("arbitrary"),
    )(xbc, cs, xn, wdt, z, h0, cw, cb, dtb, alog, e, dsk, gnw)


def _bias_tile(slope, dil):
    i = lax.broadcasted_iota(jnp.int32, (LANE, 2 * LANE), 0)
    c = lax.broadcasted_iota(jnp.int32, (LANE, 2 * LANE), 1)
    j = LANE + i - c
    valid = (j >= 0) & (j <= LANE)
    return jnp.where(valid, -(slope * (dil * j).astype(F32)), NEG), c


def _attn_tile(q, kcat, vcat, bias2):
    s = lax.dot_general(q, kcat, NT_DIMS, preferred_element_type=F32) * (ATT_SCALE * LOG2E) + bias2
    m = jnp.max(s, axis=1, keepdims=True)
    p = jnp.exp2(s - m)
    l = jnp.sum(p, axis=1, keepdims=True)
    acc = jnp.dot(p.astype(BF16), vcat, preferred_element_type=F32)
    return m, l, acc


def _rows(start, stride):
    return pl.ds(start, LANE, stride=stride) if stride > 1 else pl.ds(start, LANE)


def _shift_cache_block(step, steps_per_seq, x_ref, nxt_ref, kvn_ref, o_ref):
    R = x_ref.shape[0]
    o_ref[pl.ds(0, R - 1)] = x_ref[pl.ds(1, R - 1)]
    last = step % steps_per_seq == steps_per_seq - 1
    o_ref[R - 1] = jnp.where(last, kvn_ref[...], nxt_ref[0])


def _attn_kernel(steps_per_seq, slope_ref, q_ref, k_ref, v_ref, kp_ref, vp_ref, gate_ref,
                 x_ref, nxt_ref, kvn_ref, wa_ref, wb_ref, y_ref, oc_ref, wa_out, wb_out, *scr):
    step = ((pl.program_id(0) * pl.num_programs(1) + pl.program_id(1)) * pl.num_programs(2)
            + pl.program_id(2))
    _shift_cache_block(step, steps_per_seq, x_ref, nxt_ref, kvn_ref, oc_ref)
    wa_out[...] = wa_ref[...].astype(BF16)
    wb_out[...] = wb_ref[...].astype(BF16)
    first_block = pl.program_id(2) == 0
    slope = slope_ref[0:1, 0:1]
    npat = len(B_PATTERNS)
    m_scr, l_scr, acc_scr = scr[:npat], scr[npat:2 * npat], scr[2 * npat:]
    for pat, (_, dil) in enumerate(B_PATTERNS):
        bias_full, c = _bias_tile(slope, dil)
        bias_full = jnp.where(bias_full > NEG, bias_full * LOG2E, NEG)
        bias0 = jnp.where((c >= LANE) | jnp.logical_not(first_block), bias_full, NEG)
        nsub = ATT_BLOCK // dil // LANE
        for r in range(dil):
            kc = vc = None
            for sub in range(nsub):
                rows = _rows(sub * LANE * dil + r, dil)
                if sub == 0:
                    prow = _rows((nsub - 1) * LANE * dil + r, dil)
                    kp, vp, bias = kp_ref[prow, :], vp_ref[prow, :], bias0
                else:
                    kp, vp, bias = kc, vc, bias_full
                kc, vc = k_ref[rows, :], v_ref[rows, :]
                kcat = jnp.concatenate([kp, kc], axis=0).astype(BF16)
                vcat = jnp.concatenate([vp, vc], axis=0).astype(BF16)
                m, l, acc = _attn_tile(q_ref[rows, :].astype(BF16), kcat, vcat, bias)
                m_scr[pat][rows, :] = jnp.broadcast_to(m, (LANE, LANE))
                l_scr[pat][rows, :] = jnp.broadcast_to(l, (LANE, LANE))
                acc_scr[pat][rows, :] = acc
    for sub in range(ATT_BLOCK // LANE):
        rs = pl.ds(sub * LANE, LANE)
        ms = [m_scr[p][rs, :] for p in range(npat)]
        m = functools.reduce(jnp.maximum, ms)
        l = acc = 0.0
        for p in range(npat):
            a = jnp.exp2(ms[p] - m)
            l = l + a * l_scr[p][rs, :]
            acc = acc + a * acc_scr[p][rs, :]
        y_ref[rs, :] = ((acc / l) * _silu(gate_ref[rs, :])).astype(y_ref.dtype)


def _attn(qkv, slopes, a_arr, cache6, kvn, wa, wb):
    B, H3, L, dh = qkv.shape
    H = H3 // 3
    R = ATT_BLOCK
    nblk = L // R
    cur = lambda part: pl.BlockSpec((None, None, R, dh), lambda b, h, i: (b, part * H + h, i, 0))
    prev = lambda part: pl.BlockSpec((None, None, R, dh),
                                     lambda b, h, i: (b, part * H + h, jnp.maximum(i - 1, 0), 0))
    slope = pl.BlockSpec((None, 8, LANE), lambda b, h, i: (h, 0, 0))
    gate = pl.BlockSpec((R, LANE), lambda b, h, i: (b * nblk + i, 10 * TN // LANE + h))

    nb, W = cache6.shape[1], cache6.shape[2]
    steps = B * H * nblk
    per = steps // nb
    rows = W // per
    assert per * nb == steps and rows * per == W, (steps, nb, W)
    tail = (2, B_HEADS, B_HEAD_DIM)
    step = lambda b, h, i: (b * H + h) * nblk + i
    cblk = pl.BlockSpec((None, None, rows) + tail,
                        lambda b, h, i: (0, step(b, h, i) // per, step(b, h, i) % per, 0, 0, 0))
    cnext = pl.BlockSpec((None, None, 1) + tail,
                         lambda b, h, i: (0, step(b, h, i) // per,
                                          jnp.minimum((step(b, h, i) % per + 1) * rows, W - 1), 0, 0, 0))
    cnew = pl.BlockSpec((None,) + tail, lambda b, h, i: (step(b, h, i) // per, 0, 0, 0))
    wa_spec, wa_shape = _cast_specs(wa, steps, step)
    wb_spec, wb_shape = _cast_specs(wb, steps, step)
    return pl.pallas_call(
        functools.partial(_attn_kernel, per),
        grid=(B, H, nblk),
        in_specs=[slope, cur(0), cur(1), cur(2), prev(1), prev(2), gate, cblk, cnext, cnew,
                  wa_spec, wb_spec],
        out_specs=[pl.BlockSpec((R, LANE), lambda b, h, i: (b * nblk + i, h)), cblk,
                   wa_spec, wb_spec],
        out_shape=[jax.ShapeDtypeStruct((B * L, H * dh), BF16),
                   jax.ShapeDtypeStruct(cache6.shape, F32), wa_shape, wb_shape],
        scratch_shapes=[pltpu.VMEM((R, LANE), F32)] * (3 * len(B_PATTERNS)),
        compiler_params=_cp("arbitrary", "arbitrary", "arbitrary"),
    )(slopes, qkv, qkv, qkv, qkv, qkv, a_arr, cache6, cache6, kvn, wa, wb)


def _attn_step_kernel(bias_ref, q_ref, kn_ref, vn_ref, gate_ref, c1_ref, c4_ref, c16_ref, y_ref):
    q, kn, vn = q_ref[...], kn_ref[...], vn_ref[...]
    s_self = jnp.sum(q * kn, axis=-1, keepdims=True) * ATT_SCALE
    blocks = (c1_ref, c4_ref, c16_ref)
    scores = []
    for p, c_ref in enumerate(blocks):
        kblk = c_ref[:, 0]
        scores.append(jnp.sum(kblk * q[None], axis=-1, keepdims=True) * ATT_SCALE - bias_ref[p])
    m = s_self
    for s in scores:
        m = jnp.maximum(m, jnp.max(s, axis=0))
    p_self = float(len(B_PATTERNS)) * jnp.exp(s_self - m)
    l = p_self
    o = p_self * vn
    for s, c_ref in zip(scores, blocks):
        p = jnp.exp(s - m[None])
        l = l + jnp.sum(p, axis=0)
        o = o + jnp.sum(p * c_ref[:, 1], axis=0)
    y_ref[...] = (o / l) * _silu(gate_ref[...])


def _attn_step(q, kn, vn, gate, cache6, bias):
    nb = q.shape[0]
    W = B_MAX_WINDOW
    hspec = pl.BlockSpec((None, B_HEADS, B_HEAD_DIM), lambda b: (b, 0, 0))
    tail = (2, B_HEADS, B_HEAD_DIM)
    views, specs = [], []
    for _, dil in B_PATTERNS:
        if dil == 1:
            views.append(cache6)
            specs.append(pl.BlockSpec((None, None, LANE) + tail,
                                      lambda b: (0, b, W // LANE - 1, 0, 0, 0)))
        else:
            views.append(cache6.reshape((1, nb, W // dil, dil) + tail))
            specs.append(pl.BlockSpec((None, None, LANE, None) + tail,
                                      lambda b, dil=dil: (0, b, W // dil // LANE - 1, 0, 0, 0, 0)))
    return pl.pallas_call(
        _attn_step_kernel,
        grid=(nb,),
        in_specs=[pl.BlockSpec(bias.shape, lambda b: (0, 0, 0, 0)), hspec, hspec, hspec, hspec]
                 + specs,
        out_specs=hspec,
        out_shape=jax.ShapeDtypeStruct((nb, B_HEADS, B_HEAD_DIM), F32),
        compiler_params=_cp("arbitrary"),
    )(bias, q, kn, vn, gate, *views)


def _cmul(ar, ai, br, bi):
    return ar * br - ai * bi, ar * bi + ai * br


def _blockdiag_rows(x):
    hi = pltpu.roll(x, C_STATE, 1)
    g = lax.broadcasted_iota(jnp.int32, (LANE, LANE), 0) // C_GROUP
    return jnp.concatenate([jnp.where(g == 2 * b, x, jnp.where(g == 2 * b + 1, hi, 0.0))
                            for b in range(S5_NS // LANE)], axis=1)


def _blockdiag_cols(x):
    piece = lambda g: x[g * C_STATE:(g + 1) * C_STATE, :]
    return jnp.concatenate([piece(0)] + [pltpu.roll(piece(g), C_GROUP * g, 1)
                                         for g in range(1, S5_BLK)], axis=0)


def _s5_prep_kernel(are_ref, aim_ref, ldt_ref, bre_ref, bim_ref, ctre_ref, ctim_ref,
                    t_ref, wb_ref, cb_ref, lam_ref, lamq_ref, cpl_ref):
    Q = S5_Q
    are, aim = are_ref[...], aim_ref[...]
    dt = jnp.exp(ldt_ref[...])
    mag = jnp.exp(dt * are)
    l_re = mag * jnp.cos(dt * aim)
    l_im = mag * jnp.sin(dt * aim)
    den = are * are + aim * aim
    f_re = ((l_re - 1.0) * are + l_im * aim) / den
    f_im = (l_im * are - (l_re - 1.0) * aim) / den
    lr, li = l_re[0:1, :], l_im[0:1, :]
    bb_re, bb_im = _cmul(f_re[0:1, :], f_im[0:1, :], _blockdiag_rows(bre_ref[...]),
                         _blockdiag_rows(bim_ref[...]))
    ctre, ctim = _blockdiag_cols(ctre_ref[...]), _blockdiag_cols(ctim_ref[...])
    cplain = jnp.concatenate([ctre, -ctim], axis=0).astype(BF16)
    cpl_ref[...] = cplain
    stk = jnp.concatenate([l_re, l_im, jnp.zeros((LANE - 16, S5_NS), F32)], axis=0).T
    lc_re, lc_im = stk[:, 0:1], stk[:, 8:9]
    p_re, p_im = jnp.ones((1, S5_NS), F32), jnp.zeros((1, S5_NS), F32)
    pc_re, pc_im = lc_re, lc_im
    dts = []
    for tau in range(Q):
        e_re, e_im = _cmul(bb_re, bb_im, p_re, p_im)
        wp = jnp.concatenate([e_re, e_im], axis=1).astype(BF16)
        s = Q - 1 - tau
        wb_ref[s * LANE:(s + 1) * LANE, :] = wp
        dts.append(jnp.dot(wp, cplain, preferred_element_type=F32).astype(BF16))
        c_re, c_im = _cmul(ctre, ctim, pc_re, pc_im)
        cb_ref[:, tau * LANE:(tau + 1) * LANE] = jnp.concatenate([c_re, -c_im], axis=0).astype(BF16)
        p_re, p_im = _cmul(p_re, p_im, lr, li)
        pc_re, pc_im = _cmul(pc_re, pc_im, lc_re, lc_im)
    zero = jnp.zeros((LANE, LANE), BF16)
    for s in range(Q):
        for q in range(Q):
            t_ref[s * LANE:(s + 1) * LANE, q * LANE:(q + 1) * LANE] = dts[q - s] if q >= s else zero
    lam_ref[...] = jnp.concatenate([l_re, l_im], axis=1)
    sub = lax.broadcasted_iota(jnp.int32, (8, 1), 0)
    q_re, q_im = p_re, p_im
    tab_re, tab_im = jnp.zeros((8, S5_NS), F32), jnp.zeros((8, S5_NS), F32)
    for r in range(8):
        tab_re = jnp.where(sub == r, q_re, tab_re)
        tab_im = jnp.where(sub == r, q_im, tab_im)
        q_re, q_im = _cmul(q_re, q_im, p_re, p_im)
    lamq_ref[...] = jnp.concatenate([tab_re, tab_im], axis=1)


def _s5_prep(are, aim, ldt, bre, bim, ctre, ctim):
    Q = S5_Q
    row = pl.BlockSpec((None, 8, S5_NS), lambda i: (i, 0, 0))
    blk = lambda a, b: pl.BlockSpec((None, a, b), lambda i: (i, 0, 0))
    return pl.pallas_call(
        _s5_prep_kernel,
        grid=(S5_NBLK,),
        in_specs=[row, row, row, blk(LANE, LANE), blk(LANE, LANE), blk(S5_NS, LANE), blk(S5_NS, LANE)],
        out_specs=[blk(Q * LANE, Q * LANE), blk(Q * LANE, 2 * S5_NS), blk(2 * S5_NS, Q * LANE),
                   blk(8, 2 * S5_NS), blk(8, 2 * S5_NS), blk(2 * S5_NS, LANE)],
        out_shape=[jax.ShapeDtypeStruct((S5_NBLK, Q * LANE, Q * LANE), BF16),
                   jax.ShapeDtypeStruct((S5_NBLK, Q * LANE, 2 * S5_NS), BF16),
                   jax.ShapeDtypeStruct((S5_NBLK, 2 * S5_NS, Q * LANE), BF16),
                   jax.ShapeDtypeStruct((S5_NBLK, 8, 2 * S5_NS), F32),
                   jax.ShapeDtypeStruct((S5_NBLK, 8, 2 * S5_NS), F32),
                   jax.ShapeDtypeStruct((S5_NBLK, 2 * S5_NS, LANE), BF16)],
        compiler_params=_cp("arbitrary"),
    )(are, aim, ldt, bre, bim, ctre, ctim)


def _s5_main_kernel(nb, u_ref, t_ref, wb_ref, cb_ref, lamq_ref, dsk_ref, g_ref, hfin_ref,
                    lhs_scr, s_scr, hp_scr, gs_scr):
    Q = S5_Q
    NS = S5_NS
    nrow = u_ref.shape[0] // Q
    nch = nrow // nb
    for s in range(Q):
        lhs_scr[:, s * LANE:(s + 1) * LANE] = u_ref[pl.ds(s, nrow, stride=Q), :].astype(BF16)
    tab = lamq_ref[...]
    tab_re, tab_im = tab[:, :NS], tab[:, NS:]
    sub = lax.broadcasted_iota(jnp.int32, (8, 1), 0)

    def scan_tile(row0, h_re, h_im):
        st = s_scr[pl.ds(row0, 8), :]
        x_re, x_im = st[:, :NS], st[:, NS:]
        for sh in (1, 2, 4):
            y_re = jnp.where(sub >= sh, pltpu.roll(x_re, sh, 0), 0.0)
            y_im = jnp.where(sub >= sh, pltpu.roll(x_im, sh, 0), 0.0)
            d_re, d_im = _cmul(tab_re[sh - 1:sh, :], tab_im[sh - 1:sh, :], y_re, y_im)
            x_re, x_im = x_re + d_re, x_im + d_im
        c_re, c_im = _cmul(tab_re, tab_im, h_re, h_im)
        x_re, x_im = x_re + c_re, x_im + c_im
        p_re = jnp.where(sub == 0, h_re, pltpu.roll(x_re, 1, 0))
        p_im = jnp.where(sub == 0, h_im, pltpu.roll(x_im, 1, 0))
        hp_scr[pl.ds(row0, 8), :] = jnp.concatenate([p_re, p_im], axis=1)
        return x_re[7:8, :], x_im[7:8, :]

    lhs = lhs_scr[...]
    s_scr[...] = jnp.dot(lhs, wb_ref[...], preferred_element_type=F32)
    state = [(jnp.zeros((1, NS), F32),) * 2] * nb
    for t in range(nch // 8):
        state = [scan_tile(b * nch + t * 8, *state[b]) for b in range(nb)]
    for b in range(nb):
        hfin_ref[b:b + 1, :] = jnp.concatenate(state[b], axis=1)
    y = (jnp.dot(lhs, t_ref[...], preferred_element_type=F32)
         + jnp.dot(hp_scr[...].astype(BF16), cb_ref[...], preferred_element_type=F32))
    for q in range(Q):
        rows = pl.ds(q, nrow, stride=Q)
        yq = y[:, q * LANE:(q + 1) * LANE] + dsk_ref[...] * u_ref[rows, :]
        gs_scr[rows, :] = jax.nn.gelu(yq)
    g_ref[...] = gs_scr[...].astype(g_ref.dtype)


def _s5_main(u_gate, nb, t, wb, cb, lamq, dsk):
    Q = S5_Q
    M = u_gate.shape[0]
    nrow = M // Q
    blk = lambda a, b: pl.BlockSpec((None, a, b), lambda i: (i, 0, 0))
    return pl.pallas_call(
        functools.partial(_s5_main_kernel, nb),
        grid=(S5_NBLK,),
        in_specs=[pl.BlockSpec((M, LANE), lambda i: (0, i)),
                  blk(Q * LANE, Q * LANE), blk(Q * LANE, 2 * S5_NS), blk(2 * S5_NS, Q * LANE),
                  blk(8, 2 * S5_NS), pl.BlockSpec((1, LANE), lambda i: (0, i))],
        out_specs=[pl.BlockSpec((M, LANE), lambda i: (0, i)), blk(nb, 2 * S5_NS)],
        out_shape=[jax.ShapeDtypeStruct((M, E1), BF16),
                   jax.ShapeDtypeStruct((S5_NBLK, nb, 2 * S5_NS), F32)],
        scratch_shapes=[pltpu.VMEM((nrow, Q * LANE), BF16), pltpu.VMEM((nrow, 2 * S5_NS), F32),
                        pltpu.VMEM((nrow, 2 * S5_NS), F32), pltpu.VMEM((M, LANE), F32)],
        compiler_params=_cp("arbitrary"),
    )(u_gate, t, wb, cb, lamq, dsk)


S5_STEP_BLOCKS = 4


def _s5_step_kernel(u_ref, h0_ref, bb_ref, cpl_ref, lam_ref, dsk_ref, g_ref, hn_ref):
    NS = S5_NS
    for k in range(S5_STEP_BLOCKS):
        cs = slice(k * LANE, (k + 1) * LANE)
        u = u_ref[:, cs]
        s = jnp.dot(u.astype(BF16), bb_ref[k], preferred_element_type=F32)
        lam = lam_ref[k, 0:1, :]
        h0 = h0_ref[k]
        n_re, n_im = _cmul(lam[:, :NS], lam[:, NS:], h0[:, :NS], h0[:, NS:])
        hn = jnp.concatenate([n_re + s[:, :NS], n_im + s[:, NS:]], axis=1)
        hn_ref[k] = hn
        y = jnp.dot(hn.astype(BF16), cpl_ref[k], preferred_element_type=F32) + dsk_ref[:, cs] * u
        g_ref[:, cs] = jax.nn.gelu(y)


def _s5_step(u_gate, h0, wb, cpl, lam, dsk):
    nb = u_gate.shape[0]
    G = S5_STEP_BLOCKS
    blk = lambda a, b: pl.BlockSpec((G, a, b), lambda i: (i, 0, 0))
    return pl.pallas_call(
        _s5_step_kernel,
        grid=(S5_NBLK // G,),
        in_specs=[pl.BlockSpec((nb, G * LANE), lambda i: (0, i)), blk(nb, 2 * S5_NS),
                  pl.BlockSpec((G, LANE, 2 * S5_NS), lambda i: (i, S5_Q - 1, 0)),
                  blk(2 * S5_NS, LANE), blk(8, 2 * S5_NS),
                  pl.BlockSpec((1, G * LANE), lambda i: (0, i))],
        out_specs=[pl.BlockSpec((nb, G * LANE), lambda i: (0, i)), blk(nb, 2 * S5_NS)],
        out_shape=[jax.ShapeDtypeStruct((nb, E1), F32),
                   jax.ShapeDtypeStruct((S5_NBLK, nb, 2 * S5_NS), F32)],
        compiler_params=_cp("arbitrary"),
    )(u_gate, h0, wb, cpl, lam, dsk)


def _state_to_blocks(st):
    nb = st.shape[0]
    x = st.reshape(nb, S5_NBLK, S5_NS, 2)
    return jnp.transpose(x, (1, 0, 3, 2)).reshape(S5_NBLK, nb, 2 * S5_NS)


def _blocks_to_state(h):
    nb = h.shape[1]
    x = h.reshape(S5_NBLK, nb, 2, S5_NS)
    return jnp.transpose(x, (1, 0, 3, 2)).reshape(nb, C_GROUPS, C_STATE, 2)


def kernel(x_prompt, x_sample, state_a_ssm, state_a_conv, cache_b_kv, state_c, norm_w, final_norm_w, even_w_in, a_conv_w, a_conv_b, a_dt_bias, a_A_log, a_D, a_norm_w, even_w_out, odd_w_in, c_A_re, c_A_im, c_B_re, c_B_im, c_C_re, c_C_im, c_D, c_log_dt, c_glu_w, c_glu_b, odd_w_out):
    B, L, D = x_prompt.shape
    nS = x_sample.shape[0]
    M = B * L
    Q = S5_Q

    nw0 = norm_w[0][None, :]
    nw1 = norm_w[1][None, :]
    fnw = final_norm_w[None, :]
    cw, cb = a_conv_w[0], a_conv_b[0][None, :]
    pad_h = lambda v: jnp.pad(v, (0, LANE - A_HEADS))[None, :]
    dtb, alog = pad_h(a_dt_bias[0]), pad_h(a_A_log[0])
    dsk_a = jnp.repeat(a_D[0], A_HEAD_DIM)[None, :]
    gnw = a_norm_w[0][None, :]
    head_expand = np.kron(np.eye(A_HEADS, dtype=np.float32), np.ones((1, A_HEAD_DIM), np.float32))
    head_expand = jnp.asarray(np.pad(head_expand, ((0, LANE - A_HEADS), (0, 0)))).astype(BF16)
    slopes_np = np.asarray(2.0 ** (-8.0 * np.arange(1, B_HEADS + 1) / B_HEADS), np.float32)
    slopes = jnp.asarray(np.broadcast_to(slopes_np[:, None, None], (B_HEADS, 8, LANE)))
    dist = np.stack([dil * (LANE - np.arange(LANE)) for _, dil in B_PATTERNS]).astype(np.float32)
    step_bias = jnp.asarray((dist[:, :, None] * slopes_np[None, None, :])[..., None])

    rows8 = lambda p: jnp.broadcast_to(p.reshape(S5_NBLK, 1, S5_NS), (S5_NBLK, 8, S5_NS))
    lane_pad = lambda p: jnp.pad(p, ((0, 0), (0, 0), (0, LANE - p.shape[-1])))
    b_rows = lambda p: lane_pad(jnp.transpose(p, (0, 2, 1)).reshape(S5_NBLK, LANE, C_STATE))
    c_rows = lambda p: lane_pad(jnp.transpose(p, (0, 2, 1)).reshape(S5_NBLK, S5_NS, C_GROUP))
    ldt = jnp.repeat(c_log_dt[0], C_STATE).reshape(S5_NBLK, S5_NS)
    t_m, wb_m, cb_m, lam, lamq, cpl = _s5_prep(
        rows8(c_A_re[0]), rows8(c_A_im[0]), rows8(ldt), b_rows(c_B_re[0]), b_rows(c_B_im[0]),
        c_rows(c_C_re[0]), c_rows(c_C_im[0]))
    w0t = jnp.transpose(even_w_in[0])
    s1 = A_WIDTH + A_CONV_DIM
    s2 = s1 + A_HEADS
    w_a = jnp.concatenate([w0t[A_WIDTH:s1], w0t[:A_WIDTH], w0t[s2 + 6144:]], axis=0).astype(BF16)
    w_qkv = w0t[s2:s2 + 6144].astype(BF16)
    w_dt = jnp.pad(w0t[s1:s2], ((0, LANE - A_HEADS), (0, 0))).astype(BF16)
    dsk_c = c_D[0][None, :]
    glu_b = c_glu_b[0][None, :]

    xs = x_sample.reshape(nS, D)
    xp = x_prompt.reshape(M, D)

    p1 = _mm_norm(xp, nw0, w_a, 1024, A_TILE, keep_xn=True, cast=odd_w_in[0], wt=True, rider=xs)
    a_arr, a_s, w_in1 = p1["o"], p1["os"], p1["cast_out"]
    qkv, kv_keep, w_out0, qkv_s = _proj_qkv(p1["xn"], w_qkv, B, L, 1024, even_w_out[0], p1["xs_n"])
    col = lambda a, b: a_s[:, a:b]
    hv = lambda part: jnp.transpose(qkv_s[part * B_HEADS:(part + 1) * B_HEADS], (1, 0, 2))
    gate_s = col(5120, 7168).reshape(nS, B_HEADS, B_HEAD_DIM)
    kvn = jnp.stack([hv(1), hv(2)], axis=1)
    y_a, ssm_p, tail_p = _ssd(a_arr, p1["xn"], w_dt, B, L, cw, cb, dtb, alog, dsk_a, gnw)
    y_b, new_cache, w_glu, w_out1 = _attn(qkv, slopes, a_arr, cache_b_kv, kvn,
                                          c_glu_w[0], odd_w_out[0])
    y_as, ssm_s, conv_s = _ssd_step(col(0, 3072)[:, None, :], state_a_conv[0],
                                    p1["xs_n"][:, None, :], w_dt, col(3072, 5120)[:, None, :],
                                    state_a_ssm[0].reshape(nS, A_WIDTH, A_STATE),
                                    cw, cb, dtb, alog, head_expand, dsk_a, gnw)
    y_bs = _attn_step(hv(0), hv(1), hv(2), gate_s, cache_b_kv, step_bias)
    hp1, hs1 = _outproj0(y_a, y_b, w_out0, xp, 1024,
                         (y_as.reshape(nS, A_WIDTH), y_bs.reshape(nS, A_WIDTH), xs))

    p3 = _mm_norm(hp1, nw1, w_in1, 1024, 1024, rider=hs1)
    u_gate, u_gate_s = p3["o"], p3["os"]
    g, hfin_p = _s5_main(u_gate, B, t_m, wb_m, cb_m, lamq, dsk_c)
    g_s, hnew_s = _s5_step(u_gate_s, _state_to_blocks(state_c[0]), wb_m, cpl, lam, dsk_c)
    y1, y1_s = _glu(g, w_glu, glu_b, u_gate, 1024, (g_s, u_gate_s))
    y_prompt, y_sample = _outproj1(y1, w_out1, hp1, fnw, 1024, (y1_s, hs1))
    y_prompt = y_prompt.reshape(B, L, D)
    y_sample = y_sample.reshape(nS, 1, D)

    return (y_prompt, y_sample,
            ssm_p.reshape(1, B, A_HEADS, A_HEAD_DIM, A_STATE),
            tail_p[:, 5:8, :][None],
            kv_keep[None],
            _blocks_to_state(hfin_p)[None],
            ssm_s.reshape(1, nS, A_HEADS, A_HEAD_DIM, A_STATE),
            conv_s[None],
            new_cache,
            _blocks_to_state(hnew_s)[None])
```

```python
import functools

import numpy as np
import jax
import jax.numpy as jnp
from jax import lax
from jax.experimental import pallas as pl
from jax.experimental.pallas import tpu as pltpu

F32 = jnp.float32
BF16 = jnp.bfloat16

D_MODEL = 2048
EPS = 1e-5
A_WIDTH = 2048
A_HEAD_DIM = 64
A_HEADS = 32
A_GROUPS = 4
A_STATE = 128
A_CONV_W = 4
A_CONV_DIM = 3072
SSD_CHUNK = 128
B_HEADS = 16
B_HEAD_DIM = 128
B_PATTERNS = ((128, 1), (512, 4), (2048, 16))
B_MAX_WINDOW = 2048
C_GROUP = 16
C_GROUPS = 256
C_STATE = 64
E1 = 4096

LANE = 128
S5_Q = 8
S5_BLK = 8
S5_NBLK = C_GROUPS // S5_BLK
S5_NS = S5_BLK * C_STATE
TN = 512
A_TILE = 1024
ATT_BLOCK = 2048
NEG = -1e30
ATT_SCALE = B_HEAD_DIM ** -0.5
LOG2E = 1.4426950408889634
VMEM_LIMIT = 56 * 1024 * 1024

NT_DIMS = (((1,), (1,)), ((), ()))


def _cp(*sem):
    return pltpu.CompilerParams(dimension_semantics=sem, vmem_limit_bytes=VMEM_LIMIT)


def _silu(x):
    return x * jax.nn.sigmoid(x)


def _softplus(x):
    return jnp.maximum(x, 0.0) + jnp.log1p(jnp.exp(-jnp.abs(x)))


def _split3(a):
    hi = a.astype(BF16)
    r = a - hi.astype(F32)
    mid = r.astype(BF16)
    lo = (r - mid.astype(F32)).astype(BF16)
    return hi, mid, lo


def _dot3_left(m, a):
    hi, mid, lo = _split3(a)
    d = lambda p: jnp.dot(m, p, preferred_element_type=F32)
    return (d(lo) + d(mid)) + d(hi)


def _dot3_right(a, m):
    hi, mid, lo = _split3(a)
    d = lambda p: jnp.dot(p, m, preferred_element_type=F32)
    return (d(lo) + d(mid)) + d(hi)


def _rms(x, w):
    ms = jnp.mean(x * x, axis=-1, keepdims=True)
    return (x * lax.rsqrt(ms + EPS)) * w


QKV_HEADS = 8
CAST_ROWS = 64


def _cast_specs(w, nsteps, step_of):
    rows, cols = w.shape
    per = CAST_ROWS
    while rows // per > nsteps:
        per *= 2
    nblk = rows // per
    assert nblk * per == rows, (w.shape, nsteps)
    spec = pl.BlockSpec((per, cols), lambda *ids: (jnp.minimum(step_of(*ids), nblk - 1), 0))
    return spec, jax.ShapeDtypeStruct(w.shape, BF16)


def _proj_qkv_kernel(rpb, nskip, x_ref, w_ref, src_ref, xs_ref, qkv_ref, kv_ref, dst_ref, qkvs_ref):
    i, j = pl.program_id(0), pl.program_id(1)
    dst_ref[...] = src_ref[...].astype(BF16)

    def heads(x, out_ref, kv_heads=None):
        acc = _dot_w(x.astype(BF16), w_ref[...], True)
        for hh in range(QKV_HEADS):
            out_ref[hh] = acc[:, hh * LANE:(hh + 1) * LANE]
            if kv_heads is not None:
                kv_ref[:, kv_heads + hh, :] = acc[:, hh * LANE:(hh + 1) * LANE]

    pl.when(i == 0)(lambda: heads(xs_ref[...], qkvs_ref))
    tiles_per = B_HEADS // QKV_HEADS
    in_window = (i % rpb >= nskip) & (j >= tiles_per)
    pl.when(jnp.logical_not(in_window))(lambda: heads(x_ref[...], qkv_ref))
    for part in range(tiles_per):
        pl.when(in_window & (j % tiles_per == part))(
            lambda part=part: heads(x_ref[...], qkv_ref, part * QKV_HEADS))


def _proj_qkv(xn, w, B, L, tm, cast, xs_n):
    M = B * L
    ms = xs_n.shape[0]
    tn = QKV_HEADS * LANE
    rpb = L // tm
    keep = min(B_MAX_WINDOW, L)
    nskip = (L - keep) // tm
    tiles_per = B_HEADS // QKV_HEADS
    nj = w.shape[0] // tn
    rcol = _rider_col(nj)

    def kv_map(i, j):
        ri = i % rpb
        return (i // rpb, jnp.maximum(ri - nskip, 0),
                jnp.where((ri < nskip) | (j < 2 * tiles_per), 0, 1), 0, 0)

    cast_spec, cast_shape = _cast_specs(cast, (M // tm) * nj, lambda i, j: i * nj + j)
    return pl.pallas_call(
        functools.partial(_proj_qkv_kernel, rpb, nskip),
        grid=(M // tm, nj),
        in_specs=[pl.BlockSpec((tm, D_MODEL), lambda i, j: (i, 0)),
                  pl.BlockSpec((tn, D_MODEL), lambda i, j: (j, 0)),
                  cast_spec,
                  pl.BlockSpec((ms, D_MODEL), lambda i, j: (0, 0))],
        out_specs=[pl.BlockSpec((None, QKV_HEADS, tm, LANE), lambda i, j: (i // rpb, j, i % rpb, 0)),
                   pl.BlockSpec((None, tm, None, B_HEADS, B_HEAD_DIM), kv_map),
                   cast_spec,
                   pl.BlockSpec((QKV_HEADS, ms, LANE), lambda i, j: (rcol(i, j), 0, 0))],
        out_shape=[jax.ShapeDtypeStruct((B, 3 * B_HEADS, L, B_HEAD_DIM), F32),
                   jax.ShapeDtypeStruct((B, keep, 2, B_HEADS, B_HEAD_DIM), F32),
                   cast_shape,
                   jax.ShapeDtypeStruct((3 * B_HEADS, ms, B_HEAD_DIM), F32)],
        compiler_params=_cp("arbitrary", "arbitrary"),
    )(xn, w, cast, xs_n)


def _dot_w(x, w, wt):
    if wt:
        return lax.dot_general(x, w, NT_DIMS, preferred_element_type=F32)
    return jnp.dot(x, w, preferred_element_type=F32)


def _rider_col(nj):
    return lambda i, j: jnp.where(i == 0, j, nj - 1)


def _mm_norm_kernel(names, wt, *refs):
    r = dict(zip(names, refs))
    j = pl.program_id(1)
    if "cast_src" in r:
        r["cast_out"][...] = r["cast_src"][...].astype(BF16)

    def rows(x_ref, xn_ref, o_ref):
        @pl.when(j == 0)
        def _():
            xn_ref[...] = _rms(x_ref[...], r["nw"][...]).astype(xn_ref.dtype)

        o_ref[...] = _dot_w(xn_ref[...].astype(BF16), r["w"][...], wt)

    rows(r["x"], r["xn"], r["o"])
    if "xs" in r:
        pl.when(pl.program_id(0) == 0)(lambda: rows(r["xs"], r["xs_n"], r["os"]))


def _mm_norm(x, nw, w, tm, tn=TN, keep_xn=False, cast=None, wt=False, rider=None):
    M = x.shape[0]
    N, K = w.shape if wt else w.shape[::-1]
    nj = N // tn
    ins = {"x": (x, pl.BlockSpec((tm, K), lambda i, j: (i, 0))),
           "nw": (nw, pl.BlockSpec((1, K), lambda i, j: (0, 0))),
           "w": (w, pl.BlockSpec((tn, K), lambda i, j: (j, 0)) if wt else
                 pl.BlockSpec((K, tn), lambda i, j: (0, j)))}
    outs = {"o": (jax.ShapeDtypeStruct((M, N), F32), pl.BlockSpec((tm, tn), lambda i, j: (i, j)))}
    scratch = {}
    xn_spec = pl.BlockSpec((tm, K), lambda i, j: (i, 0))
    if keep_xn:
        outs["xn"] = (jax.ShapeDtypeStruct((M, K), BF16), xn_spec)
    else:
        scratch["xn"] = pltpu.VMEM((tm, K), BF16)
    if cast is not None:
        spec, shape = _cast_specs(cast, (M // tm) * nj, lambda i, j: i * nj + j)
        ins["cast_src"] = (cast, spec)
        outs["cast_out"] = (shape, spec)
    if rider is not None:
        ms = rider.shape[0]
        rcol = _rider_col(nj)
        ins["xs"] = (rider, pl.BlockSpec((ms, K), lambda i, j: (0, 0)))
        outs["os"] = (jax.ShapeDtypeStruct((ms, N), F32),
                      pl.BlockSpec((ms, tn), lambda i, j: (0, rcol(i, j))))
        if keep_xn:
            outs["xs_n"] = (jax.ShapeDtypeStruct((ms, K), F32), pl.BlockSpec((ms, K), lambda i, j: (0, 0)))
        else:
            scratch["xs_n"] = pltpu.VMEM((ms, K), F32)
    out = pl.pallas_call(
        functools.partial(_mm_norm_kernel, tuple(ins) + tuple(outs) + tuple(scratch), wt),
        grid=(M // tm, nj),
        in_specs=[s for _, s in ins.values()],
        out_specs=[s for _, s in outs.values()],
        out_shape=[a for a, _ in outs.values()],
        scratch_shapes=list(scratch.values()),
        compiler_params=_cp("arbitrary", "arbitrary"),
    )(*[a for a, _ in ins.values()])
    return dict(zip(outs, out))


def _outproj0_kernel(ya_ref, yb_ref, wa_ref, wb_ref, res_ref, yas_ref, ybs_ref, ress_ref,
                     o_ref, os_ref):
    def rows(ya, yb, res, o):
        o[...] = (res[...] + jnp.dot(ya[...].astype(BF16), wa_ref[...], preferred_element_type=F32)
                  + jnp.dot(yb[...].astype(BF16), wb_ref[...], preferred_element_type=F32))

    rows(ya_ref, yb_ref, res_ref, o_ref)
    pl.when(pl.program_id(0) == 0)(lambda: rows(yas_ref, ybs_ref, ress_ref, os_ref))


def _outproj0(ya, yb, w, res, tm, rider):
    M, K = ya.shape
    ms = rider[0].shape[0]
    nj = D_MODEL // TN
    rcol = _rider_col(nj)
    row = pl.BlockSpec((tm, K), lambda i, j: (i, 0))
    srow = pl.BlockSpec((ms, K), lambda i, j: (0, 0))
    stile = pl.BlockSpec((ms, TN), lambda i, j: (0, rcol(i, j)))
    return pl.pallas_call(
        _outproj0_kernel,
        grid=(M // tm, nj),
        in_specs=[row, row,
                  pl.BlockSpec((K, TN), lambda i, j: (0, j)),
                  pl.BlockSpec((K, TN), lambda i, j: (1, j)),
                  pl.BlockSpec((tm, TN), lambda i, j: (i, j)), srow, srow, stile],
        out_specs=[pl.BlockSpec((tm, TN), lambda i, j: (i, j)), stile],
        out_shape=[jax.ShapeDtypeStruct((M, D_MODEL), F32), jax.ShapeDtypeStruct((ms, D_MODEL), F32)],
        compiler_params=_cp("arbitrary", "arbitrary"),
    )(ya, yb, w, w, res, *rider)


def _glu_kernel(g_ref, wa_ref, wb_ref, ba_ref, bb_ref, gate_ref, gs_ref, gates_ref, o_ref, os_ref):
    def rows(g_ref, gate_ref, o_ref):
        tm = g_ref.shape[0]
        hr = min(tm, 128)
        for r in range(tm // hr):
            rs = pl.ds(r * hr, hr)
            g = g_ref[rs, :].astype(BF16)
            ga = jnp.dot(g, wa_ref[...], preferred_element_type=F32) + ba_ref[...]
            gb = jnp.dot(g, wb_ref[...], preferred_element_type=F32) + bb_ref[...]
            o_ref[rs, :] = (ga * jax.nn.sigmoid(gb) * _silu(gate_ref[rs, :])).astype(o_ref.dtype)

    rows(g_ref, gate_ref, o_ref)
    pl.when(pl.program_id(0) == 0)(lambda: rows(gs_ref, gates_ref, os_ref))


def _glu(g, w, bias, u_gate, tm, rider):
    M = g.shape[0]
    gs, ugs = rider
    ms = gs.shape[0]
    nb = E1 // TN
    rcol = _rider_col(nb)
    return pl.pallas_call(
        _glu_kernel,
        grid=(M // tm, nb),
        in_specs=[pl.BlockSpec((tm, E1), lambda i, j: (i, 0)),
                  pl.BlockSpec((E1, TN), lambda i, j: (0, j)),
                  pl.BlockSpec((E1, TN), lambda i, j: (0, j + nb)),
                  pl.BlockSpec((1, TN), lambda i, j: (0, j)),
                  pl.BlockSpec((1, TN), lambda i, j: (0, j + nb)),
                  pl.BlockSpec((tm, TN), lambda i, j: (i, j + nb)),
                  pl.BlockSpec((ms, E1), lambda i, j: (0, 0)),
                  pl.BlockSpec((ms, TN), lambda i, j: (0, nb + rcol(i, j)))],
        out_specs=[pl.BlockSpec((tm, TN), lambda i, j: (i, j)),
                   pl.BlockSpec((ms, TN), lambda i, j: (0, rcol(i, j)))],
        out_shape=[jax.ShapeDtypeStruct((M, E1), BF16), jax.ShapeDtypeStruct((ms, E1), F32)],
        compiler_params=_cp("arbitrary", "arbitrary"),
    )(g, w, w, bias, bias, u_gate, gs, ugs)


def _outproj1_kernel(y_ref, w_ref, res_ref, fnw_ref, ys_ref, ress_ref, o_ref, os_ref):
    j = pl.program_id(1)
    nj = D_MODEL // TN

    def rows(y_ref, res_ref, o_ref):
        h = res_ref[...] + jnp.dot(y_ref[...].astype(BF16), w_ref[...], preferred_element_type=F32)
        for jj in range(nj):
            @pl.when(j == jj)
            def _(jj=jj):
                o_ref[:, jj * TN:(jj + 1) * TN] = h

        @pl.when(j == nj - 1)
        def _():
            o_ref[...] = _rms(o_ref[...], fnw_ref[...])

    rows(y_ref, res_ref, o_ref)
    pl.when(pl.program_id(0) == 0)(lambda: rows(ys_ref, ress_ref, os_ref))


def _outproj1(y, w, res, fnw, tm, rider):
    M, K = y.shape
    ys, ress = rider
    ms = ys.shape[0]
    nj = D_MODEL // TN
    rcol = _rider_col(nj)
    return pl.pallas_call(
        _outproj1_kernel,
        grid=(M // tm, nj),
        in_specs=[pl.BlockSpec((tm, K), lambda i, j: (i, 0)),
                  pl.BlockSpec((K, TN), lambda i, j: (0, j)),
                  pl.BlockSpec((tm, TN), lambda i, j: (i, j)),
                  pl.BlockSpec((1, D_MODEL), lambda i, j: (0, 0)),
                  pl.BlockSpec((ms, K), lambda i, j: (0, 0)),
                  pl.BlockSpec((ms, TN), lambda i, j: (0, rcol(i, j)))],
        out_specs=[pl.BlockSpec((tm, D_MODEL), lambda i, j: (i, 0)),
                   pl.BlockSpec((ms, D_MODEL), lambda i, j: (0, 0))],
        out_shape=[jax.ShapeDtypeStruct((M, D_MODEL), F32), jax.ShapeDtypeStruct((ms, D_MODEL), F32)],
        compiler_params=_cp("arbitrary", "arbitrary"),
    )(y, w, res, fnw, ys, ress)


def _ssd_kernel(xbc_ref, dt_ref, z0_ref, z1_ref, cw_ref, cb_ref, dtb_ref, alog_ref, dsk_ref, gnw_ref,
                y_ref, st_ref, tail_ref, h_scr, ext_scr, y_scr):
    c = pl.program_id(1)
    Q = SSD_CHUNK

    @pl.when(c == 0)
    def _():
        h_scr[...] = jnp.zeros_like(h_scr)
        ext_scr[0:8, :] = jnp.zeros((8, A_CONV_DIM), F32)

    x_raw = xbc_ref[...]
    ext_scr[8:8 + Q, :] = x_raw
    conv = cb_ref[...] + cw_ref[3:4, :] * x_raw
    for k in (1, 2, 3):
        conv = conv + cw_ref[3 - k:4 - k, :] * ext_scr[pl.ds(8 - k, Q), :]
    ext_scr[0:8, :] = x_raw[Q - 8:Q, :]
    xc = _silu(conv)

    dt = _softplus(dt_ref[...] + dtb_ref[...])
    a = dt * (-jnp.exp(alog_ref[...]))
    qi = lax.broadcasted_iota(jnp.int32, (Q, Q), 0)
    si = lax.broadcasted_iota(jnp.int32, (Q, Q), 1)
    tri = qi >= si
    a_cum = _dot3_left(tri.astype(BF16), a)
    a_cum_t = a_cum.T
    lo_half = si < A_HEAD_DIM

    for g in range(A_GROUPS):
        bg = xc[:, A_WIDTH + g * A_STATE:A_WIDTH + (g + 1) * A_STATE].astype(BF16)
        cg = xc[:, A_WIDTH + (A_GROUPS + g) * A_STATE:A_WIDTH + (A_GROUPS + g + 1) * A_STATE].astype(BF16)
        cbm = lax.dot_general(cg, bg, NT_DIMS, preferred_element_type=F32)
        hprev = h_scr[g * 512:(g + 1) * 512, :]
        yoff = lax.dot_general(cg, hprev.astype(BF16), NT_DIMS, preferred_element_type=F32)
        for jj in range(4):
            p = g * 4 + jj
            h0, h1 = 2 * p, 2 * p + 1
            dtp = jnp.where(lo_half, dt[:, h0:h0 + 1], dt[:, h1:h1 + 1])
            ap = jnp.where(lo_half, a_cum[:, h0:h0 + 1], a_cum[:, h1:h1 + 1])
            x_p = xc[:, p * LANE:(p + 1) * LANE]
            xdt = x_p * dtp
            g0 = cbm * jnp.exp(jnp.where(tri, a_cum[:, h0:h0 + 1] - a_cum_t[h0:h0 + 1, :], NEG))
            g1 = cbm * jnp.exp(jnp.where(tri, a_cum[:, h1:h1 + 1] - a_cum_t[h1:h1 + 1, :], NEG))
            lhs = jnp.concatenate([g0, g1], axis=1).astype(BF16)
            rhs = jnp.concatenate([jnp.where(lo_half, xdt, 0.0), jnp.where(lo_half, 0.0, xdt)],
                                  axis=0).astype(BF16)
            ydiag = jnp.dot(lhs, rhs, preferred_element_type=F32)
            y_scr[:, p * LANE:(p + 1) * LANE] = (
                ydiag + yoff[:, jj * LANE:(jj + 1) * LANE] * jnp.exp(ap)
                + dsk_ref[:, p * LANE:(p + 1) * LANE] * x_p)
            xd = xdt * jnp.exp(ap[Q - 1:Q, :] - ap)
            st = jnp.dot(xd.T.astype(BF16), bg, preferred_element_type=F32)
            cd = jnp.exp(jnp.where(qi < A_HEAD_DIM, a_cum_t[h0:h0 + 1, Q - 1:Q],
                                   a_cum_t[h1:h1 + 1, Q - 1:Q]))
            h_scr[p * LANE:(p + 1) * LANE, :] = cd * hprev[jj * LANE:(jj + 1) * LANE, :] + st

    z = jnp.concatenate([z0_ref[...], z1_ref[...]], axis=1)
    y_ref[...] = _rms(y_scr[...] * _silu(z), gnw_ref[...]).astype(y_ref.dtype)

    @pl.when(c == pl.num_programs(1) - 1)
    def _():
        st_ref[...] = h_scr[...]
        tail_ref[...] = x_raw[Q - 8:Q, :]


def _ssd(a_arr, B, L, cw, cb, dtb, alog, dsk, gnw):
    nc = L // SSD_CHUNK
    Q = SSD_CHUNK
    row = lambda b, c: b * nc + c
    full = lambda shp: pl.BlockSpec(shp, lambda b, c: (0, 0))
    return pl.pallas_call(
        _ssd_kernel,
        grid=(B, nc),
        in_specs=[pl.BlockSpec((Q, A_CONV_DIM), lambda b, c: (row(b, c), 0)),
                  pl.BlockSpec((Q, LANE), lambda b, c: (row(b, c), 14 * TN // LANE)),
                  pl.BlockSpec((Q, A_WIDTH // 2), lambda b, c: (row(b, c), 3)),
                  pl.BlockSpec((Q, A_WIDTH // 2), lambda b, c: (row(b, c), 4)),
                  full((A_CONV_W, A_CONV_DIM)), full((1, A_CONV_DIM)), full((1, LANE)),
                  full((1, LANE)), full((1, A_WIDTH)), full((1, A_WIDTH))],
        out_specs=[pl.BlockSpec((Q, A_WIDTH), lambda b, c: (row(b, c), 0)),
                   pl.BlockSpec((None, A_WIDTH, A_STATE), lambda b, c: (b, 0, 0)),
                   pl.BlockSpec((None, 8, A_CONV_DIM), lambda b, c: (b, 0, 0))],
        out_shape=[jax.ShapeDtypeStruct((B * L, A_WIDTH), BF16),
                   jax.ShapeDtypeStruct((B, A_WIDTH, A_STATE), F32),
                   jax.ShapeDtypeStruct((B, 8, A_CONV_DIM), F32)],
        scratch_shapes=[pltpu.VMEM((A_WIDTH, A_STATE), F32), pltpu.VMEM((8 + Q, A_CONV_DIM), F32),
                        pltpu.VMEM((Q, A_WIDTH), F32)],
        compiler_params=_cp("arbitrary", "arbitrary"),
    )(a_arr, a_arr, a_arr, a_arr, cw, cb, dtb, alog, dsk, gnw)


def _ssd_step_kernel(xbc_ref, cs_ref, dt_ref, z_ref, h0_ref, cw_ref, cb_ref, dtb_ref, alog_ref,
                     e_ref, dsk_ref, gnw_ref, y_ref, hn_ref, nc_ref):
    x_raw = xbc_ref[...]
    cs = cs_ref[...]
    conv = (cb_ref[...] + cw_ref[0:1, :] * cs[0:1, :] + cw_ref[1:2, :] * cs[1:2, :]
            + cw_ref[2:3, :] * cs[2:3, :] + cw_ref[3:4, :] * x_raw)
    xc = _silu(conv)
    nc_ref[0:1, :] = cs[1:2, :]
    nc_ref[1:2, :] = cs[2:3, :]
    nc_ref[2:3, :] = x_raw

    dt = _softplus(dt_ref[...] + dtb_ref[...])
    da = dt * (-jnp.exp(alog_ref[...]))
    e = e_ref[...]
    dt_e = _dot3_right(jnp.broadcast_to(dt, (8, LANE)), e)[0:1, :]
    da_e = _dot3_right(jnp.broadcast_to(da, (8, LANE)), e)[0:1, :]
    xs = xc[:, :A_WIDTH]
    sub = lax.broadcasted_iota(jnp.int32, (8, 1), 0)
    rows = jnp.where(sub == 0, xs * dt_e, jnp.where(sub == 1, jnp.exp(da_e), 0.0))
    cols = jnp.concatenate([rows, jnp.zeros((LANE - 8, A_WIDTH), F32)], axis=0).T
    xdt_col = cols[:, 0:1]
    dec_col = cols[:, 1:2]
    ys = []
    for g in range(A_GROUPS):
        bg = xc[:, A_WIDTH + g * A_STATE:A_WIDTH + (g + 1) * A_STATE]
        cg = xc[:, A_WIDTH + (A_GROUPS + g) * A_STATE:A_WIDTH + (A_GROUPS + g + 1) * A_STATE]
        sl = slice(g * 512, (g + 1) * 512)
        hn = dec_col[sl, :] * h0_ref[sl, :] + xdt_col[sl, :] * bg
        hn_ref[sl, :] = hn
        c8 = jnp.broadcast_to(cg, (8, A_STATE)).astype(BF16)
        ys.append(lax.dot_general(c8, hn.astype(BF16), NT_DIMS, preferred_element_type=F32)[0:1, :])
    y = jnp.concatenate(ys, axis=1) + dsk_ref[...] * xs
    y_ref[...] = _rms(y * _silu(z_ref[...]), gnw_ref[...])


def _ssd_step(xbc, cs, dt, z, h0, cw, cb, dtb, alog, e, dsk, gnw):
    nb = xbc.shape[0]
    per_b = lambda n: pl.BlockSpec((None, 1, n), lambda b: (b, 0, 0))
    full = lambda shp: pl.BlockSpec(shp, lambda b: (0, 0))
    return pl.pallas_call(
        _ssd_step_kernel,
        grid=(nb,),
        in_specs=[per_b(A_CONV_DIM), pl.BlockSpec((None, 3, A_CONV_DIM), lambda b: (b, 0, 0)),
                  per_b(LANE), per_b(A_WIDTH),
                  pl.BlockSpec((None, A_WIDTH, A_STATE), lambda b: (b, 0, 0)),
                  full((A_CONV_W, A_CONV_DIM)), full((1, A_CONV_DIM)), full((1, LANE)),
                  full((1, LANE)), full((LANE, A_WIDTH)), full((1, A_WIDTH)), full((1, A_WIDTH))],
        out_specs=[per_b(A_WIDTH),
                   pl.BlockSpec((None, A_WIDTH, A_STATE), lambda b: (b, 0, 0)),
                   pl.BlockSpec((None, 3, A_CONV_DIM), lambda b: (b, 0, 0))],
        out_shape=[jax.ShapeDtypeStruct((nb, 1, A_WIDTH), F32),
                   jax.ShapeDtypeStruct((nb, A_WIDTH, A_STATE), F32),
                   jax.ShapeDtypeStruct((nb, 3, A_CONV_DIM), F32)],
        compiler_params=_cp("arbitrary"),
    )(xbc, cs, dt, z, h0, cw, cb, dtb, alog, e, dsk, gnw)


def _bias_tile(slope, dil):
    i = lax.broadcasted_iota(jnp.int32, (LANE, 2 * LANE), 0)
    c = lax.broadcasted_iota(jnp.int32, (LANE, 2 * LANE), 1)
    j = LANE + i - c
    valid = (j >= 0) & (j <= LANE)
    return jnp.where(valid, -(slope * (dil * j).astype(F32)), NEG), c


def _attn_tile(q, kcat, vcat, bias2):
    s = lax.dot_general(q, kcat, NT_DIMS, preferred_element_type=F32) * (ATT_SCALE * LOG2E) + bias2
    m = jnp.max(s, axis=1, keepdims=True)
    p = jnp.exp2(s - m)
    l = jnp.sum(p, axis=1, keepdims=True)
    acc = jnp.dot(p.astype(BF16), vcat, preferred_element_type=F32)
    return m, l, acc


def _rows(start, stride):
    return pl.ds(start, LANE, stride=stride) if stride > 1 else pl.ds(start, LANE)


def _shift_cache_block(step, steps_per_seq, x_ref, nxt_ref, kvn_ref, o_ref):
    R = x_ref.shape[0]
    o_ref[pl.ds(0, R - 1)] = x_ref[pl.ds(1, R - 1)]
    last = step % steps_per_seq == steps_per_seq - 1
    o_ref[R - 1] = jnp.where(last, kvn_ref[...], nxt_ref[0])


def _attn_kernel(steps_per_seq, slope_ref, q_ref, k_ref, v_ref, kp_ref, vp_ref, gate_ref,
                 x_ref, nxt_ref, kvn_ref, wa_ref, wb_ref, y_ref, oc_ref, wa_out, wb_out, *scr):
    step = ((pl.program_id(0) * pl.num_programs(1) + pl.program_id(1)) * pl.num_programs(2)
            + pl.program_id(2))
    _shift_cache_block(step, steps_per_seq, x_ref, nxt_ref, kvn_ref, oc_ref)
    wa_out[...] = wa_ref[...].astype(BF16)
    wb_out[...] = wb_ref[...].astype(BF16)
    first_block = pl.program_id(2) == 0
    slope = slope_ref[0:1, 0:1]
    npat = len(B_PATTERNS)
    m_scr, l_scr, acc_scr = scr[:npat], scr[npat:2 * npat], scr[2 * npat:]
    for pat, (_, dil) in enumerate(B_PATTERNS):
        bias_full, c = _bias_tile(slope, dil)
        bias_full = jnp.where(bias_full > NEG, bias_full * LOG2E, NEG)
        bias0 = jnp.where((c >= LANE) | jnp.logical_not(first_block), bias_full, NEG)
        nsub = ATT_BLOCK // dil // LANE
        for r in range(dil):
            kc = vc = None
            for sub in range(nsub):
                rows = _rows(sub * LANE * dil + r, dil)
                if sub == 0:
                    prow = _rows((nsub - 1) * LANE * dil + r, dil)
                    kp, vp, bias = kp_ref[prow, :], vp_ref[prow, :], bias0
                else:
                    kp, vp, bias = kc, vc, bias_full
                kc, vc = k_ref[rows, :], v_ref[rows, :]
                kcat = jnp.concatenate([kp, kc], axis=0).astype(BF16)
                vcat = jnp.concatenate([vp, vc], axis=0).astype(BF16)
                m, l, acc = _attn_tile(q_ref[rows, :].astype(BF16), kcat, vcat, bias)
                m_scr[pat][rows, :] = jnp.broadcast_to(m, (LANE, LANE))
                l_scr[pat][rows, :] = jnp.broadcast_to(l, (LANE, LANE))
                acc_scr[pat][rows, :] = acc
    for sub in range(ATT_BLOCK // LANE):
        rs = pl.ds(sub * LANE, LANE)
        ms = [m_scr[p][rs, :] for p in range(npat)]
        m = functools.reduce(jnp.maximum, ms)
        l = acc = 0.0
        for p in range(npat):
            a = jnp.exp2(ms[p] - m)
            l = l + a * l_scr[p][rs, :]
            acc = acc + a * acc_scr[p][rs, :]
        y_ref[rs, :] = ((acc / l) * _silu(gate_ref[rs, :])).astype(y_ref.dtype)


def _attn(qkv, slopes, a_arr, cache6, kvn, wa, wb):
    B, H3, L, dh = qkv.shape
    H = H3 // 3
    R = ATT_BLOCK
    nblk = L // R
    cur = lambda part: pl.BlockSpec((None, None, R, dh), lambda b, h, i: (b, part * H + h, i, 0))
    prev = lambda part: pl.BlockSpec((None, None, R, dh),
                                     lambda b, h, i: (b, part * H + h, jnp.maximum(i - 1, 0), 0))
    slope = pl.BlockSpec((None, 8, LANE), lambda b, h, i: (h, 0, 0))
    gate = pl.BlockSpec((R, LANE), lambda b, h, i: (b * nblk + i, 10 * TN // LANE + h))

    nb, W = cache6.shape[1], cache6.shape[2]
    steps = B * H * nblk
    per = steps // nb
    rows = W // per
    assert per * nb == steps and rows * per == W, (steps, nb, W)
    tail = (2, B_HEADS, B_HEAD_DIM)
    step = lambda b, h, i: (b * H + h) * nblk + i
    cblk = pl.BlockSpec((None, None, rows) + tail,
                        lambda b, h, i: (0, step(b, h, i) // per, step(b, h, i) % per, 0, 0, 0))
    cnext = pl.BlockSpec((None, None, 1) + tail,
                         lambda b, h, i: (0, step(b, h, i) // per,
                                          jnp.minimum((step(b, h, i) % per + 1) * rows, W - 1), 0, 0, 0))
    cnew = pl.BlockSpec((None,) + tail, lambda b, h, i: (step(b, h, i) // per, 0, 0, 0))
    wa_spec, wa_shape = _cast_specs(wa, steps, step)
    wb_spec, wb_shape = _cast_specs(wb, steps, step)
    return pl.pallas_call(
        functools.partial(_attn_kernel, per),
        grid=(B, H, nblk),
        in_specs=[slope, cur(0), cur(1), cur(2), prev(1), prev(2), gate, cblk, cnext, cnew,
                  wa_spec, wb_spec],
        out_specs=[pl.BlockSpec((R, LANE), lambda b, h, i: (b * nblk + i, h)), cblk,
                   wa_spec, wb_spec],
        out_shape=[jax.ShapeDtypeStruct((B * L, H * dh), BF16),
                   jax.ShapeDtypeStruct(cache6.shape, F32), wa_shape, wb_shape],
        scratch_shapes=[pltpu.VMEM((R, LANE), F32)] * (3 * len(B_PATTERNS)),
        compiler_params=_cp("arbitrary", "arbitrary", "arbitrary"),
    )(slopes, qkv, qkv, qkv, qkv, qkv, a_arr, cache6, cache6, kvn, wa, wb)


def _attn_step_kernel(bias_ref, q_ref, kn_ref, vn_ref, gate_ref, c1_ref, c4_ref, c16_ref, y_ref):
    q, kn, vn = q_ref[...], kn_ref[...], vn_ref[...]
    s_self = jnp.sum(q * kn, axis=-1, keepdims=True) * ATT_SCALE
    blocks = (c1_ref, c4_ref, c16_ref)
    scores = []
    for p, c_ref in enumerate(blocks):
        kblk = c_ref[:, 0]
        scores.append(jnp.sum(kblk * q[None], axis=-1, keepdims=True) * ATT_SCALE - bias_ref[p])
    m = s_self
    for s in scores:
        m = jnp.maximum(m, jnp.max(s, axis=0))
    p_self = float(len(B_PATTERNS)) * jnp.exp(s_self - m)
    l = p_self
    o = p_self * vn
    for s, c_ref in zip(scores, blocks):
        p = jnp.exp(s - m[None])
        l = l + jnp.sum(p, axis=0)
        o = o + jnp.sum(p * c_ref[:, 1], axis=0)
    y_ref[...] = (o / l) * _silu(gate_ref[...])


def _attn_step(q, kn, vn, gate, cache6, bias):
    nb = q.shape[0]
    W = B_MAX_WINDOW
    hspec = pl.BlockSpec((None, B_HEADS, B_HEAD_DIM), lambda b: (b, 0, 0))
    tail = (2, B_HEADS, B_HEAD_DIM)
    views, specs = [], []
    for _, dil in B_PATTERNS:
        if dil == 1:
            views.append(cache6)
            specs.append(pl.BlockSpec((None, None, LANE) + tail,
                                      lambda b: (0, b, W // LANE - 1, 0, 0, 0)))
        else:
            views.append(cache6.reshape((1, nb, W // dil, dil) + tail))
            specs.append(pl.BlockSpec((None, None, LANE, None) + tail,
                                      lambda b, dil=dil: (0, b, W // dil // LANE - 1, 0, 0, 0, 0)))
    return pl.pallas_call(
        _attn_step_kernel,
        grid=(nb,),
        in_specs=[pl.BlockSpec(bias.shape, lambda b: (0, 0, 0, 0)), hspec, hspec, hspec, hspec]
                 + specs,
        out_specs=hspec,
        out_shape=jax.ShapeDtypeStruct((nb, B_HEADS, B_HEAD_DIM), F32),
        compiler_params=_cp("arbitrary"),
    )(bias, q, kn, vn, gate, *views)


def _cmul(ar, ai, br, bi):
    return ar * br - ai * bi, ar * bi + ai * br


def _blockdiag_rows(x):
    hi = pltpu.roll(x, C_STATE, 1)
    g = lax.broadcasted_iota(jnp.int32, (LANE, LANE), 0) // C_GROUP
    return jnp.concatenate([jnp.where(g == 2 * b, x, jnp.where(g == 2 * b + 1, hi, 0.0))
                            for b in range(S5_NS // LANE)], axis=1)


def _blockdiag_cols(x):
    piece = lambda g: x[g * C_STATE:(g + 1) * C_STATE, :]
    return jnp.concatenate([piece(0)] + [pltpu.roll(piece(g), C_GROUP * g, 1)
                                         for g in range(1, S5_BLK)], axis=0)


def _s5_prep_kernel(are_ref, aim_ref, ldt_ref, bre_ref, bim_ref, ctre_ref, ctim_ref,
                    t_ref, wb_ref, cb_ref, lam_ref, lamq_ref, cpl_ref):
    Q = S5_Q
    are, aim = are_ref[...], aim_ref[...]
    dt = jnp.exp(ldt_ref[...])
    mag = jnp.exp(dt * are)
    l_re = mag * jnp.cos(dt * aim)
    l_im = mag * jnp.sin(dt * aim)
    den = are * are + aim * aim
    f_re = ((l_re - 1.0) * are + l_im * aim) / den
    f_im = (l_im * are - (l_re - 1.0) * aim) / den
    lr, li = l_re[0:1, :], l_im[0:1, :]
    bb_re, bb_im = _cmul(f_re[0:1, :], f_im[0:1, :], _blockdiag_rows(bre_ref[...]),
                         _blockdiag_rows(bim_ref[...]))
    ctre, ctim = _blockdiag_cols(ctre_ref[...]), _blockdiag_cols(ctim_ref[...])
    cplain = jnp.concatenate([ctre, -ctim], axis=0).astype(BF16)
    cpl_ref[...] = cplain
    stk = jnp.concatenate([l_re, l_im, jnp.zeros((LANE - 16, S5_NS), F32)], axis=0).T
    lc_re, lc_im = stk[:, 0:1], stk[:, 8:9]
    p_re, p_im = jnp.ones((1, S5_NS), F32), jnp.zeros((1, S5_NS), F32)
    pc_re, pc_im = lc_re, lc_im
    dts = []
    for tau in range(Q):
        e_re, e_im = _cmul(bb_re, bb_im, p_re, p_im)
        wp = jnp.concatenate([e_re, e_im], axis=1).astype(BF16)
        s = Q - 1 - tau
        wb_ref[s * LANE:(s + 1) * LANE, :] = wp
        dts.append(jnp.dot(wp, cplain, preferred_element_type=F32).astype(BF16))
        c_re, c_im = _cmul(ctre, ctim, pc_re, pc_im)
        cb_ref[:, tau * LANE:(tau + 1) * LANE] = jnp.concatenate([c_re, -c_im], axis=0).astype(BF16)
        p_re, p_im = _cmul(p_re, p_im, lr, li)
        pc_re, pc_im = _cmul(pc_re, pc_im, lc_re, lc_im)
    zero = jnp.zeros((LANE, LANE), BF16)
    for s in range(Q):
        for q in range(Q):
            t_ref[s * LANE:(s + 1) * LANE, q * LANE:(q + 1) * LANE] = dts[q - s] if q >= s else zero
    lam_ref[...] = jnp.concatenate([l_re, l_im], axis=1)
    sub = lax.broadcasted_iota(jnp.int32, (8, 1), 0)
    q_re, q_im = p_re, p_im
    tab_re, tab_im = jnp.zeros((8, S5_NS), F32), jnp.zeros((8, S5_NS), F32)
    for r in range(8):
        tab_re = jnp.where(sub == r, q_re, tab_re)
        tab_im = jnp.where(sub == r, q_im, tab_im)
        q_re, q_im = _cmul(q_re, q_im, p_re, p_im)
    lamq_ref[...] = jnp.concatenate([tab_re, tab_im], axis=1)


def _s5_prep(are, aim, ldt, bre, bim, ctre, ctim):
    Q = S5_Q
    row = pl.BlockSpec((None, 8, S5_NS), lambda i: (i, 0, 0))
    blk = lambda a, b: pl.BlockSpec((None, a, b), lambda i: (i, 0, 0))
    return pl.pallas_call(
        _s5_prep_kernel,
        grid=(S5_NBLK,),
        in_specs=[row, row, row, blk(LANE, LANE), blk(LANE, LANE), blk(S5_NS, LANE), blk(S5_NS, LANE)],
        out_specs=[blk(Q * LANE, Q * LANE), blk(Q * LANE, 2 * S5_NS), blk(2 * S5_NS, Q * LANE),
                   blk(8, 2 * S5_NS), blk(8, 2 * S5_NS), blk(2 * S5_NS, LANE)],
        out_shape=[jax.ShapeDtypeStruct((S5_NBLK, Q * LANE, Q * LANE), BF16),
                   jax.ShapeDtypeStruct((S5_NBLK, Q * LANE, 2 * S5_NS), BF16),
                   jax.ShapeDtypeStruct((S5_NBLK, 2 * S5_NS, Q * LANE), BF16),
                   jax.ShapeDtypeStruct((S5_NBLK, 8, 2 * S5_NS), F32),
                   jax.ShapeDtypeStruct((S5_NBLK, 8, 2 * S5_NS), F32),
                   jax.ShapeDtypeStruct((S5_NBLK, 2 * S5_NS, LANE), BF16)],
        compiler_params=_cp("arbitrary"),
    )(are, aim, ldt, bre, bim, ctre, ctim)


def _s5_main_kernel(nb, u_ref, t_ref, wb_ref, cb_ref, lamq_ref, dsk_ref, g_ref, hfin_ref,
                    lhs_scr, s_scr, hp_scr, gs_scr):
    Q = S5_Q
    NS = S5_NS
    nrow = u_ref.shape[0] // Q
    nch = nrow // nb
    for s in range(Q):
        lhs_scr[:, s * LANE:(s + 1) * LANE] = u_ref[pl.ds(s, nrow, stride=Q), :].astype(BF16)
    tab = lamq_ref[...]
    tab_re, tab_im = tab[:, :NS], tab[:, NS:]
    sub = lax.broadcasted_iota(jnp.int32, (8, 1), 0)

    def scan_tile(row0, h_re, h_im):
        st = s_scr[pl.ds(row0, 8), :]
        x_re, x_im = st[:, :NS], st[:, NS:]
        for sh in (1, 2, 4):
            y_re = jnp.where(sub >= sh, pltpu.roll(x_re, sh, 0), 0.0)
            y_im = jnp.where(sub >= sh, pltpu.roll(x_im, sh, 0), 0.0)
            d_re, d_im = _cmul(tab_re[sh - 1:sh, :], tab_im[sh - 1:sh, :], y_re, y_im)
            x_re, x_im = x_re + d_re, x_im + d_im
        c_re, c_im = _cmul(tab_re, tab_im, h_re, h_im)
        x_re, x_im = x_re + c_re, x_im + c_im
        p_re = jnp.where(sub == 0, h_re, pltpu.roll(x_re, 1, 0))
        p_im = jnp.where(sub == 0, h_im, pltpu.roll(x_im, 1, 0))
        hp_scr[pl.ds(row0, 8), :] = jnp.concatenate([p_re, p_im], axis=1)
        return x_re[7:8, :], x_im[7:8, :]

    lhs = lhs_scr[...]
    s_scr[...] = jnp.dot(lhs, wb_ref[...], preferred_element_type=F32)
    state = [(jnp.zeros((1, NS), F32),) * 2] * nb
    for t in range(nch // 8):
        state = [scan_tile(b * nch + t * 8, *state[b]) for b in range(nb)]
    for b in range(nb):
        hfin_ref[b:b + 1, :] = jnp.concatenate(state[b], axis=1)
    y = (jnp.dot(lhs, t_ref[...], preferred_element_type=F32)
         + jnp.dot(hp_scr[...].astype(BF16), cb_ref[...], preferred_element_type=F32))
    for q in range(Q):
        rows = pl.ds(q, nrow, stride=Q)
        yq = y[:, q * LANE:(q + 1) * LANE] + dsk_ref[...] * u_ref[rows, :]
        gs_scr[rows, :] = jax.nn.gelu(yq)
    g_ref[...] = gs_scr[...].astype(g_ref.dtype)


def _s5_main(u_gate, nb, t, wb, cb, lamq, dsk):
    Q = S5_Q
    M = u_gate.shape[0]
    nrow = M // Q
    blk = lambda a, b: pl.BlockSpec((None, a, b), lambda i: (i, 0, 0))
    return pl.pallas_call(
        functools.partial(_s5_main_kernel, nb),
        grid=(S5_NBLK,),
        in_specs=[pl.BlockSpec((M, LANE), lambda i: (0, i)),
                  blk(Q * LANE, Q * LANE), blk(Q * LANE, 2 * S5_NS), blk(2 * S5_NS, Q * LANE),
                  blk(8, 2 * S5_NS), pl.BlockSpec((1, LANE), lambda i: (0, i))],
        out_specs=[pl.BlockSpec((M, LANE), lambda i: (0, i)), blk(nb, 2 * S5_NS)],
        out_shape=[jax.ShapeDtypeStruct((M, E1), BF16),
                   jax.ShapeDtypeStruct((S5_NBLK, nb, 2 * S5_NS), F32)],
        scratch_shapes=[pltpu.VMEM((nrow, Q * LANE), BF16), pltpu.VMEM((nrow, 2 * S5_NS), F32),
                        pltpu.VMEM((nrow, 2 * S5_NS), F32), pltpu.VMEM((M, LANE), F32)],
        compiler_params=_cp("arbitrary"),
    )(u_gate, t, wb, cb, lamq, dsk)


S5_STEP_BLOCKS = 4


def _s5_step_kernel(u_ref, h0_ref, bb_ref, cpl_ref, lam_ref, dsk_ref, g_ref, hn_ref):
    NS = S5_NS
    for k in range(S5_STEP_BLOCKS):
        cs = slice(k * LANE, (k + 1) * LANE)
        u = u_ref[:, cs]
        s = jnp.dot(u.astype(BF16), bb_ref[k], preferred_element_type=F32)
        lam = lam_ref[k, 0:1, :]
        h0 = h0_ref[k]
        n_re, n_im = _cmul(lam[:, :NS], lam[:, NS:], h0[:, :NS], h0[:, NS:])
        hn = jnp.concatenate([n_re + s[:, :NS], n_im + s[:, NS:]], axis=1)
        hn_ref[k] = hn
        y = jnp.dot(hn.astype(BF16), cpl_ref[k], preferred_element_type=F32) + dsk_ref[:, cs] * u
        g_ref[:, cs] = jax.nn.gelu(y)


def _s5_step(u_gate, h0, wb, cpl, lam, dsk):
    nb = u_gate.shape[0]
    G = S5_STEP_BLOCKS
    blk = lambda a, b: pl.BlockSpec((G, a, b), lambda i: (i, 0, 0))
    return pl.pallas_call(
        _s5_step_kernel,
        grid=(S5_NBLK // G,),
        in_specs=[pl.BlockSpec((nb, G * LANE), lambda i: (0, i)), blk(nb, 2 * S5_NS),
                  pl.BlockSpec((G, LANE, 2 * S5_NS), lambda i: (i, S5_Q - 1, 0)),
                  blk(2 * S5_NS, LANE), blk(8, 2 * S5_NS),
                  pl.BlockSpec((1, G * LANE), lambda i: (0, i))],
        out_specs=[pl.BlockSpec((nb, G * LANE), lambda i: (0, i)), blk(nb, 2 * S5_NS)],
        out_shape=[jax.ShapeDtypeStruct((nb, E1), F32),
                   jax.ShapeDtypeStruct((S5_NBLK, nb, 2 * S5_NS), F32)],
        compiler_params=_cp("arbitrary"),
    )(u_gate, h0, wb, cpl, lam, dsk)


def _state_to_blocks(st):
    nb = st.shape[0]
    x = st.reshape(nb, S5_NBLK, S5_NS, 2)
    return jnp.transpose(x, (1, 0, 3, 2)).reshape(S5_NBLK, nb, 2 * S5_NS)


def _blocks_to_state(h):
    nb = h.shape[1]
    x = h.reshape(S5_NBLK, nb, 2, S5_NS)
    return jnp.transpose(x, (1, 0, 3, 2)).reshape(nb, C_GROUPS, C_STATE, 2)


def kernel(x_prompt, x_sample, state_a_ssm, state_a_conv, cache_b_kv, state_c, norm_w, final_norm_w, even_w_in, a_conv_w, a_conv_b, a_dt_bias, a_A_log, a_D, a_norm_w, even_w_out, odd_w_in, c_A_re, c_A_im, c_B_re, c_B_im, c_C_re, c_C_im, c_D, c_log_dt, c_glu_w, c_glu_b, odd_w_out):
    B, L, D = x_prompt.shape
    nS = x_sample.shape[0]
    M = B * L
    Q = S5_Q

    nw0 = norm_w[0][None, :]
    nw1 = norm_w[1][None, :]
    fnw = final_norm_w[None, :]
    cw, cb = a_conv_w[0], a_conv_b[0][None, :]
    pad_h = lambda v: jnp.pad(v, (0, LANE - A_HEADS))[None, :]
    dtb, alog = pad_h(a_dt_bias[0]), pad_h(a_A_log[0])
    dsk_a = jnp.repeat(a_D[0], A_HEAD_DIM)[None, :]
    gnw = a_norm_w[0][None, :]
    head_expand = np.kron(np.eye(A_HEADS, dtype=np.float32), np.ones((1, A_HEAD_DIM), np.float32))
    head_expand = jnp.asarray(np.pad(head_expand, ((0, LANE - A_HEADS), (0, 0)))).astype(BF16)
    slopes_np = np.asarray(2.0 ** (-8.0 * np.arange(1, B_HEADS + 1) / B_HEADS), np.float32)
    slopes = jnp.asarray(np.broadcast_to(slopes_np[:, None, None], (B_HEADS, 8, LANE)))
    dist = np.stack([dil * (LANE - np.arange(LANE)) for _, dil in B_PATTERNS]).astype(np.float32)
    step_bias = jnp.asarray((dist[:, :, None] * slopes_np[None, None, :])[..., None])

    rows8 = lambda p: jnp.broadcast_to(p.reshape(S5_NBLK, 1, S5_NS), (S5_NBLK, 8, S5_NS))
    lane_pad = lambda p: jnp.pad(p, ((0, 0), (0, 0), (0, LANE - p.shape[-1])))
    b_rows = lambda p: lane_pad(jnp.transpose(p, (0, 2, 1)).reshape(S5_NBLK, LANE, C_STATE))
    c_rows = lambda p: lane_pad(jnp.transpose(p, (0, 2, 1)).reshape(S5_NBLK, S5_NS, C_GROUP))
    ldt = jnp.repeat(c_log_dt[0], C_STATE).reshape(S5_NBLK, S5_NS)
    t_m, wb_m, cb_m, lam, lamq, cpl = _s5_prep(
        rows8(c_A_re[0]), rows8(c_A_im[0]), rows8(ldt), b_rows(c_B_re[0]), b_rows(c_B_im[0]),
        c_rows(c_C_re[0]), c_rows(c_C_im[0]))
    w0t = jnp.transpose(even_w_in[0])
    s1 = A_WIDTH + A_CONV_DIM
    s2 = s1 + A_HEADS
    w_a = jnp.concatenate([w0t[A_WIDTH:s1], w0t[:A_WIDTH], w0t[s2 + 6144:], w0t[s1:s2],
                           jnp.zeros((A_TILE - A_HEADS, D), F32)], axis=0).astype(BF16)
    w_qkv = w0t[s2:s2 + 6144].astype(BF16)
    dsk_c = c_D[0][None, :]
    glu_b = c_glu_b[0][None, :]

    xs = x_sample.reshape(nS, D)
    xp = x_prompt.reshape(M, D)

    p1 = _mm_norm(xp, nw0, w_a, 1024, A_TILE, keep_xn=True, cast=c_glu_w[0], wt=True, rider=xs)
    a_arr, a_s, w_glu = p1["o"], p1["os"], p1["cast_out"]
    qkv, kv_keep, w_in1, qkv_s = _proj_qkv(p1["xn"], w_qkv, B, L, 1024, odd_w_in[0], p1["xs_n"])
    col = lambda a, b: a_s[:, a:b]
    hv = lambda part: jnp.transpose(qkv_s[part * B_HEADS:(part + 1) * B_HEADS], (1, 0, 2))
    gate_s = col(5120, 7168).reshape(nS, B_HEADS, B_HEAD_DIM)
    kvn = jnp.stack([hv(1), hv(2)], axis=1)
    y_a, ssm_p, tail_p = _ssd(a_arr, B, L, cw, cb, dtb, alog, dsk_a, gnw)
    y_b, new_cache, w_out0, w_out1 = _attn(qkv, slopes, a_arr, cache_b_kv, kvn,
                                           even_w_out[0], odd_w_out[0])
    y_as, ssm_s, conv_s = _ssd_step(col(0, 3072)[:, None, :], state_a_conv[0],
                                    col(7168, 7296)[:, None, :], col(3072, 5120)[:, None, :],
                                    state_a_ssm[0].reshape(nS, A_WIDTH, A_STATE),
                                    cw, cb, dtb, alog, head_expand, dsk_a, gnw)
    y_bs = _attn_step(hv(0), hv(1), hv(2), gate_s, cache_b_kv, step_bias)
    hp1, hs1 = _outproj0(y_a, y_b, w_out0, xp, 1024,
                         (y_as.reshape(nS, A_WIDTH), y_bs.reshape(nS, A_WIDTH), xs))

    p3 = _mm_norm(hp1, nw1, w_in1, 1024, 1024, rider=hs1)
    u_gate, u_gate_s = p3["o"], p3["os"]
    g, hfin_p = _s5_main(u_gate, B, t_m, wb_m, cb_m, lamq, dsk_c)
    g_s, hnew_s = _s5_step(u_gate_s, _state_to_blocks(state_c[0]), wb_m, cpl, lam, dsk_c)
    y1, y1_s = _glu(g, w_glu, glu_b, u_gate, 1024, (g_s, u_gate_s))
    y_prompt, y_sample = _outproj1(y1, w_out1, hp1, fnw, 1024, (y1_s, hs1))
    y_prompt = y_prompt.reshape(B, L, D)
    y_sample = y_sample.reshape(nS, 1, D)

    return (y_prompt, y_sample,
            ssm_p.reshape(1, B, A_HEADS, A_HEAD_DIM, A_STATE),
            tail_p[:, 5:8, :][None],
            kv_keep[None],
            _blocks_to_state(hfin_p)[None],
            ssm_s.reshape(1, nS, A_HEADS, A_HEAD_DIM, A_STATE),
            conv_s[None],
            new_cache,
            _blocks_to_state(hnew_s)[None])
```
